```python
import math
import jax, jax.numpy as jnp
from jax import lax
import numpy as np

D_MODEL = 2048
BATCH = 2
SEQ = 4096
DEPTH = 1

CHUNK = 64
N_MEM = 256
BRANCH_WIDTH = D_MODEL // 2
N_BRANCH = 3
GLA_HEADS = 4
GLA_DK = (D_MODEL // 4) // GLA_HEADS
GLA_DV = BRANCH_WIDTH // GLA_HEADS
GLA_RANK = 16
GLA_GATE_NORM = 16.0
DIFF_HEADS = 4
DIFF_DV = BRANCH_WIDTH // DIFF_HEADS
DIFF_DH = DIFF_DV // 2
Q_BLOCK = 128
MEM_HEADS = 4
MEM_DH = BRANCH_WIDTH // MEM_HEADS
_FF_RAW = -(-8 * D_MODEL // 3)
D_FF = -(-_FF_RAW // 256) * 256

NORM_EPS = 1e-6
NEG_INF = -1e30

IN_SIZES = (
    GLA_HEADS * GLA_DK,
    GLA_HEADS * GLA_DK,
    GLA_HEADS * GLA_DV,
    GLA_HEADS * GLA_DV,
    GLA_RANK,
    DIFF_HEADS * 2 * DIFF_DH,
    DIFF_HEADS * 2 * DIFF_DH,
    DIFF_HEADS * DIFF_DV,
    MEM_HEADS * MEM_DH,
)

kernel_name = "hybrid_gla_diffattn_memxattn_swiglu"


def rms_norm(x, g):
    xf = x.astype(jnp.float32)
    y = xf * lax.rsqrt(jnp.mean(xf * xf, axis=-1, keepdims=True) + NORM_EPS)
    return (y * g.astype(jnp.float32)).astype(x.dtype)


def split_cols(z, sizes):
    parts, start = [], 0
    for s in sizes:
        parts.append(z[..., start:start + s])
        start += s
    return parts


def gla_mixer(q, k, v, g, a_low, w_alpha_up, b_alpha, norm_g):
    B, S, _ = q.shape
    N = S // CHUNK
    f32 = jnp.float32
    log_a = jax.nn.log_sigmoid((a_low @ w_alpha_up + b_alpha).astype(f32)) / GLA_GATE_NORM

    def heads(t, d):
        return t.reshape(B, N, CHUNK, GLA_HEADS, d).transpose(0, 3, 1, 2, 4)

    qh = heads(q, GLA_DK).astype(f32) * (GLA_DK ** -0.5)
    kh = heads(k, GLA_DK).astype(f32)
    vh = heads(v, GLA_DV).astype(f32)
    bcum = jnp.cumsum(heads(log_a, GLA_DK), axis=3)
    b_last = bcum[:, :, :, -1:, :]
    q_dec = qh * jnp.exp(bcum)
    k_dec = kh * jnp.exp(-bcum)
    causal = jnp.tril(jnp.ones((CHUNK, CHUNK), dtype=bool))
    att = jnp.where(causal, jnp.einsum('bhnld,bhnmd->bhnlm', q_dec, k_dec), 0.0)
    o_intra = jnp.einsum('bhnlm,bhnmv->bhnlv', att, vh)
    k_tail = kh * jnp.exp(b_last - bcum)
    inc = jnp.einsum('bhnld,bhnlv->nbhdv', k_tail, vh)
    decay = jnp.exp(b_last[:, :, :, 0, :]).transpose(2, 0, 1, 3)

    def step(state, inp):
        dec, add = inp
        return dec[..., None] * state + add, state

    s0 = jnp.zeros((B, GLA_HEADS, GLA_DK, GLA_DV), f32)
    _, s_prev = lax.scan(step, s0, (decay, inc))
    o_inter = jnp.einsum('bhnld,nbhdv->bhnlv', q_dec, s_prev)
    o = (o_intra + o_inter).transpose(0, 2, 3, 1, 4).reshape(B, S, GLA_HEADS, GLA_DV)
    o = rms_norm(o, norm_g).reshape(B, S, GLA_HEADS * GLA_DV)
    return (o * jax.nn.silu(g.astype(f32))).astype(q.dtype)


def diff_mixer(q, k, v, qn_g, kn_g, lq1, lk1, lq2, lk2, subln_g, lam_init):
    B, S, _ = q.shape
    f32 = jnp.float32
    qh = rms_norm(q.reshape(B, S, DIFF_HEADS, 2, DIFF_DH), qn_g).transpose(0, 2, 3, 1, 4)
    kh = rms_norm(k.reshape(B, S, DIFF_HEADS, 2, DIFF_DH), kn_g).transpose(0, 2, 3, 1, 4)
    vh = v.reshape(B, S, DIFF_HEADS, DIFF_DV).transpose(0, 2, 1, 3)
    lam = (jnp.exp(jnp.sum(lq1 * lk1).astype(f32)) - jnp.exp(jnp.sum(lq2 * lk2).astype(f32))
           + lam_init)
    nb = S // Q_BLOCK
    q_blocks = qh.reshape(B, DIFF_HEADS, 2, nb, Q_BLOCK, DIFF_DH).transpose(3, 0, 1, 2, 4, 5)
    k_chunk = jnp.arange(S) // CHUNK
    scale = DIFF_DH ** -0.5

    def one_block(args):
        qb, blk = args
        q_chunk = (blk * Q_BLOCK + jnp.arange(Q_BLOCK)) // CHUNK
        mask = k_chunk[None, :] <= q_chunk[:, None]
        s = jnp.einsum('bhjqd,bhjkd->bhjqk', qb, kh).astype(f32) * scale
        p = jax.nn.softmax(jnp.where(mask, s, NEG_INF), axis=-1)
        a = p[:, :, 0] - lam * p[:, :, 1]
        return jnp.einsum('bhqk,bhkv->bhqv', a.astype(vh.dtype), vh)

    o = lax.map(one_block, (q_blocks, jnp.arange(nb)))
    o = o.transpose(1, 0, 3, 2, 4).reshape(B, S, DIFF_HEADS, DIFF_DV)
    o = rms_norm(o, subln_g) * (1.0 - lam_init)
    return o.reshape(B, S, DIFF_HEADS * DIFF_DV)


def mem_cross_attention(q, mem_n, w_mem_kv, qn_g, kn_g):
    B, S, _ = q.shape
    M = mem_n.shape[1]
    qh = rms_norm(q.reshape(B, S, MEM_HEADS, MEM_DH), qn_g)
    km, vm = jnp.split(mem_n @ w_mem_kv, 2, axis=-1)
    kh = rms_norm(km.reshape(B, M, MEM_HEADS, MEM_DH), kn_g)
    vh = vm.reshape(B, M, MEM_HEADS, MEM_DH)
    s = jnp.einsum('bshd,bmhd->bhsm', qh, kh).astype(jnp.float32) * (MEM_DH ** -0.5)
    p = jax.nn.softmax(s, axis=-1).astype(vh.dtype)
    o = jnp.einsum('bhsm,bmhd->bshd', p, vh)
    return o.reshape(B, S, MEM_HEADS * MEM_DH)


def setup_inputs(seed: int = 0) -> dict:
    key = jax.random.key(seed)
    ks = jax.random.split(key, 32)
    f32 = jnp.float32
    L = DEPTH
    d_in = sum(IN_SIZES)

    def nrm(k, shape, scale):
        return jax.random.normal(k, shape, f32) * scale

    def gain(k, shape):
        return 1.0 + 0.02 * jax.random.normal(k, shape, f32)

    return {
        "x": nrm(ks[0], (BATCH, SEQ, D_MODEL), 1.0),
        "mem": nrm(ks[1], (BATCH, N_MEM, D_MODEL), 1.0),
        "norm_mix_g": gain(ks[2], (L, D_MODEL)),
        "norm_mem_g": gain(ks[3], (L, D_MODEL)),
        "w_in": nrm(ks[4], (L, D_MODEL, d_in), D_MODEL ** -0.5),
        "gla_w_alpha_up": nrm(ks[5], (L, GLA_RANK, GLA_HEADS * GLA_DK), GLA_RANK ** -0.5),
        "gla_b_alpha": nrm(ks[6], (L, GLA_HEADS * GLA_DK), 0.1),
        "gla_norm_g": gain(ks[7], (L, GLA_DV)),
        "diff_q_norm_g": gain(ks[8], (L, DIFF_DH)),
        "diff_k_norm_g": gain(ks[9], (L, DIFF_DH)),
        "diff_lambda_q1": nrm(ks[10], (L, DIFF_DH), 0.1),
        "diff_lambda_k1": nrm(ks[11], (L, DIFF_DH), 0.1),
        "diff_lambda_q2": nrm(ks[12], (L, DIFF_DH), 0.1),
        "diff_lambda_k2": nrm(ks[13], (L, DIFF_DH), 0.1),
        "diff_subln_g": gain(ks[14], (L, DIFF_DV)),
        "mem_q_norm_g": gain(ks[15], (L, MEM_DH)),
        "mem_k_norm_g": gain(ks[16], (L, MEM_DH)),
        "w_mem_kv": nrm(ks[17], (L, D_MODEL, 2 * MEM_HEADS * MEM_DH), D_MODEL ** -0.5),
        "w_branch": nrm(ks[18], (L, N_BRANCH, BRANCH_WIDTH, D_MODEL), BRANCH_WIDTH ** -0.5),
        "w_gate": nrm(ks[19], (L, D_MODEL, N_BRANCH * D_MODEL), D_MODEL ** -0.5),
        "b_gate": nrm(ks[20], (L, N_BRANCH * D_MODEL), 0.02),
        "w_out": nrm(ks[21], (L, D_MODEL, D_MODEL), D_MODEL ** -0.5),
        "norm_ffn_g": gain(ks[22], (L, D_MODEL)),
        "w_ffn_in": nrm(ks[23], (L, D_MODEL, 2 * D_FF), D_MODEL ** -0.5),
        "w_ffn_down": nrm(ks[24], (L, D_FF, D_MODEL), D_FF ** -0.5),
    }


def reference(x, mem, norm_mix_g, norm_mem_g, w_in, gla_w_alpha_up, gla_b_alpha, gla_norm_g,
              diff_q_norm_g, diff_k_norm_g, diff_lambda_q1, diff_lambda_k1, diff_lambda_q2,
              diff_lambda_k2, diff_subln_g, mem_q_norm_g, mem_k_norm_g, w_mem_kv, w_branch,
              w_gate, b_gate, w_out, norm_ffn_g, w_ffn_in, w_ffn_down):
    B, S, D = x.shape
    for l in range(DEPTH):
        lam_init = 0.8 - 0.6 * math.exp(-0.3 * l)
        h = rms_norm(x, norm_mix_g[l])
        z = h @ w_in[l]
        (g_q, g_k, g_v, g_g, g_a, d_q, d_k, d_v, m_q) = split_cols(z, IN_SIZES)
        y_gla = gla_mixer(g_q, g_k, g_v, g_g, g_a, gla_w_alpha_up[l], gla_b_alpha[l], gla_norm_g[l])
        y_diff = diff_mixer(d_q, d_k, d_v, diff_q_norm_g[l], diff_k_norm_g[l],
                            diff_lambda_q1[l], diff_lambda_k1[l], diff_lambda_q2[l],
                            diff_lambda_k2[l], diff_subln_g[l], lam_init)
        y_mem = mem_cross_attention(m_q, rms_norm(mem, norm_mem_g[l]), w_mem_kv[l],
                                    mem_q_norm_g[l], mem_k_norm_g[l])
        y = jnp.stack([y_gla.astype(x.dtype), y_diff.astype(x.dtype), y_mem.astype(x.dtype)],
                      axis=2)
        proj = jnp.einsum('bsiw,iwd->bsid', y, w_branch[l])
        gates = jax.nn.sigmoid((h @ w_gate[l] + b_gate[l]).astype(jnp.float32))
        gates = gates.reshape(B, S, N_BRANCH, D).astype(proj.dtype)
        merged = jnp.sum(gates * proj, axis=2)
        x = x + (merged @ w_out[l]).astype(x.dtype)
        hf = rms_norm(x, norm_ffn_g[l])
        f_gate, f_up = jnp.split(hf @ w_ffn_in[l], 2, axis=-1)
        x = x + ((jax.nn.silu(f_gate) * f_up) @ w_ffn_down[l]).astype(x.dtype)
    return x
```

```python
import functools

import jax
import jax.numpy as jnp
from jax import lax
from jax.experimental import pallas as pl
from jax.experimental.pallas import tpu as pltpu

F32 = jnp.float32
BF16 = jnp.bfloat16

CHUNK = 64
GLA_HEADS = 4
GLA_DK = 128
GLA_DV = 256
GLA_RANK = 16
GLA_GATE_NORM = 16.0
DIFF_HEADS = 4
DIFF_DH = 128
DIFF_DV = 256
MEM_HEADS = 4
MEM_DH = 256
N_BRANCH = 3
NORM_EPS = 1e-6
NEG_INF = -1e30
LAM_INIT = 0.8 - 0.6 * 1.0

LANES = 128
VMEM_LIMIT = 56 * 1024 * 1024

Z_GLA_Q, Z_GLA_K, Z_GLA_V, Z_GLA_G = 0, 512, 1024, 2048
Z_DIFF_Q, Z_DIFF_K, Z_DIFF_V, Z_MEM_Q = 3072, 4096, 5120, 6144
Z_WIDTH = 7168
IN_TILE = 1024


def _params(*sem):
    return pltpu.CompilerParams(dimension_semantics=sem, vmem_limit_bytes=VMEM_LIMIT)


def _nt_dot(a, b):
    return lax.dot_general(a, b, (((1,), (1,)), ((), ())), preferred_element_type=F32)


def _tn_dot(a, b, precision=None):
    return lax.dot_general(a, b, (((0,), (0,)), ((), ())), preferred_element_type=F32,
                           precision=precision)


def _rms(v, gain):
    ms = jnp.mean(v * v, axis=-1, keepdims=True)
    return v * lax.rsqrt(ms + NORM_EPS) * gain


def _sigmoid(v):
    return 1.0 / (1.0 + jnp.exp(-v))


def _norm_mix_kernel(x_ref, g_ref, wa_ref, h_ref, a_ref):
    h = _rms(x_ref[...], g_ref[...]).astype(BF16)
    h_ref[...] = h
    a_ref[...] = jnp.dot(h, wa_ref[...], preferred_element_type=F32)


def _norm_mix(x2, g, wa, tr=512):
    m, d = x2.shape
    return pl.pallas_call(
        _norm_mix_kernel,
        out_shape=(jax.ShapeDtypeStruct((m, d), BF16), jax.ShapeDtypeStruct((m, LANES), F32)),
        grid=(m // tr,),
        in_specs=[pl.BlockSpec((tr, d), lambda i: (i, 0)),
                  pl.BlockSpec((1, d), lambda i: (0, 0)),
                  pl.BlockSpec((d, LANES), lambda i: (0, 0))],
        out_specs=(pl.BlockSpec((tr, d), lambda i: (i, 0)),
                   pl.BlockSpec((tr, LANES), lambda i: (i, 0))),
        compiler_params=_params("parallel"),
        name="norm_mix",
    )(x2, g, wa)


def _norm_rows_kernel(x_ref, g_ref, h_ref):
    h_ref[...] = _rms(x_ref[...], g_ref[...]).astype(BF16)


def _norm_rows(x2, g, tr=512):
    m, d = x2.shape
    return pl.pallas_call(
        _norm_rows_kernel,
        out_shape=jax.ShapeDtypeStruct((m, d), BF16),
        grid=(m // tr,),
        in_specs=[pl.BlockSpec((tr, d), lambda i: (i, 0)),
                  pl.BlockSpec((1, d), lambda i: (0, 0))],
        out_specs=pl.BlockSpec((tr, d), lambda i: (i, 0)),
        compiler_params=_params("parallel"),
        name="norm_rows",
    )(x2, g)


def _store_group_norm(acc, gain, width, scale, out_ref):
    for s in range(0, acc.shape[1], width):
        blk = acc[:, s:s + width]
        out_ref[:, s:s + width] = (_rms(blk, gain) * scale).astype(out_ref.dtype)


def _in_proj_kernel(h_ref, w_ref, dq_g_ref, dk_g_ref, mq_g_ref, z_ref):
    j = pl.program_id(1)
    acc = jnp.dot(h_ref[...], w_ref[...], preferred_element_type=F32)
    j_dq, j_dk, j_mq = Z_DIFF_Q // IN_TILE, Z_DIFF_K // IN_TILE, Z_MEM_Q // IN_TILE

    @pl.when((j != j_dq) & (j != j_dk) & (j != j_mq))
    def _():
        z_ref[...] = acc.astype(z_ref.dtype)

    @pl.when(j == j_dq)
    def _():
        _store_group_norm(acc, dq_g_ref[...], DIFF_DH, DIFF_DH ** -0.5, z_ref)

    @pl.when(j == j_dk)
    def _():
        _store_group_norm(acc, dk_g_ref[...], DIFF_DH, 1.0, z_ref)

    @pl.when(j == j_mq)
    def _():
        _store_group_norm(acc, mq_g_ref[...], MEM_DH, MEM_DH ** -0.5, z_ref)


def _in_proj(h, w, dq_g, dk_g, mq_g, tm=1024):
    m, d = h.shape
    n = w.shape[1]
    return pl.pallas_call(
        _in_proj_kernel,
        out_shape=jax.ShapeDtypeStruct((m, n), BF16),
        grid=(m // tm, n // IN_TILE),
        in_specs=[pl.BlockSpec((tm, d), lambda i, j: (i, 0)),
                  pl.BlockSpec((d, IN_TILE), lambda i, j: (0, j)),
                  pl.BlockSpec((1, DIFF_DH), lambda i, j: (0, 0)),
                  pl.BlockSpec((1, DIFF_DH), lambda i, j: (0, 0)),
                  pl.BlockSpec((1, MEM_DH), lambda i, j: (0, 0))],
        out_specs=pl.BlockSpec((tm, IN_TILE), lambda i, j: (i, j)),
        compiler_params=_params("parallel", "arbitrary"),
        name="in_proj",
    )(h, w, dq_g, dk_g, mq_g)


def _gla_kernel(q_ref, k_ref, v_ref, g_ref, a_ref, wup_ref, bal_ref, ng_ref, o_ref, s_ref, *, ts):
    @pl.when(pl.program_id(2) == 0)
    def _():
        s_ref[...] = jnp.zeros_like(s_ref)

    hi = lax.Precision.HIGHEST
    pre = jnp.dot(a_ref[...], wup_ref[...], preferred_element_type=F32, precision=hi) + bal_ref[...]
    log_a = (jnp.minimum(pre, 0.0) - jnp.log1p(jnp.exp(-jnp.abs(pre)))) * (1.0 / GLA_GATE_NORM)

    row = lax.broadcasted_iota(jnp.int32, (CHUNK, CHUNK), 0)
    col = lax.broadcasted_iota(jnp.int32, (CHUNK, CHUNK), 1)
    causal = row >= col
    tril = causal.astype(F32)
    ones = jnp.ones((CHUNK, GLA_DK), F32)

    for c in range(ts // CHUNK):
        rows = slice(c * CHUNK, (c + 1) * CHUNK)
        la = log_a[rows]
        bcum = jnp.dot(tril, la, preferred_element_type=F32, precision=hi)
        b_last = bcum[CHUNK - 1:CHUNK, :]
        q = q_ref[rows, :].astype(F32) * (GLA_DK ** -0.5)
        k = k_ref[rows, :].astype(F32)
        v = v_ref[rows, :]
        q_dec = (q * jnp.exp(bcum)).astype(BF16)
        k_dec = (k * jnp.exp(-bcum)).astype(BF16)
        k_tail = (k * jnp.exp(b_last - bcum)).astype(BF16)
        att = jnp.where(causal, _nt_dot(q_dec, k_dec), 0.0).astype(BF16)
        state = s_ref[...]
        o = (jnp.dot(att, v, preferred_element_type=F32)
             + jnp.dot(q_dec, state.astype(BF16), preferred_element_type=F32))
        inc = _tn_dot(k_tail, v)
        decay_col = jnp.exp(_tn_dot(la, ones, precision=hi))
        decay = jnp.concatenate([decay_col] * (GLA_DV // GLA_DK), axis=1)
        s_ref[...] = decay * state + inc
        gate = g_ref[rows, :].astype(F32)
        o_ref[rows, :] = (_rms(o, ng_ref[...]) * (gate * _sigmoid(gate))).astype(o_ref.dtype)


def _gla(z, a_low, wup, bal, ng, batch, seq, ts=512):
    m = z.shape[0]
    nt = seq // ts
    rows = lambda b, h, t: b * nt + t
    return pl.pallas_call(
        functools.partial(_gla_kernel, ts=ts),
        out_shape=jax.ShapeDtypeStruct((m, GLA_HEADS * GLA_DV), BF16),
        grid=(batch, GLA_HEADS, nt),
        in_specs=[pl.BlockSpec((ts, GLA_DK), lambda b, h, t: (rows(b, h, t), Z_GLA_Q // GLA_DK + h)),
                  pl.BlockSpec((ts, GLA_DK), lambda b, h, t: (rows(b, h, t), Z_GLA_K // GLA_DK + h)),
                  pl.BlockSpec((ts, GLA_DV), lambda b, h, t: (rows(b, h, t), Z_GLA_V // GLA_DV + h)),
                  pl.BlockSpec((ts, GLA_DV), lambda b, h, t: (rows(b, h, t), Z_GLA_G // GLA_DV + h)),
                  pl.BlockSpec((ts, LANES), lambda b, h, t: (rows(b, h, t), 0)),
                  pl.BlockSpec((LANES, GLA_DK), lambda b, h, t: (0, h)),
                  pl.BlockSpec((1, GLA_DK), lambda b, h, t: (0, h)),
                  pl.BlockSpec((1, GLA_DV), lambda b, h, t: (0, 0))],
        out_specs=pl.BlockSpec((ts, GLA_DV), lambda b, h, t: (rows(b, h, t), h)),
        scratch_shapes=[pltpu.VMEM((GLA_DK, GLA_DV), F32)],
        compiler_params=_params("parallel", "parallel", "arbitrary"),
        name="gla",
    )(z, z, z, z, a_low, wup, bal, ng)


def _diff_kernel(q_ref, k_ref, v_ref, lq1_ref, lk1_ref, lq2_ref, lk2_ref, sg_ref, o_ref, *, tq):
    qi = pl.program_id(2)
    q = q_ref[...]
    qs = (q[:, :DIFF_DH], q[:, DIFF_DH:])

    def block(kb, carry, masked):
        start = pl.multiple_of(kb * tq, tq)
        kblk = k_ref[pl.ds(start, tq), :]
        vblk = v_ref[pl.ds(start, tq), :]
        if masked:
            qc = lax.broadcasted_iota(jnp.int32, (tq, tq), 0) // CHUNK
            kc = lax.broadcasted_iota(jnp.int32, (tq, tq), 1) // CHUNK
            visible = kc <= qc
        out = []
        for comp in range(2):
            m_old, l_old, acc_old = carry[comp]
            s = _nt_dot(qs[comp], kblk[:, comp * DIFF_DH:(comp + 1) * DIFF_DH])
            if masked:
                s = jnp.where(visible, s, NEG_INF)
            m_new = jnp.maximum(m_old, jnp.max(s, axis=-1, keepdims=True))
            alpha = jnp.exp(m_old - m_new)
            p = jnp.exp(s - m_new)
            l_new = alpha * l_old + jnp.sum(p, axis=-1, keepdims=True)
            acc_new = alpha * acc_old + jnp.dot(p.astype(BF16), vblk, preferred_element_type=F32)
            out.append((m_new, l_new, acc_new))
        return tuple(out)

    init_one = (jnp.full((tq, 1), NEG_INF, F32), jnp.zeros((tq, 1), F32),
                jnp.zeros((tq, DIFF_DV), F32))
    carry = lax.fori_loop(0, qi, functools.partial(block, masked=False), (init_one, init_one))
    (_, l1, acc1), (_, l2, acc2) = block(qi, carry, masked=True)

    lam = (jnp.exp(jnp.sum(lq1_ref[...] * lk1_ref[...], axis=-1, keepdims=True))
           - jnp.exp(jnp.sum(lq2_ref[...] * lk2_ref[...], axis=-1, keepdims=True)) + LAM_INIT)
    o = acc1 / l1 - lam * (acc2 / l2)
    o_ref[...] = (_rms(o, sg_ref[...]) * (1.0 - LAM_INIT)).astype(o_ref.dtype)


def _diff_attn(z, lq1, lk1, lq2, lk2, sg, batch, seq, tq=512):
    m = z.shape[0]
    nq = seq // tq
    vec = pl.BlockSpec((1, DIFF_DH), lambda b, h, i: (0, 0))
    return pl.pallas_call(
        functools.partial(_diff_kernel, tq=tq),
        out_shape=jax.ShapeDtypeStruct((m, DIFF_HEADS * DIFF_DV), BF16),
        grid=(batch, DIFF_HEADS, nq),
        in_specs=[pl.BlockSpec((tq, DIFF_DV), lambda b, h, i: (b * nq + i, Z_DIFF_Q // DIFF_DV + h)),
                  pl.BlockSpec((seq, DIFF_DV), lambda b, h, i: (b, Z_DIFF_K // DIFF_DV + h)),
                  pl.BlockSpec((seq, DIFF_DV), lambda b, h, i: (b, Z_DIFF_V // DIFF_DV + h)),
                  vec, vec, vec, vec,
                  pl.BlockSpec((1, DIFF_DV), lambda b, h, i: (0, 0))],
        out_specs=pl.BlockSpec((tq, DIFF_DV), lambda b, h, i: (b * nq + i, h)),
        compiler_params=_params("parallel", "parallel", "arbitrary"),
        name="diff_attn",
    )(z, z, z, lq1, lk1, lq2, lk2, sg)


def _mem_kv_kernel(mn_ref, w_ref, kg_ref, kv_ref, *, n_key_tiles):
    j = pl.program_id(0)
    acc = jnp.dot(mn_ref[...], w_ref[...], preferred_element_type=F32)

    @pl.when(j < n_key_tiles)
    def _():
        _store_group_norm(acc, kg_ref[...], MEM_DH, 1.0, kv_ref)

    @pl.when(j >= n_key_tiles)
    def _():
        kv_ref[...] = acc.astype(kv_ref.dtype)


def _mem_kv(mem_n, w_kv, kg, tn=512):
    m, d = mem_n.shape
    n = w_kv.shape[1]
    return pl.pallas_call(
        functools.partial(_mem_kv_kernel, n_key_tiles=(n // 2) // tn),
        out_shape=jax.ShapeDtypeStruct((m, n), BF16),
        grid=(n // tn,),
        in_specs=[pl.BlockSpec((m, d), lambda j: (0, 0)),
                  pl.BlockSpec((d, tn), lambda j: (0, j)),
                  pl.BlockSpec((1, MEM_DH), lambda j: (0, 0))],
        out_specs=pl.BlockSpec((m, tn), lambda j: (0, j)),
        compiler_params=_params("parallel"),
        name="mem_kv",
    )(mem_n, w_kv, kg)


def _mem_attn_kernel(q_ref, k_ref, v_ref, o_ref):
    s = _nt_dot(q_ref[...], k_ref[...])
    e = jnp.exp(s - jnp.max(s, axis=-1, keepdims=True))
    p = (e / jnp.sum(e, axis=-1, keepdims=True)).astype(BF16)
    o_ref[...] = jnp.dot(p, v_ref[...], preferred_element_type=F32).astype(o_ref.dtype)


def _mem_attn(z, kv, batch, seq, n_mem, tm=1024):
    m = z.shape[0]
    nt = seq // tm
    return pl.pallas_call(
        _mem_attn_kernel,
        out_shape=jax.ShapeDtypeStruct((m, MEM_HEADS * MEM_DH), BF16),
        grid=(batch, nt, MEM_HEADS),
        in_specs=[pl.BlockSpec((tm, MEM_DH), lambda b, t, h: (b * nt + t, Z_MEM_Q // MEM_DH + h)),
                  pl.BlockSpec((n_mem, MEM_DH), lambda b, t, h: (b, h)),
                  pl.BlockSpec((n_mem, MEM_DH), lambda b, t, h: (b, MEM_HEADS + h))],
        out_specs=pl.BlockSpec((tm, MEM_DH), lambda b, t, h: (b * nt + t, h)),
        compiler_params=_params("parallel", "parallel", "parallel"),
        name="mem_attn",
    )(z, kv, kv)


def _gate_merge_kernel(h_ref, y0_ref, y1_ref, y2_ref, wg0_ref, wg1_ref, wg2_ref,
                       bg0_ref, bg1_ref, bg2_ref, wb0_ref, wb1_ref, wb2_ref, o_ref):
    h = h_ref[...]
    merged = None
    for y_ref, wg_ref, bg_ref, wb_ref in ((y0_ref, wg0_ref, bg0_ref, wb0_ref),
                                          (y1_ref, wg1_ref, bg1_ref, wb1_ref),
                                          (y2_ref, wg2_ref, bg2_ref, wb2_ref)):
        gate = _sigmoid(jnp.dot(h, wg_ref[...], preferred_element_type=F32) + bg_ref[...])
        term = gate * jnp.dot(y_ref[...], wb_ref[...], preferred_element_type=F32)
        merged = term if merged is None else merged + term
    o_ref[...] = merged.astype(o_ref.dtype)


def _gate_merge(h, ys, w_gate, b_gate, w_branch, tm=1024, tn=256):
    m, d = h.shape
    bw = w_branch.shape[1]
    nj = d // tn
    act = lambda width: pl.BlockSpec((tm, width), lambda i, j: (i, 0))
    wg = lambda b: pl.BlockSpec((d, tn), lambda i, j: (0, b * nj + j))
    bg = lambda b: pl.BlockSpec((1, tn), lambda i, j: (0, b * nj + j))
    wb = lambda b: pl.BlockSpec((None, bw, tn), lambda i, j: (b, 0, j))
    return pl.pallas_call(
        _gate_merge_kernel,
        out_shape=jax.ShapeDtypeStruct((m, d), BF16),
        grid=(m // tm, nj),
        in_specs=[act(d), act(bw), act(bw), act(bw), wg(0), wg(1), wg(2), bg(0), bg(1), bg(2),
                  wb(0), wb(1), wb(2)],
        out_specs=pl.BlockSpec((tm, tn), lambda i, j: (i, j)),
        compiler_params=_params("parallel", "arbitrary"),
        name="gate_merge",
    )(h, ys[0], ys[1], ys[2], w_gate, w_gate, w_gate, b_gate, b_gate, b_gate,
      w_branch, w_branch, w_branch)


def _out_proj_kernel(mg_ref, w_ref, x_ref, g_ref, x1_ref, hf_ref):
    x1 = x_ref[...] + jnp.dot(mg_ref[...], w_ref[...], preferred_element_type=F32)
    x1_ref[...] = x1
    hf_ref[...] = _rms(x1, g_ref[...]).astype(hf_ref.dtype)


def _out_proj(merged, w_out, x2, g, tm=512):
    m, d = x2.shape
    row = lambda i: (i, 0)
    fixed = lambda i: (0, 0)
    return pl.pallas_call(
        _out_proj_kernel,
        out_shape=(jax.ShapeDtypeStruct((m, d), F32), jax.ShapeDtypeStruct((m, d), BF16)),
        grid=(m // tm,),
        in_specs=[pl.BlockSpec((tm, d), row), pl.BlockSpec((d, d), fixed),
                  pl.BlockSpec((tm, d), row), pl.BlockSpec((1, d), fixed)],
        out_specs=(pl.BlockSpec((tm, d), row), pl.BlockSpec((tm, d), row)),
        compiler_params=_params("parallel"),
        name="out_proj",
    )(merged, w_out, x2, g)


def _ffn_up_kernel(hf_ref, wg_ref, wu_ref, a_ref):
    hf = hf_ref[...]
    gate = jnp.dot(hf, wg_ref[...], preferred_element_type=F32)
    up = jnp.dot(hf, wu_ref[...], preferred_element_type=F32)
    a_ref[...] = (gate * _sigmoid(gate) * up).astype(a_ref.dtype)


def _ffn_up(hf, w_in, tm=1024, tf=512):
    m, d = hf.shape
    d_ff = w_in.shape[1] // 2
    nj = d_ff // tf
    return pl.pallas_call(
        _ffn_up_kernel,
        out_shape=jax.ShapeDtypeStruct((m, d_ff), BF16),
        grid=(m // tm, nj),
        in_specs=[pl.BlockSpec((tm, d), lambda i, j: (i, 0)),
                  pl.BlockSpec((d, tf), lambda i, j: (0, j)),
                  pl.BlockSpec((d, tf), lambda i, j: (0, nj + j))],
        out_specs=pl.BlockSpec((tm, tf), lambda i, j: (i, j)),
        compiler_params=_params("parallel", "arbitrary"),
        name="ffn_up",
    )(hf, w_in, w_in)


def _ffn_down_kernel(a_ref, w_ref, x1_ref, o_ref):
    o_ref[...] = x1_ref[...] + jnp.dot(a_ref[...], w_ref[...], preferred_element_type=F32)


def _ffn_down(a, w_down, x1, tm=1024, tn=512):
    m, d_ff = a.shape
    d = w_down.shape[1]
    return pl.pallas_call(
        _ffn_down_kernel,
        out_shape=jax.ShapeDtypeStruct((m, d), F32),
        grid=(m // tm, d // tn),
        in_specs=[pl.BlockSpec((tm, d_ff), lambda i, j: (i, 0)),
                  pl.BlockSpec((d_ff, tn), lambda i, j: (0, j)),
                  pl.BlockSpec((tm, tn), lambda i, j: (i, j))],
        out_specs=pl.BlockSpec((tm, tn), lambda i, j: (i, j)),
        compiler_params=_params("parallel", "arbitrary"),
        name="ffn_down",
    )(a, w_down, x1)


def kernel(x, mem, norm_mix_g, norm_mem_g, w_in, gla_w_alpha_up, gla_b_alpha, gla_norm_g,
           diff_q_norm_g, diff_k_norm_g, diff_lambda_q1, diff_lambda_k1, diff_lambda_q2,
           diff_lambda_k2, diff_subln_g, mem_q_norm_g, mem_k_norm_g, w_mem_kv, w_branch,
           w_gate, b_gate, w_out, norm_ffn_g, w_ffn_in, w_ffn_down):
    batch, seq, d = x.shape
    n_mem = mem.shape[1]
    depth = w_in.shape[0]
    assert depth == 1, "LAM_INIT is the layer-0 value"
    x2 = x.reshape(batch * seq, d)
    mem2 = mem.reshape(batch * n_mem, d)
    for l in range(depth):
        a0 = Z_GLA_G + GLA_HEADS * GLA_DV
        w_main = jnp.concatenate([w_in[l][:, :a0], w_in[l][:, a0 + GLA_RANK:]], axis=1).astype(BF16)
        w_alow = jnp.pad(w_in[l][:, a0:a0 + GLA_RANK], ((0, 0), (0, LANES - GLA_RANK))).astype(BF16)
        w_up = jnp.pad(gla_w_alpha_up[l], ((0, LANES - GLA_RANK), (0, 0)))
        row = lambda v: v.reshape(1, -1)

        h, a_low = _norm_mix(x2, row(norm_mix_g[l]), w_alow)
        z = _in_proj(h, w_main, row(diff_q_norm_g[l]), row(diff_k_norm_g[l]), row(mem_q_norm_g[l]))
        y_gla = _gla(z, a_low, w_up, row(gla_b_alpha[l]), row(gla_norm_g[l]), batch, seq)
        y_diff = _diff_attn(z, row(diff_lambda_q1[l]), row(diff_lambda_k1[l]),
                            row(diff_lambda_q2[l]), row(diff_lambda_k2[l]),
                            row(diff_subln_g[l]), batch, seq)
        mem_n = _norm_rows(mem2, row(norm_mem_g[l]))
        kv = _mem_kv(mem_n, w_mem_kv[l].astype(BF16), row(mem_k_norm_g[l]))
        y_mem = _mem_attn(z, kv, batch, seq, n_mem)
        merged = _gate_merge(h, (y_gla, y_diff, y_mem), w_gate[l].astype(BF16), row(b_gate[l]),
                             w_branch[l].astype(BF16))
        x1, hf = _out_proj(merged, w_out[l].astype(BF16), x2, row(norm_ffn_g[l]))
        a = _ffn_up(hf, w_ffn_in[l].astype(BF16))
        x2 = _ffn_down(a, w_ffn_down[l].astype(BF16), x1)
    return x2.reshape(batch, seq, d)
```

```python
import functools

import jax
import jax.numpy as jnp
from jax import lax
from jax.experimental import pallas as pl
from jax.experimental.pallas import tpu as pltpu

F32 = jnp.float32
BF16 = jnp.bfloat16

CHUNK = 64
GLA_HEADS = 4
GLA_DK = 128
GLA_DV = 256
GLA_RANK = 16
GLA_GATE_NORM = 16.0
DIFF_HEADS = 4
DIFF_DH = 128
DIFF_DV = 256
MEM_HEADS = 4
MEM_DH = 256
N_BRANCH = 3
NORM_EPS = 1e-6
NEG_INF = -1e30
LAM_INIT = 0.8 - 0.6 * 1.0
LOG2_E = 1.4426950408889634

LANES = 128
VMEM_LIMIT = 56 * 1024 * 1024

Z_GLA_Q, Z_GLA_K, Z_GLA_V, Z_GLA_G = 0, 512, 1024, 2048
Z_DIFF_Q, Z_DIFF_K, Z_DIFF_V, Z_MEM_Q = 3072, 4096, 5120, 6144
Z_WIDTH = 7168
IN_TILE = 1024


def _params(*sem):
    return pltpu.CompilerParams(dimension_semantics=sem, vmem_limit_bytes=VMEM_LIMIT)


def _nt_dot(a, b):
    return lax.dot_general(a, b, (((1,), (1,)), ((), ())), preferred_element_type=F32)


def _tn_dot(a, b, precision=None):
    return lax.dot_general(a, b, (((0,), (0,)), ((), ())), preferred_element_type=F32,
                           precision=precision)


def _rms(v, gain):
    ms = jnp.mean(v * v, axis=-1, keepdims=True)
    return v * lax.rsqrt(ms + NORM_EPS) * gain


def _sigmoid(v):
    return 1.0 / (1.0 + jnp.exp(-v))


def _norm_mix_kernel(x_ref, g_ref, wa_ref, h_ref, a_ref):
    h = _rms(x_ref[...], g_ref[...]).astype(BF16)
    h_ref[...] = h
    a_ref[...] = jnp.dot(h, wa_ref[...], preferred_element_type=F32)


def _norm_mix(x2, g, wa, tr=512):
    m, d = x2.shape
    return pl.pallas_call(
        _norm_mix_kernel,
        out_shape=(jax.ShapeDtypeStruct((m, d), BF16), jax.ShapeDtypeStruct((m, LANES), F32)),
        grid=(m // tr,),
        in_specs=[pl.BlockSpec((tr, d), lambda i: (i, 0)),
                  pl.BlockSpec((1, d), lambda i: (0, 0)),
                  pl.BlockSpec((d, LANES), lambda i: (0, 0))],
        out_specs=(pl.BlockSpec((tr, d), lambda i: (i, 0)),
                   pl.BlockSpec((tr, LANES), lambda i: (i, 0))),
        compiler_params=_params("parallel"),
        name="norm_mix",
    )(x2, g, wa)


def _norm_rows_kernel(x_ref, g_ref, h_ref):
    h_ref[...] = _rms(x_ref[...], g_ref[...]).astype(BF16)


def _norm_rows(x2, g, tr=512):
    m, d = x2.shape
    return pl.pallas_call(
        _norm_rows_kernel,
        out_shape=jax.ShapeDtypeStruct((m, d), BF16),
        grid=(m // tr,),
        in_specs=[pl.BlockSpec((tr, d), lambda i: (i, 0)),
                  pl.BlockSpec((1, d), lambda i: (0, 0))],
        out_specs=pl.BlockSpec((tr, d), lambda i: (i, 0)),
        compiler_params=_params("parallel"),
        name="norm_rows",
    )(x2, g)


def _store_group_norm(acc, gain, width, scale, out_ref):
    for s in range(0, acc.shape[1], width):
        blk = acc[:, s:s + width]
        out_ref[:, s:s + width] = (_rms(blk, gain) * scale).astype(out_ref.dtype)


def _in_proj_kernel(h_ref, w_ref, dq_g_ref, dk_g_ref, mq_g_ref, z_ref):
    j = pl.program_id(1)
    acc = jnp.dot(h_ref[...], w_ref[...], preferred_element_type=F32)
    j_dq, j_dk, j_mq = Z_DIFF_Q // IN_TILE, Z_DIFF_K // IN_TILE, Z_MEM_Q // IN_TILE

    @pl.when((j != j_dq) & (j != j_dk) & (j != j_mq))
    def _():
        z_ref[...] = acc.astype(z_ref.dtype)

    @pl.when(j == j_dq)
    def _():
        _store_group_norm(acc, dq_g_ref[...], DIFF_DH, DIFF_DH ** -0.5 * LOG2_E, z_ref)

    @pl.when(j == j_dk)
    def _():
        _store_group_norm(acc, dk_g_ref[...], DIFF_DH, 1.0, z_ref)

    @pl.when(j == j_mq)
    def _():
        _store_group_norm(acc, mq_g_ref[...], MEM_DH, MEM_DH ** -0.5, z_ref)


def _in_proj(h, w, dq_g, dk_g, mq_g, tm=1024):
    m, d = h.shape
    n = w.shape[1]
    return pl.pallas_call(
        _in_proj_kernel,
        out_shape=jax.ShapeDtypeStruct((m, n), BF16),
        grid=(m // tm, n // IN_TILE),
        in_specs=[pl.BlockSpec((tm, d), lambda i, j: (i, 0)),
                  pl.BlockSpec((d, IN_TILE), lambda i, j: (0, j)),
                  pl.BlockSpec((1, DIFF_DH), lambda i, j: (0, 0)),
                  pl.BlockSpec((1, DIFF_DH), lambda i, j: (0, 0)),
                  pl.BlockSpec((1, MEM_DH), lambda i, j: (0, 0))],
        out_specs=pl.BlockSpec((tm, IN_TILE), lambda i, j: (i, j)),
        compiler_params=_params("parallel", "arbitrary"),
        name="in_proj",
    )(h, w, dq_g, dk_g, mq_g)


def _gla_kernel(q_ref, k_ref, v_ref, g_ref, a_ref, wup_ref, bal_ref, ng_ref, o_ref, s_ref, *, ts):
    @pl.when(pl.program_id(2) == 0)
    def _():
        s_ref[...] = jnp.zeros_like(s_ref)

    hi = lax.Precision.HIGHEST
    pre = jnp.dot(a_ref[...], wup_ref[...], preferred_element_type=F32, precision=hi) + bal_ref[...]
    log_a = (jnp.minimum(pre, 0.0) - jnp.log1p(jnp.exp(-jnp.abs(pre)))) * (1.0 / GLA_GATE_NORM)

    row = lax.broadcasted_iota(jnp.int32, (CHUNK, CHUNK), 0)
    col = lax.broadcasted_iota(jnp.int32, (CHUNK, CHUNK), 1)
    causal = row >= col
    tril = causal.astype(F32)
    ones = jnp.ones((CHUNK, GLA_DK), F32)

    for c in range(ts // CHUNK):
        rows = slice(c * CHUNK, (c + 1) * CHUNK)
        la = log_a[rows]
        bcum = jnp.dot(tril, la, preferred_element_type=F32, precision=hi)
        b_last = bcum[CHUNK - 1:CHUNK, :]
        q = q_ref[rows, :].astype(F32) * (GLA_DK ** -0.5)
        k = k_ref[rows, :].astype(F32)
        v = v_ref[rows, :]
        q_dec = (q * jnp.exp(bcum)).astype(BF16)
        k_dec = (k * jnp.exp(-bcum)).astype(BF16)
        k_tail = (k * jnp.exp(b_last - bcum)).astype(BF16)
        att = jnp.where(causal, _nt_dot(q_dec, k_dec), 0.0).astype(BF16)
        state = s_ref[...]
        o = (jnp.dot(att, v, preferred_element_type=F32)
             + jnp.dot(q_dec, state.astype(BF16), preferred_element_type=F32))
        inc = _tn_dot(k_tail, v)
        decay_col = jnp.exp(_tn_dot(la, ones, precision=hi))
        decay = jnp.concatenate([decay_col] * (GLA_DV // GLA_DK), axis=1)
        s_ref[...] = decay * state + inc
        gate = g_ref[rows, :].astype(F32)
        o_ref[rows, :] = (_rms(o, ng_ref[...]) * (gate * _sigmoid(gate))).astype(o_ref.dtype)


def _gla(z, a_low, wup, bal, ng, batch, seq, ts=512):
    m = z.shape[0]
    nt = seq // ts
    rows = lambda b, h, t: b * nt + t
    return pl.pallas_call(
        functools.partial(_gla_kernel, ts=ts),
        out_shape=jax.ShapeDtypeStruct((m, GLA_HEADS * GLA_DV), BF16),
        grid=(batch, GLA_HEADS, nt),
        in_specs=[pl.BlockSpec((ts, GLA_DK), lambda b, h, t: (rows(b, h, t), Z_GLA_Q // GLA_DK + h)),
                  pl.BlockSpec((ts, GLA_DK), lambda b, h, t: (rows(b, h, t), Z_GLA_K // GLA_DK + h)),
                  pl.BlockSpec((ts, GLA_DV), lambda b, h, t: (rows(b, h, t), Z_GLA_V // GLA_DV + h)),
                  pl.BlockSpec((ts, GLA_DV), lambda b, h, t: (rows(b, h, t), Z_GLA_G // GLA_DV + h)),
                  pl.BlockSpec((ts, LANES), lambda b, h, t: (rows(b, h, t), 0)),
                  pl.BlockSpec((LANES, GLA_DK), lambda b, h, t: (0, h)),
                  pl.BlockSpec((1, GLA_DK), lambda b, h, t: (0, h)),
                  pl.BlockSpec((1, GLA_DV), lambda b, h, t: (0, 0))],
        out_specs=pl.BlockSpec((ts, GLA_DV), lambda b, h, t: (rows(b, h, t), h)),
        scratch_shapes=[pltpu.VMEM((GLA_DK, GLA_DV), F32)],
        compiler_params=_params("parallel", "parallel", "arbitrary"),
        name="gla",
    )(z, z, z, z, a_low, wup, bal, ng)


def _diff_kernel(q_ref, k_ref, v_ref, lq1_ref, lk1_ref, lq2_ref, lk2_ref, sg_ref, o_ref,
                 s_scr, p_scr, acc_scr, m_scr, l_scr, a_scr, *, tq):
    qi = pl.program_id(2)
    n_lane_tiles = tq // LANES
    m_scr[...] = jnp.full_like(m_scr, NEG_INF)
    l_scr[...] = jnp.zeros_like(l_scr)
    acc_scr[...] = jnp.zeros_like(acc_scr)
    first_half = lax.broadcasted_iota(jnp.int32, (CHUNK, LANES), 1) < CHUNK

    def softmax_rows(comp, r, diag):
        rows = slice(r * CHUNK, (r + 1) * CHUNK)
        n_vis = r // 2 + 1 if diag else n_lane_tiles
        parts = []
        for t in range(n_vis):
            part = s_scr[comp, rows, t * LANES:(t + 1) * LANES]
            if diag and 2 * t == r:
                part = jnp.where(first_half, part, NEG_INF)
            parts.append(part)
        tile_max = functools.reduce(jnp.maximum, parts)
        m_old = m_scr[comp, rows, :]
        m_new = jnp.maximum(m_old, jnp.max(tile_max, axis=-1, keepdims=True))
        alpha = jnp.exp2(m_old - m_new)
        ps = [jnp.exp2(part - m_new) for part in parts]
        l_cur = jnp.sum(functools.reduce(jnp.add, ps), axis=-1, keepdims=True)
        l_scr[comp, rows, :] = alpha * l_scr[comp, rows, :] + l_cur
        m_scr[comp, rows, :] = m_new
        a_scr[comp, rows, :] = alpha
        for t in range(n_lane_tiles):
            tile = ps[t].astype(BF16) if t < n_vis else jnp.zeros((CHUNK, LANES), BF16)
            p_scr[comp, rows, t * LANES:(t + 1) * LANES] = tile

    def block(kb, diag):
        start = pl.multiple_of(kb * tq, tq)
        for comp in range(2):
            cols = slice(comp * DIFF_DH, (comp + 1) * DIFF_DH)
            s_scr[comp] = _nt_dot(q_ref[:, cols], k_ref[pl.ds(start, tq), cols])
        v = v_ref[pl.ds(start, tq), :]
        for comp in range(2):
            for r in range(tq // CHUNK):
                softmax_rows(comp, r, diag)
            alpha = a_scr[comp]
            acc_scr[comp] = (jnp.concatenate([alpha] * (DIFF_DV // LANES), axis=1) * acc_scr[comp]
                             + jnp.dot(p_scr[comp], v, preferred_element_type=F32))

    def full_block(kb, carry):
        block(kb, diag=False)
        return carry

    lax.fori_loop(0, qi, full_block, 0)
    block(qi, diag=True)

    lam = (jnp.exp(jnp.sum(lq1_ref[...] * lk1_ref[...], axis=-1, keepdims=True))
           - jnp.exp(jnp.sum(lq2_ref[...] * lk2_ref[...], axis=-1, keepdims=True)) + LAM_INIT)
    widen = lambda v: jnp.concatenate([v] * (DIFF_DV // LANES), axis=1)
    o = acc_scr[0] / widen(l_scr[0]) - lam * (acc_scr[1] / widen(l_scr[1]))
    o_ref[...] = (_rms(o, sg_ref[...]) * (1.0 - LAM_INIT)).astype(o_ref.dtype)


def _diff_attn(z, lq1, lk1, lq2, lk2, sg, batch, seq, tq=512):
    m = z.shape[0]
    nq = seq // tq
    vec = pl.BlockSpec((1, DIFF_DH), lambda b, h, i: (0, 0))
    return pl.pallas_call(
        functools.partial(_diff_kernel, tq=tq),
        out_shape=jax.ShapeDtypeStruct((m, DIFF_HEADS * DIFF_DV), BF16),
        grid=(batch, DIFF_HEADS, nq),
        in_specs=[pl.BlockSpec((tq, DIFF_DV), lambda b, h, i: (b * nq + i, Z_DIFF_Q // DIFF_DV + h)),
                  pl.BlockSpec((seq, DIFF_DV), lambda b, h, i: (b, Z_DIFF_K // DIFF_DV + h)),
                  pl.BlockSpec((seq, DIFF_DV), lambda b, h, i: (b, Z_DIFF_V // DIFF_DV + h)),
                  vec, vec, vec, vec,
                  pl.BlockSpec((1, DIFF_DV), lambda b, h, i: (0, 0))],
        out_specs=pl.BlockSpec((tq, DIFF_DV), lambda b, h, i: (b * nq + i, h)),
        scratch_shapes=[pltpu.VMEM((2, tq, tq), F32),
                        pltpu.VMEM((2, tq, tq), BF16),
                        pltpu.VMEM((2, tq, DIFF_DV), F32),
                        pltpu.VMEM((2, tq, LANES), F32),
                        pltpu.VMEM((2, tq, LANES), F32),
                        pltpu.VMEM((2, tq, LANES), F32)],
        compiler_params=_params("parallel", "parallel", "arbitrary"),
        name="diff_attn",
    )(z, z, z, lq1, lk1, lq2, lk2, sg)


def _mem_kv_kernel(mn_ref, w_ref, kg_ref, kv_ref, *, n_key_tiles):
    j = pl.program_id(0)
    acc = jnp.dot(mn_ref[...], w_ref[...], preferred_element_type=F32)

    @pl.when(j < n_key_tiles)
    def _():
        _store_group_norm(acc, kg_ref[...], MEM_DH, 1.0, kv_ref)

    @pl.when(j >= n_key_tiles)
    def _():
        kv_ref[...] = acc.astype(kv_ref.dtype)


def _mem_kv(mem_n, w_kv, kg, tn=512):
    m, d = mem_n.shape
    n = w_kv.shape[1]
    return pl.pallas_call(
        functools.partial(_mem_kv_kernel, n_key_tiles=(n // 2) // tn),
        out_shape=jax.ShapeDtypeStruct((m, n), BF16),
        grid=(n // tn,),
        in_specs=[pl.BlockSpec((m, d), lambda j: (0, 0)),
                  pl.BlockSpec((d, tn), lambda j: (0, j)),
                  pl.BlockSpec((1, MEM_DH), lambda j: (0, 0))],
        out_specs=pl.BlockSpec((m, tn), lambda j: (0, j)),
        compiler_params=_params("parallel"),
        name="mem_kv",
    )(mem_n, w_kv, kg)


def _mem_attn_kernel(q_ref, k_ref, v_ref, o_ref):
    s = _nt_dot(q_ref[...], k_ref[...])
    e = jnp.exp(s - jnp.max(s, axis=-1, keepdims=True))
    p = (e / jnp.sum(e, axis=-1, keepdims=True)).astype(BF16)
    o_ref[...] = jnp.dot(p, v_ref[...], preferred_element_type=F32).astype(o_ref.dtype)


def _mem_attn(z, kv, batch, seq, n_mem, tm=1024):
    m = z.shape[0]
    nt = seq // tm
    return pl.pallas_call(
        _mem_attn_kernel,
        out_shape=jax.ShapeDtypeStruct((m, MEM_HEADS * MEM_DH), BF16),
        grid=(batch, nt, MEM_HEADS),
        in_specs=[pl.BlockSpec((tm, MEM_DH), lambda b, t, h: (b * nt + t, Z_MEM_Q // MEM_DH + h)),
                  pl.BlockSpec((n_mem, MEM_DH), lambda b, t, h: (b, h)),
                  pl.BlockSpec((n_mem, MEM_DH), lambda b, t, h: (b, MEM_HEADS + h))],
        out_specs=pl.BlockSpec((tm, MEM_DH), lambda b, t, h: (b * nt + t, h)),
        compiler_params=_params("parallel", "parallel", "parallel"),
        name="mem_attn",
    )(z, kv, kv)


def _gate_merge_kernel(h_ref, y0_ref, y1_ref, y2_ref, wg0_ref, wg1_ref, wg2_ref,
                       bg0_ref, bg1_ref, bg2_ref, wb0_ref, wb1_ref, wb2_ref, o_ref):
    h = h_ref[...]
    merged = None
    for y_ref, wg_ref, bg_ref, wb_ref in ((y0_ref, wg0_ref, bg0_ref, wb0_ref),
                                          (y1_ref, wg1_ref, bg1_ref, wb1_ref),
                                          (y2_ref, wg2_ref, bg2_ref, wb2_ref)):
        gate = _sigmoid(jnp.dot(h, wg_ref[...], preferred_element_type=F32) + bg_ref[...])
        term = gate * jnp.dot(y_ref[...], wb_ref[...], preferred_element_type=F32)
        merged = term if merged is None else merged + term
    o_ref[...] = merged.astype(o_ref.dtype)


def _gate_merge(h, ys, w_gate, b_gate, w_branch, tm=1024, tn=256):
    m, d = h.shape
    bw = w_branch.shape[1]
    nj = d // tn
    act = lambda width: pl.BlockSpec((tm, width), lambda i, j: (i, 0))
    wg = lambda b: pl.BlockSpec((d, tn), lambda i, j: (0, b * nj + j))
    bg = lambda b: pl.BlockSpec((1, tn), lambda i, j: (0, b * nj + j))
    wb = lambda b: pl.BlockSpec((None, bw, tn), lambda i, j: (b, 0, j))
    return pl.pallas_call(
        _gate_merge_kernel,
        out_shape=jax.ShapeDtypeStruct((m, d), BF16),
        grid=(m // tm, nj),
        in_specs=[act(d), act(bw), act(bw), act(bw), wg(0), wg(1), wg(2), bg(0), bg(1), bg(2),
                  wb(0), wb(1), wb(2)],
        out_specs=pl.BlockSpec((tm, tn), lambda i, j: (i, j)),
        compiler_params=_params("parallel", "arbitrary"),
        name="gate_merge",
    )(h, ys[0], ys[1], ys[2], w_gate, w_gate, w_gate, b_gate, b_gate, b_gate,
      w_branch, w_branch, w_branch)


def _out_proj_kernel(mg_ref, w_ref, x_ref, g_ref, x1_ref, hf_ref):
    x1 = x_ref[...] + jnp.dot(mg_ref[...], w_ref[...], preferred_element_type=F32)
    x1_ref[...] = x1
    hf_ref[...] = _rms(x1, g_ref[...]).astype(hf_ref.dtype)


def _out_proj(merged, w_out, x2, g, tm=512):
    m, d = x2.shape
    row = lambda i: (i, 0)
    fixed = lambda i: (0, 0)
    return pl.pallas_call(
        _out_proj_kernel,
        out_shape=(jax.ShapeDtypeStruct((m, d), F32), jax.ShapeDtypeStruct((m, d), BF16)),
        grid=(m // tm,),
        in_specs=[pl.BlockSpec((tm, d), row), pl.BlockSpec((d, d), fixed),
                  pl.BlockSpec((tm, d), row), pl.BlockSpec((1, d), fixed)],
        out_specs=(pl.BlockSpec((tm, d), row), pl.BlockSpec((tm, d), row)),
        compiler_params=_params("parallel"),
        name="out_proj",
    )(merged, w_out, x2, g)


def _ffn_up_kernel(hf_ref, wg_ref, wu_ref, a_ref):
    hf = hf_ref[...]
    gate = jnp.dot(hf, wg_ref[...], preferred_element_type=F32)
    up = jnp.dot(hf, wu_ref[...], preferred_element_type=F32)
    a_ref[...] = (gate * _sigmoid(gate) * up).astype(a_ref.dtype)


def _ffn_up(hf, w_in, tm=1024, tf=512):
    m, d = hf.shape
    d_ff = w_in.shape[1] // 2
    nj = d_ff // tf
    return pl.pallas_call(
        _ffn_up_kernel,
        out_shape=jax.ShapeDtypeStruct((m, d_ff), BF16),
        grid=(m // tm, nj),
        in_specs=[pl.BlockSpec((tm, d), lambda i, j: (i, 0)),
                  pl.BlockSpec((d, tf), lambda i, j: (0, j)),
                  pl.BlockSpec((d, tf), lambda i, j: (0, nj + j))],
        out_specs=pl.BlockSpec((tm, tf), lambda i, j: (i, j)),
        compiler_params=_params("parallel", "arbitrary"),
        name="ffn_up",
    )(hf, w_in, w_in)


def _ffn_down_kernel(a_ref, w_ref, x1_ref, o_ref):
    o_ref[...] = x1_ref[...] + jnp.dot(a_ref[...], w_ref[...], preferred_element_type=F32)


def _ffn_down(a, w_down, x1, tm=1024, tn=512):
    m, d_ff = a.shape
    d = w_down.shape[1]
    return pl.pallas_call(
        _ffn_down_kernel,
        out_shape=jax.ShapeDtypeStruct((m, d), F32),
        grid=(m // tm, d // tn),
        in_specs=[pl.BlockSpec((tm, d_ff), lambda i, j: (i, 0)),
                  pl.BlockSpec((d_ff, tn), lambda i, j: (0, j)),
                  pl.BlockSpec((tm, tn), lambda i, j: (i, j))],
        out_specs=pl.BlockSpec((tm, tn), lambda i, j: (i, j)),
        compiler_params=_params("parallel", "arbitrary"),
        name="ffn_down",
    )(a, w_down, x1)


def kernel(x, mem, norm_mix_g, norm_mem_g, w_in, gla_w_alpha_up, gla_b_alpha, gla_norm_g,
           diff_q_norm_g, diff_k_norm_g, diff_lambda_q1, diff_lambda_k1, diff_lambda_q2,
           diff_lambda_k2, diff_subln_g, mem_q_norm_g, mem_k_norm_g, w_mem_kv, w_branch,
           w_gate, b_gate, w_out, norm_ffn_g, w_ffn_in, w_ffn_down):
    batch, seq, d = x.shape
    n_mem = mem.shape[1]
    depth = w_in.shape[0]
    assert depth == 1, "LAM_INIT is the layer-0 value"
    x2 = x.reshape(batch * seq, d)
    mem2 = mem.reshape(batch * n_mem, d)
    for l in range(depth):
        a0 = Z_GLA_G + GLA_HEADS * GLA_DV
        w_main = jnp.concatenate([w_in[l][:, :a0], w_in[l][:, a0 + GLA_RANK:]], axis=1).astype(BF16)
        w_alow = jnp.pad(w_in[l][:, a0:a0 + GLA_RANK], ((0, 0), (0, LANES - GLA_RANK))).astype(BF16)
        w_up = jnp.pad(gla_w_alpha_up[l], ((0, LANES - GLA_RANK), (0, 0)))
        row = lambda v: v.reshape(1, -1)

        h, a_low = _norm_mix(x2, row(norm_mix_g[l]), w_alow)
        z = _in_proj(h, w_main, row(diff_q_norm_g[l]), row(diff_k_norm_g[l]), row(mem_q_norm_g[l]))
        y_gla = _gla(z, a_low, w_up, row(gla_b_alpha[l]), row(gla_norm_g[l]), batch, seq)
        y_diff = _diff_attn(z, row(diff_lambda_q1[l]), row(diff_lambda_k1[l]),
                            row(diff_lambda_q2[l]), row(diff_lambda_k2[l]),
                            row(diff_subln_g[l]), batch, seq)
        mem_n = _norm_rows(mem2, row(norm_mem_g[l]))
        kv = _mem_kv(mem_n, w_mem_kv[l].astype(BF16), row(mem_k_norm_g[l]))
        y_mem = _mem_attn(z, kv, batch, seq, n_mem)
        merged = _gate_merge(h, (y_gla, y_diff, y_mem), w_gate[l].astype(BF16), row(b_gate[l]),
                             w_branch[l].astype(BF16))
        x1, hf = _out_proj(merged, w_out[l].astype(BF16), x2, row(norm_ffn_g[l]))
        a = _ffn_up(hf, w_ffn_in[l].astype(BF16))
        x2 = _ffn_down(a, w_ffn_down[l].astype(BF16), x1)
    return x2.reshape(batch, seq, d)
```

```python
import functools

import jax
import jax.numpy as jnp
from jax import lax
from jax.experimental import pallas as pl
from jax.experimental.pallas import tpu as pltpu

F32 = jnp.float32
BF16 = jnp.bfloat16

CHUNK = 64
GLA_HEADS = 4
GLA_DK = 128
GLA_DV = 256
GLA_RANK = 16
GLA_GATE_NORM = 16.0
DIFF_HEADS = 4
DIFF_DH = 128
DIFF_DV = 256
MEM_HEADS = 4
MEM_DH = 256
N_BRANCH = 3
NORM_EPS = 1e-6
NEG_INF = -1e30
LAM_INIT = 0.8 - 0.6 * 1.0
LOG2_E = 1.4426950408889634

LANES = 128
VMEM_LIMIT = 56 * 1024 * 1024

Z_GLA_Q, Z_GLA_K, Z_GLA_V, Z_GLA_G = 0, 512, 1024, 2048
Z_DIFF_Q, Z_DIFF_K, Z_DIFF_V, Z_MEM_Q = 3072, 4096, 5120, 6144
Z_WIDTH = 7168
IN_TILE = 1024


def _params(*sem):
    return pltpu.CompilerParams(dimension_semantics=sem, vmem_limit_bytes=VMEM_LIMIT)


def _nt_dot(a, b):
    return lax.dot_general(a, b, (((1,), (1,)), ((), ())), preferred_element_type=F32)


def _tn_dot(a, b, precision=None):
    return lax.dot_general(a, b, (((0,), (0,)), ((), ())), preferred_element_type=F32,
                           precision=precision)


def _rms(v, gain):
    ms = jnp.mean(v * v, axis=-1, keepdims=True)
    return v * lax.rsqrt(ms + NORM_EPS) * gain


def _sigmoid(v):
    return 1.0 / (1.0 + jnp.exp(-v))


def _norm_mix_kernel(x_ref, g_ref, wa_ref, wup_ref, bal_ref, h_ref, la_ref):
    h = _rms(x_ref[...], g_ref[...]).astype(BF16)
    h_ref[...] = h
    a_low = jnp.dot(h, wa_ref[...], preferred_element_type=F32)
    pre = jnp.dot(a_low, wup_ref[...], preferred_element_type=F32,
                  precision=lax.Precision.HIGHEST) + bal_ref[...]
    log_sig = jnp.minimum(pre, 0.0) - jnp.log1p(jnp.exp(-jnp.abs(pre)))
    la_ref[...] = log_sig * (1.0 / GLA_GATE_NORM)


def _norm_mix(x2, g, wa, wup, bal, tr=512):
    m, d = x2.shape
    n = wup.shape[1]
    fixed = lambda i: (0, 0)
    return pl.pallas_call(
        _norm_mix_kernel,
        out_shape=(jax.ShapeDtypeStruct((m, d), BF16), jax.ShapeDtypeStruct((m, n), F32)),
        grid=(m // tr,),
        in_specs=[pl.BlockSpec((tr, d), lambda i: (i, 0)),
                  pl.BlockSpec((1, d), fixed),
                  pl.BlockSpec((d, LANES), fixed),
                  pl.BlockSpec((LANES, n), fixed),
                  pl.BlockSpec((1, n), fixed)],
        out_specs=(pl.BlockSpec((tr, d), lambda i: (i, 0)),
                   pl.BlockSpec((tr, n), lambda i: (i, 0))),
        compiler_params=_params("parallel"),
        name="norm_mix",
    )(x2, g, wa, wup, bal)


def _norm_rows_kernel(x_ref, g_ref, h_ref):
    h_ref[...] = _rms(x_ref[...], g_ref[...]).astype(BF16)


def _norm_rows(x2, g, tr=512):
    m, d = x2.shape
    return pl.pallas_call(
        _norm_rows_kernel,
        out_shape=jax.ShapeDtypeStruct((m, d), BF16),
        grid=(m // tr,),
        in_specs=[pl.BlockSpec((tr, d), lambda i: (i, 0)),
                  pl.BlockSpec((1, d), lambda i: (0, 0))],
        out_specs=pl.BlockSpec((tr, d), lambda i: (i, 0)),
        compiler_params=_params("parallel"),
        name="norm_rows",
    )(x2, g)


def _store_group_norm(acc, gain, width, scale, out_ref):
    for s in range(0, acc.shape[1], width):
        blk = acc[:, s:s + width]
        out_ref[:, s:s + width] = (_rms(blk, gain) * scale).astype(out_ref.dtype)


def _in_proj_kernel(h_ref, w_ref, dq_g_ref, dk_g_ref, mq_g_ref, z_ref):
    j = pl.program_id(1)
    acc = jnp.dot(h_ref[...], w_ref[...], preferred_element_type=F32)
    j_dq, j_dk, j_mq = Z_DIFF_Q // IN_TILE, Z_DIFF_K // IN_TILE, Z_MEM_Q // IN_TILE

    @pl.when((j != j_dq) & (j != j_dk) & (j != j_mq))
    def _():
        z_ref[...] = acc.astype(z_ref.dtype)

    @pl.when(j == j_dq)
    def _():
        _store_group_norm(acc, dq_g_ref[...], DIFF_DH, DIFF_DH ** -0.5 * LOG2_E, z_ref)

    @pl.when(j == j_dk)
    def _():
        _store_group_norm(acc, dk_g_ref[...], DIFF_DH, 1.0, z_ref)

    @pl.when(j == j_mq)
    def _():
        _store_group_norm(acc, mq_g_ref[...], MEM_DH, MEM_DH ** -0.5, z_ref)


def _in_proj(h, w, dq_g, dk_g, mq_g, tm=1024):
    m, d = h.shape
    n = w.shape[1]
    return pl.pallas_call(
        _in_proj_kernel,
        out_shape=jax.ShapeDtypeStruct((m, n), BF16),
        grid=(m // tm, n // IN_TILE),
        in_specs=[pl.BlockSpec((tm, d), lambda i, j: (i, 0)),
                  pl.BlockSpec((d, IN_TILE), lambda i, j: (0, j)),
                  pl.BlockSpec((1, DIFF_DH), lambda i, j: (0, 0)),
                  pl.BlockSpec((1, DIFF_DH), lambda i, j: (0, 0)),
                  pl.BlockSpec((1, MEM_DH), lambda i, j: (0, 0))],
        out_specs=pl.BlockSpec((tm, IN_TILE), lambda i, j: (i, j)),
        compiler_params=_params("parallel", "arbitrary"),
        name="in_proj",
    )(h, w, dq_g, dk_g, mq_g)


def _chunk_cumsum(x):
    row_in_chunk = lax.broadcasted_iota(jnp.int32, x.shape, 0) % CHUNK
    shift = 1
    while shift < CHUNK:
        x = x + jnp.where(row_in_chunk >= shift, pltpu.roll(x, shift, 0), 0.0)
        shift *= 2
    return x


def _gla_kernel(q_ref, k_ref, v_ref, g_ref, la_ref, ng_ref, o_ref, s_ref, *, ts):
    @pl.when(pl.program_id(2) == 0)
    def _():
        s_ref[...] = jnp.zeros_like(s_ref)

    bcum_all = _chunk_cumsum(la_ref[...])
    row = lax.broadcasted_iota(jnp.int32, (CHUNK, CHUNK), 0)
    col = lax.broadcasted_iota(jnp.int32, (CHUNK, CHUNK), 1)
    causal = row >= col

    for c in range(ts // CHUNK):
        rows = slice(c * CHUNK, (c + 1) * CHUNK)
        bcum = bcum_all[rows]
        b_last = bcum[CHUNK - 1:CHUNK, :]
        q = q_ref[rows, :].astype(F32) * (GLA_DK ** -0.5)
        k = k_ref[rows, :].astype(F32)
        v = v_ref[rows, :]
        q_dec = (q * jnp.exp(bcum)).astype(BF16)
        k_dec = (k * jnp.exp(-bcum)).astype(BF16)
        k_tail = (k * jnp.exp(b_last - bcum)).astype(BF16)
        att = jnp.where(causal, _nt_dot(q_dec, k_dec), 0.0).astype(BF16)
        state = s_ref[...]
        o = jnp.dot(jnp.concatenate([q_dec, att], axis=1),
                    jnp.concatenate([state.astype(BF16), v], axis=0), preferred_element_type=F32)
        inc = _tn_dot(k_tail, v)
        decay_col = jnp.broadcast_to(jnp.exp(b_last), (GLA_DK, GLA_DK)).T
        decay = jnp.concatenate([decay_col] * (GLA_DV // GLA_DK), axis=1)
        s_ref[...] = decay * state + inc
        gate = g_ref[rows, :].astype(F32)
        o_ref[rows, :] = (_rms(o, ng_ref[...]) * (gate * _sigmoid(gate))).astype(o_ref.dtype)


def _gla(z, log_a, ng, batch, seq, ts=512):
    m = z.shape[0]
    nt = seq // ts
    rows = lambda b, h, t: b * nt + t
    return pl.pallas_call(
        functools.partial(_gla_kernel, ts=ts),
        out_shape=jax.ShapeDtypeStruct((m, GLA_HEADS * GLA_DV), BF16),
        grid=(batch, GLA_HEADS, nt),
        in_specs=[pl.BlockSpec((ts, GLA_DK), lambda b, h, t: (rows(b, h, t), Z_GLA_Q // GLA_DK + h)),
                  pl.BlockSpec((ts, GLA_DK), lambda b, h, t: (rows(b, h, t), Z_GLA_K // GLA_DK + h)),
                  pl.BlockSpec((ts, GLA_DV), lambda b, h, t: (rows(b, h, t), Z_GLA_V // GLA_DV + h)),
                  pl.BlockSpec((ts, GLA_DV), lambda b, h, t: (rows(b, h, t), Z_GLA_G // GLA_DV + h)),
                  pl.BlockSpec((ts, GLA_DK), lambda b, h, t: (rows(b, h, t), h)),
                  pl.BlockSpec((1, GLA_DV), lambda b, h, t: (0, 0))],
        out_specs=pl.BlockSpec((ts, GLA_DV), lambda b, h, t: (rows(b, h, t), h)),
        scratch_shapes=[pltpu.VMEM((GLA_DK, GLA_DV), F32)],
        compiler_params=_params("parallel", "parallel", "arbitrary"),
        name="gla",
    )(z, z, z, z, log_a, ng)


def _diff_kernel(q_ref, k_ref, v_ref, lq1_ref, lk1_ref, lq2_ref, lk2_ref, sg_ref, o_ref,
                 s_scr, p_scr, acc_scr, m_scr, l_scr, a_scr, *, tq):
    qi = pl.program_id(2)
    n_lane_tiles = tq // LANES
    m_scr[...] = jnp.full_like(m_scr, NEG_INF)
    l_scr[...] = jnp.zeros_like(l_scr)
    acc_scr[...] = jnp.zeros_like(acc_scr)
    first_half = lax.broadcasted_iota(jnp.int32, (CHUNK, LANES), 1) < CHUNK

    def softmax_rows(comp, r, diag):
        rows = slice(r * CHUNK, (r + 1) * CHUNK)
        n_vis = r // 2 + 1 if diag else n_lane_tiles
        parts = []
        for t in range(n_vis):
            part = s_scr[comp, rows, t * LANES:(t + 1) * LANES]
            if diag and 2 * t == r:
                part = jnp.where(first_half, part, NEG_INF)
            parts.append(part)
        tile_max = functools.reduce(jnp.maximum, parts)
        m_old = m_scr[comp, rows, :]
        m_new = jnp.maximum(m_old, jnp.max(tile_max, axis=-1, keepdims=True))
        alpha = jnp.exp2(m_old - m_new)
        ps = [jnp.exp2(part - m_new) for part in parts]
        l_cur = jnp.sum(functools.reduce(jnp.add, ps), axis=-1, keepdims=True)
        l_scr[comp, rows, :] = alpha * l_scr[comp, rows, :] + l_cur
        m_scr[comp, rows, :] = m_new
        a_scr[comp, rows, :] = alpha
        for t in range(n_lane_tiles):
            tile = ps[t].astype(BF16) if t < n_vis else jnp.zeros((CHUNK, LANES), BF16)
            p_scr[comp, rows, t * LANES:(t + 1) * LANES] = tile

    def block(kb, diag):
        start = pl.multiple_of(kb * tq, tq)
        for comp in range(2):
            cols = slice(comp * DIFF_DH, (comp + 1) * DIFF_DH)
            s_scr[comp] = _nt_dot(q_ref[:, cols], k_ref[pl.ds(start, tq), cols])
        v = v_ref[pl.ds(start, tq), :]
        for comp in range(2):
            for r in range(tq // CHUNK):
                softmax_rows(comp, r, diag)
            alpha = a_scr[comp]
            acc_scr[comp] = (jnp.concatenate([alpha] * (DIFF_DV // LANES), axis=1) * acc_scr[comp]
                             + jnp.dot(p_scr[comp], v, preferred_element_type=F32))

    def full_block(kb, carry):
        block(kb, diag=False)
        return carry

    lax.fori_loop(0, qi, full_block, 0)
    block(qi, diag=True)

    lam = (jnp.exp(jnp.sum(lq1_ref[...] * lk1_ref[...], axis=-1, keepdims=True))
           - jnp.exp(jnp.sum(lq2_ref[...] * lk2_ref[...], axis=-1, keepdims=True)) + LAM_INIT)
    widen = lambda v: jnp.concatenate([v] * (DIFF_DV // LANES), axis=1)
    o = acc_scr[0] / widen(l_scr[0]) - lam * (acc_scr[1] / widen(l_scr[1]))
    o_ref[...] = (_rms(o, sg_ref[...]) * (1.0 - LAM_INIT)).astype(o_ref.dtype)


def _diff_attn(z, lq1, lk1, lq2, lk2, sg, batch, seq, tq=512):
    m = z.shape[0]
    nq = seq // tq
    vec = pl.BlockSpec((1, DIFF_DH), lambda b, h, i: (0, 0))
    return pl.pallas_call(
        functools.partial(_diff_kernel, tq=tq),
        out_shape=jax.ShapeDtypeStruct((m, DIFF_HEADS * DIFF_DV), BF16),
        grid=(batch, DIFF_HEADS, nq),
        in_specs=[pl.BlockSpec((tq, DIFF_DV), lambda b, h, i: (b * nq + i, Z_DIFF_Q // DIFF_DV + h)),
                  pl.BlockSpec((seq, DIFF_DV), lambda b, h, i: (b, Z_DIFF_K // DIFF_DV + h)),
                  pl.BlockSpec((seq, DIFF_DV), lambda b, h, i: (b, Z_DIFF_V // DIFF_DV + h)),
                  vec, vec, vec, vec,
                  pl.BlockSpec((1, DIFF_DV), lambda b, h, i: (0, 0))],
        out_specs=pl.BlockSpec((tq, DIFF_DV), lambda b, h, i: (b * nq + i, h)),
        scratch_shapes=[pltpu.VMEM((2, tq, tq), F32),
                        pltpu.VMEM((2, tq, tq), BF16),
                        pltpu.VMEM((2, tq, DIFF_DV), F32),
                        pltpu.VMEM((2, tq, LANES), F32),
                        pltpu.VMEM((2, tq, LANES), F32),
                        pltpu.VMEM((2, tq, LANES), F32)],
        compiler_params=_params("parallel", "parallel", "arbitrary"),
        name="diff_attn",
    )(z, z, z, lq1, lk1, lq2, lk2, sg)


def _mem_kv_kernel(mn_ref, w_ref, kg_ref, kv_ref, *, n_key_tiles):
    j = pl.program_id(0)
    acc = jnp.dot(mn_ref[...], w_ref[...], preferred_element_type=F32)

    @pl.when(j < n_key_tiles)
    def _():
        _store_group_norm(acc, kg_ref[...], MEM_DH, 1.0, kv_ref)

    @pl.when(j >= n_key_tiles)
    def _():
        kv_ref[...] = acc.astype(kv_ref.dtype)


def _mem_kv(mem_n, w_kv, kg, tn=512):
    m, d = mem_n.shape
    n = w_kv.shape[1]
    return pl.pallas_call(
        functools.partial(_mem_kv_kernel, n_key_tiles=(n // 2) // tn),
        out_shape=jax.ShapeDtypeStruct((m, n), BF16),
        grid=(n // tn,),
        in_specs=[pl.BlockSpec((m, d), lambda j: (0, 0)),
                  pl.BlockSpec((d, tn), lambda j: (0, j)),
                  pl.BlockSpec((1, MEM_DH), lambda j: (0, 0))],
        out_specs=pl.BlockSpec((m, tn), lambda j: (0, j)),
        compiler_params=_params("parallel"),
        name="mem_kv",
    )(mem_n, w_kv, kg)


def _mem_attn_kernel(q_ref, k_ref, v_ref, o_ref):
    s = _nt_dot(q_ref[...], k_ref[...])
    e = jnp.exp(s - jnp.max(s, axis=-1, keepdims=True))
    p = (e / jnp.sum(e, axis=-1, keepdims=True)).astype(BF16)
    o_ref[...] = jnp.dot(p, v_ref[...], preferred_element_type=F32).astype(o_ref.dtype)


def _mem_attn(z, kv, batch, seq, n_mem, tm=1024):
    m = z.shape[0]
    nt = seq // tm
    return pl.pallas_call(
        _mem_attn_kernel,
        out_shape=jax.ShapeDtypeStruct((m, MEM_HEADS * MEM_DH), BF16),
        grid=(batch, nt, MEM_HEADS),
        in_specs=[pl.BlockSpec((tm, MEM_DH), lambda b, t, h: (b * nt + t, Z_MEM_Q // MEM_DH + h)),
                  pl.BlockSpec((n_mem, MEM_DH), lambda b, t, h: (b, h)),
                  pl.BlockSpec((n_mem, MEM_DH), lambda b, t, h: (b, MEM_HEADS + h))],
        out_specs=pl.BlockSpec((tm, MEM_DH), lambda b, t, h: (b * nt + t, h)),
        compiler_params=_params("parallel", "parallel", "parallel"),
        name="mem_attn",
    )(z, kv, kv)


def _gate_merge_kernel(h_ref, y0_ref, y1_ref, y2_ref, wg0_ref, wg1_ref, wg2_ref,
                       bg0_ref, bg1_ref, bg2_ref, wb0_ref, wb1_ref, wb2_ref, o_ref):
    h = h_ref[...]
    merged = None
    for y_ref, wg_ref, bg_ref, wb_ref in ((y0_ref, wg0_ref, bg0_ref, wb0_ref),
                                          (y1_ref, wg1_ref, bg1_ref, wb1_ref),
                                          (y2_ref, wg2_ref, bg2_ref, wb2_ref)):
        gate = _sigmoid(jnp.dot(h, wg_ref[...], preferred_element_type=F32) + bg_ref[...])
        term = gate * jnp.dot(y_ref[...], wb_ref[...], preferred_element_type=F32)
        merged = term if merged is None else merged + term
    o_ref[...] = merged.astype(o_ref.dtype)


def _gate_merge(h, ys, w_gate, b_gate, w_branch, tm=1024, tn=256):
    m, d = h.shape
    bw = w_branch.shape[1]
    nj = d // tn
    act = lambda width: pl.BlockSpec((tm, width), lambda i, j: (i, 0))
    wg = lambda b: pl.BlockSpec((d, tn), lambda i, j: (0, b * nj + j))
    bg = lambda b: pl.BlockSpec((1, tn), lambda i, j: (0, b * nj + j))
    wb = lambda b: pl.BlockSpec((None, bw, tn), lambda i, j: (b, 0, j))
    return pl.pallas_call(
        _gate_merge_kernel,
        out_shape=jax.ShapeDtypeStruct((m, d), BF16),
        grid=(m // tm, nj),
        in_specs=[act(d), act(bw), act(bw), act(bw), wg(0), wg(1), wg(2), bg(0), bg(1), bg(2),
                  wb(0), wb(1), wb(2)],
        out_specs=pl.BlockSpec((tm, tn), lambda i, j: (i, j)),
        compiler_params=_params("parallel", "arbitrary"),
        name="gate_merge",
    )(h, ys[0], ys[1], ys[2], w_gate, w_gate, w_gate, b_gate, b_gate, b_gate,
      w_branch, w_branch, w_branch)


def _out_proj_kernel(mg_ref, w_ref, x_ref, g_ref, x1_ref, hf_ref):
    x1 = x_ref[...] + jnp.dot(mg_ref[...], w_ref[...], preferred_element_type=F32)
    x1_ref[...] = x1
    hf_ref[...] = _rms(x1, g_ref[...]).astype(hf_ref.dtype)


def _out_proj(merged, w_out, x2, g, tm=512):
    m, d = x2.shape
    row = lambda i: (i, 0)
    fixed = lambda i: (0, 0)
    return pl.pallas_call(
        _out_proj_kernel,
        out_shape=(jax.ShapeDtypeStruct((m, d), F32), jax.ShapeDtypeStruct((m, d), BF16)),
        grid=(m // tm,),
        in_specs=[pl.BlockSpec((tm, d), row), pl.BlockSpec((d, d), fixed),
                  pl.BlockSpec((tm, d), row), pl.BlockSpec((1, d), fixed)],
        out_specs=(pl.BlockSpec((tm, d), row), pl.BlockSpec((tm, d), row)),
        compiler_params=_params("parallel"),
        name="out_proj",
    )(merged, w_out, x2, g)


def _ffn_up_kernel(hf_ref, wg_ref, wu_ref, a_ref):
    hf = hf_ref[...]
    gate = jnp.dot(hf, wg_ref[...], preferred_element_type=F32)
    up = jnp.dot(hf, wu_ref[...], preferred_element_type=F32)
    a_ref[...] = (gate * _sigmoid(gate) * up).astype(a_ref.dtype)


def _ffn_up(hf, w_in, tm=1024, tf=512):
    m, d = hf.shape
    d_ff = w_in.shape[1] // 2
    nj = d_ff // tf
    return pl.pallas_call(
        _ffn_up_kernel,
        out_shape=jax.ShapeDtypeStruct((m, d_ff), BF16),
        grid=(m // tm, nj),
        in_specs=[pl.BlockSpec((tm, d), lambda i, j: (i, 0)),
                  pl.BlockSpec((d, tf), lambda i, j: (0, j)),
                  pl.BlockSpec((d, tf), lambda i, j: (0, nj + j))],
        out_specs=pl.BlockSpec((tm, tf), lambda i, j: (i, j)),
        compiler_params=_params("parallel", "arbitrary"),
        name="ffn_up",
    )(hf, w_in, w_in)


def _ffn_down_kernel(a_ref, w_ref, x1_ref, o_ref):
    o_ref[...] = x1_ref[...] + jnp.dot(a_ref[...], w_ref[...], preferred_element_type=F32)


def _ffn_down(a, w_down, x1, tm=1024, tn=512):
    m, d_ff = a.shape
    d = w_down.shape[1]
    return pl.pallas_call(
        _ffn_down_kernel,
        out_shape=jax.ShapeDtypeStruct((m, d), F32),
        grid=(m // tm, d // tn),
        in_specs=[pl.BlockSpec((tm, d_ff), lambda i, j: (i, 0)),
                  pl.BlockSpec((d_ff, tn), lambda i, j: (0, j)),
                  pl.BlockSpec((tm, tn), lambda i, j: (i, j))],
        out_specs=pl.BlockSpec((tm, tn), lambda i, j: (i, j)),
        compiler_params=_params("parallel", "arbitrary"),
        name="ffn_down",
    )(a, w_down, x1)


def kernel(x, mem, norm_mix_g, norm_mem_g, w_in, gla_w_alpha_up, gla_b_alpha, gla_norm_g,
           diff_q_norm_g, diff_k_norm_g, diff_lambda_q1, diff_lambda_k1, diff_lambda_q2,
           diff_lambda_k2, diff_subln_g, mem_q_norm_g, mem_k_norm_g, w_mem_kv, w_branch,
           w_gate, b_gate, w_out, norm_ffn_g, w_ffn_in, w_ffn_down):
    batch, seq, d = x.shape
    n_mem = mem.shape[1]
    depth = w_in.shape[0]
    assert depth == 1, "LAM_INIT is the layer-0 value"
    x2 = x.reshape(batch * seq, d)
    mem2 = mem.reshape(batch * n_mem, d)
    for l in range(depth):
        a0 = Z_GLA_G + GLA_HEADS * GLA_DV
        w_main = jnp.concatenate([w_in[l][:, :a0], w_in[l][:, a0 + GLA_RANK:]], axis=1).astype(BF16)
        w_alow = jnp.pad(w_in[l][:, a0:a0 + GLA_RANK], ((0, 0), (0, LANES - GLA_RANK))).astype(BF16)
        w_up = jnp.pad(gla_w_alpha_up[l], ((0, LANES - GLA_RANK), (0, 0)))
        row = lambda v: v.reshape(1, -1)

        h, log_a = _norm_mix(x2, row(norm_mix_g[l]), w_alow, w_up, row(gla_b_alpha[l]))
        z = _in_proj(h, w_main, row(diff_q_norm_g[l]), row(diff_k_norm_g[l]), row(mem_q_norm_g[l]))
        y_gla = _gla(z, log_a, row(gla_norm_g[l]), batch, seq)
        y_diff = _diff_attn(z, row(diff_lambda_q1[l]), row(diff_lambda_k1[l]),
                            row(diff_lambda_q2[l]), row(diff_lambda_k2[l]),
                            row(diff_subln_g[l]), batch, seq)
        mem_n = _norm_rows(mem2, row(norm_mem_g[l]))
        kv = _mem_kv(mem_n, w_mem_kv[l].astype(BF16), row(mem_k_norm_g[l]))
        y_mem = _mem_attn(z, kv, batch, seq, n_mem)
        merged = _gate_merge(h, (y_gla, y_diff, y_mem), w_gate[l].astype(BF16), row(b_gate[l]),
                             w_branch[l].astype(BF16))
        x1, hf = _out_proj(merged, w_out[l].astype(BF16), x2, row(norm_ffn_g[l]))
        a = _ffn_up(hf, w_ffn_in[l].astype(BF16))
        x2 = _ffn_down(a, w_ffn_down[l].astype(BF16), x1)
    return x2.reshape(batch, seq, d)
```

```python
import functools

import jax
import jax.numpy as jnp
from jax import lax
from jax.experimental import pallas as pl
from jax.experimental.pallas import tpu as pltpu

F32 = jnp.float32
BF16 = jnp.bfloat16

CHUNK = 64
GLA_HEADS = 4
GLA_DK = 128
GLA_DV = 256
GLA_RANK = 16
GLA_GATE_NORM = 16.0
DIFF_HEADS = 4
DIFF_DH = 128
DIFF_DV = 256
MEM_HEADS = 4
MEM_DH = 256
N_BRANCH = 3
NORM_EPS = 1e-6
NEG_INF = -1e30
LAM_INIT = 0.8 - 0.6 * 1.0
LOG2_E = 1.4426950408889634

LANES = 128
VMEM_LIMIT = 56 * 1024 * 1024

Z_GLA_Q, Z_GLA_K, Z_GLA_V, Z_GLA_G = 0, 512, 1024, 2048
Z_DIFF_Q, Z_DIFF_K, Z_DIFF_V, Z_MEM_Q = 3072, 4096, 5120, 6144
Z_WIDTH = 7168
IN_TILE = 1024


def _params(*sem):
    return pltpu.CompilerParams(dimension_semantics=sem, vmem_limit_bytes=VMEM_LIMIT)


def _nt_dot(a, b):
    return lax.dot_general(a, b, (((1,), (1,)), ((), ())), preferred_element_type=F32)


def _tn_dot(a, b, precision=None):
    return lax.dot_general(a, b, (((0,), (0,)), ((), ())), preferred_element_type=F32,
                           precision=precision)


def _rms(v, gain):
    ms = jnp.mean(v * v, axis=-1, keepdims=True)
    return v * lax.rsqrt(ms + NORM_EPS) * gain


def _sigmoid(v):
    return 1.0 / (1.0 + jnp.exp(-v))


def _norm_mix_kernel(x_ref, g_ref, wa_ref, wup_ref, bal_ref, h_ref, la_ref):
    h = _rms(x_ref[...], g_ref[...]).astype(BF16)
    h_ref[...] = h
    a_low = jnp.dot(h, wa_ref[...], preferred_element_type=F32)
    pre = jnp.dot(a_low, wup_ref[...], preferred_element_type=F32,
                  precision=lax.Precision.HIGHEST) + bal_ref[...]
    log_sig = jnp.minimum(pre, 0.0) - jnp.log1p(jnp.exp(-jnp.abs(pre)))
    la_ref[...] = log_sig * (1.0 / GLA_GATE_NORM)


def _norm_mix(x2, g, wa, wup, bal, tr=512):
    m, d = x2.shape
    n = wup.shape[1]
    fixed = lambda i: (0, 0)
    return pl.pallas_call(
        _norm_mix_kernel,
        out_shape=(jax.ShapeDtypeStruct((m, d), BF16), jax.ShapeDtypeStruct((m, n), F32)),
        grid=(m // tr,),
        in_specs=[pl.BlockSpec((tr, d), lambda i: (i, 0)),
                  pl.BlockSpec((1, d), fixed),
                  pl.BlockSpec((d, LANES), fixed),
                  pl.BlockSpec((LANES, n), fixed),
                  pl.BlockSpec((1, n), fixed)],
        out_specs=(pl.BlockSpec((tr, d), lambda i: (i, 0)),
                   pl.BlockSpec((tr, n), lambda i: (i, 0))),
        compiler_params=_params("parallel"),
        name="norm_mix",
    )(x2, g, wa, wup, bal)


def _norm_rows_kernel(x_ref, g_ref, h_ref):
    h_ref[...] = _rms(x_ref[...], g_ref[...]).astype(BF16)


def _norm_rows(x2, g, tr=512):
    m, d = x2.shape
    return pl.pallas_call(
        _norm_rows_kernel,
        out_shape=jax.ShapeDtypeStruct((m, d), BF16),
        grid=(m // tr,),
        in_specs=[pl.BlockSpec((tr, d), lambda i: (i, 0)),
                  pl.BlockSpec((1, d), lambda i: (0, 0))],
        out_specs=pl.BlockSpec((tr, d), lambda i: (i, 0)),
        compiler_params=_params("parallel"),
        name="norm_rows",
    )(x2, g)


def _store_group_norm(acc, gain, width, scale, out_ref):
    for s in range(0, acc.shape[1], width):
        blk = acc[:, s:s + width]
        out_ref[:, s:s + width] = (_rms(blk, gain) * scale).astype(out_ref.dtype)


def _in_proj_kernel(h_ref, w_lo_ref, w_hi_ref, dq_g_ref, dk_g_ref, mq_g_ref, z_ref, w_scr):
    j = pl.program_id(0)
    first_shifted = Z_DIFF_Q // IN_TILE

    @pl.when((pl.program_id(1) == 0) & (j < first_shifted))
    def _():
        w_scr[...] = w_lo_ref[...].astype(BF16)

    @pl.when((pl.program_id(1) == 0) & (j >= first_shifted))
    def _():
        w_scr[...] = jnp.concatenate([w_lo_ref[:, GLA_RANK:], w_hi_ref[:, :GLA_RANK]],
                                     axis=1).astype(BF16)

    acc = jnp.dot(h_ref[...], w_scr[...], preferred_element_type=F32)
    j_dq, j_dk, j_mq = Z_DIFF_Q // IN_TILE, Z_DIFF_K // IN_TILE, Z_MEM_Q // IN_TILE

    @pl.when((j != j_dq) & (j != j_dk) & (j != j_mq))
    def _():
        z_ref[...] = acc.astype(z_ref.dtype)

    @pl.when(j == j_dq)
    def _():
        _store_group_norm(acc, dq_g_ref[...], DIFF_DH, DIFF_DH ** -0.5 * LOG2_E, z_ref)

    @pl.when(j == j_dk)
    def _():
        _store_group_norm(acc, dk_g_ref[...], DIFF_DH, 1.0, z_ref)

    @pl.when(j == j_mq)
    def _():
        _store_group_norm(acc, mq_g_ref[...], MEM_DH, MEM_DH ** -0.5, z_ref)


def _in_proj(h, w_in, dq_g, dk_g, mq_g, tm=1024):
    m, d = h.shape
    assert w_in.shape[1] == Z_WIDTH + GLA_RANK
    hi_per_tile = IN_TILE // LANES
    return pl.pallas_call(
        _in_proj_kernel,
        out_shape=jax.ShapeDtypeStruct((m, Z_WIDTH), BF16),
        grid=(Z_WIDTH // IN_TILE, m // tm),
        in_specs=[pl.BlockSpec((tm, d), lambda j, i: (i, 0)),
                  pl.BlockSpec((d, IN_TILE), lambda j, i: (0, j)),
                  pl.BlockSpec((d, LANES), lambda j, i: (0, (j + 1) * hi_per_tile)),
                  pl.BlockSpec((1, DIFF_DH), lambda j, i: (0, 0)),
                  pl.BlockSpec((1, DIFF_DH), lambda j, i: (0, 0)),
                  pl.BlockSpec((1, MEM_DH), lambda j, i: (0, 0))],
        out_specs=pl.BlockSpec((tm, IN_TILE), lambda j, i: (i, j)),
        scratch_shapes=[pltpu.VMEM((d, IN_TILE), BF16)],
        compiler_params=_params("arbitrary", "arbitrary"),
        name="in_proj",
    )(h, w_in, w_in, dq_g, dk_g, mq_g)


def _chunk_cumsum(x):
    row_in_chunk = lax.broadcasted_iota(jnp.int32, x.shape, 0) % CHUNK
    shift = 1
    while shift < CHUNK:
        x = x + jnp.where(row_in_chunk >= shift, pltpu.roll(x, shift, 0), 0.0)
        shift *= 2
    return x


def _gla_kernel(q_ref, k_ref, v_ref, g_ref, la_ref, ng_ref, o_ref, s_ref, *, ts):
    @pl.when(pl.program_id(2) == 0)
    def _():
        s_ref[...] = jnp.zeros_like(s_ref)

    bcum_all = _chunk_cumsum(la_ref[...])
    row = lax.broadcasted_iota(jnp.int32, (CHUNK, CHUNK), 0)
    col = lax.broadcasted_iota(jnp.int32, (CHUNK, CHUNK), 1)
    causal = row >= col

    for c in range(ts // CHUNK):
        rows = slice(c * CHUNK, (c + 1) * CHUNK)
        bcum = bcum_all[rows]
        b_last = bcum[CHUNK - 1:CHUNK, :]
        q = q_ref[rows, :].astype(F32) * (GLA_DK ** -0.5)
        k = k_ref[rows, :].astype(F32)
        v = v_ref[rows, :]
        q_dec = (q * jnp.exp(bcum)).astype(BF16)
        k_dec = (k * jnp.exp(-bcum)).astype(BF16)
        k_tail = (k * jnp.exp(b_last - bcum)).astype(BF16)
        att = jnp.where(causal, _nt_dot(q_dec, k_dec), 0.0).astype(BF16)
        state = s_ref[...]
        o = jnp.dot(jnp.concatenate([q_dec, att], axis=1),
                    jnp.concatenate([state.astype(BF16), v], axis=0), preferred_element_type=F32)
        inc = _tn_dot(k_tail, v)
        decay_col = jnp.broadcast_to(jnp.exp(b_last), (GLA_DK, GLA_DK)).T
        decay = jnp.concatenate([decay_col] * (GLA_DV // GLA_DK), axis=1)
        s_ref[...] = decay * state + inc
        gate = g_ref[rows, :].astype(F32)
        o_ref[rows, :] = (_rms(o, ng_ref[...]) * (gate * _sigmoid(gate))).astype(o_ref.dtype)


def _gla(z, log_a, ng, batch, seq, ts=512):
    m = z.shape[0]
    nt = seq // ts
    rows = lambda b, h, t: b * nt + t
    return pl.pallas_call(
        functools.partial(_gla_kernel, ts=ts),
        out_shape=jax.ShapeDtypeStruct((m, GLA_HEADS * GLA_DV), BF16),
        grid=(batch, GLA_HEADS, nt),
        in_specs=[pl.BlockSpec((ts, GLA_DK), lambda b, h, t: (rows(b, h, t), Z_GLA_Q // GLA_DK + h)),
                  pl.BlockSpec((ts, GLA_DK), lambda b, h, t: (rows(b, h, t), Z_GLA_K // GLA_DK + h)),
                  pl.BlockSpec((ts, GLA_DV), lambda b, h, t: (rows(b, h, t), Z_GLA_V // GLA_DV + h)),
                  pl.BlockSpec((ts, GLA_DV), lambda b, h, t: (rows(b, h, t), Z_GLA_G // GLA_DV + h)),
                  pl.BlockSpec((ts, GLA_DK), lambda b, h, t: (rows(b, h, t), h)),
                  pl.BlockSpec((1, GLA_DV), lambda b, h, t: (0, 0))],
        out_specs=pl.BlockSpec((ts, GLA_DV), lambda b, h, t: (rows(b, h, t), h)),
        scratch_shapes=[pltpu.VMEM((GLA_DK, GLA_DV), F32)],
        compiler_params=_params("parallel", "parallel", "arbitrary"),
        name="gla",
    )(z, z, z, z, log_a, ng)


def _diff_kernel(q_ref, k_ref, v_ref, lq1_ref, lk1_ref, lq2_ref, lk2_ref, sg_ref, o_ref,
                 s_scr, p_scr, acc_scr, m_scr, l_scr, a_scr, *, tq):
    qi = pl.program_id(2)
    n_lane_tiles = tq // LANES
    m_scr[...] = jnp.full_like(m_scr, NEG_INF)
    l_scr[...] = jnp.zeros_like(l_scr)
    acc_scr[...] = jnp.zeros_like(acc_scr)
    first_half = lax.broadcasted_iota(jnp.int32, (CHUNK, LANES), 1) < CHUNK

    def softmax_rows(comp, r, diag):
        rows = slice(r * CHUNK, (r + 1) * CHUNK)
        n_vis = r // 2 + 1 if diag else n_lane_tiles
        parts = []
        for t in range(n_vis):
            part = s_scr[comp, rows, t * LANES:(t + 1) * LANES]
            if diag and 2 * t == r:
                part = jnp.where(first_half, part, NEG_INF)
            parts.append(part)
        tile_max = functools.reduce(jnp.maximum, parts)
        m_old = m_scr[comp, rows, :]
        m_new = jnp.maximum(m_old, jnp.max(tile_max, axis=-1, keepdims=True))
        alpha = jnp.exp2(m_old - m_new)
        ps = [jnp.exp2(part - m_new) for part in parts]
        l_cur = jnp.sum(functools.reduce(jnp.add, ps), axis=-1, keepdims=True)
        l_scr[comp, rows, :] = alpha * l_scr[comp, rows, :] + l_cur
        m_scr[comp, rows, :] = m_new
        a_scr[comp, rows, :] = alpha
        for t in range(n_lane_tiles):
            tile = ps[t].astype(BF16) if t < n_vis else jnp.zeros((CHUNK, LANES), BF16)
            p_scr[comp, rows, t * LANES:(t + 1) * LANES] = tile

    def block(kb, diag):
        start = pl.multiple_of(kb * tq, tq)
        for comp in range(2):
            cols = slice(comp * DIFF_DH, (comp + 1) * DIFF_DH)
            s_scr[comp] = _nt_dot(q_ref[:, cols], k_ref[pl.ds(start, tq), cols])
        v = v_ref[pl.ds(start, tq), :]
        for comp in range(2):
            for r in range(tq // CHUNK):
                softmax_rows(comp, r, diag)
            alpha = a_scr[comp]
            acc_scr[comp] = (jnp.concatenate([alpha] * (DIFF_DV // LANES), axis=1) * acc_scr[comp]
                             + jnp.dot(p_scr[comp], v, preferred_element_type=F32))

    def full_block(kb, carry):
        block(kb, diag=False)
        return carry

    lax.fori_loop(0, qi, full_block, 0)
    block(qi, diag=True)

    lam = (jnp.exp(jnp.sum(lq1_ref[...] * lk1_ref[...], axis=-1, keepdims=True))
           - jnp.exp(jnp.sum(lq2_ref[...] * lk2_ref[...], axis=-1, keepdims=True)) + LAM_INIT)
    widen = lambda v: jnp.concatenate([v] * (DIFF_DV // LANES), axis=1)
    o = acc_scr[0] / widen(l_scr[0]) - lam * (acc_scr[1] / widen(l_scr[1]))
    o_ref[...] = (_rms(o, sg_ref[...]) * (1.0 - LAM_INIT)).astype(o_ref.dtype)


def _diff_attn(z, lq1, lk1, lq2, lk2, sg, batch, seq, tq=512):
    m = z.shape[0]
    nq = seq // tq
    vec = pl.BlockSpec((1, DIFF_DH), lambda b, h, i: (0, 0))
    return pl.pallas_call(
        functools.partial(_diff_kernel, tq=tq),
        out_shape=jax.ShapeDtypeStruct((m, DIFF_HEADS * DIFF_DV), BF16),
        grid=(batch, DIFF_HEADS, nq),
        in_specs=[pl.BlockSpec((tq, DIFF_DV), lambda b, h, i: (b * nq + i, Z_DIFF_Q // DIFF_DV + h)),
                  pl.BlockSpec((seq, DIFF_DV), lambda b, h, i: (b, Z_DIFF_K // DIFF_DV + h)),
                  pl.BlockSpec((seq, DIFF_DV), lambda b, h, i: (b, Z_DIFF_V // DIFF_DV + h)),
                  vec, vec, vec, vec,
                  pl.BlockSpec((1, DIFF_DV), lambda b, h, i: (0, 0))],
        out_specs=pl.BlockSpec((tq, DIFF_DV), lambda b, h, i: (b * nq + i, h)),
        scratch_shapes=[pltpu.VMEM((2, tq, tq), F32),
                        pltpu.VMEM((2, tq, tq), BF16),
                        pltpu.VMEM((2, tq, DIFF_DV), F32),
                        pltpu.VMEM((2, tq, LANES), F32),
                        pltpu.VMEM((2, tq, LANES), F32),
                        pltpu.VMEM((2, tq, LANES), F32)],
        compiler_params=_params("parallel", "parallel", "arbitrary"),
        name="diff_attn",
    )(z, z, z, lq1, lk1, lq2, lk2, sg)


def _mem_kv_kernel(mn_ref, w_ref, kg_ref, kv_ref, *, n_key_tiles):
    j = pl.program_id(0)
    acc = jnp.dot(mn_ref[...], w_ref[...].astype(BF16), preferred_element_type=F32)

    @pl.when(j < n_key_tiles)
    def _():
        _store_group_norm(acc, kg_ref[...], MEM_DH, 1.0, kv_ref)

    @pl.when(j >= n_key_tiles)
    def _():
        kv_ref[...] = acc.astype(kv_ref.dtype)


def _mem_kv(mem_n, w_kv, kg, tn=512):
    m, d = mem_n.shape
    n = w_kv.shape[1]
    return pl.pallas_call(
        functools.partial(_mem_kv_kernel, n_key_tiles=(n // 2) // tn),
        out_shape=jax.ShapeDtypeStruct((m, n), BF16),
        grid=(n // tn,),
        in_specs=[pl.BlockSpec((m, d), lambda j: (0, 0)),
                  pl.BlockSpec((d, tn), lambda j: (0, j)),
                  pl.BlockSpec((1, MEM_DH), lambda j: (0, 0))],
        out_specs=pl.BlockSpec((m, tn), lambda j: (0, j)),
        compiler_params=_params("parallel"),
        name="mem_kv",
    )(mem_n, w_kv, kg)


def _mem_attn_kernel(q_ref, k_ref, v_ref, o_ref):
    s = _nt_dot(q_ref[...], k_ref[...])
    e = jnp.exp(s - jnp.max(s, axis=-1, keepdims=True))
    p = (e / jnp.sum(e, axis=-1, keepdims=True)).astype(BF16)
    o_ref[...] = jnp.dot(p, v_ref[...], preferred_element_type=F32).astype(o_ref.dtype)


def _mem_attn(z, kv, batch, seq, n_mem, tm=1024):
    m = z.shape[0]
    nt = seq // tm
    return pl.pallas_call(
        _mem_attn_kernel,
        out_shape=jax.ShapeDtypeStruct((m, MEM_HEADS * MEM_DH), BF16),
        grid=(batch, nt, MEM_HEADS),
        in_specs=[pl.BlockSpec((tm, MEM_DH), lambda b, t, h: (b * nt + t, Z_MEM_Q // MEM_DH + h)),
                  pl.BlockSpec((n_mem, MEM_DH), lambda b, t, h: (b, h)),
                  pl.BlockSpec((n_mem, MEM_DH), lambda b, t, h: (b, MEM_HEADS + h))],
        out_specs=pl.BlockSpec((tm, MEM_DH), lambda b, t, h: (b * nt + t, h)),
        compiler_params=_params("parallel", "parallel", "parallel"),
        name="mem_attn",
    )(z, kv, kv)


def _gate_merge_kernel(h_ref, y0_ref, y1_ref, y2_ref, wg0_ref, wg1_ref, wg2_ref,
                       bg0_ref, bg1_ref, bg2_ref, wb0_ref, wb1_ref, wb2_ref, o_ref, wg_scr, wb_scr):
    @pl.when(pl.program_id(1) == 0)
    def _():
        for b, (wg_ref, wb_ref) in enumerate(((wg0_ref, wb0_ref), (wg1_ref, wb1_ref),
                                              (wg2_ref, wb2_ref))):
            wg_scr[b] = wg_ref[...].astype(BF16)
            wb_scr[b] = wb_ref[...].astype(BF16)

    h = h_ref[...]
    merged = None
    for b, (y_ref, bg_ref) in enumerate(((y0_ref, bg0_ref), (y1_ref, bg1_ref), (y2_ref, bg2_ref))):
        gate = _sigmoid(jnp.dot(h, wg_scr[b], preferred_element_type=F32) + bg_ref[...])
        term = gate * jnp.dot(y_ref[...], wb_scr[b], preferred_element_type=F32)
        merged = term if merged is None else merged + term
    o_ref[...] = merged.astype(o_ref.dtype)


def _gate_merge(h, ys, w_gate, b_gate, w_branch, tm=1024, tn=256):
    m, d = h.shape
    bw = w_branch.shape[1]
    nj = d // tn
    act = lambda width: pl.BlockSpec((tm, width), lambda j, i: (i, 0))
    wg = lambda b: pl.BlockSpec((d, tn), lambda j, i: (0, b * nj + j))
    bg = lambda b: pl.BlockSpec((1, tn), lambda j, i: (0, b * nj + j))
    wb = lambda b: pl.BlockSpec((None, bw, tn), lambda j, i: (b, 0, j))
    return pl.pallas_call(
        _gate_merge_kernel,
        out_shape=jax.ShapeDtypeStruct((m, d), BF16),
        grid=(nj, m // tm),
        in_specs=[act(d), act(bw), act(bw), act(bw), wg(0), wg(1), wg(2), bg(0), bg(1), bg(2),
                  wb(0), wb(1), wb(2)],
        out_specs=pl.BlockSpec((tm, tn), lambda j, i: (i, j)),
        scratch_shapes=[pltpu.VMEM((N_BRANCH, d, tn), BF16), pltpu.VMEM((N_BRANCH, bw, tn), BF16)],
        compiler_params=_params("arbitrary", "arbitrary"),
        name="gate_merge",
    )(h, ys[0], ys[1], ys[2], w_gate, w_gate, w_gate, b_gate, b_gate, b_gate,
      w_branch, w_branch, w_branch)


def _out_proj_kernel(mg_ref, w_ref, x_ref, g_ref, x1_ref, hf_ref, w_scr):
    @pl.when(pl.program_id(0) == 0)
    def _():
        w_scr[...] = w_ref[...].astype(BF16)

    x1 = x_ref[...] + jnp.dot(mg_ref[...], w_scr[...], preferred_element_type=F32)
    x1_ref[...] = x1
    hf_ref[...] = _rms(x1, g_ref[...]).astype(hf_ref.dtype)


def _out_proj(merged, w_out, x2, g, tm=512):
    m, d = x2.shape
    row = lambda i: (i, 0)
    fixed = lambda i: (0, 0)
    return pl.pallas_call(
        _out_proj_kernel,
        out_shape=(jax.ShapeDtypeStruct((m, d), F32), jax.ShapeDtypeStruct((m, d), BF16)),
        grid=(m // tm,),
        in_specs=[pl.BlockSpec((tm, d), row),
                  pl.BlockSpec((d, d), fixed, pipeline_mode=pl.Buffered(1)),
                  pl.BlockSpec((tm, d), row), pl.BlockSpec((1, d), fixed)],
        out_specs=(pl.BlockSpec((tm, d), row), pl.BlockSpec((tm, d), row)),
        scratch_shapes=[pltpu.VMEM((d, d), BF16)],
        compiler_params=_params("arbitrary"),
        name="out_proj",
    )(merged, w_out, x2, g)


def _ffn_up_kernel(hf_ref, wg_ref, wu_ref, a_ref, wg_scr, wu_scr):
    @pl.when(pl.program_id(1) == 0)
    def _():
        wg_scr[...] = wg_ref[...].astype(BF16)
        wu_scr[...] = wu_ref[...].astype(BF16)

    hf = hf_ref[...]
    gate = jnp.dot(hf, wg_scr[...], preferred_element_type=F32)
    up = jnp.dot(hf, wu_scr[...], preferred_element_type=F32)
    a_ref[...] = (gate * _sigmoid(gate) * up).astype(a_ref.dtype)


def _ffn_up(hf, w_in, tm=1024, tf=512):
    m, d = hf.shape
    d_ff = w_in.shape[1] // 2
    nj = d_ff // tf
    return pl.pallas_call(
        _ffn_up_kernel,
        out_shape=jax.ShapeDtypeStruct((m, d_ff), BF16),
        grid=(nj, m // tm),
        in_specs=[pl.BlockSpec((tm, d), lambda j, i: (i, 0)),
                  pl.BlockSpec((d, tf), lambda j, i: (0, j)),
                  pl.BlockSpec((d, tf), lambda j, i: (0, nj + j))],
        out_specs=pl.BlockSpec((tm, tf), lambda j, i: (i, j)),
        scratch_shapes=[pltpu.VMEM((d, tf), BF16), pltpu.VMEM((d, tf), BF16)],
        compiler_params=_params("arbitrary", "arbitrary"),
        name="ffn_up",
    )(hf, w_in, w_in)


def _ffn_down_kernel(a_ref, w_ref, x1_ref, o_ref, w_scr):
    @pl.when(pl.program_id(1) == 0)
    def _():
        w_scr[...] = w_ref[...].astype(BF16)

    o_ref[...] = x1_ref[...] + jnp.dot(a_ref[...], w_scr[...], preferred_element_type=F32)


def _ffn_down(a, w_down, x1, tm=512, tn=512):
    m, d_ff = a.shape
    d = w_down.shape[1]
    return pl.pallas_call(
        _ffn_down_kernel,
        out_shape=jax.ShapeDtypeStruct((m, d), F32),
        grid=(d // tn, m // tm),
        in_specs=[pl.BlockSpec((tm, d_ff), lambda j, i: (i, 0)),
                  pl.BlockSpec((d_ff, tn), lambda j, i: (0, j)),
                  pl.BlockSpec((tm, tn), lambda j, i: (i, j))],
        out_specs=pl.BlockSpec((tm, tn), lambda j, i: (i, j)),
        scratch_shapes=[pltpu.VMEM((d_ff, tn), BF16)],
        compiler_params=_params("arbitrary", "arbitrary"),
        name="ffn_down",
    )(a, w_down, x1)


def kernel(x, mem, norm_mix_g, norm_mem_g, w_in, gla_w_alpha_up, gla_b_alpha, gla_norm_g,
           diff_q_norm_g, diff_k_norm_g, diff_lambda_q1, diff_lambda_k1, diff_lambda_q2,
           diff_lambda_k2, diff_subln_g, mem_q_norm_g, mem_k_norm_g, w_mem_kv, w_branch,
           w_gate, b_gate, w_out, norm_ffn_g, w_ffn_in, w_ffn_down):
    batch, seq, d = x.shape
    n_mem = mem.shape[1]
    depth = w_in.shape[0]
    assert depth == 1, "LAM_INIT is the layer-0 value"
    x2 = x.reshape(batch * seq, d)
    mem2 = mem.reshape(batch * n_mem, d)
    for l in range(depth):
        a0 = Z_GLA_G + GLA_HEADS * GLA_DV
        w_alow = jnp.pad(w_in[l][:, a0:a0 + GLA_RANK], ((0, 0), (0, LANES - GLA_RANK))).astype(BF16)
        w_up = jnp.pad(gla_w_alpha_up[l], ((0, LANES - GLA_RANK), (0, 0)))
        row = lambda v: v.reshape(1, -1)

        h, log_a = _norm_mix(x2, row(norm_mix_g[l]), w_alow, w_up, row(gla_b_alpha[l]))
        z = _in_proj(h, w_in[l], row(diff_q_norm_g[l]), row(diff_k_norm_g[l]), row(mem_q_norm_g[l]))
        y_gla = _gla(z, log_a, row(gla_norm_g[l]), batch, seq)
        y_diff = _diff_attn(z, row(diff_lambda_q1[l]), row(diff_lambda_k1[l]),
                            row(diff_lambda_q2[l]), row(diff_lambda_k2[l]),
                            row(diff_subln_g[l]), batch, seq)
        mem_n = _norm_rows(mem2, row(norm_mem_g[l]))
        kv = _mem_kv(mem_n, w_mem_kv[l], row(mem_k_norm_g[l]))
        y_mem = _mem_attn(z, kv, batch, seq, n_mem)
        merged = _gate_merge(h, (y_gla, y_diff, y_mem), w_gate[l], row(b_gate[l]), w_branch[l])
        x1, hf = _out_proj(merged, w_out[l], x2, row(norm_ffn_g[l]))
        a = _ffn_up(hf, w_ffn_in[l])
        x2 = _ffn_down(a, w_ffn_down[l], x1)
    return x2.reshape(batch, seq, d)
```

```python
import functools

import jax
import jax.numpy as jnp
from jax import lax
from jax.experimental import pallas as pl
from jax.experimental.pallas import tpu as pltpu

F32 = jnp.float32
BF16 = jnp.bfloat16

CHUNK = 64
GLA_HEADS = 4
GLA_DK = 128
GLA_DV = 256
GLA_RANK = 16
GLA_GATE_NORM = 16.0
DIFF_HEADS = 4
DIFF_DH = 128
DIFF_DV = 256
MEM_HEADS = 4
MEM_DH = 256
N_BRANCH = 3
NORM_EPS = 1e-6
NEG_INF = -1e30
LAM_INIT = 0.8 - 0.6 * 1.0
LOG2_E = 1.4426950408889634

LANES = 128
VMEM_LIMIT = 56 * 1024 * 1024

Z_GLA_Q, Z_GLA_K, Z_GLA_V, Z_GLA_G = 0, 512, 1024, 2048
Z_DIFF_Q, Z_DIFF_K, Z_DIFF_V, Z_MEM_Q = 3072, 4096, 5120, 6144
Z_WIDTH = 7168
IN_TILE = 1024


def _params(*sem):
    return pltpu.CompilerParams(dimension_semantics=sem, vmem_limit_bytes=VMEM_LIMIT)


def _nt_dot(a, b):
    return lax.dot_general(a, b, (((1,), (1,)), ((), ())), preferred_element_type=F32)


def _tn_dot(a, b, precision=None):
    return lax.dot_general(a, b, (((0,), (0,)), ((), ())), preferred_element_type=F32,
                           precision=precision)


def _rms(v, gain):
    ms = jnp.mean(v * v, axis=-1, keepdims=True)
    return v * lax.rsqrt(ms + NORM_EPS) * gain


def _sigmoid(v):
    return 1.0 / (1.0 + jnp.exp(-v))


def _norm_mix_kernel(x_ref, g_ref, wa_ref, wup_ref, bal_ref, h_ref, la_ref):
    h = _rms(x_ref[...], g_ref[...]).astype(BF16)
    h_ref[...] = h
    a_low = _nt_dot(h, wa_ref[...].astype(BF16))
    pre = jnp.dot(a_low, wup_ref[...], preferred_element_type=F32,
                  precision=lax.Precision.HIGHEST) + bal_ref[...]
    log_sig = jnp.minimum(pre, 0.0) - jnp.log1p(jnp.exp(-jnp.abs(pre)))
    la_ref[...] = log_sig * (1.0 / GLA_GATE_NORM)


def _norm_mix(x2, g, w_in_t, wup, bal, tr=512):
    m, d = x2.shape
    n = wup.shape[1]
    fixed = lambda i: (0, 0)
    decay_block = (Z_GLA_G + GLA_HEADS * GLA_DV) // LANES
    return pl.pallas_call(
        _norm_mix_kernel,
        out_shape=(jax.ShapeDtypeStruct((m, d), BF16), jax.ShapeDtypeStruct((m, n), F32)),
        grid=(m // tr,),
        in_specs=[pl.BlockSpec((tr, d), lambda i: (i, 0)),
                  pl.BlockSpec((1, d), fixed),
                  pl.BlockSpec((LANES, d), lambda i: (decay_block, 0)),
                  pl.BlockSpec((LANES, n), fixed),
                  pl.BlockSpec((1, n), fixed)],
        out_specs=(pl.BlockSpec((tr, d), lambda i: (i, 0)),
                   pl.BlockSpec((tr, n), lambda i: (i, 0))),
        compiler_params=_params("parallel"),
        name="norm_mix",
    )(x2, g, w_in_t, wup, bal)


def _norm_rows_kernel(x_ref, g_ref, h_ref):
    h_ref[...] = _rms(x_ref[...], g_ref[...]).astype(BF16)


def _norm_rows(x2, g, tr=512):
    m, d = x2.shape
    return pl.pallas_call(
        _norm_rows_kernel,
        out_shape=jax.ShapeDtypeStruct((m, d), BF16),
        grid=(m // tr,),
        in_specs=[pl.BlockSpec((tr, d), lambda i: (i, 0)),
                  pl.BlockSpec((1, d), lambda i: (0, 0))],
        out_specs=pl.BlockSpec((tr, d), lambda i: (i, 0)),
        compiler_params=_params("parallel"),
        name="norm_rows",
    )(x2, g)


def _store_group_norm(acc, gain, width, scale, out_ref):
    for s in range(0, acc.shape[1], width):
        blk = acc[:, s:s + width]
        out_ref[:, s:s + width] = (_rms(blk, gain) * scale).astype(out_ref.dtype)


def _in_proj_kernel(h_ref, w_lo_ref, w_hi_ref, dq_g_ref, dk_g_ref, mq_g_ref, z_ref, w_scr):
    j = pl.program_id(0)
    first_shifted = Z_DIFF_Q // IN_TILE

    @pl.when((pl.program_id(1) == 0) & (j < first_shifted))
    def _():
        w_scr[...] = w_lo_ref[...].astype(BF16)

    @pl.when((pl.program_id(1) == 0) & (j >= first_shifted))
    def _():
        w_scr[:IN_TILE - GLA_RANK, :] = w_lo_ref[GLA_RANK:, :].astype(BF16)
        w_scr[IN_TILE - GLA_RANK:, :] = w_hi_ref[...].astype(BF16)

    acc = _nt_dot(h_ref[...], w_scr[...])
    j_dq, j_dk, j_mq = Z_DIFF_Q // IN_TILE, Z_DIFF_K // IN_TILE, Z_MEM_Q // IN_TILE

    @pl.when((j != j_dq) & (j != j_dk) & (j != j_mq))
    def _():
        z_ref[...] = acc.astype(z_ref.dtype)

    @pl.when(j == j_dq)
    def _():
        _store_group_norm(acc, dq_g_ref[...], DIFF_DH, DIFF_DH ** -0.5 * LOG2_E, z_ref)

    @pl.when(j == j_dk)
    def _():
        _store_group_norm(acc, dk_g_ref[...], DIFF_DH, 1.0, z_ref)

    @pl.when(j == j_mq)
    def _():
        _store_group_norm(acc, mq_g_ref[...], MEM_DH, MEM_DH ** -0.5, z_ref)


def _in_proj(h, w_in_t, dq_g, dk_g, mq_g, tm=1024):
    m, d = h.shape
    assert w_in_t.shape[0] == Z_WIDTH + GLA_RANK
    hi_per_tile = IN_TILE // GLA_RANK
    return pl.pallas_call(
        _in_proj_kernel,
        out_shape=jax.ShapeDtypeStruct((m, Z_WIDTH), BF16),
        grid=(Z_WIDTH // IN_TILE, m // tm),
        in_specs=[pl.BlockSpec((tm, d), lambda j, i: (i, 0)),
                  pl.BlockSpec((IN_TILE, d), lambda j, i: (j, 0)),
                  pl.BlockSpec((GLA_RANK, d), lambda j, i: ((j + 1) * hi_per_tile, 0)),
                  pl.BlockSpec((1, DIFF_DH), lambda j, i: (0, 0)),
                  pl.BlockSpec((1, DIFF_DH), lambda j, i: (0, 0)),
                  pl.BlockSpec((1, MEM_DH), lambda j, i: (0, 0))],
        out_specs=pl.BlockSpec((tm, IN_TILE), lambda j, i: (i, j)),
        scratch_shapes=[pltpu.VMEM((IN_TILE, d), BF16)],
        compiler_params=_params("arbitrary", "arbitrary"),
        name="in_proj",
    )(h, w_in_t, w_in_t, dq_g, dk_g, mq_g)


def _chunk_cumsum(x):
    row_in_chunk = lax.broadcasted_iota(jnp.int32, x.shape, 0) % CHUNK
    shift = 1
    while shift < CHUNK:
        x = x + jnp.where(row_in_chunk >= shift, pltpu.roll(x, shift, 0), 0.0)
        shift *= 2
    return x


def _gla_kernel(q_ref, k_ref, v_ref, g_ref, la_ref, ng_ref, o_ref, s_ref, *, ts):
    @pl.when(pl.program_id(2) == 0)
    def _():
        s_ref[...] = jnp.zeros_like(s_ref)

    bcum_all = _chunk_cumsum(la_ref[...])
    row = lax.broadcasted_iota(jnp.int32, (CHUNK, CHUNK), 0)
    col = lax.broadcasted_iota(jnp.int32, (CHUNK, CHUNK), 1)
    causal = row >= col

    for c in range(ts // CHUNK):
        rows = slice(c * CHUNK, (c + 1) * CHUNK)
        bcum = bcum_all[rows]
        b_last = bcum[CHUNK - 1:CHUNK, :]
        q = q_ref[rows, :].astype(F32) * (GLA_DK ** -0.5)
        k = k_ref[rows, :].astype(F32)
        v = v_ref[rows, :]
        q_dec = (q * jnp.exp(bcum)).astype(BF16)
        k_dec = (k * jnp.exp(-bcum)).astype(BF16)
        k_tail = (k * jnp.exp(b_last - bcum)).astype(BF16)
        att = jnp.where(causal, _nt_dot(q_dec, k_dec), 0.0).astype(BF16)
        state = s_ref[...]
        o = jnp.dot(jnp.concatenate([q_dec, att], axis=1),
                    jnp.concatenate([state.astype(BF16), v], axis=0), preferred_element_type=F32)
        inc = _tn_dot(k_tail, v)
        decay_col = jnp.broadcast_to(jnp.exp(b_last), (GLA_DK, GLA_DK)).T
        decay = jnp.concatenate([decay_col] * (GLA_DV // GLA_DK), axis=1)
        s_ref[...] = decay * state + inc
        gate = g_ref[rows, :].astype(F32)
        o_ref[rows, :] = (_rms(o, ng_ref[...]) * (gate * _sigmoid(gate))).astype(o_ref.dtype)


def _gla(z, log_a, ng, batch, seq, ts=512):
    m = z.shape[0]
    nt = seq // ts
    rows = lambda b, h, t: b * nt + t
    return pl.pallas_call(
        functools.partial(_gla_kernel, ts=ts),
        out_shape=jax.ShapeDtypeStruct((m, GLA_HEADS * GLA_DV), BF16),
        grid=(batch, GLA_HEADS, nt),
        in_specs=[pl.BlockSpec((ts, GLA_DK), lambda b, h, t: (rows(b, h, t), Z_GLA_Q // GLA_DK + h)),
                  pl.BlockSpec((ts, GLA_DK), lambda b, h, t: (rows(b, h, t), Z_GLA_K // GLA_DK + h)),
                  pl.BlockSpec((ts, GLA_DV), lambda b, h, t: (rows(b, h, t), Z_GLA_V // GLA_DV + h)),
                  pl.BlockSpec((ts, GLA_DV), lambda b, h, t: (rows(b, h, t), Z_GLA_G // GLA_DV + h)),
                  pl.BlockSpec((ts, GLA_DK), lambda b, h, t: (rows(b, h, t), h)),
                  pl.BlockSpec((1, GLA_DV), lambda b, h, t: (0, 0))],
        out_specs=pl.BlockSpec((ts, GLA_DV), lambda b, h, t: (rows(b, h, t), h)),
        scratch_shapes=[pltpu.VMEM((GLA_DK, GLA_DV), F32)],
        compiler_params=_params("parallel", "parallel", "arbitrary"),
        name="gla",
    )(z, z, z, z, log_a, ng)


def _diff_kernel(q_ref, k_ref, v_ref, lq1_ref, lk1_ref, lq2_ref, lk2_ref, sg_ref, o_ref,
                 s_scr, p_scr, acc_scr, m_scr, l_scr, a_scr, *, tq):
    qi = pl.program_id(2)
    n_lane_tiles = tq // LANES
    m_scr[...] = jnp.full_like(m_scr, NEG_INF)
    l_scr[...] = jnp.zeros_like(l_scr)
    acc_scr[...] = jnp.zeros_like(acc_scr)
    first_half = lax.broadcasted_iota(jnp.int32, (CHUNK, LANES), 1) < CHUNK

    def softmax_rows(comp, r, diag):
        rows = slice(r * CHUNK, (r + 1) * CHUNK)
        n_vis = r // 2 + 1 if diag else n_lane_tiles
        parts = []
        for t in range(n_vis):
            part = s_scr[comp, rows, t * LANES:(t + 1) * LANES]
            if diag and 2 * t == r:
                part = jnp.where(first_half, part, NEG_INF)
            parts.append(part)
        tile_max = functools.reduce(jnp.maximum, parts)
        m_old = m_scr[comp, rows, :]
        m_new = jnp.maximum(m_old, jnp.max(tile_max, axis=-1, keepdims=True))
        alpha = jnp.exp2(m_old - m_new)
        ps = [jnp.exp2(part - m_new) for part in parts]
        l_cur = jnp.sum(functools.reduce(jnp.add, ps), axis=-1, keepdims=True)
        l_scr[comp, rows, :] = alpha * l_scr[comp, rows, :] + l_cur
        m_scr[comp, rows, :] = m_new
        a_scr[comp, rows, :] = alpha
        for t in range(n_lane_tiles):
            tile = ps[t].astype(BF16) if t < n_vis else jnp.zeros((CHUNK, LANES), BF16)
            p_scr[comp, rows, t * LANES:(t + 1) * LANES] = tile

    def block(kb, diag):
        start = pl.multiple_of(kb * tq, tq)
        for comp in range(2):
            cols = slice(comp * DIFF_DH, (comp + 1) * DIFF_DH)
            s_scr[comp] = _nt_dot(q_ref[:, cols], k_ref[pl.ds(start, tq), cols])
        v = v_ref[pl.ds(start, tq), :]
        for comp in range(2):
            for r in range(tq // CHUNK):
                softmax_rows(comp, r, diag)
            alpha = a_scr[comp]
            acc_scr[comp] = (jnp.concatenate([alpha] * (DIFF_DV // LANES), axis=1) * acc_scr[comp]
                             + jnp.dot(p_scr[comp], v, preferred_element_type=F32))

    def full_block(kb, carry):
        block(kb, diag=False)
        return carry

    lax.fori_loop(0, qi, full_block, 0)
    block(qi, diag=True)

    lam = (jnp.exp(jnp.sum(lq1_ref[...] * lk1_ref[...], axis=-1, keepdims=True))
           - jnp.exp(jnp.sum(lq2_ref[...] * lk2_ref[...], axis=-1, keepdims=True)) + LAM_INIT)
    widen = lambda v: jnp.concatenate([v] * (DIFF_DV // LANES), axis=1)
    o = acc_scr[0] / widen(l_scr[0]) - lam * (acc_scr[1] / widen(l_scr[1]))
    o_ref[...] = (_rms(o, sg_ref[...]) * (1.0 - LAM_INIT)).astype(o_ref.dtype)


def _diff_attn(z, lq1, lk1, lq2, lk2, sg, batch, seq, tq=512):
    m = z.shape[0]
    nq = seq // tq
    vec = pl.BlockSpec((1, DIFF_DH), lambda b, h, i: (0, 0))
    return pl.pallas_call(
        functools.partial(_diff_kernel, tq=tq),
        out_shape=jax.ShapeDtypeStruct((m, DIFF_HEADS * DIFF_DV), BF16),
        grid=(batch, DIFF_HEADS, nq),
        in_specs=[pl.BlockSpec((tq, DIFF_DV), lambda b, h, i: (b * nq + i, Z_DIFF_Q // DIFF_DV + h)),
                  pl.BlockSpec((seq, DIFF_DV), lambda b, h, i: (b, Z_DIFF_K // DIFF_DV + h)),
                  pl.BlockSpec((seq, DIFF_DV), lambda b, h, i: (b, Z_DIFF_V // DIFF_DV + h)),
                  vec, vec, vec, vec,
                  pl.BlockSpec((1, DIFF_DV), lambda b, h, i: (0, 0))],
        out_specs=pl.BlockSpec((tq, DIFF_DV), lambda b, h, i: (b * nq + i, h)),
        scratch_shapes=[pltpu.VMEM((2, tq, tq), F32),
                        pltpu.VMEM((2, tq, tq), BF16),
                        pltpu.VMEM((2, tq, DIFF_DV), F32),
                        pltpu.VMEM((2, tq, LANES), F32),
                        pltpu.VMEM((2, tq, LANES), F32),
                        pltpu.VMEM((2, tq, LANES), F32)],
        compiler_params=_params("parallel", "parallel", "arbitrary"),
        name="diff_attn",
    )(z, z, z, lq1, lk1, lq2, lk2, sg)


def _mem_kv_kernel(mn_ref, w_ref, kg_ref, kv_ref, *, n_key_tiles):
    j = pl.program_id(0)
    acc = jnp.dot(mn_ref[...], w_ref[...].astype(BF16), preferred_element_type=F32)

    @pl.when(j < n_key_tiles)
    def _():
        _store_group_norm(acc, kg_ref[...], MEM_DH, 1.0, kv_ref)

    @pl.when(j >= n_key_tiles)
    def _():
        kv_ref[...] = acc.astype(kv_ref.dtype)


def _mem_kv(mem_n, w_kv, kg, tn=512):
    m, d = mem_n.shape
    n = w_kv.shape[1]
    return pl.pallas_call(
        functools.partial(_mem_kv_kernel, n_key_tiles=(n // 2) // tn),
        out_shape=jax.ShapeDtypeStruct((m, n), BF16),
        grid=(n // tn,),
        in_specs=[pl.BlockSpec((m, d), lambda j: (0, 0)),
                  pl.BlockSpec((d, tn), lambda j: (0, j)),
                  pl.BlockSpec((1, MEM_DH), lambda j: (0, 0))],
        out_specs=pl.BlockSpec((m, tn), lambda j: (0, j)),
        compiler_params=_params("parallel"),
        name="mem_kv",
    )(mem_n, w_kv, kg)


def _mem_attn_kernel(q_ref, k_ref, v_ref, o_ref):
    s = _nt_dot(q_ref[...], k_ref[...])
    e = jnp.exp(s - jnp.max(s, axis=-1, keepdims=True))
    p = (e / jnp.sum(e, axis=-1, keepdims=True)).astype(BF16)
    o_ref[...] = jnp.dot(p, v_ref[...], preferred_element_type=F32).astype(o_ref.dtype)


def _mem_attn(z, kv, batch, seq, n_mem, tm=1024):
    m = z.shape[0]
    nt = seq // tm
    return pl.pallas_call(
        _mem_attn_kernel,
        out_shape=jax.ShapeDtypeStruct((m, MEM_HEADS * MEM_DH), BF16),
        grid=(batch, nt, MEM_HEADS),
        in_specs=[pl.BlockSpec((tm, MEM_DH), lambda b, t, h: (b * nt + t, Z_MEM_Q // MEM_DH + h)),
                  pl.BlockSpec((n_mem, MEM_DH), lambda b, t, h: (b, h)),
                  pl.BlockSpec((n_mem, MEM_DH), lambda b, t, h: (b, MEM_HEADS + h))],
        out_specs=pl.BlockSpec((tm, MEM_DH), lambda b, t, h: (b * nt + t, h)),
        compiler_params=_params("parallel", "parallel", "parallel"),
        name="mem_attn",
    )(z, kv, kv)


def _gate_merge_kernel(h_ref, y0_ref, y1_ref, y2_ref, wg0_ref, wg1_ref, wg2_ref,
                       bg0_ref, bg1_ref, bg2_ref, wb0_ref, wb1_ref, wb2_ref, o_ref, wg_scr, wb_scr):
    @pl.when(pl.program_id(1) == 0)
    def _():
        for b, (wg_ref, wb_ref) in enumerate(((wg0_ref, wb0_ref), (wg1_ref, wb1_ref),
                                              (wg2_ref, wb2_ref))):
            wg_scr[b] = wg_ref[...].astype(BF16)
            wb_scr[b] = wb_ref[...].astype(BF16)

    h = h_ref[...]
    merged = None
    for b, (y_ref, bg_ref) in enumerate(((y0_ref, bg0_ref), (y1_ref, bg1_ref), (y2_ref, bg2_ref))):
        gate = _sigmoid(jnp.dot(h, wg_scr[b], preferred_element_type=F32) + bg_ref[...])
        term = gate * jnp.dot(y_ref[...], wb_scr[b], preferred_element_type=F32)
        merged = term if merged is None else merged + term
    o_ref[...] = merged.astype(o_ref.dtype)


def _gate_merge(h, ys, w_gate, b_gate, w_branch, tm=1024, tn=256):
    m, d = h.shape
    bw = w_branch.shape[1]
    nj = d // tn
    act = lambda width: pl.BlockSpec((tm, width), lambda j, i: (i, 0))
    wg = lambda b: pl.BlockSpec((d, tn), lambda j, i: (0, b * nj + j))
    bg = lambda b: pl.BlockSpec((1, tn), lambda j, i: (0, b * nj + j))
    wb = lambda b: pl.BlockSpec((None, bw, tn), lambda j, i: (b, 0, j))
    return pl.pallas_call(
        _gate_merge_kernel,
        out_shape=jax.ShapeDtypeStruct((m, d), BF16),
        grid=(nj, m // tm),
        in_specs=[act(d), act(bw), act(bw), act(bw), wg(0), wg(1), wg(2), bg(0), bg(1), bg(2),
                  wb(0), wb(1), wb(2)],
        out_specs=pl.BlockSpec((tm, tn), lambda j, i: (i, j)),
        scratch_shapes=[pltpu.VMEM((N_BRANCH, d, tn), BF16), pltpu.VMEM((N_BRANCH, bw, tn), BF16)],
        compiler_params=_params("arbitrary", "arbitrary"),
        name="gate_merge",
    )(h, ys[0], ys[1], ys[2], w_gate, w_gate, w_gate, b_gate, b_gate, b_gate,
      w_branch, w_branch, w_branch)


def _out_proj_kernel(mg_ref, w_ref, x_ref, g_ref, x1_ref, hf_ref, w_scr):
    @pl.when(pl.program_id(0) == 0)
    def _():
        w_scr[...] = w_ref[...].astype(BF16)

    x1 = x_ref[...] + jnp.dot(mg_ref[...], w_scr[...], preferred_element_type=F32)
    x1_ref[...] = x1
    hf_ref[...] = _rms(x1, g_ref[...]).astype(hf_ref.dtype)


def _out_proj(merged, w_out, x2, g, tm=512):
    m, d = x2.shape
    row = lambda i: (i, 0)
    fixed = lambda i: (0, 0)
    return pl.pallas_call(
        _out_proj_kernel,
        out_shape=(jax.ShapeDtypeStruct((m, d), F32), jax.ShapeDtypeStruct((m, d), BF16)),
        grid=(m // tm,),
        in_specs=[pl.BlockSpec((tm, d), row),
                  pl.BlockSpec((d, d), fixed, pipeline_mode=pl.Buffered(1)),
                  pl.BlockSpec((tm, d), row), pl.BlockSpec((1, d), fixed)],
        out_specs=(pl.BlockSpec((tm, d), row), pl.BlockSpec((tm, d), row)),
        scratch_shapes=[pltpu.VMEM((d, d), BF16)],
        compiler_params=_params("arbitrary"),
        name="out_proj",
    )(merged, w_out, x2, g)


def _ffn_up_kernel(hf_ref, wg_ref, wu_ref, a_ref, wg_scr, wu_scr):
    @pl.when(pl.program_id(1) == 0)
    def _():
        wg_scr[...] = wg_ref[...].astype(BF16)
        wu_scr[...] = wu_ref[...].astype(BF16)

    hf = hf_ref[...]
    gate = jnp.dot(hf, wg_scr[...], preferred_element_type=F32)
    up = jnp.dot(hf, wu_scr[...], preferred_element_type=F32)
    a_ref[...] = (gate * _sigmoid(gate) * up).astype(a_ref.dtype)


def _ffn_up(hf, w_in, tm=1024, tf=512):
    m, d = hf.shape
    d_ff = w_in.shape[1] // 2
    nj = d_ff // tf
    return pl.pallas_call(
        _ffn_up_kernel,
        out_shape=jax.ShapeDtypeStruct((m, d_ff), BF16),
        grid=(nj, m // tm),
        in_specs=[pl.BlockSpec((tm, d), lambda j, i: (i, 0)),
                  pl.BlockSpec((d, tf), lambda j, i: (0, j)),
                  pl.BlockSpec((d, tf), lambda j, i: (0, nj + j))],
        out_specs=pl.BlockSpec((tm, tf), lambda j, i: (i, j)),
        scratch_shapes=[pltpu.VMEM((d, tf), BF16), pltpu.VMEM((d, tf), BF16)],
        compiler_params=_params("arbitrary", "arbitrary"),
        name="ffn_up",
    )(hf, w_in, w_in)


def _ffn_down_kernel(a_ref, w_ref, x1_ref, o_ref, w_scr):
    @pl.when(pl.program_id(1) == 0)
    def _():
        w_scr[...] = w_ref[...].astype(BF16)

    o_ref[...] = x1_ref[...] + jnp.dot(a_ref[...], w_scr[...], preferred_element_type=F32)


def _ffn_down(a, w_down, x1, tm=512, tn=512):
    m, d_ff = a.shape
    d = w_down.shape[1]
    return pl.pallas_call(
        _ffn_down_kernel,
        out_shape=jax.ShapeDtypeStruct((m, d), F32),
        grid=(d // tn, m // tm),
        in_specs=[pl.BlockSpec((tm, d_ff), lambda j, i: (i, 0)),
                  pl.BlockSpec((d_ff, tn), lambda j, i: (0, j)),
                  pl.BlockSpec((tm, tn), lambda j, i: (i, j))],
        out_specs=pl.BlockSpec((tm, tn), lambda j, i: (i, j)),
        scratch_shapes=[pltpu.VMEM((d_ff, tn), BF16)],
        compiler_params=_params("arbitrary", "arbitrary"),
        name="ffn_down",
    )(a, w_down, x1)


def kernel(x, mem, norm_mix_g, norm_mem_g, w_in, gla_w_alpha_up, gla_b_alpha, gla_norm_g,
           diff_q_norm_g, diff_k_norm_g, diff_lambda_q1, diff_lambda_k1, diff_lambda_q2,
           diff_lambda_k2, diff_subln_g, mem_q_norm_g, mem_k_norm_g, w_mem_kv, w_branch,
           w_gate, b_gate, w_out, norm_ffn_g, w_ffn_in, w_ffn_down):
    batch, seq, d = x.shape
    n_mem = mem.shape[1]
    depth = w_in.shape[0]
    assert depth == 1, "LAM_INIT is the layer-0 value"
    x2 = x.reshape(batch * seq, d)
    mem2 = mem.reshape(batch * n_mem, d)
    for l in range(depth):
        w_in_t = w_in[l].T
        w_up = jnp.pad(gla_w_alpha_up[l], ((0, LANES - GLA_RANK), (0, 0)))
        row = lambda v: v.reshape(1, -1)

        h, log_a = _norm_mix(x2, row(norm_mix_g[l]), w_in_t, w_up, row(gla_b_alpha[l]))
        z = _in_proj(h, w_in_t, row(diff_q_norm_g[l]), row(diff_k_norm_g[l]), row(mem_q_norm_g[l]))
        y_gla = _gla(z, log_a, row(gla_norm_g[l]), batch, seq)
        y_diff = _diff_attn(z, row(diff_lambda_q1[l]), row(diff_lambda_k1[l]),
                            row(diff_lambda_q2[l]), row(diff_lambda_k2[l]),
                            row(diff_subln_g[l]), batch, seq)
        mem_n = _norm_rows(mem2, row(norm_mem_g[l]))
        kv = _mem_kv(mem_n, w_mem_kv[l], row(mem_k_norm_g[l]))
        y_mem = _mem_attn(z, kv, batch, seq, n_mem)
        merged = _gate_merge(h, (y_gla, y_diff, y_mem), w_gate[l], row(b_gate[l]), w_branch[l])
        x1, hf = _out_proj(merged, w_out[l], x2, row(norm_ffn_g[l]))
        a = _ffn_up(hf, w_ffn_in[l])
        x2 = _ffn_down(a, w_ffn_down[l], x1)
    return x2.reshape(batch, seq, d)
```

```python
import functools

import jax
import jax.numpy as jnp
from jax import lax
from jax.experimental import pallas as pl
from jax.experimental.pallas import tpu as pltpu

F32 = jnp.float32
BF16 = jnp.bfloat16

CHUNK = 64
GLA_HEADS = 4
GLA_DK = 128
GLA_DV = 256
GLA_RANK = 16
GLA_GATE_NORM = 16.0
DIFF_HEADS = 4
DIFF_DH = 128
DIFF_DV = 256
MEM_HEADS = 4
MEM_DH = 256
N_BRANCH = 3
NORM_EPS = 1e-6
NEG_INF = -1e30
LAM_INIT = 0.8 - 0.6 * 1.0
LOG2_E = 1.4426950408889634

LANES = 128
VMEM_LIMIT = 56 * 1024 * 1024

Z_GLA_Q, Z_GLA_K, Z_GLA_V, Z_GLA_G = 0, 512, 1024, 2048
Z_DIFF_Q, Z_DIFF_K, Z_DIFF_V, Z_MEM_Q = 3072, 4096, 5120, 6144
Z_WIDTH = 7168
IN_TILE = 1024


def _params(*sem):
    return pltpu.CompilerParams(dimension_semantics=sem, vmem_limit_bytes=VMEM_LIMIT)


def _nt_dot(a, b):
    return lax.dot_general(a, b, (((1,), (1,)), ((), ())), preferred_element_type=F32)


def _tn_dot(a, b, precision=None):
    return lax.dot_general(a, b, (((0,), (0,)), ((), ())), preferred_element_type=F32,
                           precision=precision)


def _rms(v, gain):
    ms = jnp.mean(v * v, axis=-1, keepdims=True)
    return v * lax.rsqrt(ms + NORM_EPS) * gain


def _sigmoid(v):
    return 1.0 / (1.0 + jnp.exp(-v))


BF16_SUBLANES = 16


def _rider_specs(weights, n_steps, step_of):
    specs, shapes = [], []
    for w in weights:
        rows, cols = w.shape
        chunk = BF16_SUBLANES
        while rows % chunk or rows // chunk > n_steps:
            chunk += BF16_SUBLANES
        last = rows // chunk - 1
        specs.append(pl.BlockSpec((chunk, cols),
                                  lambda *g, last=last: (jnp.minimum(step_of(*g), last), 0)))
        shapes.append(jax.ShapeDtypeStruct((rows, cols), BF16))
    return specs, shapes


def _cast_riders(in_refs, out_refs):
    for src, dst in zip(in_refs, out_refs):
        dst[...] = src[...].astype(BF16)


def _norm_mix_kernel(x_ref, g_ref, wa_ref, wup_ref, bal_ref, h_ref, la_ref):
    h = _rms(x_ref[...], g_ref[...]).astype(BF16)
    h_ref[...] = h
    a_low = _nt_dot(h, wa_ref[...].astype(BF16))
    w_up = wup_ref[...]
    a_hi, w_hi = a_low.astype(BF16), w_up.astype(BF16)
    a_lo = (a_low - a_hi.astype(F32)).astype(BF16)
    w_lo = (w_up - w_hi.astype(F32)).astype(BF16)
    pre = (jnp.dot(a_hi, w_hi, preferred_element_type=F32)
           + (jnp.dot(a_lo, w_hi, preferred_element_type=F32)
              + jnp.dot(a_hi, w_lo, preferred_element_type=F32))) + bal_ref[...]
    log_sig = jnp.minimum(pre, 0.0) - jnp.log1p(jnp.exp(-jnp.abs(pre)))
    la_ref[...] = log_sig * (1.0 / GLA_GATE_NORM)


def _norm_mix(x2, g, w_in_t, wup, bal, tr=512):
    m, d = x2.shape
    n = wup.shape[1]
    fixed = lambda i: (0, 0)
    decay_block = (Z_GLA_G + GLA_HEADS * GLA_DV) // LANES
    return pl.pallas_call(
        _norm_mix_kernel,
        out_shape=(jax.ShapeDtypeStruct((m, d), BF16), jax.ShapeDtypeStruct((m, n), F32)),
        grid=(m // tr,),
        in_specs=[pl.BlockSpec((tr, d), lambda i: (i, 0)),
                  pl.BlockSpec((1, d), fixed),
                  pl.BlockSpec((LANES, d), lambda i: (decay_block, 0)),
                  pl.BlockSpec((LANES, n), fixed),
                  pl.BlockSpec((1, n), fixed)],
        out_specs=(pl.BlockSpec((tr, d), lambda i: (i, 0)),
                   pl.BlockSpec((tr, n), lambda i: (i, 0))),
        compiler_params=_params("parallel"),
        name="norm_mix",
    )(x2, g, w_in_t, wup, bal)


def _norm_rows_kernel(x_ref, g_ref, h_ref):
    h_ref[...] = _rms(x_ref[...], g_ref[...]).astype(BF16)


def _norm_rows(x2, g, tr=512):
    m, d = x2.shape
    return pl.pallas_call(
        _norm_rows_kernel,
        out_shape=jax.ShapeDtypeStruct((m, d), BF16),
        grid=(m // tr,),
        in_specs=[pl.BlockSpec((tr, d), lambda i: (i, 0)),
                  pl.BlockSpec((1, d), lambda i: (0, 0))],
        out_specs=pl.BlockSpec((tr, d), lambda i: (i, 0)),
        compiler_params=_params("parallel"),
        name="norm_rows",
    )(x2, g)


def _store_group_norm(acc, gain, width, scale, out_ref):
    for s in range(0, acc.shape[1], width):
        blk = acc[:, s:s + width]
        out_ref[:, s:s + width] = (_rms(blk, gain) * scale).astype(out_ref.dtype)


def _in_proj_kernel(h_ref, w_lo_ref, w_hi_ref, dq_g_ref, dk_g_ref, mq_g_ref, r0_ref, r1_ref,
                    z_ref, c0_ref, c1_ref, w_scr):
    _cast_riders((r0_ref, r1_ref), (c0_ref, c1_ref))
    j = pl.program_id(0)
    first_shifted = Z_DIFF_Q // IN_TILE

    @pl.when((pl.program_id(1) == 0) & (j < first_shifted))
    def _():
        w_scr[...] = w_lo_ref[...].astype(BF16)

    @pl.when((pl.program_id(1) == 0) & (j >= first_shifted))
    def _():
        w_scr[:IN_TILE - GLA_RANK, :] = w_lo_ref[GLA_RANK:, :].astype(BF16)
        w_scr[IN_TILE - GLA_RANK:, :] = w_hi_ref[...].astype(BF16)

    acc = _nt_dot(h_ref[...], w_scr[...])
    j_dq, j_dk, j_mq = Z_DIFF_Q // IN_TILE, Z_DIFF_K // IN_TILE, Z_MEM_Q // IN_TILE

    @pl.when((j != j_dq) & (j != j_dk) & (j != j_mq))
    def _():
        z_ref[...] = acc.astype(z_ref.dtype)

    @pl.when(j == j_dq)
    def _():
        _store_group_norm(acc, dq_g_ref[...], DIFF_DH, DIFF_DH ** -0.5 * LOG2_E, z_ref)

    @pl.when(j == j_dk)
    def _():
        _store_group_norm(acc, dk_g_ref[...], DIFF_DH, 1.0, z_ref)

    @pl.when(j == j_mq)
    def _():
        _store_group_norm(acc, mq_g_ref[...], MEM_DH, MEM_DH ** -0.5, z_ref)


def _in_proj(h, w_in_t, dq_g, dk_g, mq_g, riders, tm=1024):
    m, d = h.shape
    assert w_in_t.shape[0] == Z_WIDTH + GLA_RANK
    hi_per_tile = IN_TILE // GLA_RANK
    nj, ni = Z_WIDTH // IN_TILE, m // tm
    r_specs, r_shapes = _rider_specs(riders, nj * ni, lambda j, i: j * ni + i)
    return pl.pallas_call(
        _in_proj_kernel,
        out_shape=(jax.ShapeDtypeStruct((m, Z_WIDTH), BF16), *r_shapes),
        grid=(nj, ni),
        in_specs=[pl.BlockSpec((tm, d), lambda j, i: (i, 0)),
                  pl.BlockSpec((IN_TILE, d), lambda j, i: (j, 0)),
                  pl.BlockSpec((GLA_RANK, d), lambda j, i: ((j + 1) * hi_per_tile, 0)),
                  pl.BlockSpec((1, DIFF_DH), lambda j, i: (0, 0)),
                  pl.BlockSpec((1, DIFF_DH), lambda j, i: (0, 0)),
                  pl.BlockSpec((1, MEM_DH), lambda j, i: (0, 0)),
                  *r_specs],
        out_specs=(pl.BlockSpec((tm, IN_TILE), lambda j, i: (i, j)), *r_specs),
        scratch_shapes=[pltpu.VMEM((IN_TILE, d), BF16)],
        compiler_params=_params("arbitrary", "arbitrary"),
        name="in_proj",
    )(h, w_in_t, w_in_t, dq_g, dk_g, mq_g, *riders)


def _chunk_cumsum(x):
    row_in_chunk = lax.broadcasted_iota(jnp.int32, x.shape, 0) % CHUNK
    shift = 1
    while shift < CHUNK:
        x = x + jnp.where(row_in_chunk >= shift, pltpu.roll(x, shift, 0), 0.0)
        shift *= 2
    return x


def _gla_kernel(q_ref, k_ref, v_ref, g_ref, la_ref, ng_ref, o_ref, s_ref, *, ts):
    @pl.when(pl.program_id(2) == 0)
    def _():
        s_ref[...] = jnp.zeros_like(s_ref)

    bcum_all = _chunk_cumsum(la_ref[...])
    row = lax.broadcasted_iota(jnp.int32, (CHUNK, CHUNK), 0)
    col = lax.broadcasted_iota(jnp.int32, (CHUNK, CHUNK), 1)
    causal = row >= col

    for c in range(ts // CHUNK):
        rows = slice(c * CHUNK, (c + 1) * CHUNK)
        bcum = bcum_all[rows]
        b_last = bcum[CHUNK - 1:CHUNK, :]
        q = q_ref[rows, :].astype(F32) * (GLA_DK ** -0.5)
        k = k_ref[rows, :].astype(F32)
        v = v_ref[rows, :]
        q_dec = (q * jnp.exp(bcum)).astype(BF16)
        k_dec = (k * jnp.exp(-bcum)).astype(BF16)
        k_tail = (k * jnp.exp(b_last - bcum)).astype(BF16)
        att = jnp.where(causal, _nt_dot(q_dec, k_dec), 0.0).astype(BF16)
        state = s_ref[...]
        o = jnp.dot(jnp.concatenate([q_dec, att], axis=1),
                    jnp.concatenate([state.astype(BF16), v], axis=0), preferred_element_type=F32)
        inc = _tn_dot(k_tail, v)
        decay_col = jnp.broadcast_to(jnp.exp(b_last), (GLA_DK, GLA_DK)).T
        decay = jnp.concatenate([decay_col] * (GLA_DV // GLA_DK), axis=1)
        s_ref[...] = decay * state + inc
        gate = g_ref[rows, :].astype(F32)
        o_ref[rows, :] = (_rms(o, ng_ref[...]) * (gate * _sigmoid(gate))).astype(o_ref.dtype)


def _gla(z, log_a, ng, batch, seq, ts=512):
    m = z.shape[0]
    nt = seq // ts
    rows = lambda b, h, t: b * nt + t
    return pl.pallas_call(
        functools.partial(_gla_kernel, ts=ts),
        out_shape=jax.ShapeDtypeStruct((m, GLA_HEADS * GLA_DV), BF16),
        grid=(batch, GLA_HEADS, nt),
        in_specs=[pl.BlockSpec((ts, GLA_DK), lambda b, h, t: (rows(b, h, t), Z_GLA_Q // GLA_DK + h)),
                  pl.BlockSpec((ts, GLA_DK), lambda b, h, t: (rows(b, h, t), Z_GLA_K // GLA_DK + h)),
                  pl.BlockSpec((ts, GLA_DV), lambda b, h, t: (rows(b, h, t), Z_GLA_V // GLA_DV + h)),
                  pl.BlockSpec((ts, GLA_DV), lambda b, h, t: (rows(b, h, t), Z_GLA_G // GLA_DV + h)),
                  pl.BlockSpec((ts, GLA_DK), lambda b, h, t: (rows(b, h, t), h)),
                  pl.BlockSpec((1, GLA_DV), lambda b, h, t: (0, 0))],
        out_specs=pl.BlockSpec((ts, GLA_DV), lambda b, h, t: (rows(b, h, t), h)),
        scratch_shapes=[pltpu.VMEM((GLA_DK, GLA_DV), F32)],
        compiler_params=_params("parallel", "parallel", "arbitrary"),
        name="gla",
    )(z, z, z, z, log_a, ng)


def _diff_kernel(q_ref, k_ref, v_ref, lq1_ref, lk1_ref, lq2_ref, lk2_ref, sg_ref, o_ref,
                 s_scr, p_scr, acc_scr, m_scr, l_scr, a_scr, *, tq):
    qi = pl.program_id(2)
    n_lane_tiles = tq // LANES
    m_scr[...] = jnp.full_like(m_scr, NEG_INF)
    l_scr[...] = jnp.zeros_like(l_scr)
    acc_scr[...] = jnp.zeros_like(acc_scr)
    first_half = lax.broadcasted_iota(jnp.int32, (CHUNK, LANES), 1) < CHUNK

    def softmax_rows(slot, comp, r, diag):
        rows = slice(r * CHUNK, (r + 1) * CHUNK)
        n_vis = r // 2 + 1 if diag else n_lane_tiles
        parts = []
        for t in range(n_vis):
            part = s_scr[slot, comp, rows, t * LANES:(t + 1) * LANES]
            if diag and 2 * t == r:
                part = jnp.where(first_half, part, NEG_INF)
            parts.append(part)
        tile_max = functools.reduce(jnp.maximum, parts)
        m_old = m_scr[comp, rows, :]
        m_new = jnp.maximum(m_old, jnp.max(tile_max, axis=-1, keepdims=True))
        alpha = jnp.exp2(m_old - m_new)
        ps = [jnp.exp2(part - m_new) for part in parts]
        l_cur = jnp.sum(functools.reduce(jnp.add, ps), axis=-1, keepdims=True)
        l_scr[comp, rows, :] = alpha * l_scr[comp, rows, :] + l_cur
        m_scr[comp, rows, :] = m_new
        a_scr[comp, rows, :] = alpha
        for t in range(n_lane_tiles):
            tile = ps[t].astype(BF16) if t < n_vis else jnp.zeros((CHUNK, LANES), BF16)
            p_scr[comp, rows, t * LANES:(t + 1) * LANES] = tile

    def scores(kb, slot):
        start = pl.multiple_of(kb * tq, tq)
        for comp in range(2):
            cols = slice(comp * DIFF_DH, (comp + 1) * DIFF_DH)
            s_scr[slot, comp] = _nt_dot(q_ref[:, cols], k_ref[pl.ds(start, tq), cols])

    def accumulate(kb, slot, diag):
        start = pl.multiple_of(kb * tq, tq)
        v = v_ref[pl.ds(start, tq), :]
        for comp in range(2):
            for r in range(tq // CHUNK):
                softmax_rows(slot, comp, r, diag)
            alpha = a_scr[comp]
            acc_scr[comp] = (jnp.concatenate([alpha] * (DIFF_DV // LANES), axis=1) * acc_scr[comp]
                             + jnp.dot(p_scr[comp], v, preferred_element_type=F32))

    def full_block(kb, carry):
        scores(kb, 0)
        accumulate(kb, 0, diag=False)
        return carry

    lax.fori_loop(0, qi, full_block, 0)
    scores(qi, 0)
    accumulate(qi, 0, diag=True)

    lam = (jnp.exp(jnp.sum(lq1_ref[...] * lk1_ref[...], axis=-1, keepdims=True))
           - jnp.exp(jnp.sum(lq2_ref[...] * lk2_ref[...], axis=-1, keepdims=True)) + LAM_INIT)
    widen = lambda v: jnp.concatenate([v] * (DIFF_DV // LANES), axis=1)
    o = acc_scr[0] / widen(l_scr[0]) - lam * (acc_scr[1] / widen(l_scr[1]))
    o_ref[...] = (_rms(o, sg_ref[...]) * (1.0 - LAM_INIT)).astype(o_ref.dtype)


def _diff_attn(z, lq1, lk1, lq2, lk2, sg, batch, seq, tq=512):
    m = z.shape[0]
    nq = seq // tq
    vec = pl.BlockSpec((1, DIFF_DH), lambda b, h, i: (0, 0))
    return pl.pallas_call(
        functools.partial(_diff_kernel, tq=tq),
        out_shape=jax.ShapeDtypeStruct((m, DIFF_HEADS * DIFF_DV), BF16),
        grid=(batch, DIFF_HEADS, nq),
        in_specs=[pl.BlockSpec((tq, DIFF_DV), lambda b, h, i: (b * nq + i, Z_DIFF_Q // DIFF_DV + h)),
                  pl.BlockSpec((seq, DIFF_DV), lambda b, h, i: (b, Z_DIFF_K // DIFF_DV + h)),
                  pl.BlockSpec((seq, DIFF_DV), lambda b, h, i: (b, Z_DIFF_V // DIFF_DV + h)),
                  vec, vec, vec, vec,
                  pl.BlockSpec((1, DIFF_DV), lambda b, h, i: (0, 0))],
        out_specs=pl.BlockSpec((tq, DIFF_DV), lambda b, h, i: (b * nq + i, h)),
        scratch_shapes=[pltpu.VMEM((1, 2, tq, tq), F32),
                        pltpu.VMEM((2, tq, tq), BF16),
                        pltpu.VMEM((2, tq, DIFF_DV), F32),
                        pltpu.VMEM((2, tq, LANES), F32),
                        pltpu.VMEM((2, tq, LANES), F32),
                        pltpu.VMEM((2, tq, LANES), F32)],
        compiler_params=_params("parallel", "parallel", "arbitrary"),
        name="diff_attn",
    )(z, z, z, lq1, lk1, lq2, lk2, sg)


def _mem_kv_kernel(mn_ref, w_ref, kg_ref, kv_ref, *, n_key_tiles):
    j = pl.program_id(0)
    acc = jnp.dot(mn_ref[...], w_ref[...].astype(BF16), preferred_element_type=F32)

    @pl.when(j < n_key_tiles)
    def _():
        _store_group_norm(acc, kg_ref[...], MEM_DH, 1.0, kv_ref)

    @pl.when(j >= n_key_tiles)
    def _():
        kv_ref[...] = acc.astype(kv_ref.dtype)


def _mem_kv(mem_n, w_kv, kg, tn=512):
    m, d = mem_n.shape
    n = w_kv.shape[1]
    return pl.pallas_call(
        functools.partial(_mem_kv_kernel, n_key_tiles=(n // 2) // tn),
        out_shape=jax.ShapeDtypeStruct((m, n), BF16),
        grid=(n // tn,),
        in_specs=[pl.BlockSpec((m, d), lambda j: (0, 0)),
                  pl.BlockSpec((d, tn), lambda j: (0, j)),
                  pl.BlockSpec((1, MEM_DH), lambda j: (0, 0))],
        out_specs=pl.BlockSpec((m, tn), lambda j: (0, j)),
        compiler_params=_params("parallel"),
        name="mem_kv",
    )(mem_n, w_kv, kg)


def _mem_attn_kernel(q_ref, k_ref, v_ref, o_ref):
    for head in range(MEM_HEADS):
        cols = slice(head * MEM_DH, (head + 1) * MEM_DH)
        s = _nt_dot(q_ref[:, cols], k_ref[:, cols])
        e = jnp.exp(s - jnp.max(s, axis=-1, keepdims=True))
        p = (e / jnp.sum(e, axis=-1, keepdims=True)).astype(BF16)
        o_ref[:, cols] = jnp.dot(p, v_ref[:, cols], preferred_element_type=F32).astype(o_ref.dtype)


def _mem_attn(z, kv, batch, seq, n_mem, tm=512):
    m = z.shape[0]
    nt = seq // tm
    width = MEM_HEADS * MEM_DH
    return pl.pallas_call(
        _mem_attn_kernel,
        out_shape=jax.ShapeDtypeStruct((m, width), BF16),
        grid=(batch, nt),
        in_specs=[pl.BlockSpec((tm, width), lambda b, t: (b * nt + t, Z_MEM_Q // width)),
                  pl.BlockSpec((n_mem, width), lambda b, t: (b, 0)),
                  pl.BlockSpec((n_mem, width), lambda b, t: (b, 1))],
        out_specs=pl.BlockSpec((tm, width), lambda b, t: (b * nt + t, 0)),
        compiler_params=_params("parallel", "parallel"),
        name="mem_attn",
    )(z, kv, kv)


def _gate_merge_kernel(h_ref, y0_ref, y1_ref, y2_ref, wg0_ref, wg1_ref, wg2_ref,
                       bg0_ref, bg1_ref, bg2_ref, wb0_ref, wb1_ref, wb2_ref, r0_ref, r1_ref,
                       o_ref, c0_ref, c1_ref):
    _cast_riders((r0_ref, r1_ref), (c0_ref, c1_ref))
    h = h_ref[...]
    merged = None
    for y_ref, wg_ref, bg_ref, wb_ref in ((y0_ref, wg0_ref, bg0_ref, wb0_ref),
                                          (y1_ref, wg1_ref, bg1_ref, wb1_ref),
                                          (y2_ref, wg2_ref, bg2_ref, wb2_ref)):
        gate = _sigmoid(jnp.dot(h, wg_ref[...], preferred_element_type=F32) + bg_ref[...])
        term = gate * jnp.dot(y_ref[...], wb_ref[...], preferred_element_type=F32)
        merged = term if merged is None else merged + term
    o_ref[...] = merged.astype(o_ref.dtype)


def _gate_merge(h, ys, w_gate, b_gate, w_branch, riders, tm=1024, tn=256):
    m, d = h.shape
    bw = w_branch.shape[1]
    ni, nj = m // tm, d // tn
    act = lambda width: pl.BlockSpec((tm, width), lambda i, j: (i, 0))
    wg = lambda b: pl.BlockSpec((d, tn), lambda i, j: (0, b * nj + j))
    bg = lambda b: pl.BlockSpec((1, tn), lambda i, j: (0, b * nj + j))
    wb = lambda b: pl.BlockSpec((None, bw, tn), lambda i, j: (b, 0, j))
    r_specs, r_shapes = _rider_specs(riders, ni * nj, lambda i, j: i * nj + j)
    return pl.pallas_call(
        _gate_merge_kernel,
        out_shape=(jax.ShapeDtypeStruct((m, d), BF16), *r_shapes),
        grid=(ni, nj),
        in_specs=[act(d), act(bw), act(bw), act(bw), wg(0), wg(1), wg(2), bg(0), bg(1), bg(2),
                  wb(0), wb(1), wb(2), *r_specs],
        out_specs=(pl.BlockSpec((tm, tn), lambda i, j: (i, j)), *r_specs),
        compiler_params=_params("arbitrary", "arbitrary"),
        name="gate_merge",
    )(h, ys[0], ys[1], ys[2], w_gate, w_gate, w_gate, b_gate, b_gate, b_gate,
      w_branch, w_branch, w_branch, *riders)


def _out_proj_kernel(mg_ref, w_ref, x_ref, g_ref, x1_ref, hf_ref):
    x1 = x_ref[...] + jnp.dot(mg_ref[...], w_ref[...], preferred_element_type=F32)
    x1_ref[...] = x1
    hf_ref[...] = _rms(x1, g_ref[...]).astype(hf_ref.dtype)


def _out_proj(merged, w_out, x2, g, tm=512):
    m, d = x2.shape
    row = lambda i: (i, 0)
    fixed = lambda i: (0, 0)
    return pl.pallas_call(
        _out_proj_kernel,
        out_shape=(jax.ShapeDtypeStruct((m, d), F32), jax.ShapeDtypeStruct((m, d), BF16)),
        grid=(m // tm,),
        in_specs=[pl.BlockSpec((tm, d), row),
                  pl.BlockSpec((d, d), fixed, pipeline_mode=pl.Buffered(1)),
                  pl.BlockSpec((tm, d), row), pl.BlockSpec((1, d), fixed)],
        out_specs=(pl.BlockSpec((tm, d), row), pl.BlockSpec((tm, d), row)),
        compiler_params=_params("parallel"),
        name="out_proj",
    )(merged, w_out, x2, g)


def _ffn_up_kernel(hf_ref, wg_ref, wu_ref, r0_ref, a_ref, c0_ref):
    _cast_riders((r0_ref,), (c0_ref,))
    hf = hf_ref[...]
    gate = jnp.dot(hf, wg_ref[...], preferred_element_type=F32)
    up = jnp.dot(hf, wu_ref[...], preferred_element_type=F32)
    a_ref[...] = (gate * _sigmoid(gate) * up).astype(a_ref.dtype)


def _ffn_up(hf, w_in, riders, tm=1024, tf=512):
    m, d = hf.shape
    d_ff = w_in.shape[1] // 2
    ni, nj = m // tm, d_ff // tf
    r_specs, r_shapes = _rider_specs(riders, ni * nj, lambda i, j: i * nj + j)
    return pl.pallas_call(
        _ffn_up_kernel,
        out_shape=(jax.ShapeDtypeStruct((m, d_ff), BF16), *r_shapes),
        grid=(ni, nj),
        in_specs=[pl.BlockSpec((tm, d), lambda i, j: (i, 0)),
                  pl.BlockSpec((d, tf), lambda i, j: (0, j)),
                  pl.BlockSpec((d, tf), lambda i, j: (0, nj + j)),
                  *r_specs],
        out_specs=(pl.BlockSpec((tm, tf), lambda i, j: (i, j)), *r_specs),
        compiler_params=_params("arbitrary", "arbitrary"),
        name="ffn_up",
    )(hf, w_in, w_in, *riders)


def _ffn_down_kernel(a_ref, w_ref, x1_ref, o_ref):
    o_ref[...] = x1_ref[...] + jnp.dot(a_ref[...], w_ref[...], preferred_element_type=F32)


def _ffn_down(a, w_down, x1, tm=1024, tn=512):
    m, d_ff = a.shape
    d = w_down.shape[1]
    return pl.pallas_call(
        _ffn_down_kernel,
        out_shape=jax.ShapeDtypeStruct((m, d), F32),
        grid=(m // tm, d // tn),
        in_specs=[pl.BlockSpec((tm, d_ff), lambda i, j: (i, 0)),
                  pl.BlockSpec((d_ff, tn), lambda i, j: (0, j)),
                  pl.BlockSpec((tm, tn), lambda i, j: (i, j))],
        out_specs=pl.BlockSpec((tm, tn), lambda i, j: (i, j)),
        compiler_params=_params("parallel", "arbitrary"),
        name="ffn_down",
    )(a, w_down, x1)


def kernel(x, mem, norm_mix_g, norm_mem_g, w_in, gla_w_alpha_up, gla_b_alpha, gla_norm_g,
           diff_q_norm_g, diff_k_norm_g, diff_lambda_q1, diff_lambda_k1, diff_lambda_q2,
           diff_lambda_k2, diff_subln_g, mem_q_norm_g, mem_k_norm_g, w_mem_kv, w_branch,
           w_gate, b_gate, w_out, norm_ffn_g, w_ffn_in, w_ffn_down):
    batch, seq, d = x.shape
    n_mem = mem.shape[1]
    depth = w_in.shape[0]
    assert depth == 1, "LAM_INIT is the layer-0 value"
    x2 = x.reshape(batch * seq, d)
    mem2 = mem.reshape(batch * n_mem, d)
    for l in range(depth):
        w_in_t = w_in[l].T
        w_up = jnp.pad(gla_w_alpha_up[l], ((0, LANES - GLA_RANK), (0, 0)))
        row = lambda v: v.reshape(1, -1)

        h, log_a = _norm_mix(x2, row(norm_mix_g[l]), w_in_t, w_up, row(gla_b_alpha[l]))
        z, w_gate_bf, w_branch_bf = _in_proj(
            h, w_in_t, row(diff_q_norm_g[l]), row(diff_k_norm_g[l]), row(mem_q_norm_g[l]),
            riders=(w_gate[l], w_branch[l].reshape(-1, d)))
        y_gla = _gla(z, log_a, row(gla_norm_g[l]), batch, seq)
        y_diff = _diff_attn(z, row(diff_lambda_q1[l]), row(diff_lambda_k1[l]),
                            row(diff_lambda_q2[l]), row(diff_lambda_k2[l]),
                            row(diff_subln_g[l]), batch, seq)
        mem_n = _norm_rows(mem2, row(norm_mem_g[l]))
        kv = _mem_kv(mem_n, w_mem_kv[l], row(mem_k_norm_g[l]))
        y_mem = _mem_attn(z, kv, batch, seq, n_mem)
        merged, w_out_bf, w_ffn_in_bf = _gate_merge(
            h, (y_gla, y_diff, y_mem), w_gate_bf, row(b_gate[l]),
            w_branch_bf.reshape(w_branch[l].shape), riders=(w_out[l], w_ffn_in[l]))
        x1, hf = _out_proj(merged, w_out_bf, x2, row(norm_ffn_g[l]))
        a, w_ffn_down_bf = _ffn_up(hf, w_ffn_in_bf, riders=(w_ffn_down[l],))
        x2 = _ffn_down(a, w_ffn_down_bf, x1)
    return x2.reshape(batch, seq, d)
```

```python
import functools

import jax
import jax.numpy as jnp
from jax import lax
from jax.experimental import pallas as pl
from jax.experimental.pallas import tpu as pltpu

F32 = jnp.float32
BF16 = jnp.bfloat16

CHUNK = 64
GLA_HEADS = 4
GLA_DK = 128
GLA_DV = 256
GLA_RANK = 16
GLA_GATE_NORM = 16.0
DIFF_HEADS = 4
DIFF_DH = 128
DIFF_DV = 256
MEM_HEADS = 4
MEM_DH = 256
N_BRANCH = 3
NORM_EPS = 1e-6
NEG_INF = -1e30
LAM_INIT = 0.8 - 0.6 * 1.0
LOG2_E = 1.4426950408889634

LANES = 128
VMEM_LIMIT = 56 * 1024 * 1024

IN_TILE = 1024
Z_GLA_Q, Z_GLA_K, Z_GLA_V, Z_GLA_G = 0, 512, 1024, 2048
Z_DIFF_Q, Z_DIFF_K, Z_MEM_Q = 3072, 4096, 5120
Z_WIDTH = 6144
W_SRC_TILES = (0, 1, 2, 3, 4, 6)
W_DIFF_V_TILE = 5
W_ROWS = 7 * IN_TILE + GLA_RANK
W_FIRST_SHIFTED_TILE = 3


def _params(*sem):
    return pltpu.CompilerParams(dimension_semantics=sem, vmem_limit_bytes=VMEM_LIMIT)


def _nt_dot(a, b):
    return lax.dot_general(a, b, (((1,), (1,)), ((), ())), preferred_element_type=F32)


def _tn_dot(a, b, precision=None):
    return lax.dot_general(a, b, (((0,), (0,)), ((), ())), preferred_element_type=F32,
                           precision=precision)


def _rms(v, gain):
    ms = jnp.mean(v * v, axis=-1, keepdims=True)
    return v * lax.rsqrt(ms + NORM_EPS) * gain


def _sigmoid(v):
    return 1.0 / (1.0 + jnp.exp(-v))


BF16_SUBLANES = 16


def _rider_specs(weights, n_steps, step_of):
    specs, shapes = [], []
    for w in weights:
        rows, cols = w.shape
        chunk = BF16_SUBLANES
        while rows % chunk or rows // chunk > n_steps:
            chunk += BF16_SUBLANES
        last = rows // chunk - 1
        specs.append(pl.BlockSpec((chunk, cols),
                                  lambda *g, last=last: (jnp.minimum(step_of(*g), last), 0)))
        shapes.append(jax.ShapeDtypeStruct((rows, cols), BF16))
    return specs, shapes


def _cast_riders(in_refs, out_refs):
    for src, dst in zip(in_refs, out_refs):
        dst[...] = src[...].astype(BF16)


def _norm_mix_kernel(x_ref, g_ref, wa_ref, wup_ref, bal_ref, h_ref, la_ref):
    h = _rms(x_ref[...], g_ref[...]).astype(BF16)
    h_ref[...] = h
    a_low = _nt_dot(h, wa_ref[...].astype(BF16))
    w_up = wup_ref[...]
    a_hi, w_hi = a_low.astype(BF16), w_up.astype(BF16)
    a_lo = (a_low - a_hi.astype(F32)).astype(BF16)
    w_lo = (w_up - w_hi.astype(F32)).astype(BF16)
    pre = (jnp.dot(a_hi, w_hi, preferred_element_type=F32)
           + (jnp.dot(a_lo, w_hi, preferred_element_type=F32)
              + jnp.dot(a_hi, w_lo, preferred_element_type=F32))) + bal_ref[...]
    log_sig = jnp.minimum(pre, 0.0) - jnp.log1p(jnp.exp(-jnp.abs(pre)))
    la_ref[...] = log_sig * (1.0 / GLA_GATE_NORM)


def _norm_mix(x2, g, w_in_t, wup, bal, tr=512):
    m, d = x2.shape
    n = wup.shape[1]
    fixed = lambda i: (0, 0)
    decay_block = (Z_GLA_G + GLA_HEADS * GLA_DV) // LANES
    return pl.pallas_call(
        _norm_mix_kernel,
        out_shape=(jax.ShapeDtypeStruct((m, d), BF16), jax.ShapeDtypeStruct((m, n), F32)),
        grid=(m // tr,),
        in_specs=[pl.BlockSpec((tr, d), lambda i: (i, 0)),
                  pl.BlockSpec((1, d), fixed),
                  pl.BlockSpec((LANES, d), lambda i: (decay_block, 0)),
                  pl.BlockSpec((LANES, n), fixed),
                  pl.BlockSpec((1, n), fixed)],
        out_specs=(pl.BlockSpec((tr, d), lambda i: (i, 0)),
                   pl.BlockSpec((tr, n), lambda i: (i, 0))),
        compiler_params=_params("parallel"),
        name="norm_mix",
    )(x2, g, w_in_t, wup, bal)


def _norm_rows_kernel(x_ref, g_ref, h_ref):
    h_ref[...] = _rms(x_ref[...], g_ref[...]).astype(BF16)


def _norm_rows(x2, g, tr=512):
    m, d = x2.shape
    return pl.pallas_call(
        _norm_rows_kernel,
        out_shape=jax.ShapeDtypeStruct((m, d), BF16),
        grid=(m // tr,),
        in_specs=[pl.BlockSpec((tr, d), lambda i: (i, 0)),
                  pl.BlockSpec((1, d), lambda i: (0, 0))],
        out_specs=pl.BlockSpec((tr, d), lambda i: (i, 0)),
        compiler_params=_params("parallel"),
        name="norm_rows",
    )(x2, g)


def _store_group_norm(acc, gain, width, scale, out_ref):
    for s in range(0, acc.shape[1], width):
        blk = acc[:, s:s + width]
        out_ref[:, s:s + width] = (_rms(blk, gain) * scale).astype(out_ref.dtype)


def _w_tile_specs(d, src_tile):
    hi_per_tile = IN_TILE // GLA_RANK
    return [pl.BlockSpec((IN_TILE, d), lambda *g: (src_tile(*g), 0)),
            pl.BlockSpec((GLA_RANK, d), lambda *g: ((src_tile(*g) + 1) * hi_per_tile, 0))]


def _cast_w_tile(w_lo_ref, w_hi_ref, w_scr, first_step, shifted):
    @pl.when(first_step & jnp.logical_not(shifted))
    def _():
        w_scr[...] = w_lo_ref[...].astype(BF16)

    @pl.when(first_step & shifted)
    def _():
        w_scr[:IN_TILE - GLA_RANK, :] = w_lo_ref[GLA_RANK:, :].astype(BF16)
        w_scr[IN_TILE - GLA_RANK:, :] = w_hi_ref[...].astype(BF16)


def _in_proj_kernel(h_ref, w_lo_ref, w_hi_ref, dq_g_ref, dk_g_ref, mq_g_ref, r0_ref, r1_ref,
                    z_ref, c0_ref, c1_ref, w_scr):
    _cast_riders((r0_ref, r1_ref), (c0_ref, c1_ref))
    j = pl.program_id(0)
    _cast_w_tile(w_lo_ref, w_hi_ref, w_scr, pl.program_id(1) == 0, j >= W_FIRST_SHIFTED_TILE)
    acc = _nt_dot(h_ref[...], w_scr[...])
    j_dq, j_dk, j_mq = Z_DIFF_Q // IN_TILE, Z_DIFF_K // IN_TILE, Z_MEM_Q // IN_TILE

    @pl.when((j != j_dq) & (j != j_dk) & (j != j_mq))
    def _():
        z_ref[...] = acc.astype(z_ref.dtype)

    @pl.when(j == j_dq)
    def _():
        _store_group_norm(acc, dq_g_ref[...], DIFF_DH, DIFF_DH ** -0.5 * LOG2_E, z_ref)

    @pl.when(j == j_dk)
    def _():
        _store_group_norm(acc, dk_g_ref[...], DIFF_DH, 1.0, z_ref)

    @pl.when(j == j_mq)
    def _():
        _store_group_norm(acc, mq_g_ref[...], MEM_DH, MEM_DH ** -0.5, z_ref)


def _in_proj(h, w_in_t, dq_g, dk_g, mq_g, riders, tm=1024):
    m, d = h.shape
    assert w_in_t.shape[0] == W_ROWS
    nj, ni = Z_WIDTH // IN_TILE, m // tm
    assert W_SRC_TILES == tuple(j + (j >= W_DIFF_V_TILE) for j in range(nj))
    r_specs, r_shapes = _rider_specs(riders, nj * ni, lambda j, i: j * ni + i)
    return pl.pallas_call(
        _in_proj_kernel,
        out_shape=(jax.ShapeDtypeStruct((m, Z_WIDTH), BF16), *r_shapes),
        grid=(nj, ni),
        in_specs=[pl.BlockSpec((tm, d), lambda j, i: (i, 0)),
                  *_w_tile_specs(d, lambda j, i: jnp.where(j >= W_DIFF_V_TILE, j + 1, j)),
                  pl.BlockSpec((1, DIFF_DH), lambda j, i: (0, 0)),
                  pl.BlockSpec((1, DIFF_DH), lambda j, i: (0, 0)),
                  pl.BlockSpec((1, MEM_DH), lambda j, i: (0, 0)),
                  *r_specs],
        out_specs=(pl.BlockSpec((tm, IN_TILE), lambda j, i: (i, j)), *r_specs),
        scratch_shapes=[pltpu.VMEM((IN_TILE, d), BF16)],
        compiler_params=_params("arbitrary", "arbitrary"),
        name="in_proj",
    )(h, w_in_t, w_in_t, dq_g, dk_g, mq_g, *riders)


def _v_proj_t_kernel(h_ref, w_lo_ref, w_hi_ref, vt_ref, w_scr):
    _cast_w_tile(w_lo_ref, w_hi_ref, w_scr, pl.program_id(0) == 0,
                 W_DIFF_V_TILE >= W_FIRST_SHIFTED_TILE)
    vt_ref[...] = _nt_dot(w_scr[...], h_ref[...]).astype(vt_ref.dtype)


def _v_proj_t(h, w_in_t, tm=1024):
    m, d = h.shape
    return pl.pallas_call(
        _v_proj_t_kernel,
        out_shape=jax.ShapeDtypeStruct((IN_TILE, m), BF16),
        grid=(m // tm,),
        in_specs=[pl.BlockSpec((tm, d), lambda i: (i, 0)),
                  *_w_tile_specs(d, lambda i: W_DIFF_V_TILE)],
        out_specs=pl.BlockSpec((IN_TILE, tm), lambda i: (0, i)),
        scratch_shapes=[pltpu.VMEM((IN_TILE, d), BF16)],
        compiler_params=_params("arbitrary"),
        name="v_proj_t",
    )(h, w_in_t, w_in_t)


def _chunk_cumsum(x):
    row_in_chunk = lax.broadcasted_iota(jnp.int32, x.shape, 0) % CHUNK
    shift = 1
    while shift < CHUNK:
        x = x + jnp.where(row_in_chunk >= shift, pltpu.roll(x, shift, 0), 0.0)
        shift *= 2
    return x


def _gla_kernel(q_ref, k_ref, v_ref, g_ref, la_ref, ng_ref, o_ref, s_ref, *, ts):
    @pl.when(pl.program_id(2) == 0)
    def _():
        s_ref[...] = jnp.zeros_like(s_ref)

    bcum_all = _chunk_cumsum(la_ref[...])
    row = lax.broadcasted_iota(jnp.int32, (CHUNK, CHUNK), 0)
    col = lax.broadcasted_iota(jnp.int32, (CHUNK, CHUNK), 1)
    causal = row >= col

    for c in range(ts // CHUNK):
        rows = slice(c * CHUNK, (c + 1) * CHUNK)
        bcum = bcum_all[rows]
        b_last = bcum[CHUNK - 1:CHUNK, :]
        q = q_ref[rows, :].astype(F32) * (GLA_DK ** -0.5)
        k = k_ref[rows, :].astype(F32)
        v = v_ref[rows, :]
        q_dec = (q * jnp.exp(bcum)).astype(BF16)
        k_dec = (k * jnp.exp(-bcum)).astype(BF16)
        k_tail = (k * jnp.exp(b_last - bcum)).astype(BF16)
        att = jnp.where(causal, _nt_dot(q_dec, k_dec), 0.0).astype(BF16)
        state = s_ref[...]
        o = jnp.dot(jnp.concatenate([q_dec, att], axis=1),
                    jnp.concatenate([state.astype(BF16), v], axis=0), preferred_element_type=F32)
        inc = _tn_dot(k_tail, v)
        decay_col = jnp.broadcast_to(jnp.exp(b_last), (GLA_DK, GLA_DK)).T
        decay = jnp.concatenate([decay_col] * (GLA_DV // GLA_DK), axis=1)
        s_ref[...] = decay * state + inc
        gate = g_ref[rows, :].astype(F32)
        o_ref[rows, :] = (_rms(o, ng_ref[...]) * (gate * _sigmoid(gate))).astype(o_ref.dtype)


def _gla(z, log_a, ng, batch, seq, ts=512):
    m = z.shape[0]
    nt = seq // ts
    rows = lambda b, h, t: b * nt + t
    return pl.pallas_call(
        functools.partial(_gla_kernel, ts=ts),
        out_shape=jax.ShapeDtypeStruct((m, GLA_HEADS * GLA_DV), BF16),
        grid=(batch, GLA_HEADS, nt),
        in_specs=[pl.BlockSpec((ts, GLA_DK), lambda b, h, t: (rows(b, h, t), Z_GLA_Q // GLA_DK + h)),
                  pl.BlockSpec((ts, GLA_DK), lambda b, h, t: (rows(b, h, t), Z_GLA_K // GLA_DK + h)),
                  pl.BlockSpec((ts, GLA_DV), lambda b, h, t: (rows(b, h, t), Z_GLA_V // GLA_DV + h)),
                  pl.BlockSpec((ts, GLA_DV), lambda b, h, t: (rows(b, h, t), Z_GLA_G // GLA_DV + h)),
                  pl.BlockSpec((ts, GLA_DK), lambda b, h, t: (rows(b, h, t), h)),
                  pl.BlockSpec((1, GLA_DV), lambda b, h, t: (0, 0))],
        out_specs=pl.BlockSpec((ts, GLA_DV), lambda b, h, t: (rows(b, h, t), h)),
        scratch_shapes=[pltpu.VMEM((GLA_DK, GLA_DV), F32)],
        compiler_params=_params("parallel", "parallel", "arbitrary"),
        name="gla",
    )(z, z, z, z, log_a, ng)


def _diff_kernel(q_ref, k_ref, vt_ref, lq1_ref, lk1_ref, lq2_ref, lk2_ref, sg_ref, o_ref,
                 s_scr, p_scr, acc_scr, m_scr, l_scr, a_scr, *, tq):
    qi = pl.program_id(2)
    m_scr[...] = jnp.full_like(m_scr, NEG_INF)
    l_scr[...] = jnp.zeros_like(l_scr)
    acc_scr[...] = jnp.zeros_like(acc_scr)
    n_streams = q_ref.shape[1] // DIFF_DH

    def block(kb, diag):
        start = pl.multiple_of(kb * tq, tq)
        if diag:
            key_chunk = lax.broadcasted_iota(jnp.int32, (tq, tq), 0) // CHUNK
            query_chunk = lax.broadcasted_iota(jnp.int32, (tq, tq), 1) // CHUNK
            visible = key_chunk <= query_chunk
        scores = []
        for c in range(n_streams):
            cols = slice(c * DIFF_DH, (c + 1) * DIFF_DH)
            s = _nt_dot(k_ref[pl.ds(start, tq), cols], q_ref[:, cols])
            scores.append(jnp.where(visible, s, NEG_INF) if diag else s)
        for c in range(n_streams):
            head = c // 2
            vt = vt_ref[head * DIFF_DV:(head + 1) * DIFF_DV, pl.ds(start, tq)]
            m_old = m_scr[c]
            m_new = jnp.maximum(m_old, jnp.max(scores[c], axis=0, keepdims=True))
            alpha = jnp.exp2(m_old - m_new)
            p = jnp.exp2(scores[c] - m_new)
            l_scr[c] = alpha * l_scr[c] + jnp.sum(p, axis=0, keepdims=True)
            m_scr[c] = m_new
            acc_scr[c] = (alpha * acc_scr[c]
                          + jnp.dot(vt, p.astype(BF16), preferred_element_type=F32))

    def full_block(kb, carry):
        block(kb, diag=False)
        return carry

    lax.fori_loop(0, qi, full_block, 0)
    block(qi, diag=True)

    lam = (jnp.exp(jnp.sum(lq1_ref[...] * lk1_ref[...], axis=-1, keepdims=True))
           - jnp.exp(jnp.sum(lq2_ref[...] * lk2_ref[...], axis=-1, keepdims=True)) + LAM_INIT)
    for head in range(n_streams // 2):
        c1, c2 = 2 * head, 2 * head + 1
        o_t = acc_scr[c1] / l_scr[c1] - lam * (acc_scr[c2] / l_scr[c2])
        o_ref[:, head * DIFF_DV:(head + 1) * DIFF_DV] = (
            _rms(o_t.T, sg_ref[...]) * (1.0 - LAM_INIT)).astype(o_ref.dtype)


def _diff_attn(z, v_t, lq1, lk1, lq2, lk2, sg, batch, seq, tq=512, heads_per_step=2):
    m = z.shape[0]
    nq = seq // tq
    width = heads_per_step * DIFF_DV
    ns = 2 * heads_per_step
    vec = pl.BlockSpec((1, DIFF_DH), lambda b, g, i: (0, 0))
    return pl.pallas_call(
        functools.partial(_diff_kernel, tq=tq),
        out_shape=jax.ShapeDtypeStruct((m, DIFF_HEADS * DIFF_DV), BF16),
        grid=(batch, DIFF_HEADS // heads_per_step, nq),
        in_specs=[pl.BlockSpec((tq, width), lambda b, g, i: (b * nq + i, Z_DIFF_Q // width + g)),
                  pl.BlockSpec((seq, width), lambda b, g, i: (b, Z_DIFF_K // width + g)),
                  pl.BlockSpec((width, seq), lambda b, g, i: (g, b)),
                  vec, vec, vec, vec,
                  pl.BlockSpec((1, DIFF_DV), lambda b, g, i: (0, 0))],
        out_specs=pl.BlockSpec((tq, width), lambda b, g, i: (b * nq + i, g)),
        scratch_shapes=[pltpu.VMEM((ns, tq, tq), F32),
                        pltpu.VMEM((ns, tq, tq), BF16),
                        pltpu.VMEM((ns, DIFF_DV, tq), F32),
                        pltpu.VMEM((ns, 1, tq), F32),
                        pltpu.VMEM((ns, 1, tq), F32),
                        pltpu.VMEM((ns, 1, tq), F32)],
        compiler_params=_params("parallel", "parallel", "arbitrary"),
        name="diff_attn",
    )(z, z, v_t, lq1, lk1, lq2, lk2, sg)


def _mem_kv_kernel(mn_ref, w_ref, kg_ref, kv_ref, *, n_key_tiles):
    j = pl.program_id(0)
    acc = jnp.dot(mn_ref[...], w_ref[...].astype(BF16), preferred_element_type=F32)

    @pl.when(j < n_key_tiles)
    def _():
        _store_group_norm(acc, kg_ref[...], MEM_DH, 1.0, kv_ref)

    @pl.when(j >= n_key_tiles)
    def _():
        kv_ref[...] = acc.astype(kv_ref.dtype)


def _mem_kv(mem_n, w_kv, kg, tn=512):
    m, d = mem_n.shape
    n = w_kv.shape[1]
    return pl.pallas_call(
        functools.partial(_mem_kv_kernel, n_key_tiles=(n // 2) // tn),
        out_shape=jax.ShapeDtypeStruct((m, n), BF16),
        grid=(n // tn,),
        in_specs=[pl.BlockSpec((m, d), lambda j: (0, 0)),
                  pl.BlockSpec((d, tn), lambda j: (0, j)),
                  pl.BlockSpec((1, MEM_DH), lambda j: (0, 0))],
        out_specs=pl.BlockSpec((m, tn), lambda j: (0, j)),
        compiler_params=_params("parallel"),
        name="mem_kv",
    )(mem_n, w_kv, kg)


def _mem_attn_kernel(q_ref, k_ref, v_ref, o_ref):
    for head in range(MEM_HEADS):
        cols = slice(head * MEM_DH, (head + 1) * MEM_DH)
        s = _nt_dot(q_ref[:, cols], k_ref[:, cols])
        e = jnp.exp(s - jnp.max(s, axis=-1, keepdims=True))
        p = (e / jnp.sum(e, axis=-1, keepdims=True)).astype(BF16)
        o_ref[:, cols] = jnp.dot(p, v_ref[:, cols], preferred_element_type=F32).astype(o_ref.dtype)


def _mem_attn(z, kv, batch, seq, n_mem, tm=512):
    m = z.shape[0]
    nt = seq // tm
    width = MEM_HEADS * MEM_DH
    return pl.pallas_call(
        _mem_attn_kernel,
        out_shape=jax.ShapeDtypeStruct((m, width), BF16),
        grid=(batch, nt),
        in_specs=[pl.BlockSpec((tm, width), lambda b, t: (b * nt + t, Z_MEM_Q // width)),
                  pl.BlockSpec((n_mem, width), lambda b, t: (b, 0)),
                  pl.BlockSpec((n_mem, width), lambda b, t: (b, 1))],
        out_specs=pl.BlockSpec((tm, width), lambda b, t: (b * nt + t, 0)),
        compiler_params=_params("parallel", "parallel"),
        name="mem_attn",
    )(z, kv, kv)


def _gate_merge_kernel(h_ref, y0_ref, y1_ref, y2_ref, wg0_ref, wg1_ref, wg2_ref,
                       bg0_ref, bg1_ref, bg2_ref, wb0_ref, wb1_ref, wb2_ref, r0_ref, r1_ref,
                       o_ref, c0_ref, c1_ref):
    _cast_riders((r0_ref, r1_ref), (c0_ref, c1_ref))
    h = h_ref[...]
    merged = None
    for y_ref, wg_ref, bg_ref, wb_ref in ((y0_ref, wg0_ref, bg0_ref, wb0_ref),
                                          (y1_ref, wg1_ref, bg1_ref, wb1_ref),
                                          (y2_ref, wg2_ref, bg2_ref, wb2_ref)):
        gate = _sigmoid(jnp.dot(h, wg_ref[...], preferred_element_type=F32) + bg_ref[...])
        term = gate * jnp.dot(y_ref[...], wb_ref[...], preferred_element_type=F32)
        merged = term if merged is None else merged + term
    o_ref[...] = merged.astype(o_ref.dtype)


def _gate_merge(h, ys, w_gate, b_gate, w_branch, riders, tm=1024, tn=256):
    m, d = h.shape
    bw = w_branch.shape[1]
    ni, nj = m // tm, d // tn
    act = lambda width: pl.BlockSpec((tm, width), lambda i, j: (i, 0))
    wg = lambda b: pl.BlockSpec((d, tn), lambda i, j: (0, b * nj + j))
    bg = lambda b: pl.BlockSpec((1, tn), lambda i, j: (0, b * nj + j))
    wb = lambda b: pl.BlockSpec((None, bw, tn), lambda i, j: (b, 0, j))
    r_specs, r_shapes = _rider_specs(riders, ni * nj, lambda i, j: i * nj + j)
    return pl.pallas_call(
        _gate_merge_kernel,
        out_shape=(jax.ShapeDtypeStruct((m, d), BF16), *r_shapes),
        grid=(ni, nj),
        in_specs=[act(d), act(bw), act(bw), act(bw), wg(0), wg(1), wg(2), bg(0), bg(1), bg(2),
                  wb(0), wb(1), wb(2), *r_specs],
        out_specs=(pl.BlockSpec((tm, tn), lambda i, j: (i, j)), *r_specs),
        compiler_params=_params("arbitrary", "arbitrary"),
        name="gate_merge",
    )(h, ys[0], ys[1], ys[2], w_gate, w_gate, w_gate, b_gate, b_gate, b_gate,
      w_branch, w_branch, w_branch, *riders)


def _out_proj_kernel(mg_ref, w_ref, x_ref, g_ref, x1_ref, hf_ref):
    x1 = x_ref[...] + jnp.dot(mg_ref[...], w_ref[...], preferred_element_type=F32)
    x1_ref[...] = x1
    hf_ref[...] = _rms(x1, g_ref[...]).astype(hf_ref.dtype)


def _out_proj(merged, w_out, x2, g, tm=512):
    m, d = x2.shape
    row = lambda i: (i, 0)
    fixed = lambda i: (0, 0)
    return pl.pallas_call(
        _out_proj_kernel,
        out_shape=(jax.ShapeDtypeStruct((m, d), F32), jax.ShapeDtypeStruct((m, d), BF16)),
        grid=(m // tm,),
        in_specs=[pl.BlockSpec((tm, d), row),
                  pl.BlockSpec((d, d), fixed, pipeline_mode=pl.Buffered(1)),
                  pl.BlockSpec((tm, d), row), pl.BlockSpec((1, d), fixed)],
        out_specs=(pl.BlockSpec((tm, d), row), pl.BlockSpec((tm, d), row)),
        compiler_params=_params("parallel"),
        name="out_proj",
    )(merged, w_out, x2, g)


def _ffn_up_kernel(hf_ref, wg_ref, wu_ref, r0_ref, a_ref, c0_ref):
    _cast_riders((r0_ref,), (c0_ref,))
    hf = hf_ref[...]
    gate = jnp.dot(hf, wg_ref[...], preferred_element_type=F32)
    up = jnp.dot(hf, wu_ref[...], preferred_element_type=F32)
    a_ref[...] = (gate * _sigmoid(gate) * up).astype(a_ref.dtype)


def _ffn_up(hf, w_in, riders, tm=1024, tf=512):
    m, d = hf.shape
    d_ff = w_in.shape[1] // 2
    ni, nj = m // tm, d_ff // tf
    r_specs, r_shapes = _rider_specs(riders, ni * nj, lambda i, j: i * nj + j)
    return pl.pallas_call(
        _ffn_up_kernel,
        out_shape=(jax.ShapeDtypeStruct((m, d_ff), BF16), *r_shapes),
        grid=(ni, nj),
        in_specs=[pl.BlockSpec((tm, d), lambda i, j: (i, 0)),
                  pl.BlockSpec((d, tf), lambda i, j: (0, j)),
                  pl.BlockSpec((d, tf), lambda i, j: (0, nj + j)),
                  *r_specs],
        out_specs=(pl.BlockSpec((tm, tf), lambda i, j: (i, j)), *r_specs),
        compiler_params=_params("arbitrary", "arbitrary"),
        name="ffn_up",
    )(hf, w_in, w_in, *riders)


def _ffn_down_kernel(a_ref, w_ref, x1_ref, o_ref):
    o_ref[...] = x1_ref[...] + jnp.dot(a_ref[...], w_ref[...], preferred_element_type=F32)


def _ffn_down(a, w_down, x1, tm=1024, tn=512):
    m, d_ff = a.shape
    d = w_down.shape[1]
    return pl.pallas_call(
        _ffn_down_kernel,
        out_shape=jax.ShapeDtypeStruct((m, d), F32),
        grid=(m // tm, d // tn),
        in_specs=[pl.BlockSpec((tm, d_ff), lambda i, j: (i, 0)),
                  pl.BlockSpec((d_ff, tn), lambda i, j: (0, j)),
                  pl.BlockSpec((tm, tn), lambda i, j: (i, j))],
        out_specs=pl.BlockSpec((tm, tn), lambda i, j: (i, j)),
        compiler_params=_params("parallel", "arbitrary"),
        name="ffn_down",
    )(a, w_down, x1)


def kernel(x, mem, norm_mix_g, norm_mem_g, w_in, gla_w_alpha_up, gla_b_alpha, gla_norm_g,
           diff_q_norm_g, diff_k_norm_g, diff_lambda_q1, diff_lambda_k1, diff_lambda_q2,
           diff_lambda_k2, diff_subln_g, mem_q_norm_g, mem_k_norm_g, w_mem_kv, w_branch,
           w_gate, b_gate, w_out, norm_ffn_g, w_ffn_in, w_ffn_down):
    batch, seq, d = x.shape
    n_mem = mem.shape[1]
    depth = w_in.shape[0]
    assert depth == 1, "LAM_INIT is the layer-0 value"
    x2 = x.reshape(batch * seq, d)
    mem2 = mem.reshape(batch * n_mem, d)
    for l in range(depth):
        w_in_t = w_in[l].T
        w_up = jnp.pad(gla_w_alpha_up[l], ((0, LANES - GLA_RANK), (0, 0)))
        row = lambda v: v.reshape(1, -1)

        h, log_a = _norm_mix(x2, row(norm_mix_g[l]), w_in_t, w_up, row(gla_b_alpha[l]))
        z, w_gate_bf, w_branch_bf = _in_proj(
            h, w_in_t, row(diff_q_norm_g[l]), row(diff_k_norm_g[l]), row(mem_q_norm_g[l]),
            riders=(w_gate[l], w_branch[l].reshape(-1, d)))
        y_gla = _gla(z, log_a, row(gla_norm_g[l]), batch, seq)
        v_t = _v_proj_t(h, w_in_t)
        y_diff = _diff_attn(z, v_t, row(diff_lambda_q1[l]), row(diff_lambda_k1[l]),
                            row(diff_lambda_q2[l]), row(diff_lambda_k2[l]),
                            row(diff_subln_g[l]), batch, seq)
        mem_n = _norm_rows(mem2, row(norm_mem_g[l]))
        kv = _mem_kv(mem_n, w_mem_kv[l], row(mem_k_norm_g[l]))
        y_mem = _mem_attn(z, kv, batch, seq, n_mem)
        merged, w_out_bf, w_ffn_in_bf = _gate_merge(
            h, (y_gla, y_diff, y_mem), w_gate_bf, row(b_gate[l]),
            w_branch_bf.reshape(w_branch[l].shape), riders=(w_out[l], w_ffn_in[l]))
        x1, hf = _out_proj(merged, w_out_bf, x2, row(norm_ffn_g[l]))
        a, w_ffn_down_bf = _ffn_up(hf, w_ffn_in_bf, riders=(w_ffn_down[l],))
        x2 = _ffn_down(a, w_ffn_down_bf, x1)
    return x2.reshape(batch, seq, d)
```

```python
import functools

import jax
import jax.numpy as jnp
from jax import lax
from jax.experimental import pallas as pl
from jax.experimental.pallas import tpu as pltpu

F32 = jnp.float32
BF16 = jnp.bfloat16

CHUNK = 64
GLA_HEADS = 4
GLA_DK = 128
GLA_DV = 256
GLA_RANK = 16
GLA_GATE_NORM = 16.0
DIFF_HEADS = 4
DIFF_DH = 128
DIFF_DV = 256
MEM_HEADS = 4
MEM_DH = 256
N_BRANCH = 3
NORM_EPS = 1e-6
NEG_INF = -1e30
LAM_INIT = 0.8 - 0.6 * 1.0
LOG2_E = 1.4426950408889634

LANES = 128
VMEM_LIMIT = 56 * 1024 * 1024

IN_TILE = 1024
Z_GLA_Q, Z_GLA_K, Z_GLA_V, Z_GLA_G = 0, 512, 1024, 2048
Z_DIFF_Q, Z_DIFF_K, Z_MEM_Q = 3072, 4096, 5120
Z_WIDTH = 6144
W_SRC_TILES = (0, 1, 2, 3, 4, 6)
W_DIFF_V_TILE = 5
W_ROWS = 7 * IN_TILE + GLA_RANK
W_FIRST_SHIFTED_TILE = 3


def _params(*sem):
    return pltpu.CompilerParams(dimension_semantics=sem, vmem_limit_bytes=VMEM_LIMIT)


def _nt_dot(a, b):
    return lax.dot_general(a, b, (((1,), (1,)), ((), ())), preferred_element_type=F32)


def _tn_dot(a, b, precision=None):
    return lax.dot_general(a, b, (((0,), (0,)), ((), ())), preferred_element_type=F32,
                           precision=precision)


def _rms(v, gain):
    ms = jnp.mean(v * v, axis=-1, keepdims=True)
    return v * lax.rsqrt(ms + NORM_EPS) * gain


def _sigmoid(v):
    return 1.0 / (1.0 + jnp.exp(-v))


BF16_SUBLANES = 16


def _rider_specs(weights, n_steps, step_of):
    specs, shapes = [], []
    for w in weights:
        rows, cols = w.shape
        chunk = BF16_SUBLANES
        while rows % chunk or rows // chunk > n_steps:
            chunk += BF16_SUBLANES
        last = rows // chunk - 1
        specs.append(pl.BlockSpec((chunk, cols),
                                  lambda *g, last=last: (jnp.minimum(step_of(*g), last), 0)))
        shapes.append(jax.ShapeDtypeStruct((rows, cols), BF16))
    return specs, shapes


def _cast_riders(in_refs, out_refs):
    for src, dst in zip(in_refs, out_refs):
        dst[...] = src[...].astype(BF16)


def _norm_mix_kernel(x_ref, g_ref, wa_ref, wup_ref, bal_ref, h_ref, la_ref):
    h = _rms(x_ref[...], g_ref[...]).astype(BF16)
    h_ref[...] = h
    a_low = _nt_dot(h, wa_ref[...].astype(BF16))
    w_up = wup_ref[...]
    a_hi, w_hi = a_low.astype(BF16), w_up.astype(BF16)
    a_lo = (a_low - a_hi.astype(F32)).astype(BF16)
    w_lo = (w_up - w_hi.astype(F32)).astype(BF16)
    pre = (jnp.dot(a_hi, w_hi, preferred_element_type=F32)
           + (jnp.dot(a_lo, w_hi, preferred_element_type=F32)
              + jnp.dot(a_hi, w_lo, preferred_element_type=F32))) + bal_ref[...]
    log_sig = jnp.minimum(pre, 0.0) - jnp.log1p(jnp.exp(-jnp.abs(pre)))
    la_ref[...] = log_sig * (1.0 / GLA_GATE_NORM)


def _norm_mix(x2, g, w_in_t, wup, bal, tr=512):
    m, d = x2.shape
    n = wup.shape[1]
    fixed = lambda i: (0, 0)
    decay_block = (Z_GLA_G + GLA_HEADS * GLA_DV) // LANES
    return pl.pallas_call(
        _norm_mix_kernel,
        out_shape=(jax.ShapeDtypeStruct((m, d), BF16), jax.ShapeDtypeStruct((m, n), F32)),
        grid=(m // tr,),
        in_specs=[pl.BlockSpec((tr, d), lambda i: (i, 0)),
                  pl.BlockSpec((1, d), fixed),
                  pl.BlockSpec((LANES, d), lambda i: (decay_block, 0)),
                  pl.BlockSpec((LANES, n), fixed),
                  pl.BlockSpec((1, n), fixed)],
        out_specs=(pl.BlockSpec((tr, d), lambda i: (i, 0)),
                   pl.BlockSpec((tr, n), lambda i: (i, 0))),
        compiler_params=_params("parallel"),
        name="norm_mix",
    )(x2, g, w_in_t, wup, bal)


def _norm_rows_kernel(x_ref, g_ref, h_ref):
    h_ref[...] = _rms(x_ref[...], g_ref[...]).astype(BF16)


def _norm_rows(x2, g, tr=512):
    m, d = x2.shape
    return pl.pallas_call(
        _norm_rows_kernel,
        out_shape=jax.ShapeDtypeStruct((m, d), BF16),
        grid=(m // tr,),
        in_specs=[pl.BlockSpec((tr, d), lambda i: (i, 0)),
                  pl.BlockSpec((1, d), lambda i: (0, 0))],
        out_specs=pl.BlockSpec((tr, d), lambda i: (i, 0)),
        compiler_params=_params("parallel"),
        name="norm_rows",
    )(x2, g)


def _store_group_norm(acc, gain, width, scale, out_ref):
    for s in range(0, acc.shape[1], width):
        blk = acc[:, s:s + width]
        out_ref[:, s:s + width] = (_rms(blk, gain) * scale).astype(out_ref.dtype)


def _w_tile_specs(d, src_tile):
    hi_per_tile = IN_TILE // GLA_RANK
    return [pl.BlockSpec((IN_TILE, d), lambda *g: (src_tile(*g), 0)),
            pl.BlockSpec((GLA_RANK, d), lambda *g: ((src_tile(*g) + 1) * hi_per_tile, 0))]


def _cast_w_tile(w_lo_ref, w_hi_ref, w_scr, first_step, shifted):
    @pl.when(first_step & jnp.logical_not(shifted))
    def _():
        w_scr[...] = w_lo_ref[...].astype(BF16)

    @pl.when(first_step & shifted)
    def _():
        w_scr[:IN_TILE - GLA_RANK, :] = w_lo_ref[GLA_RANK:, :].astype(BF16)
        w_scr[IN_TILE - GLA_RANK:, :] = w_hi_ref[...].astype(BF16)


def _in_proj_kernel(h_ref, w_lo_ref, w_hi_ref, dq_g_ref, dk_g_ref, mq_g_ref, r0_ref, r1_ref,
                    z_ref, c0_ref, c1_ref, w_scr):
    _cast_riders((r0_ref, r1_ref), (c0_ref, c1_ref))
    j = pl.program_id(0)
    _cast_w_tile(w_lo_ref, w_hi_ref, w_scr, pl.program_id(1) == 0, j >= W_FIRST_SHIFTED_TILE)
    acc = _nt_dot(h_ref[...], w_scr[...])
    j_dq, j_dk, j_mq = Z_DIFF_Q // IN_TILE, Z_DIFF_K // IN_TILE, Z_MEM_Q // IN_TILE

    @pl.when((j != j_dq) & (j != j_dk) & (j != j_mq))
    def _():
        z_ref[...] = acc.astype(z_ref.dtype)

    @pl.when(j == j_dq)
    def _():
        _store_group_norm(acc, dq_g_ref[...], DIFF_DH, DIFF_DH ** -0.5 * LOG2_E, z_ref)

    @pl.when(j == j_dk)
    def _():
        _store_group_norm(acc, dk_g_ref[...], DIFF_DH, 1.0, z_ref)

    @pl.when(j == j_mq)
    def _():
        _store_group_norm(acc, mq_g_ref[...], MEM_DH, MEM_DH ** -0.5, z_ref)


def _in_proj(h, w_in_t, dq_g, dk_g, mq_g, riders, tm=1024):
    m, d = h.shape
    assert w_in_t.shape[0] == W_ROWS
    nj, ni = Z_WIDTH // IN_TILE, m // tm
    assert W_SRC_TILES == tuple(j + (j >= W_DIFF_V_TILE) for j in range(nj))
    r_specs, r_shapes = _rider_specs(riders, nj * ni, lambda j, i: j * ni + i)
    return pl.pallas_call(
        _in_proj_kernel,
        out_shape=(jax.ShapeDtypeStruct((m, Z_WIDTH), BF16), *r_shapes),
        grid=(nj, ni),
        in_specs=[pl.BlockSpec((tm, d), lambda j, i: (i, 0)),
                  *_w_tile_specs(d, lambda j, i: jnp.where(j >= W_DIFF_V_TILE, j + 1, j)),
                  pl.BlockSpec((1, DIFF_DH), lambda j, i: (0, 0)),
                  pl.BlockSpec((1, DIFF_DH), lambda j, i: (0, 0)),
                  pl.BlockSpec((1, MEM_DH), lambda j, i: (0, 0)),
                  *r_specs],
        out_specs=(pl.BlockSpec((tm, IN_TILE), lambda j, i: (i, j)), *r_specs),
        scratch_shapes=[pltpu.VMEM((IN_TILE, d), BF16)],
        compiler_params=_params("arbitrary", "arbitrary"),
        name="in_proj",
    )(h, w_in_t, w_in_t, dq_g, dk_g, mq_g, *riders)


def _v_proj_t_kernel(h_ref, w_lo_ref, w_hi_ref, vt_ref, w_scr):
    _cast_w_tile(w_lo_ref, w_hi_ref, w_scr, pl.program_id(0) == 0,
                 W_DIFF_V_TILE >= W_FIRST_SHIFTED_TILE)
    vt_ref[...] = _nt_dot(w_scr[...], h_ref[...]).astype(vt_ref.dtype)


def _v_proj_t(h, w_in_t, tm=1024):
    m, d = h.shape
    return pl.pallas_call(
        _v_proj_t_kernel,
        out_shape=jax.ShapeDtypeStruct((IN_TILE, m), BF16),
        grid=(m // tm,),
        in_specs=[pl.BlockSpec((tm, d), lambda i: (i, 0)),
                  *_w_tile_specs(d, lambda i: W_DIFF_V_TILE)],
        out_specs=pl.BlockSpec((IN_TILE, tm), lambda i: (0, i)),
        scratch_shapes=[pltpu.VMEM((IN_TILE, d), BF16)],
        compiler_params=_params("arbitrary"),
        name="v_proj_t",
    )(h, w_in_t, w_in_t)


def _chunk_cumsum(x):
    row_in_chunk = lax.broadcasted_iota(jnp.int32, x.shape, 0) % CHUNK
    shift = 1
    while shift < CHUNK:
        x = x + jnp.where(row_in_chunk >= shift, pltpu.roll(x, shift, 0), 0.0)
        shift *= 2
    return x


def _gla_kernel(q_ref, k_ref, v_ref, g_ref, la_ref, ng_ref, o_ref, s_ref, *, ts):
    @pl.when(pl.program_id(2) == 0)
    def _():
        s_ref[...] = jnp.zeros_like(s_ref)

    bcum_all = _chunk_cumsum(la_ref[...])
    row = lax.broadcasted_iota(jnp.int32, (CHUNK, CHUNK), 0)
    col = lax.broadcasted_iota(jnp.int32, (CHUNK, CHUNK), 1)
    causal = row >= col

    for c in range(ts // CHUNK):
        rows = slice(c * CHUNK, (c + 1) * CHUNK)
        bcum = bcum_all[rows]
        b_last = bcum[CHUNK - 1:CHUNK, :]
        q = q_ref[rows, :].astype(F32) * (GLA_DK ** -0.5)
        k = k_ref[rows, :].astype(F32)
        v = v_ref[rows, :]
        q_dec = (q * jnp.exp(bcum)).astype(BF16)
        k_dec = (k * jnp.exp(-bcum)).astype(BF16)
        k_tail = (k * jnp.exp(b_last - bcum)).astype(BF16)
        att = jnp.where(causal, _nt_dot(q_dec, k_dec), 0.0).astype(BF16)
        state = s_ref[...]
        o = jnp.dot(jnp.concatenate([q_dec, att], axis=1),
                    jnp.concatenate([state.astype(BF16), v], axis=0), preferred_element_type=F32)
        inc = _tn_dot(k_tail, v)
        decay_col = jnp.broadcast_to(jnp.exp(b_last), (GLA_DK, GLA_DK)).T
        decay = jnp.concatenate([decay_col] * (GLA_DV // GLA_DK), axis=1)
        s_ref[...] = decay * state + inc
        gate = g_ref[rows, :].astype(F32)
        o_ref[rows, :] = (_rms(o, ng_ref[...]) * (gate * _sigmoid(gate))).astype(o_ref.dtype)


def _gla(z, log_a, ng, batch, seq, ts=2048):
    m = z.shape[0]
    nt = seq // ts
    rows = lambda b, h, t: b * nt + t
    return pl.pallas_call(
        functools.partial(_gla_kernel, ts=ts),
        out_shape=jax.ShapeDtypeStruct((m, GLA_HEADS * GLA_DV), BF16),
        grid=(batch, GLA_HEADS, nt),
        in_specs=[pl.BlockSpec((ts, GLA_DK), lambda b, h, t: (rows(b, h, t), Z_GLA_Q // GLA_DK + h)),
                  pl.BlockSpec((ts, GLA_DK), lambda b, h, t: (rows(b, h, t), Z_GLA_K // GLA_DK + h)),
                  pl.BlockSpec((ts, GLA_DV), lambda b, h, t: (rows(b, h, t), Z_GLA_V // GLA_DV + h)),
                  pl.BlockSpec((ts, GLA_DV), lambda b, h, t: (rows(b, h, t), Z_GLA_G // GLA_DV + h)),
                  pl.BlockSpec((ts, GLA_DK), lambda b, h, t: (rows(b, h, t), h)),
                  pl.BlockSpec((1, GLA_DV), lambda b, h, t: (0, 0))],
        out_specs=pl.BlockSpec((ts, GLA_DV), lambda b, h, t: (rows(b, h, t), h)),
        scratch_shapes=[pltpu.VMEM((GLA_DK, GLA_DV), F32)],
        compiler_params=_params("parallel", "parallel", "arbitrary"),
        name="gla",
    )(z, z, z, z, log_a, ng)


def _diff_kernel(q_ref, k_ref, vt_ref, lq1_ref, lk1_ref, lq2_ref, lk2_ref, sg_ref, o_ref,
                 acc_scr, m_scr, l_scr, *, tq):
    qi = pl.program_id(2)
    m_scr[...] = jnp.full_like(m_scr, NEG_INF)
    l_scr[...] = jnp.zeros_like(l_scr)
    acc_scr[...] = jnp.zeros_like(acc_scr)
    n_streams = q_ref.shape[1] // DIFF_DH
    half = tq // 2

    def update(c, lanes, s, vt):
        m_old = m_scr[c, :, lanes]
        m_new = jnp.maximum(m_old, jnp.max(s, axis=0, keepdims=True))
        alpha = jnp.exp2(m_old - m_new)
        p = jnp.exp2(s - m_new)
        l_scr[c, :, lanes] = alpha * l_scr[c, :, lanes] + jnp.sum(p, axis=0, keepdims=True)
        m_scr[c, :, lanes] = m_new
        acc_scr[c, :, lanes] = (alpha * acc_scr[c, :, lanes]
                                + jnp.dot(vt, p.astype(BF16), preferred_element_type=F32))

    def values_t(c, start, n):
        head = c // 2
        return vt_ref[head * DIFF_DV:(head + 1) * DIFF_DV, pl.ds(start, n)]

    def full_block(kb, carry):
        start = pl.multiple_of(kb * tq, tq)
        scores = []
        for c in range(n_streams):
            cols = slice(c * DIFF_DH, (c + 1) * DIFF_DH)
            scores.append(_nt_dot(k_ref[pl.ds(start, tq), cols], q_ref[:, cols]))
        for c in range(n_streams):
            update(c, slice(0, tq), scores[c], values_t(c, start, tq))
        return carry

    def diag_block(kb):
        lo = pl.multiple_of(kb * tq, tq)
        hi = pl.multiple_of(kb * tq + half, half)
        key_chunk = lax.broadcasted_iota(jnp.int32, (half, half), 0) // CHUNK
        query_chunk = lax.broadcasted_iota(jnp.int32, (half, half), 1) // CHUNK
        visible = key_chunk <= query_chunk
        scores = []
        for c in range(n_streams):
            cols = slice(c * DIFF_DH, (c + 1) * DIFF_DH)
            k_lo, k_hi = k_ref[pl.ds(lo, half), cols], k_ref[pl.ds(hi, half), cols]
            q_lo, q_hi = q_ref[:half, cols], q_ref[half:, cols]
            s_lo = jnp.where(visible, _nt_dot(k_lo, q_lo), NEG_INF)
            s_hi = jnp.concatenate([_nt_dot(k_lo, q_hi),
                                    jnp.where(visible, _nt_dot(k_hi, q_hi), NEG_INF)], axis=0)
            scores.append((s_lo, s_hi))
        for c in range(n_streams):
            update(c, slice(0, half), scores[c][0], values_t(c, lo, half))
            update(c, slice(half, tq), scores[c][1], values_t(c, lo, tq))

    lax.fori_loop(0, qi, full_block, 0)
    diag_block(qi)

    lam = (jnp.exp(jnp.sum(lq1_ref[...] * lk1_ref[...], axis=-1, keepdims=True))
           - jnp.exp(jnp.sum(lq2_ref[...] * lk2_ref[...], axis=-1, keepdims=True)) + LAM_INIT)
    for head in range(n_streams // 2):
        c1, c2 = 2 * head, 2 * head + 1
        o_t = acc_scr[c1] / l_scr[c1] - lam * (acc_scr[c2] / l_scr[c2])
        o_ref[:, head * DIFF_DV:(head + 1) * DIFF_DV] = (
            _rms(o_t.T, sg_ref[...]) * (1.0 - LAM_INIT)).astype(o_ref.dtype)


def _diff_attn(z, v_t, lq1, lk1, lq2, lk2, sg, batch, seq, tq=512, heads_per_step=4):
    m = z.shape[0]
    nq = seq // tq
    width = heads_per_step * DIFF_DV
    ns = 2 * heads_per_step
    vec = pl.BlockSpec((1, DIFF_DH), lambda b, g, i: (0, 0))
    return pl.pallas_call(
        functools.partial(_diff_kernel, tq=tq),
        out_shape=jax.ShapeDtypeStruct((m, DIFF_HEADS * DIFF_DV), BF16),
        grid=(batch, DIFF_HEADS // heads_per_step, nq),
        in_specs=[pl.BlockSpec((tq, width), lambda b, g, i: (b * nq + i, Z_DIFF_Q // width + g)),
                  pl.BlockSpec((seq, width), lambda b, g, i: (b, Z_DIFF_K // width + g)),
                  pl.BlockSpec((width, seq), lambda b, g, i: (g, b)),
                  vec, vec, vec, vec,
                  pl.BlockSpec((1, DIFF_DV), lambda b, g, i: (0, 0))],
        out_specs=pl.BlockSpec((tq, width), lambda b, g, i: (b * nq + i, g)),
        scratch_shapes=[pltpu.VMEM((ns, DIFF_DV, tq), F32),
                        pltpu.VMEM((ns, 1, tq), F32),
                        pltpu.VMEM((ns, 1, tq), F32)],
        compiler_params=_params("parallel", "parallel", "arbitrary"),
        name="diff_attn",
    )(z, z, v_t, lq1, lk1, lq2, lk2, sg)


def _mem_kv_kernel(mn_ref, w_ref, kg_ref, kv_ref, *, n_key_tiles):
    j = pl.program_id(0)
    acc = jnp.dot(mn_ref[...], w_ref[...].astype(BF16), preferred_element_type=F32)

    @pl.when(j < n_key_tiles)
    def _():
        _store_group_norm(acc, kg_ref[...], MEM_DH, 1.0, kv_ref)

    @pl.when(j >= n_key_tiles)
    def _():
        kv_ref[...] = acc.astype(kv_ref.dtype)


def _mem_kv(mem_n, w_kv, kg, tn=512):
    m, d = mem_n.shape
    n = w_kv.shape[1]
    return pl.pallas_call(
        functools.partial(_mem_kv_kernel, n_key_tiles=(n // 2) // tn),
        out_shape=jax.ShapeDtypeStruct((m, n), BF16),
        grid=(n // tn,),
        in_specs=[pl.BlockSpec((m, d), lambda j: (0, 0)),
                  pl.BlockSpec((d, tn), lambda j: (0, j)),
                  pl.BlockSpec((1, MEM_DH), lambda j: (0, 0))],
        out_specs=pl.BlockSpec((m, tn), lambda j: (0, j)),
        compiler_params=_params("parallel"),
        name="mem_kv",
    )(mem_n, w_kv, kg)


def _mem_attn_kernel(q_ref, k_ref, v_ref, o_ref):
    for head in range(MEM_HEADS):
        cols = slice(head * MEM_DH, (head + 1) * MEM_DH)
        s = _nt_dot(q_ref[:, cols], k_ref[:, cols])
        e = jnp.exp(s - jnp.max(s, axis=-1, keepdims=True))
        p = (e / jnp.sum(e, axis=-1, keepdims=True)).astype(BF16)
        o_ref[:, cols] = jnp.dot(p, v_ref[:, cols], preferred_element_type=F32).astype(o_ref.dtype)


def _mem_attn(z, kv, batch, seq, n_mem, tm=512):
    m = z.shape[0]
    nt = seq // tm
    width = MEM_HEADS * MEM_DH
    return pl.pallas_call(
        _mem_attn_kernel,
        out_shape=jax.ShapeDtypeStruct((m, width), BF16),
        grid=(batch, nt),
        in_specs=[pl.BlockSpec((tm, width), lambda b, t: (b * nt + t, Z_MEM_Q // width)),
                  pl.BlockSpec((n_mem, width), lambda b, t: (b, 0)),
                  pl.BlockSpec((n_mem, width), lambda b, t: (b, 1))],
        out_specs=pl.BlockSpec((tm, width), lambda b, t: (b * nt + t, 0)),
        compiler_params=_params("parallel", "parallel"),
        name="mem_attn",
    )(z, kv, kv)


def _gate_merge_kernel(h_ref, y0_ref, y1_ref, y2_ref, wg0_ref, wg1_ref, wg2_ref,
                       bg0_ref, bg1_ref, bg2_ref, wb0_ref, wb1_ref, wb2_ref, r0_ref, r1_ref,
                       o_ref, c0_ref, c1_ref):
    _cast_riders((r0_ref, r1_ref), (c0_ref, c1_ref))
    h = h_ref[...]
    merged = None
    for y_ref, wg_ref, bg_ref, wb_ref in ((y0_ref, wg0_ref, bg0_ref, wb0_ref),
                                          (y1_ref, wg1_ref, bg1_ref, wb1_ref),
                                          (y2_ref, wg2_ref, bg2_ref, wb2_ref)):
        gate = _sigmoid(jnp.dot(h, wg_ref[...], preferred_element_type=F32) + bg_ref[...])
        term = gate * jnp.dot(y_ref[...], wb_ref[...], preferred_element_type=F32)
        merged = term if merged is None else merged + term
    o_ref[...] = merged.astype(o_ref.dtype)


def _gate_merge(h, ys, w_gate, b_gate, w_branch, riders, tm=1024, tn=256):
    m, d = h.shape
    bw = w_branch.shape[1]
    ni, nj = m // tm, d // tn
    act = lambda width: pl.BlockSpec((tm, width), lambda i, j: (i, 0))
    wg = lambda b: pl.BlockSpec((d, tn), lambda i, j: (0, b * nj + j))
    bg = lambda b: pl.BlockSpec((1, tn), lambda i, j: (0, b * nj + j))
    wb = lambda b: pl.BlockSpec((None, bw, tn), lambda i, j: (b, 0, j))
    r_specs, r_shapes = _rider_specs(riders, ni * nj, lambda i, j: i * nj + j)
    return pl.pallas_call(
        _gate_merge_kernel,
        out_shape=(jax.ShapeDtypeStruct((m, d), BF16), *r_shapes),
        grid=(ni, nj),
        in_specs=[act(d), act(bw), act(bw), act(bw), wg(0), wg(1), wg(2), bg(0), bg(1), bg(2),
                  wb(0), wb(1), wb(2), *r_specs],
        out_specs=(pl.BlockSpec((tm, tn), lambda i, j: (i, j)), *r_specs),
        compiler_params=_params("arbitrary", "arbitrary"),
        name="gate_merge",
    )(h, ys[0], ys[1], ys[2], w_gate, w_gate, w_gate, b_gate, b_gate, b_gate,
      w_branch, w_branch, w_branch, *riders)


def _out_proj_kernel(mg_ref, w_ref, x_ref, g_ref, x1_ref, hf_ref):
    x1 = x_ref[...] + jnp.dot(mg_ref[...], w_ref[...], preferred_element_type=F32)
    x1_ref[...] = x1
    hf_ref[...] = _rms(x1, g_ref[...]).astype(hf_ref.dtype)


def _out_proj(merged, w_out, x2, g, tm=512):
    m, d = x2.shape
    row = lambda i: (i, 0)
    fixed = lambda i: (0, 0)
    return pl.pallas_call(
        _out_proj_kernel,
        out_shape=(jax.ShapeDtypeStruct((m, d), F32), jax.ShapeDtypeStruct((m, d), BF16)),
        grid=(m // tm,),
        in_specs=[pl.BlockSpec((tm, d), row),
                  pl.BlockSpec((d, d), fixed, pipeline_mode=pl.Buffered(1)),
                  pl.BlockSpec((tm, d), row), pl.BlockSpec((1, d), fixed)],
        out_specs=(pl.BlockSpec((tm, d), row), pl.BlockSpec((tm, d), row)),
        compiler_params=_params("parallel"),
        name="out_proj",
    )(merged, w_out, x2, g)


def _ffn_up_kernel(hf_ref, wg_ref, wu_ref, r0_ref, a_ref, c0_ref):
    _cast_riders((r0_ref,), (c0_ref,))
    hf = hf_ref[...]
    gate = jnp.dot(hf, wg_ref[...], preferred_element_type=F32)
    up = jnp.dot(hf, wu_ref[...], preferred_element_type=F32)
    a_ref[...] = (gate * _sigmoid(gate) * up).astype(a_ref.dtype)


def _ffn_up(hf, w_in, riders, tm=1024, tf=512):
    m, d = hf.shape
    d_ff = w_in.shape[1] // 2
    ni, nj = m // tm, d_ff // tf
    r_specs, r_shapes = _rider_specs(riders, ni * nj, lambda i, j: i * nj + j)
    return pl.pallas_call(
        _ffn_up_kernel,
        out_shape=(jax.ShapeDtypeStruct((m, d_ff), BF16), *r_shapes),
        grid=(ni, nj),
        in_specs=[pl.BlockSpec((tm, d), lambda i, j: (i, 0)),
                  pl.BlockSpec((d, tf), lambda i, j: (0, j)),
                  pl.BlockSpec((d, tf), lambda i, j: (0, nj + j)),
                  *r_specs],
        out_specs=(pl.BlockSpec((tm, tf), lambda i, j: (i, j)), *r_specs),
        compiler_params=_params("arbitrary", "arbitrary"),
        name="ffn_up",
    )(hf, w_in, w_in, *riders)


def _ffn_down_kernel(a_ref, w_ref, x1_ref, o_ref):
    o_ref[...] = x1_ref[...] + jnp.dot(a_ref[...], w_ref[...], preferred_element_type=F32)


def _ffn_down(a, w_down, x1, tm=1024, tn=512):
    m, d_ff = a.shape
    d = w_down.shape[1]
    return pl.pallas_call(
        _ffn_down_kernel,
        out_shape=jax.ShapeDtypeStruct((m, d), F32),
        grid=(m // tm, d // tn),
        in_specs=[pl.BlockSpec((tm, d_ff), lambda i, j: (i, 0)),
                  pl.BlockSpec((d_ff, tn), lambda i, j: (0, j)),
                  pl.BlockSpec((tm, tn), lambda i, j: (i, j))],
        out_specs=pl.BlockSpec((tm, tn), lambda i, j: (i, j)),
        compiler_params=_params("parallel", "arbitrary"),
        name="ffn_down",
    )(a, w_down, x1)


def kernel(x, mem, norm_mix_g, norm_mem_g, w_in, gla_w_alpha_up, gla_b_alpha, gla_norm_g,
           diff_q_norm_g, diff_k_norm_g, diff_lambda_q1, diff_lambda_k1, diff_lambda_q2,
           diff_lambda_k2, diff_subln_g, mem_q_norm_g, mem_k_norm_g, w_mem_kv, w_branch,
           w_gate, b_gate, w_out, norm_ffn_g, w_ffn_in, w_ffn_down):
    batch, seq, d = x.shape
    n_mem = mem.shape[1]
    depth = w_in.shape[0]
    assert depth == 1, "LAM_INIT is the layer-0 value"
    x2 = x.reshape(batch * seq, d)
    mem2 = mem.reshape(batch * n_mem, d)
    for l in range(depth):
        w_in_t = w_in[l].T
        w_up = jnp.pad(gla_w_alpha_up[l], ((0, LANES - GLA_RANK), (0, 0)))
        row = lambda v: v.reshape(1, -1)

        h, log_a = _norm_mix(x2, row(norm_mix_g[l]), w_in_t, w_up, row(gla_b_alpha[l]))
        z, w_gate_bf, w_branch_bf = _in_proj(
            h, w_in_t, row(diff_q_norm_g[l]), row(diff_k_norm_g[l]), row(mem_q_norm_g[l]),
            riders=(w_gate[l], w_branch[l].reshape(-1, d)))
        y_gla = _gla(z, log_a, row(gla_norm_g[l]), batch, seq)
        v_t = _v_proj_t(h, w_in_t)
        y_diff = _diff_attn(z, v_t, row(diff_lambda_q1[l]), row(diff_lambda_k1[l]),
                            row(diff_lambda_q2[l]), row(diff_lambda_k2[l]),
                            row(diff_subln_g[l]), batch, seq)
        mem_n = _norm_rows(mem2, row(norm_mem_g[l]))
        kv = _mem_kv(mem_n, w_mem_kv[l], row(mem_k_norm_g[l]))
        y_mem = _mem_attn(z, kv, batch, seq, n_mem)
        merged, w_out_bf, w_ffn_in_bf = _gate_merge(
            h, (y_gla, y_diff, y_mem), w_gate_bf, row(b_gate[l]),
            w_branch_bf.reshape(w_branch[l].shape), riders=(w_out[l], w_ffn_in[l]))
        x1, hf = _out_proj(merged, w_out_bf, x2, row(norm_ffn_g[l]))
        a, w_ffn_down_bf = _ffn_up(hf, w_ffn_in_bf, riders=(w_ffn_down[l],))
        x2 = _ffn_down(a, w_ffn_down_bf, x1)
    return x2.reshape(batch, seq, d)
```

```python
import functools

import jax
import jax.numpy as jnp
from jax import lax
from jax.experimental import pallas as pl
from jax.experimental.pallas import tpu as pltpu

F32 = jnp.float32
BF16 = jnp.bfloat16

CHUNK = 64
GLA_HEADS = 4
GLA_DK = 128
GLA_DV = 256
GLA_RANK = 16
GLA_GATE_NORM = 16.0
DIFF_HEADS = 4
DIFF_DH = 128
DIFF_DV = 256
MEM_HEADS = 4
MEM_DH = 256
N_BRANCH = 3
NORM_EPS = 1e-6
NEG_INF = -1e30
LAM_INIT = 0.8 - 0.6 * 1.0
LOG2_E = 1.4426950408889634
SCORE_LIMIT = 60.0
DIFF_Q_SCALE = DIFF_DH ** -0.5 * LOG2_E

LANES = 128
VMEM_LIMIT = 56 * 1024 * 1024

IN_TILE = 1024
Z_GLA_Q, Z_GLA_K, Z_GLA_V, Z_GLA_G = 0, 512, 1024, 2048
Z_DIFF_Q, Z_DIFF_K, Z_MEM_Q = 3072, 4096, 5120
Z_WIDTH = 6144
W_SRC_TILES = (0, 1, 2, 3, 4, 6)
W_DIFF_V_TILE = 5
W_ROWS = 7 * IN_TILE + GLA_RANK
W_FIRST_SHIFTED_TILE = 3


def _params(*sem):
    return pltpu.CompilerParams(dimension_semantics=sem, vmem_limit_bytes=VMEM_LIMIT)


def _nt_dot(a, b):
    return lax.dot_general(a, b, (((1,), (1,)), ((), ())), preferred_element_type=F32)


def _tn_dot(a, b, precision=None):
    return lax.dot_general(a, b, (((0,), (0,)), ((), ())), preferred_element_type=F32,
                           precision=precision)


def _rms(v, gain):
    ms = jnp.mean(v * v, axis=-1, keepdims=True)
    return v * lax.rsqrt(ms + NORM_EPS) * gain


def _sigmoid(v):
    return 1.0 / (1.0 + jnp.exp(-v))


BF16_SUBLANES = 16


def _rider_specs(weights, n_steps, step_of):
    specs, shapes = [], []
    for w in weights:
        rows, cols = w.shape
        chunk = BF16_SUBLANES
        while rows % chunk or rows // chunk > n_steps:
            chunk += BF16_SUBLANES
        last = rows // chunk - 1
        specs.append(pl.BlockSpec((chunk, cols),
                                  lambda *g, last=last: (jnp.minimum(step_of(*g), last), 0)))
        shapes.append(jax.ShapeDtypeStruct((rows, cols), BF16))
    return specs, shapes


def _cast_riders(in_refs, out_refs):
    for src, dst in zip(in_refs, out_refs):
        dst[...] = src[...].astype(BF16)


def _norm_mix_kernel(x_ref, g_ref, wa_ref, wup_ref, bal_ref, h_ref, la_ref):
    h = _rms(x_ref[...], g_ref[...]).astype(BF16)
    h_ref[...] = h
    a_low = _nt_dot(h, wa_ref[...].astype(BF16))
    w_up = wup_ref[...]
    a_hi, w_hi = a_low.astype(BF16), w_up.astype(BF16)
    a_lo = (a_low - a_hi.astype(F32)).astype(BF16)
    w_lo = (w_up - w_hi.astype(F32)).astype(BF16)
    pre = (jnp.dot(a_hi, w_hi, preferred_element_type=F32)
           + (jnp.dot(a_lo, w_hi, preferred_element_type=F32)
              + jnp.dot(a_hi, w_lo, preferred_element_type=F32))) + bal_ref[...]
    log_sig = jnp.minimum(pre, 0.0) - jnp.log1p(jnp.exp(-jnp.abs(pre)))
    la_ref[...] = log_sig * (1.0 / GLA_GATE_NORM)


def _norm_mix(x2, g, w_in_t, wup, bal, tr=512):
    m, d = x2.shape
    n = wup.shape[1]
    fixed = lambda i: (0, 0)
    decay_block = (Z_GLA_G + GLA_HEADS * GLA_DV) // LANES
    return pl.pallas_call(
        _norm_mix_kernel,
        out_shape=(jax.ShapeDtypeStruct((m, d), BF16), jax.ShapeDtypeStruct((m, n), F32)),
        grid=(m // tr,),
        in_specs=[pl.BlockSpec((tr, d), lambda i: (i, 0)),
                  pl.BlockSpec((1, d), fixed),
                  pl.BlockSpec((LANES, d), lambda i: (decay_block, 0)),
                  pl.BlockSpec((LANES, n), fixed),
                  pl.BlockSpec((1, n), fixed)],
        out_specs=(pl.BlockSpec((tr, d), lambda i: (i, 0)),
                   pl.BlockSpec((tr, n), lambda i: (i, 0))),
        compiler_params=_params("parallel"),
        name="norm_mix",
    )(x2, g, w_in_t, wup, bal)


def _norm_rows_kernel(x_ref, g_ref, h_ref):
    h_ref[...] = _rms(x_ref[...], g_ref[...]).astype(BF16)


def _norm_rows(x2, g, tr=512):
    m, d = x2.shape
    return pl.pallas_call(
        _norm_rows_kernel,
        out_shape=jax.ShapeDtypeStruct((m, d), BF16),
        grid=(m // tr,),
        in_specs=[pl.BlockSpec((tr, d), lambda i: (i, 0)),
                  pl.BlockSpec((1, d), lambda i: (0, 0))],
        out_specs=pl.BlockSpec((tr, d), lambda i: (i, 0)),
        compiler_params=_params("parallel"),
        name="norm_rows",
    )(x2, g)


def _store_group_norm(acc, gain, width, scale, out_ref):
    for s in range(0, acc.shape[1], width):
        blk = acc[:, s:s + width]
        out_ref[:, s:s + width] = (_rms(blk, gain) * scale).astype(out_ref.dtype)


def _w_tile_specs(d, src_tile):
    hi_per_tile = IN_TILE // GLA_RANK
    return [pl.BlockSpec((IN_TILE, d), lambda *g: (src_tile(*g), 0)),
            pl.BlockSpec((GLA_RANK, d), lambda *g: ((src_tile(*g) + 1) * hi_per_tile, 0))]


def _cast_w_tile(w_lo_ref, w_hi_ref, w_scr, first_step, shifted):
    @pl.when(first_step & jnp.logical_not(shifted))
    def _():
        w_scr[...] = w_lo_ref[...].astype(BF16)

    @pl.when(first_step & shifted)
    def _():
        w_scr[:IN_TILE - GLA_RANK, :] = w_lo_ref[GLA_RANK:, :].astype(BF16)
        w_scr[IN_TILE - GLA_RANK:, :] = w_hi_ref[...].astype(BF16)


def _in_proj_kernel(h_ref, w_lo_ref, w_hi_ref, dq_g_ref, dk_g_ref, mq_g_ref, r0_ref, r1_ref,
                    z_ref, c0_ref, c1_ref, w_scr):
    _cast_riders((r0_ref, r1_ref), (c0_ref, c1_ref))
    j = pl.program_id(0)
    _cast_w_tile(w_lo_ref, w_hi_ref, w_scr, pl.program_id(1) == 0, j >= W_FIRST_SHIFTED_TILE)
    acc = _nt_dot(h_ref[...], w_scr[...])
    j_dq, j_dk, j_mq = Z_DIFF_Q // IN_TILE, Z_DIFF_K // IN_TILE, Z_MEM_Q // IN_TILE

    @pl.when((j != j_dq) & (j != j_dk) & (j != j_mq))
    def _():
        z_ref[...] = acc.astype(z_ref.dtype)

    @pl.when(j == j_dq)
    def _():
        _store_group_norm(acc, dq_g_ref[...], DIFF_DH, DIFF_Q_SCALE, z_ref)

    @pl.when(j == j_dk)
    def _():
        _store_group_norm(acc, dk_g_ref[...], DIFF_DH, 1.0, z_ref)

    @pl.when(j == j_mq)
    def _():
        _store_group_norm(acc, mq_g_ref[...], MEM_DH, MEM_DH ** -0.5, z_ref)


def _in_proj(h, w_in_t, dq_g, dk_g, mq_g, riders, tm=1024):
    m, d = h.shape
    assert w_in_t.shape[0] == W_ROWS
    nj, ni = Z_WIDTH // IN_TILE, m // tm
    assert W_SRC_TILES == tuple(j + (j >= W_DIFF_V_TILE) for j in range(nj))
    r_specs, r_shapes = _rider_specs(riders, nj * ni, lambda j, i: j * ni + i)
    return pl.pallas_call(
        _in_proj_kernel,
        out_shape=(jax.ShapeDtypeStruct((m, Z_WIDTH), BF16), *r_shapes),
        grid=(nj, ni),
        in_specs=[pl.BlockSpec((tm, d), lambda j, i: (i, 0)),
                  *_w_tile_specs(d, lambda j, i: jnp.where(j >= W_DIFF_V_TILE, j + 1, j)),
                  pl.BlockSpec((1, DIFF_DH), lambda j, i: (0, 0)),
                  pl.BlockSpec((1, DIFF_DH), lambda j, i: (0, 0)),
                  pl.BlockSpec((1, MEM_DH), lambda j, i: (0, 0)),
                  *r_specs],
        out_specs=(pl.BlockSpec((tm, IN_TILE), lambda j, i: (i, j)), *r_specs),
        scratch_shapes=[pltpu.VMEM((IN_TILE, d), BF16)],
        compiler_params=_params("arbitrary", "arbitrary"),
        name="in_proj",
    )(h, w_in_t, w_in_t, dq_g, dk_g, mq_g, *riders)


def _v_proj_t_kernel(h_ref, w_lo_ref, w_hi_ref, vt_ref, w_scr):
    _cast_w_tile(w_lo_ref, w_hi_ref, w_scr, pl.program_id(0) == 0,
                 W_DIFF_V_TILE >= W_FIRST_SHIFTED_TILE)
    vt_ref[...] = _nt_dot(w_scr[...], h_ref[...]).astype(vt_ref.dtype)


def _v_proj_t(h, w_in_t, tm=1024):
    m, d = h.shape
    return pl.pallas_call(
        _v_proj_t_kernel,
        out_shape=jax.ShapeDtypeStruct((IN_TILE, m), BF16),
        grid=(m // tm,),
        in_specs=[pl.BlockSpec((tm, d), lambda i: (i, 0)),
                  *_w_tile_specs(d, lambda i: W_DIFF_V_TILE)],
        out_specs=pl.BlockSpec((IN_TILE, tm), lambda i: (0, i)),
        scratch_shapes=[pltpu.VMEM((IN_TILE, d), BF16)],
        compiler_params=_params("arbitrary"),
        name="v_proj_t",
    )(h, w_in_t, w_in_t)


def _chunk_cumsum(x):
    row_in_chunk = lax.broadcasted_iota(jnp.int32, x.shape, 0) % CHUNK
    shift = 1
    while shift < CHUNK:
        x = x + jnp.where(row_in_chunk >= shift, pltpu.roll(x, shift, 0), 0.0)
        shift *= 2
    return x


def _gla_kernel(q_ref, k_ref, v_ref, g_ref, la_ref, ng_ref, o_ref, s_ref, *, ts):
    @pl.when(pl.program_id(2) == 0)
    def _():
        s_ref[...] = jnp.zeros_like(s_ref)

    bcum_all = _chunk_cumsum(la_ref[...])
    row = lax.broadcasted_iota(jnp.int32, (CHUNK, CHUNK), 0)
    col = lax.broadcasted_iota(jnp.int32, (CHUNK, CHUNK), 1)
    causal = row >= col

    for c in range(ts // CHUNK):
        rows = slice(c * CHUNK, (c + 1) * CHUNK)
        bcum = bcum_all[rows]
        b_last = bcum[CHUNK - 1:CHUNK, :]
        q = q_ref[rows, :].astype(F32) * (GLA_DK ** -0.5)
        k = k_ref[rows, :].astype(F32)
        v = v_ref[rows, :]
        q_dec = (q * jnp.exp(bcum)).astype(BF16)
        k_dec = (k * jnp.exp(-bcum)).astype(BF16)
        k_tail = (k * jnp.exp(b_last - bcum)).astype(BF16)
        att = jnp.where(causal, _nt_dot(q_dec, k_dec), 0.0).astype(BF16)
        state = s_ref[...]
        o = jnp.dot(jnp.concatenate([q_dec, att], axis=1),
                    jnp.concatenate([state.astype(BF16), v], axis=0), preferred_element_type=F32)
        inc = _tn_dot(k_tail, v)
        decay_col = jnp.broadcast_to(jnp.exp(b_last), (GLA_DK, GLA_DK)).T
        decay = jnp.concatenate([decay_col] * (GLA_DV // GLA_DK), axis=1)
        s_ref[...] = decay * state + inc
        gate = g_ref[rows, :].astype(F32)
        o_ref[rows, :] = (_rms(o, ng_ref[...]) * (gate * _sigmoid(gate))).astype(o_ref.dtype)


def _gla(z, log_a, ng, batch, seq, ts=2048):
    m = z.shape[0]
    nt = seq // ts
    rows = lambda b, h, t: b * nt + t
    return pl.pallas_call(
        functools.partial(_gla_kernel, ts=ts),
        out_shape=jax.ShapeDtypeStruct((m, GLA_HEADS * GLA_DV), BF16),
        grid=(batch, GLA_HEADS, nt),
        in_specs=[pl.BlockSpec((ts, GLA_DK), lambda b, h, t: (rows(b, h, t), Z_GLA_Q // GLA_DK + h)),
                  pl.BlockSpec((ts, GLA_DK), lambda b, h, t: (rows(b, h, t), Z_GLA_K // GLA_DK + h)),
                  pl.BlockSpec((ts, GLA_DV), lambda b, h, t: (rows(b, h, t), Z_GLA_V // GLA_DV + h)),
                  pl.BlockSpec((ts, GLA_DV), lambda b, h, t: (rows(b, h, t), Z_GLA_G // GLA_DV + h)),
                  pl.BlockSpec((ts, GLA_DK), lambda b, h, t: (rows(b, h, t), h)),
                  pl.BlockSpec((1, GLA_DV), lambda b, h, t: (0, 0))],
        out_specs=pl.BlockSpec((ts, GLA_DV), lambda b, h, t: (rows(b, h, t), h)),
        scratch_shapes=[pltpu.VMEM((GLA_DK, GLA_DV), F32)],
        compiler_params=_params("parallel", "parallel", "arbitrary"),
        name="gla",
    )(z, z, z, z, log_a, ng)


def _diff_kernel(q_ref, k_ref, vt_ref, qg_ref, kg_ref, lq1_ref, lk1_ref, lq2_ref, lk2_ref, sg_ref,
                 o_ref, acc_scr, m_scr, l_scr, *, tq):
    qi = pl.program_id(2)
    m_scr[...] = jnp.full_like(m_scr, NEG_INF)
    l_scr[...] = jnp.zeros_like(l_scr)
    acc_scr[...] = jnp.zeros_like(acc_scr)
    n_streams = q_ref.shape[1] // DIFF_DH
    half = tq // 2

    score_bound = (1.02 * DIFF_DH * DIFF_Q_SCALE) * (jnp.max(jnp.abs(qg_ref[...]))
                                                    * jnp.max(jnp.abs(kg_ref[...])))
    bounded = score_bound <= SCORE_LIMIT

    def update(c, lanes, s, vt, shifted):
        if shifted:
            m_old = m_scr[c, :, lanes]
            m_new = jnp.maximum(m_old, jnp.max(s, axis=0, keepdims=True))
            alpha = jnp.exp2(m_old - m_new)
            p = jnp.exp2(s - m_new)
            l_scr[c, :, lanes] = alpha * l_scr[c, :, lanes] + jnp.sum(p, axis=0, keepdims=True)
            m_scr[c, :, lanes] = m_new
            acc_scr[c, :, lanes] = (alpha * acc_scr[c, :, lanes]
                                    + jnp.dot(vt, p.astype(BF16), preferred_element_type=F32))
        else:
            p = jnp.exp2(s)
            l_scr[c, :, lanes] = l_scr[c, :, lanes] + jnp.sum(p, axis=0, keepdims=True)
            acc_scr[c, :, lanes] = (acc_scr[c, :, lanes]
                                    + jnp.dot(vt, p.astype(BF16), preferred_element_type=F32))

    def values_t(c, start, n):
        head = c // 2
        return vt_ref[head * DIFF_DV:(head + 1) * DIFF_DV, pl.ds(start, n)]

    def full_block(kb, carry, shifted):
        start = pl.multiple_of(kb * tq, tq)
        scores = []
        for c in range(n_streams):
            cols = slice(c * DIFF_DH, (c + 1) * DIFF_DH)
            scores.append(_nt_dot(k_ref[pl.ds(start, tq), cols], q_ref[:, cols]))
        for c in range(n_streams):
            update(c, slice(0, tq), scores[c], values_t(c, start, tq), shifted)
        return carry

    def diag_block(kb, shifted):
        lo = pl.multiple_of(kb * tq, tq)
        hi = pl.multiple_of(kb * tq + half, half)
        key_chunk = lax.broadcasted_iota(jnp.int32, (half, half), 0) // CHUNK
        query_chunk = lax.broadcasted_iota(jnp.int32, (half, half), 1) // CHUNK
        visible = key_chunk <= query_chunk
        scores = []
        for c in range(n_streams):
            cols = slice(c * DIFF_DH, (c + 1) * DIFF_DH)
            k_lo, k_hi = k_ref[pl.ds(lo, half), cols], k_ref[pl.ds(hi, half), cols]
            q_lo, q_hi = q_ref[:half, cols], q_ref[half:, cols]
            s_lo = jnp.where(visible, _nt_dot(k_lo, q_lo), NEG_INF)
            s_hi = jnp.concatenate([_nt_dot(k_lo, q_hi),
                                    jnp.where(visible, _nt_dot(k_hi, q_hi), NEG_INF)], axis=0)
            scores.append((s_lo, s_hi))
        for c in range(n_streams):
            update(c, slice(0, half), scores[c][0], values_t(c, lo, half), shifted)
            update(c, slice(half, tq), scores[c][1], values_t(c, lo, tq), shifted)

    def all_blocks(shifted):
        lax.fori_loop(0, qi, functools.partial(full_block, shifted=shifted), 0)
        diag_block(qi, shifted)

    @pl.when(bounded)
    def _():
        all_blocks(shifted=False)

    @pl.when(jnp.logical_not(bounded))
    def _():
        all_blocks(shifted=True)

    lam = (jnp.exp(jnp.sum(lq1_ref[...] * lk1_ref[...], axis=-1, keepdims=True))
           - jnp.exp(jnp.sum(lq2_ref[...] * lk2_ref[...], axis=-1, keepdims=True)) + LAM_INIT)
    for head in range(n_streams // 2):
        c1, c2 = 2 * head, 2 * head + 1
        o_t = acc_scr[c1] / l_scr[c1] - lam * (acc_scr[c2] / l_scr[c2])
        o_ref[:, head * DIFF_DV:(head + 1) * DIFF_DV] = (
            _rms(o_t.T, sg_ref[...]) * (1.0 - LAM_INIT)).astype(o_ref.dtype)


def _diff_attn(z, v_t, qg, kg, lq1, lk1, lq2, lk2, sg, batch, seq, tq=512, heads_per_step=4):
    m = z.shape[0]
    nq = seq // tq
    width = heads_per_step * DIFF_DV
    ns = 2 * heads_per_step
    vec = pl.BlockSpec((1, DIFF_DH), lambda b, g, i: (0, 0))
    return pl.pallas_call(
        functools.partial(_diff_kernel, tq=tq),
        out_shape=jax.ShapeDtypeStruct((m, DIFF_HEADS * DIFF_DV), BF16),
        grid=(batch, DIFF_HEADS // heads_per_step, nq),
        in_specs=[pl.BlockSpec((tq, width), lambda b, g, i: (b * nq + i, Z_DIFF_Q // width + g)),
                  pl.BlockSpec((seq, width), lambda b, g, i: (b, Z_DIFF_K // width + g)),
                  pl.BlockSpec((width, seq), lambda b, g, i: (g, b)),
                  vec, vec, vec, vec, vec, vec,
                  pl.BlockSpec((1, DIFF_DV), lambda b, g, i: (0, 0))],
        out_specs=pl.BlockSpec((tq, width), lambda b, g, i: (b * nq + i, g)),
        scratch_shapes=[pltpu.VMEM((ns, DIFF_DV, tq), F32),
                        pltpu.VMEM((ns, 1, tq), F32),
                        pltpu.VMEM((ns, 1, tq), F32)],
        compiler_params=_params("parallel", "parallel", "arbitrary"),
        name="diff_attn",
    )(z, z, v_t, qg, kg, lq1, lk1, lq2, lk2, sg)


def _mem_kv_kernel(mn_ref, w_ref, kg_ref, kv_ref, *, n_key_tiles):
    j = pl.program_id(0)
    acc = jnp.dot(mn_ref[...], w_ref[...].astype(BF16), preferred_element_type=F32)

    @pl.when(j < n_key_tiles)
    def _():
        _store_group_norm(acc, kg_ref[...], MEM_DH, 1.0, kv_ref)

    @pl.when(j >= n_key_tiles)
    def _():
        kv_ref[...] = acc.astype(kv_ref.dtype)


def _mem_kv(mem_n, w_kv, kg, tn=512):
    m, d = mem_n.shape
    n = w_kv.shape[1]
    return pl.pallas_call(
        functools.partial(_mem_kv_kernel, n_key_tiles=(n // 2) // tn),
        out_shape=jax.ShapeDtypeStruct((m, n), BF16),
        grid=(n // tn,),
        in_specs=[pl.BlockSpec((m, d), lambda j: (0, 0)),
                  pl.BlockSpec((d, tn), lambda j: (0, j)),
                  pl.BlockSpec((1, MEM_DH), lambda j: (0, 0))],
        out_specs=pl.BlockSpec((m, tn), lambda j: (0, j)),
        compiler_params=_params("parallel"),
        name="mem_kv",
    )(mem_n, w_kv, kg)


def _mem_attn_kernel(q_ref, k_ref, v_ref, o_ref):
    for head in range(MEM_HEADS):
        cols = slice(head * MEM_DH, (head + 1) * MEM_DH)
        s = _nt_dot(q_ref[:, cols], k_ref[:, cols])
        e = jnp.exp(s - jnp.max(s, axis=-1, keepdims=True))
        p = (e / jnp.sum(e, axis=-1, keepdims=True)).astype(BF16)
        o_ref[:, cols] = jnp.dot(p, v_ref[:, cols], preferred_element_type=F32).astype(o_ref.dtype)


def _mem_attn(z, kv, batch, seq, n_mem, tm=512):
    m = z.shape[0]
    nt = seq // tm
    width = MEM_HEADS * MEM_DH
    return pl.pallas_call(
        _mem_attn_kernel,
        out_shape=jax.ShapeDtypeStruct((m, width), BF16),
        grid=(batch, nt),
        in_specs=[pl.BlockSpec((tm, width), lambda b, t: (b * nt + t, Z_MEM_Q // width)),
                  pl.BlockSpec((n_mem, width), lambda b, t: (b, 0)),
                  pl.BlockSpec((n_mem, width), lambda b, t: (b, 1))],
        out_specs=pl.BlockSpec((tm, width), lambda b, t: (b * nt + t, 0)),
        compiler_params=_params("parallel", "parallel"),
        name="mem_attn",
    )(z, kv, kv)


def _gate_merge_kernel(h_ref, y0_ref, y1_ref, y2_ref, wg0_ref, wg1_ref, wg2_ref,
                       bg0_ref, bg1_ref, bg2_ref, wb0_ref, wb1_ref, wb2_ref, r0_ref, r1_ref,
                       o_ref, c0_ref, c1_ref):
    _cast_riders((r0_ref, r1_ref), (c0_ref, c1_ref))
    h = h_ref[...]
    merged = None
    for y_ref, wg_ref, bg_ref, wb_ref in ((y0_ref, wg0_ref, bg0_ref, wb0_ref),
                                          (y1_ref, wg1_ref, bg1_ref, wb1_ref),
                                          (y2_ref, wg2_ref, bg2_ref, wb2_ref)):
        gate = _sigmoid(jnp.dot(h, wg_ref[...], preferred_element_type=F32) + bg_ref[...])
        term = gate * jnp.dot(y_ref[...], wb_ref[...], preferred_element_type=F32)
        merged = term if merged is None else merged + term
    o_ref[...] = merged.astype(o_ref.dtype)


def _gate_merge(h, ys, w_gate, b_gate, w_branch, riders, tm=1024, tn=256):
    m, d = h.shape
    bw = w_branch.shape[1]
    ni, nj = m // tm, d // tn
    act = lambda width: pl.BlockSpec((tm, width), lambda i, j: (i, 0))
    wg = lambda b: pl.BlockSpec((d, tn), lambda i, j: (0, b * nj + j))
    bg = lambda b: pl.BlockSpec((1, tn), lambda i, j: (0, b * nj + j))
    wb = lambda b: pl.BlockSpec((None, bw, tn), lambda i, j: (b, 0, j))
    r_specs, r_shapes = _rider_specs(riders, ni * nj, lambda i, j: i * nj + j)
    return pl.pallas_call(
        _gate_merge_kernel,
        out_shape=(jax.ShapeDtypeStruct((m, d), BF16), *r_shapes),
        grid=(ni, nj),
        in_specs=[act(d), act(bw), act(bw), act(bw), wg(0), wg(1), wg(2), bg(0), bg(1), bg(2),
                  wb(0), wb(1), wb(2), *r_specs],
        out_specs=(pl.BlockSpec((tm, tn), lambda i, j: (i, j)), *r_specs),
        compiler_params=_params("arbitrary", "arbitrary"),
        name="gate_merge",
    )(h, ys[0], ys[1], ys[2], w_gate, w_gate, w_gate, b_gate, b_gate, b_gate,
      w_branch, w_branch, w_branch, *riders)


def _out_proj_kernel(mg_ref, w_ref, x_ref, g_ref, x1_ref, hf_ref):
    x1 = x_ref[...] + jnp.dot(mg_ref[...], w_ref[...], preferred_element_type=F32)
    x1_ref[...] = x1
    hf_ref[...] = _rms(x1, g_ref[...]).astype(hf_ref.dtype)


def _out_proj(merged, w_out, x2, g, tm=512):
    m, d = x2.shape
    row = lambda i: (i, 0)
    fixed = lambda i: (0, 0)
    return pl.pallas_call(
        _out_proj_kernel,
        out_shape=(jax.ShapeDtypeStruct((m, d), F32), jax.ShapeDtypeStruct((m, d), BF16)),
        grid=(m // tm,),
        in_specs=[pl.BlockSpec((tm, d), row),
                  pl.BlockSpec((d, d), fixed, pipeline_mode=pl.Buffered(1)),
                  pl.BlockSpec((tm, d), row), pl.BlockSpec((1, d), fixed)],
        out_specs=(pl.BlockSpec((tm, d), row), pl.BlockSpec((tm, d), row)),
        compiler_params=_params("parallel"),
        name="out_proj",
    )(merged, w_out, x2, g)


def _ffn_up_kernel(hf_ref, wg_ref, wu_ref, r0_ref, a_ref, c0_ref):
    _cast_riders((r0_ref,), (c0_ref,))
    hf = hf_ref[...]
    gate = jnp.dot(hf, wg_ref[...], preferred_element_type=F32)
    up = jnp.dot(hf, wu_ref[...], preferred_element_type=F32)
    a_ref[...] = (gate * _sigmoid(gate) * up).astype(a_ref.dtype)


def _ffn_up(hf, w_in, riders, tm=1024, tf=512):
    m, d = hf.shape
    d_ff = w_in.shape[1] // 2
    ni, nj = m // tm, d_ff // tf
    r_specs, r_shapes = _rider_specs(riders, ni * nj, lambda i, j: i * nj + j)
    return pl.pallas_call(
        _ffn_up_kernel,
        out_shape=(jax.ShapeDtypeStruct((m, d_ff), BF16), *r_shapes),
        grid=(ni, nj),
        in_specs=[pl.BlockSpec((tm, d), lambda i, j: (i, 0)),
                  pl.BlockSpec((d, tf), lambda i, j: (0, j)),
                  pl.BlockSpec((d, tf), lambda i, j: (0, nj + j)),
                  *r_specs],
        out_specs=(pl.BlockSpec((tm, tf), lambda i, j: (i, j)), *r_specs),
        compiler_params=_params("arbitrary", "arbitrary"),
        name="ffn_up",
    )(hf, w_in, w_in, *riders)


def _ffn_down_kernel(a_ref, w_ref, x1_ref, o_ref):
    o_ref[...] = x1_ref[...] + jnp.dot(a_ref[...], w_ref[...], preferred_element_type=F32)


def _ffn_down(a, w_down, x1, tm=1024, tn=512):
    m, d_ff = a.shape
    d = w_down.shape[1]
    return pl.pallas_call(
        _ffn_down_kernel,
        out_shape=jax.ShapeDtypeStruct((m, d), F32),
        grid=(m // tm, d // tn),
        in_specs=[pl.BlockSpec((tm, d_ff), lambda i, j: (i, 0)),
                  pl.BlockSpec((d_ff, tn), lambda i, j: (0, j)),
                  pl.BlockSpec((tm, tn), lambda i, j: (i, j))],
        out_specs=pl.BlockSpec((tm, tn), lambda i, j: (i, j)),
        compiler_params=_params("parallel", "arbitrary"),
        name="ffn_down",
    )(a, w_down, x1)


def kernel(x, mem, norm_mix_g, norm_mem_g, w_in, gla_w_alpha_up, gla_b_alpha, gla_norm_g,
           diff_q_norm_g, diff_k_norm_g, diff_lambda_q1, diff_lambda_k1, diff_lambda_q2,
           diff_lambda_k2, diff_subln_g, mem_q_norm_g, mem_k_norm_g, w_mem_kv, w_branch,
           w_gate, b_gate, w_out, norm_ffn_g, w_ffn_in, w_ffn_down):
    batch, seq, d = x.shape
    n_mem = mem.shape[1]
    depth = w_in.shape[0]
    assert depth == 1, "LAM_INIT is the layer-0 value"
    x2 = x.reshape(batch * seq, d)
    mem2 = mem.reshape(batch * n_mem, d)
    for l in range(depth):
        w_in_t = w_in[l].T
        w_up = jnp.pad(gla_w_alpha_up[l], ((0, LANES - GLA_RANK), (0, 0)))
        row = lambda v: v.reshape(1, -1)

        h, log_a = _norm_mix(x2, row(norm_mix_g[l]), w_in_t, w_up, row(gla_b_alpha[l]))
        z, w_gate_bf, w_branch_bf = _in_proj(
            h, w_in_t, row(diff_q_norm_g[l]), row(diff_k_norm_g[l]), row(mem_q_norm_g[l]),
            riders=(w_gate[l], w_branch[l].reshape(-1, d)))
        y_gla = _gla(z, log_a, row(gla_norm_g[l]), batch, seq)
        v_t = _v_proj_t(h, w_in_t)
        y_diff = _diff_attn(z, v_t, row(diff_q_norm_g[l]), row(diff_k_norm_g[l]),
                            row(diff_lambda_q1[l]), row(diff_lambda_k1[l]),
                            row(diff_lambda_q2[l]), row(diff_lambda_k2[l]),
                            row(diff_subln_g[l]), batch, seq)
        mem_n = _norm_rows(mem2, row(norm_mem_g[l]))
        kv = _mem_kv(mem_n, w_mem_kv[l], row(mem_k_norm_g[l]))
        y_mem = _mem_attn(z, kv, batch, seq, n_mem)
        merged, w_out_bf, w_ffn_in_bf = _gate_merge(
            h, (y_gla, y_diff, y_mem), w_gate_bf, row(b_gate[l]),
            w_branch_bf.reshape(w_branch[l].shape), riders=(w_out[l], w_ffn_in[l]))
        x1, hf = _out_proj(merged, w_out_bf, x2, row(norm_ffn_g[l]))
        a, w_ffn_down_bf = _ffn_up(hf, w_ffn_in_bf, riders=(w_ffn_down[l],))
        x2 = _ffn_down(a, w_ffn_down_bf, x1)
    return x2.reshape(batch, seq, d)
```

```python
import functools

import jax
import jax.numpy as jnp
from jax import lax
from jax.experimental import pallas as pl
from jax.experimental.pallas import tpu as pltpu

F32 = jnp.float32
BF16 = jnp.bfloat16

CHUNK = 64
GLA_HEADS = 4
GLA_DK = 128
GLA_DV = 256
GLA_RANK = 16
GLA_GATE_NORM = 16.0
DIFF_HEADS = 4
DIFF_DH = 128
DIFF_DV = 256
MEM_HEADS = 4
MEM_DH = 256
N_BRANCH = 3
NORM_EPS = 1e-6
NEG_INF = -1e30
LAM_INIT = 0.8 - 0.6 * 1.0
LOG2_E = 1.4426950408889634
SCORE_LIMIT = 60.0
DIFF_Q_SCALE = DIFF_DH ** -0.5 * LOG2_E

LANES = 128
VMEM_LIMIT = 56 * 1024 * 1024

IN_TILE = 1024
Z_GLA_Q, Z_GLA_K, Z_GLA_V, Z_GLA_G = 0, 512, 1024, 2048
Z_DIFF_Q, Z_DIFF_K, Z_MEM_Q = 3072, 4096, 5120
Z_WIDTH = 6144
W_SRC_TILES = (0, 1, 2, 3, 4, 6)
W_DIFF_V_TILE = 5
W_ROWS = 7 * IN_TILE + GLA_RANK
W_FIRST_SHIFTED_TILE = 3


def _params(*sem):
    return pltpu.CompilerParams(dimension_semantics=sem, vmem_limit_bytes=VMEM_LIMIT)


def _nt_dot(a, b):
    return lax.dot_general(a, b, (((1,), (1,)), ((), ())), preferred_element_type=F32)


def _tn_dot(a, b, precision=None):
    return lax.dot_general(a, b, (((0,), (0,)), ((), ())), preferred_element_type=F32,
                           precision=precision)


def _rms(v, gain):
    ms = jnp.mean(v * v, axis=-1, keepdims=True)
    return v * lax.rsqrt(ms + NORM_EPS) * gain


def _sigmoid(v):
    return 1.0 / (1.0 + jnp.exp(-v))


BF16_SUBLANES = 16


def _rider_specs(weights, n_steps, step_of):
    specs, shapes = [], []
    for w in weights:
        rows, cols = w.shape
        chunk = BF16_SUBLANES
        while rows % chunk or rows // chunk > n_steps:
            chunk += BF16_SUBLANES
        last = rows // chunk - 1
        specs.append(pl.BlockSpec((chunk, cols),
                                  lambda *g, last=last: (jnp.minimum(step_of(*g), last), 0)))
        shapes.append(jax.ShapeDtypeStruct((rows, cols), BF16))
    return specs, shapes


def _cast_riders(in_refs, out_refs):
    for src, dst in zip(in_refs, out_refs):
        dst[...] = src[...].astype(BF16)


def _norm_mix_kernel(x_ref, g_ref, wa_ref, wup_ref, bal_ref, h_ref, la_ref):
    h = _rms(x_ref[...], g_ref[...]).astype(BF16)
    h_ref[...] = h
    a_low = _nt_dot(h, wa_ref[...].astype(BF16))
    w_up = wup_ref[...]
    a_hi, w_hi = a_low.astype(BF16), w_up.astype(BF16)
    a_lo = (a_low - a_hi.astype(F32)).astype(BF16)
    w_lo = (w_up - w_hi.astype(F32)).astype(BF16)
    pre = (jnp.dot(a_hi, w_hi, preferred_element_type=F32)
           + (jnp.dot(a_lo, w_hi, preferred_element_type=F32)
              + jnp.dot(a_hi, w_lo, preferred_element_type=F32))) + bal_ref[...]
    log_sig = jnp.minimum(pre, 0.0) - jnp.log1p(jnp.exp(-jnp.abs(pre)))
    la_ref[...] = log_sig * (1.0 / GLA_GATE_NORM)


def _norm_mix(x2, g, w_in_t, wup, bal, tr=512):
    m, d = x2.shape
    n = wup.shape[1]
    fixed = lambda i: (0, 0)
    decay_block = (Z_GLA_G + GLA_HEADS * GLA_DV) // LANES
    return pl.pallas_call(
        _norm_mix_kernel,
        out_shape=(jax.ShapeDtypeStruct((m, d), BF16), jax.ShapeDtypeStruct((m, n), F32)),
        grid=(m // tr,),
        in_specs=[pl.BlockSpec((tr, d), lambda i: (i, 0)),
                  pl.BlockSpec((1, d), fixed),
                  pl.BlockSpec((LANES, d), lambda i: (decay_block, 0)),
                  pl.BlockSpec((LANES, n), fixed),
                  pl.BlockSpec((1, n), fixed)],
        out_specs=(pl.BlockSpec((tr, d), lambda i: (i, 0)),
                   pl.BlockSpec((tr, n), lambda i: (i, 0))),
        compiler_params=_params("parallel"),
        name="norm_mix",
    )(x2, g, w_in_t, wup, bal)


def _norm_rows_kernel(x_ref, g_ref, h_ref):
    h_ref[...] = _rms(x_ref[...], g_ref[...]).astype(BF16)


def _norm_rows(x2, g, tr=512):
    m, d = x2.shape
    return pl.pallas_call(
        _norm_rows_kernel,
        out_shape=jax.ShapeDtypeStruct((m, d), BF16),
        grid=(m // tr,),
        in_specs=[pl.BlockSpec((tr, d), lambda i: (i, 0)),
                  pl.BlockSpec((1, d), lambda i: (0, 0))],
        out_specs=pl.BlockSpec((tr, d), lambda i: (i, 0)),
        compiler_params=_params("parallel"),
        name="norm_rows",
    )(x2, g)


def _store_group_norm(acc, gain, width, scale, out_ref):
    for s in range(0, acc.shape[1], width):
        blk = acc[:, s:s + width]
        out_ref[:, s:s + width] = (_rms(blk, gain) * scale).astype(out_ref.dtype)


def _w_tile_specs(d, src_tile):
    hi_per_tile = IN_TILE // GLA_RANK
    return [pl.BlockSpec((IN_TILE, d), lambda *g: (src_tile(*g), 0)),
            pl.BlockSpec((GLA_RANK, d), lambda *g: ((src_tile(*g) + 1) * hi_per_tile, 0))]


def _cast_w_tile(w_lo_ref, w_hi_ref, w_scr, first_step, shifted):
    @pl.when(first_step & jnp.logical_not(shifted))
    def _():
        w_scr[...] = w_lo_ref[...].astype(BF16)

    @pl.when(first_step & shifted)
    def _():
        w_scr[:IN_TILE - GLA_RANK, :] = w_lo_ref[GLA_RANK:, :].astype(BF16)
        w_scr[IN_TILE - GLA_RANK:, :] = w_hi_ref[...].astype(BF16)


def _in_proj_kernel(h_ref, w_lo_ref, w_hi_ref, dq_g_ref, dk_g_ref, mq_g_ref, r0_ref, r1_ref,
                    z_ref, c0_ref, c1_ref, w_scr):
    _cast_riders((r0_ref, r1_ref), (c0_ref, c1_ref))
    j = pl.program_id(0)
    _cast_w_tile(w_lo_ref, w_hi_ref, w_scr, pl.program_id(1) == 0, j >= W_FIRST_SHIFTED_TILE)
    j_dq, j_dk, j_mq = Z_DIFF_Q // IN_TILE, Z_DIFF_K // IN_TILE, Z_MEM_Q // IN_TILE

    def tile(epilogue, row_parts):
        part = h_ref.shape[0] // row_parts
        for r in range(row_parts):
            rows = slice(r * part, (r + 1) * part)
            epilogue(_nt_dot(h_ref[rows, :], w_scr[...]), z_ref.at[rows, :])

    def plain(acc, out_ref):
        out_ref[...] = acc.astype(out_ref.dtype)

    @pl.when((j != j_dq) & (j != j_dk) & (j != j_mq))
    def _():
        tile(plain, 1)

    @pl.when(j == j_dq)
    def _():
        tile(lambda acc, out: _store_group_norm(acc, dq_g_ref[...], DIFF_DH, DIFF_Q_SCALE, out), 2)

    @pl.when(j == j_dk)
    def _():
        tile(lambda acc, out: _store_group_norm(acc, dk_g_ref[...], DIFF_DH, 1.0, out), 2)

    @pl.when(j == j_mq)
    def _():
        tile(lambda acc, out: _store_group_norm(acc, mq_g_ref[...], MEM_DH, MEM_DH ** -0.5, out), 2)


def _in_proj(h, w_in_t, dq_g, dk_g, mq_g, riders, tm=1024):
    m, d = h.shape
    assert w_in_t.shape[0] == W_ROWS
    nj, ni = Z_WIDTH // IN_TILE, m // tm
    assert W_SRC_TILES == tuple(j + (j >= W_DIFF_V_TILE) for j in range(nj))
    r_specs, r_shapes = _rider_specs(riders, nj * ni, lambda j, i: j * ni + i)
    return pl.pallas_call(
        _in_proj_kernel,
        out_shape=(jax.ShapeDtypeStruct((m, Z_WIDTH), BF16), *r_shapes),
        grid=(nj, ni),
        in_specs=[pl.BlockSpec((tm, d), lambda j, i: (i, 0)),
                  *_w_tile_specs(d, lambda j, i: jnp.where(j >= W_DIFF_V_TILE, j + 1, j)),
                  pl.BlockSpec((1, DIFF_DH), lambda j, i: (0, 0)),
                  pl.BlockSpec((1, DIFF_DH), lambda j, i: (0, 0)),
                  pl.BlockSpec((1, MEM_DH), lambda j, i: (0, 0)),
                  *r_specs],
        out_specs=(pl.BlockSpec((tm, IN_TILE), lambda j, i: (i, j)), *r_specs),
        scratch_shapes=[pltpu.VMEM((IN_TILE, d), BF16)],
        compiler_params=_params("arbitrary", "arbitrary"),
        name="in_proj",
    )(h, w_in_t, w_in_t, dq_g, dk_g, mq_g, *riders)


def _v_proj_t_kernel(h_ref, w_lo_ref, w_hi_ref, mq_ref, mk_ref, mv_ref, vt_ref, ym_ref, w_scr, *,
                     sub_rows):
    _cast_w_tile(w_lo_ref, w_hi_ref, w_scr, pl.program_id(0) == 0,
                 W_DIFF_V_TILE >= W_FIRST_SHIFTED_TILE)
    vt_ref[...] = _nt_dot(w_scr[...], h_ref[...]).astype(vt_ref.dtype)
    for r in range(0, mq_ref.shape[0], sub_rows):
        rows = slice(r, r + sub_rows)
        for head in range(MEM_HEADS):
            cols = slice(head * MEM_DH, (head + 1) * MEM_DH)
            s = _nt_dot(mq_ref[rows, cols], mk_ref[:, cols])
            e = jnp.exp(s - jnp.max(s, axis=-1, keepdims=True))
            p = (e / jnp.sum(e, axis=-1, keepdims=True)).astype(BF16)
            ym_ref[rows, cols] = jnp.dot(p, mv_ref[:, cols],
                                         preferred_element_type=F32).astype(ym_ref.dtype)


def _v_proj_t(h, w_in_t, z, kv, seq, n_mem, tm=1024, sub_rows=512):
    m, d = h.shape
    width = MEM_HEADS * MEM_DH
    tiles_per_batch = seq // tm
    return pl.pallas_call(
        functools.partial(_v_proj_t_kernel, sub_rows=sub_rows),
        out_shape=(jax.ShapeDtypeStruct((IN_TILE, m), BF16), jax.ShapeDtypeStruct((m, width), BF16)),
        grid=(m // tm,),
        in_specs=[pl.BlockSpec((tm, d), lambda i: (i, 0)),
                  *_w_tile_specs(d, lambda i: W_DIFF_V_TILE),
                  pl.BlockSpec((tm, width), lambda i: (i, Z_MEM_Q // width)),
                  pl.BlockSpec((n_mem, width), lambda i: (i // tiles_per_batch, 0)),
                  pl.BlockSpec((n_mem, width), lambda i: (i // tiles_per_batch, 1))],
        out_specs=(pl.BlockSpec((IN_TILE, tm), lambda i: (0, i)),
                   pl.BlockSpec((tm, width), lambda i: (i, 0))),
        scratch_shapes=[pltpu.VMEM((IN_TILE, d), BF16)],
        compiler_params=_params("arbitrary"),
        name="v_proj_t",
    )(h, w_in_t, w_in_t, z, kv, kv)


def _chunk_cumsum(x):
    row_in_chunk = lax.broadcasted_iota(jnp.int32, x.shape, 0) % CHUNK
    shift = 1
    while shift < CHUNK:
        x = x + jnp.where(row_in_chunk >= shift, pltpu.roll(x, shift, 0), 0.0)
        shift *= 2
    return x


def _gla_kernel(q_ref, k_ref, v_ref, g_ref, la_ref, ng_ref, o_ref, s_ref, *, ts):
    @pl.when(pl.program_id(2) == 0)
    def _():
        s_ref[...] = jnp.zeros_like(s_ref)

    bcum_all = _chunk_cumsum(la_ref[...])
    row = lax.broadcasted_iota(jnp.int32, (CHUNK, CHUNK), 0)
    col = lax.broadcasted_iota(jnp.int32, (CHUNK, CHUNK), 1)
    causal = row >= col

    for c in range(ts // CHUNK):
        rows = slice(c * CHUNK, (c + 1) * CHUNK)
        bcum = bcum_all[rows]
        b_last = bcum[CHUNK - 1:CHUNK, :]
        q = q_ref[rows, :].astype(F32) * (GLA_DK ** -0.5)
        k = k_ref[rows, :].astype(F32)
        v = v_ref[rows, :]
        q_dec = (q * jnp.exp(bcum)).astype(BF16)
        k_dec = (k * jnp.exp(-bcum)).astype(BF16)
        k_tail = (k * jnp.exp(b_last - bcum)).astype(BF16)
        att = jnp.where(causal, _nt_dot(q_dec, k_dec), 0.0).astype(BF16)
        state = s_ref[...]
        o = jnp.dot(jnp.concatenate([q_dec, att], axis=1),
                    jnp.concatenate([state.astype(BF16), v], axis=0), preferred_element_type=F32)
        inc = _tn_dot(k_tail, v)
        decay_col = jnp.broadcast_to(jnp.exp(b_last), (GLA_DK, GLA_DK)).T
        decay = jnp.concatenate([decay_col] * (GLA_DV // GLA_DK), axis=1)
        s_ref[...] = decay * state + inc
        gate = g_ref[rows, :].astype(F32)
        o_ref[rows, :] = (_rms(o, ng_ref[...]) * (gate * _sigmoid(gate))).astype(o_ref.dtype)


def _gla(z, log_a, ng, batch, seq, ts=2048):
    m = z.shape[0]
    nt = seq // ts
    rows = lambda b, h, t: b * nt + t
    return pl.pallas_call(
        functools.partial(_gla_kernel, ts=ts),
        out_shape=jax.ShapeDtypeStruct((m, GLA_HEADS * GLA_DV), BF16),
        grid=(batch, GLA_HEADS, nt),
        in_specs=[pl.BlockSpec((ts, GLA_DK), lambda b, h, t: (rows(b, h, t), Z_GLA_Q // GLA_DK + h)),
                  pl.BlockSpec((ts, GLA_DK), lambda b, h, t: (rows(b, h, t), Z_GLA_K // GLA_DK + h)),
                  pl.BlockSpec((ts, GLA_DV), lambda b, h, t: (rows(b, h, t), Z_GLA_V // GLA_DV + h)),
                  pl.BlockSpec((ts, GLA_DV), lambda b, h, t: (rows(b, h, t), Z_GLA_G // GLA_DV + h)),
                  pl.BlockSpec((ts, GLA_DK), lambda b, h, t: (rows(b, h, t), h)),
                  pl.BlockSpec((1, GLA_DV), lambda b, h, t: (0, 0))],
        out_specs=pl.BlockSpec((ts, GLA_DV), lambda b, h, t: (rows(b, h, t), h)),
        scratch_shapes=[pltpu.VMEM((GLA_DK, GLA_DV), F32)],
        compiler_params=_params("parallel", "parallel", "arbitrary"),
        name="gla",
    )(z, z, z, z, log_a, ng)


def _diff_kernel(q_ref, k_ref, vt_ref, qg_ref, kg_ref, lq1_ref, lk1_ref, lq2_ref, lk2_ref, sg_ref,
                 o_ref, acc_scr, m_scr, l_scr, *, tq):
    qi = pl.program_id(2)
    m_scr[...] = jnp.full_like(m_scr, NEG_INF)
    l_scr[...] = jnp.zeros_like(l_scr)
    acc_scr[...] = jnp.zeros_like(acc_scr)
    n_streams = q_ref.shape[1] // DIFF_DH
    half = tq // 2

    score_bound = (1.02 * DIFF_DH * DIFF_Q_SCALE) * (jnp.max(jnp.abs(qg_ref[...]))
                                                    * jnp.max(jnp.abs(kg_ref[...])))
    bounded = score_bound <= SCORE_LIMIT

    def update(c, lanes, s, vt, shifted):
        if shifted:
            m_old = m_scr[c, :, lanes]
            m_new = jnp.maximum(m_old, jnp.max(s, axis=0, keepdims=True))
            alpha = jnp.exp2(m_old - m_new)
            p = jnp.exp2(s - m_new)
            l_scr[c, :, lanes] = alpha * l_scr[c, :, lanes] + jnp.sum(p, axis=0, keepdims=True)
            m_scr[c, :, lanes] = m_new
            acc_scr[c, :, lanes] = (alpha * acc_scr[c, :, lanes]
                                    + jnp.dot(vt, p.astype(BF16), preferred_element_type=F32))
        else:
            p = jnp.exp2(s)
            l_scr[c, :, lanes] = l_scr[c, :, lanes] + jnp.sum(p, axis=0, keepdims=True)
            acc_scr[c, :, lanes] = (acc_scr[c, :, lanes]
                                    + jnp.dot(vt, p.astype(BF16), preferred_element_type=F32))

    def values_t(c, start, n):
        head = c // 2
        return vt_ref[head * DIFF_DV:(head + 1) * DIFF_DV, pl.ds(start, n)]

    def full_block(kb, carry, shifted):
        start = pl.multiple_of(kb * tq, tq)
        scores = []
        for c in range(n_streams):
            cols = slice(c * DIFF_DH, (c + 1) * DIFF_DH)
            scores.append(_nt_dot(k_ref[pl.ds(start, tq), cols], q_ref[:, cols]))
        for c in range(n_streams):
            update(c, slice(0, tq), scores[c], values_t(c, start, tq), shifted)
        return carry

    def diag_block(kb, shifted):
        lo = pl.multiple_of(kb * tq, tq)
        hi = pl.multiple_of(kb * tq + half, half)
        key_chunk = lax.broadcasted_iota(jnp.int32, (half, half), 0) // CHUNK
        query_chunk = lax.broadcasted_iota(jnp.int32, (half, half), 1) // CHUNK
        visible = key_chunk <= query_chunk
        scores = []
        for c in range(n_streams):
            cols = slice(c * DIFF_DH, (c + 1) * DIFF_DH)
            k_lo, k_hi = k_ref[pl.ds(lo, half), cols], k_ref[pl.ds(hi, half), cols]
            q_lo, q_hi = q_ref[:half, cols], q_ref[half:, cols]
            s_lo = jnp.where(visible, _nt_dot(k_lo, q_lo), NEG_INF)
            s_hi = jnp.concatenate([_nt_dot(k_lo, q_hi),
                                    jnp.where(visible, _nt_dot(k_hi, q_hi), NEG_INF)], axis=0)
            scores.append((s_lo, s_hi))
        for c in range(n_streams):
            update(c, slice(0, half), scores[c][0], values_t(c, lo, half), shifted)
            update(c, slice(half, tq), scores[c][1], values_t(c, lo, tq), shifted)

    def all_blocks(shifted):
        lax.fori_loop(0, qi, functools.partial(full_block, shifted=shifted), 0)
        diag_block(qi, shifted)

    @pl.when(bounded)
    def _():
        all_blocks(shifted=False)

    @pl.when(jnp.logical_not(bounded))
    def _():
        all_blocks(shifted=True)

    lam = (jnp.exp(jnp.sum(lq1_ref[...] * lk1_ref[...], axis=-1, keepdims=True))
           - jnp.exp(jnp.sum(lq2_ref[...] * lk2_ref[...], axis=-1, keepdims=True)) + LAM_INIT)
    for head in range(n_streams // 2):
        c1, c2 = 2 * head, 2 * head + 1
        o_t = acc_scr[c1] / l_scr[c1] - lam * (acc_scr[c2] / l_scr[c2])
        o_ref[:, head * DIFF_DV:(head + 1) * DIFF_DV] = (
            _rms(o_t.T, sg_ref[...]) * (1.0 - LAM_INIT)).astype(o_ref.dtype)


def _diff_attn(z, v_t, qg, kg, lq1, lk1, lq2, lk2, sg, batch, seq, tq=512, heads_per_step=4):
    m = z.shape[0]
    nq = seq // tq
    width = heads_per_step * DIFF_DV
    ns = 2 * heads_per_step
    vec = pl.BlockSpec((1, DIFF_DH), lambda b, g, i: (0, 0))
    return pl.pallas_call(
        functools.partial(_diff_kernel, tq=tq),
        out_shape=jax.ShapeDtypeStruct((m, DIFF_HEADS * DIFF_DV), BF16),
        grid=(batch, DIFF_HEADS // heads_per_step, nq),
        in_specs=[pl.BlockSpec((tq, width), lambda b, g, i: (b * nq + i, Z_DIFF_Q // width + g)),
                  pl.BlockSpec((seq, width), lambda b, g, i: (b, Z_DIFF_K // width + g)),
                  pl.BlockSpec((width, seq), lambda b, g, i: (g, b)),
                  vec, vec, vec, vec, vec, vec,
                  pl.BlockSpec((1, DIFF_DV), lambda b, g, i: (0, 0))],
        out_specs=pl.BlockSpec((tq, width), lambda b, g, i: (b * nq + i, g)),
        scratch_shapes=[pltpu.VMEM((ns, DIFF_DV, tq), F32),
                        pltpu.VMEM((ns, 1, tq), F32),
                        pltpu.VMEM((ns, 1, tq), F32)],
        compiler_params=_params("parallel", "parallel", "arbitrary"),
        name="diff_attn",
    )(z, z, v_t, qg, kg, lq1, lk1, lq2, lk2, sg)


def _mem_kv_kernel(mn_ref, w_ref, kg_ref, kv_ref, *, n_key_tiles):
    j = pl.program_id(0)
    acc = jnp.dot(mn_ref[...], w_ref[...].astype(BF16), preferred_element_type=F32)

    @pl.when(j < n_key_tiles)
    def _():
        _store_group_norm(acc, kg_ref[...], MEM_DH, 1.0, kv_ref)

    @pl.when(j >= n_key_tiles)
    def _():
        kv_ref[...] = acc.astype(kv_ref.dtype)


def _mem_kv(mem_n, w_kv, kg, tn=512):
    m, d = mem_n.shape
    n = w_kv.shape[1]
    return pl.pallas_call(
        functools.partial(_mem_kv_kernel, n_key_tiles=(n // 2) // tn),
        out_shape=jax.ShapeDtypeStruct((m, n), BF16),
        grid=(n // tn,),
        in_specs=[pl.BlockSpec((m, d), lambda j: (0, 0)),
                  pl.BlockSpec((d, tn), lambda j: (0, j)),
                  pl.BlockSpec((1, MEM_DH), lambda j: (0, 0))],
        out_specs=pl.BlockSpec((m, tn), lambda j: (0, j)),
        compiler_params=_params("parallel"),
        name="mem_kv",
    )(mem_n, w_kv, kg)


def _gate_merge_kernel(h_ref, y0_ref, y1_ref, y2_ref, wg0_ref, wg1_ref, wg2_ref,
                       bg0_ref, bg1_ref, bg2_ref, wb0_ref, wb1_ref, wb2_ref, r0_ref, r1_ref,
                       o_ref, c0_ref, c1_ref):
    _cast_riders((r0_ref, r1_ref), (c0_ref, c1_ref))
    h = h_ref[...]
    merged = None
    for y_ref, wg_ref, bg_ref, wb_ref in ((y0_ref, wg0_ref, bg0_ref, wb0_ref),
                                          (y1_ref, wg1_ref, bg1_ref, wb1_ref),
                                          (y2_ref, wg2_ref, bg2_ref, wb2_ref)):
        gate = _sigmoid(jnp.dot(h, wg_ref[...], preferred_element_type=F32) + bg_ref[...])
        term = gate * jnp.dot(y_ref[...], wb_ref[...], preferred_element_type=F32)
        merged = term if merged is None else merged + term
    o_ref[...] = merged.astype(o_ref.dtype)


def _gate_merge(h, ys, w_gate, b_gate, w_branch, riders, tm=1024, tn=256):
    m, d = h.shape
    bw = w_branch.shape[1]
    ni, nj = m // tm, d // tn
    act = lambda width: pl.BlockSpec((tm, width), lambda i, j: (i, 0))
    wg = lambda b: pl.BlockSpec((d, tn), lambda i, j: (0, b * nj + j))
    bg = lambda b: pl.BlockSpec((1, tn), lambda i, j: (0, b * nj + j))
    wb = lambda b: pl.BlockSpec((None, bw, tn), lambda i, j: (b, 0, j))
    r_specs, r_shapes = _rider_specs(riders, ni * nj, lambda i, j: i * nj + j)
    return pl.pallas_call(
        _gate_merge_kernel,
        out_shape=(jax.ShapeDtypeStruct((m, d), BF16), *r_shapes),
        grid=(ni, nj),
        in_specs=[act(d), act(bw), act(bw), act(bw), wg(0), wg(1), wg(2), bg(0), bg(1), bg(2),
                  wb(0), wb(1), wb(2), *r_specs],
        out_specs=(pl.BlockSpec((tm, tn), lambda i, j: (i, j)), *r_specs),
        compiler_params=_params("arbitrary", "arbitrary"),
        name="gate_merge",
    )(h, ys[0], ys[1], ys[2], w_gate, w_gate, w_gate, b_gate, b_gate, b_gate,
      w_branch, w_branch, w_branch, *riders)


def _out_proj_kernel(mg_ref, w_ref, x_ref, g_ref, x1_ref, hf_ref):
    x1 = x_ref[...] + jnp.dot(mg_ref[...], w_ref[...], preferred_element_type=F32)
    x1_ref[...] = x1
    hf_ref[...] = _rms(x1, g_ref[...]).astype(hf_ref.dtype)


def _out_proj(merged, w_out, x2, g, tm=512):
    m, d = x2.shape
    row = lambda i: (i, 0)
    fixed = lambda i: (0, 0)
    return pl.pallas_call(
        _out_proj_kernel,
        out_shape=(jax.ShapeDtypeStruct((m, d), F32), jax.ShapeDtypeStruct((m, d), BF16)),
        grid=(m // tm,),
        in_specs=[pl.BlockSpec((tm, d), row),
                  pl.BlockSpec((d, d), fixed, pipeline_mode=pl.Buffered(1)),
                  pl.BlockSpec((tm, d), row), pl.BlockSpec((1, d), fixed)],
        out_specs=(pl.BlockSpec((tm, d), row), pl.BlockSpec((tm, d), row)),
        compiler_params=_params("parallel"),
        name="out_proj",
    )(merged, w_out, x2, g)


def _ffn_up_kernel(hf_ref, wg_ref, wu_ref, r0_ref, a_ref, c0_ref):
    _cast_riders((r0_ref,), (c0_ref,))
    hf = hf_ref[...]
    gate = jnp.dot(hf, wg_ref[...], preferred_element_type=F32)
    up = jnp.dot(hf, wu_ref[...], preferred_element_type=F32)
    a_ref[...] = (gate * _sigmoid(gate) * up).astype(a_ref.dtype)


def _ffn_up(hf, w_in, riders, tm=1024, tf=512):
    m, d = hf.shape
    d_ff = w_in.shape[1] // 2
    ni, nj = m // tm, d_ff // tf
    r_specs, r_shapes = _rider_specs(riders, ni * nj, lambda i, j: i * nj + j)
    return pl.pallas_call(
        _ffn_up_kernel,
        out_shape=(jax.ShapeDtypeStruct((m, d_ff), BF16), *r_shapes),
        grid=(ni, nj),
        in_specs=[pl.BlockSpec((tm, d), lambda i, j: (i, 0)),
                  pl.BlockSpec((d, tf), lambda i, j: (0, j)),
                  pl.BlockSpec((d, tf), lambda i, j: (0, nj + j)),
                  *r_specs],
        out_specs=(pl.BlockSpec((tm, tf), lambda i, j: (i, j)), *r_specs),
        compiler_params=_params("arbitrary", "arbitrary"),
        name="ffn_up",
    )(hf, w_in, w_in, *riders)


def _ffn_down_kernel(a_ref, w_ref, x1_ref, o_ref):
    o_ref[...] = x1_ref[...] + jnp.dot(a_ref[...], w_ref[...], preferred_element_type=F32)


def _ffn_down(a, w_down, x1, tm=1024, tn=512):
    m, d_ff = a.shape
    d = w_down.shape[1]
    return pl.pallas_call(
        _ffn_down_kernel,
        out_shape=jax.ShapeDtypeStruct((m, d), F32),
        grid=(m // tm, d // tn),
        in_specs=[pl.BlockSpec((tm, d_ff), lambda i, j: (i, 0)),
                  pl.BlockSpec((d_ff, tn), lambda i, j: (0, j)),
                  pl.BlockSpec((tm, tn), lambda i, j: (i, j))],
        out_specs=pl.BlockSpec((tm, tn), lambda i, j: (i, j)),
        compiler_params=_params("parallel", "arbitrary"),
        name="ffn_down",
    )(a, w_down, x1)


def kernel(x, mem, norm_mix_g, norm_mem_g, w_in, gla_w_alpha_up, gla_b_alpha, gla_norm_g,
           diff_q_norm_g, diff_k_norm_g, diff_lambda_q1, diff_lambda_k1, diff_lambda_q2,
           diff_lambda_k2, diff_subln_g, mem_q_norm_g, mem_k_norm_g, w_mem_kv, w_branch,
           w_gate, b_gate, w_out, norm_ffn_g, w_ffn_in, w_ffn_down):
    batch, seq, d = x.shape
    n_mem = mem.shape[1]
    depth = w_in.shape[0]
    assert depth == 1, "LAM_INIT is the layer-0 value"
    x2 = x.reshape(batch * seq, d)
    mem2 = mem.reshape(batch * n_mem, d)
    for l in range(depth):
        w_in_t = w_in[l].T
        w_up = jnp.pad(gla_w_alpha_up[l], ((0, LANES - GLA_RANK), (0, 0)))
        row = lambda v: v.reshape(1, -1)

        h, log_a = _norm_mix(x2, row(norm_mix_g[l]), w_in_t, w_up, row(gla_b_alpha[l]))
        z, w_gate_bf, w_branch_bf = _in_proj(
            h, w_in_t, row(diff_q_norm_g[l]), row(diff_k_norm_g[l]), row(mem_q_norm_g[l]),
            riders=(w_gate[l], w_branch[l].reshape(-1, d)))
        y_gla = _gla(z, log_a, row(gla_norm_g[l]), batch, seq)
        mem_n = _norm_rows(mem2, row(norm_mem_g[l]))
        kv = _mem_kv(mem_n, w_mem_kv[l], row(mem_k_norm_g[l]))
        v_t, y_mem = _v_proj_t(h, w_in_t, z, kv, seq, n_mem)
        y_diff = _diff_attn(z, v_t, row(diff_q_norm_g[l]), row(diff_k_norm_g[l]),
                            row(diff_lambda_q1[l]), row(diff_lambda_k1[l]),
                            row(diff_lambda_q2[l]), row(diff_lambda_k2[l]),
                            row(diff_subln_g[l]), batch, seq)
        merged, w_out_bf, w_ffn_in_bf = _gate_merge(
            h, (y_gla, y_diff, y_mem), w_gate_bf, row(b_gate[l]),
            w_branch_bf.reshape(w_branch[l].shape), riders=(w_out[l], w_ffn_in[l]))
        x1, hf = _out_proj(merged, w_out_bf, x2, row(norm_ffn_g[l]))
        a, w_ffn_down_bf = _ffn_up(hf, w_ffn_in_bf, riders=(w_ffn_down[l],))
        x2 = _ffn_down(a, w_ffn_down_bf, x1)
    return x2.reshape(batch, seq, d)
```

```python
import functools

import jax
import jax.numpy as jnp
from jax import lax
from jax.experimental import pallas as pl
from jax.experimental.pallas import tpu as pltpu

F32 = jnp.float32
BF16 = jnp.bfloat16

CHUNK = 64
GLA_HEADS = 4
GLA_DK = 128
GLA_DV = 256
GLA_RANK = 16
GLA_GATE_NORM = 16.0
DIFF_HEADS = 4
DIFF_DH = 128
DIFF_DV = 256
MEM_HEADS = 4
MEM_DH = 256
N_BRANCH = 3
NORM_EPS = 1e-6
NEG_INF = -1e30
LAM_INIT = 0.8 - 0.6 * 1.0
LOG2_E = 1.4426950408889634
SCORE_LIMIT = 60.0
DIFF_Q_SCALE = DIFF_DH ** -0.5 * LOG2_E

LANES = 128
VMEM_LIMIT = 56 * 1024 * 1024

IN_TILE = 1024
Z_GLA_Q, Z_GLA_K, Z_GLA_V, Z_GLA_G = 0, 512, 1024, 2048
Z_DIFF_Q, Z_DIFF_K, Z_MEM_Q = 3072, 4096, 5120
Z_WIDTH = 6144
W_SRC_TILES = (0, 1, 2, 3, 4, 6)
W_DIFF_V_TILE = 5
W_ROWS = 7 * IN_TILE + GLA_RANK
W_FIRST_SHIFTED_TILE = 3


def _params(*sem):
    return pltpu.CompilerParams(dimension_semantics=sem, vmem_limit_bytes=VMEM_LIMIT)


def _nt_dot(a, b):
    return lax.dot_general(a, b, (((1,), (1,)), ((), ())), preferred_element_type=F32)


def _tn_dot(a, b, precision=None):
    return lax.dot_general(a, b, (((0,), (0,)), ((), ())), preferred_element_type=F32,
                           precision=precision)


def _rms(v, gain):
    ms = jnp.mean(v * v, axis=-1, keepdims=True)
    return v * lax.rsqrt(ms + NORM_EPS) * gain


def _sigmoid(v):
    return 1.0 / (1.0 + jnp.exp(-v))


BF16_SUBLANES = 16


def _rider_specs(weights, n_steps, step_of):
    specs, shapes = [], []
    for w in weights:
        rows, cols = w.shape
        chunk = BF16_SUBLANES
        while rows % chunk or rows // chunk > n_steps:
            chunk += BF16_SUBLANES
        last = rows // chunk - 1
        specs.append(pl.BlockSpec((chunk, cols),
                                  lambda *g, last=last: (jnp.minimum(step_of(*g), last), 0)))
        shapes.append(jax.ShapeDtypeStruct((rows, cols), BF16))
    return specs, shapes


def _cast_riders(in_refs, out_refs):
    for src, dst in zip(in_refs, out_refs):
        dst[...] = src[...].astype(BF16)


def _norm_mix_kernel(x_ref, g_ref, wa_ref, wup_ref, bal_ref, h_ref, la_ref):
    h = _rms(x_ref[...], g_ref[...]).astype(BF16)
    h_ref[...] = h
    a_low = _nt_dot(h, wa_ref[...].astype(BF16))
    w_up = wup_ref[...]
    a_hi, w_hi = a_low.astype(BF16), w_up.astype(BF16)
    a_lo = (a_low - a_hi.astype(F32)).astype(BF16)
    w_lo = (w_up - w_hi.astype(F32)).astype(BF16)
    pre = (jnp.dot(a_hi, w_hi, preferred_element_type=F32)
           + (jnp.dot(a_lo, w_hi, preferred_element_type=F32)
              + jnp.dot(a_hi, w_lo, preferred_element_type=F32))) + bal_ref[...]
    log_sig = jnp.minimum(pre, 0.0) - jnp.log1p(jnp.exp(-jnp.abs(pre)))
    la_ref[...] = log_sig * (1.0 / GLA_GATE_NORM)


def _norm_mix(x2, g, w_in_t, wup, bal, tr=512):
    m, d = x2.shape
    n = wup.shape[1]
    fixed = lambda i: (0, 0)
    decay_block = (Z_GLA_G + GLA_HEADS * GLA_DV) // LANES
    return pl.pallas_call(
        _norm_mix_kernel,
        out_shape=(jax.ShapeDtypeStruct((m, d), BF16), jax.ShapeDtypeStruct((m, n), F32)),
        grid=(m // tr,),
        in_specs=[pl.BlockSpec((tr, d), lambda i: (i, 0)),
                  pl.BlockSpec((1, d), fixed),
                  pl.BlockSpec((LANES, d), lambda i: (decay_block, 0)),
                  pl.BlockSpec((LANES, n), fixed),
                  pl.BlockSpec((1, n), fixed)],
        out_specs=(pl.BlockSpec((tr, d), lambda i: (i, 0)),
                   pl.BlockSpec((tr, n), lambda i: (i, 0))),
        compiler_params=_params("parallel"),
        name="norm_mix",
    )(x2, g, w_in_t, wup, bal)


def _norm_rows_kernel(x_ref, g_ref, h_ref):
    h_ref[...] = _rms(x_ref[...], g_ref[...]).astype(BF16)


def _norm_rows(x2, g, tr=512):
    m, d = x2.shape
    return pl.pallas_call(
        _norm_rows_kernel,
        out_shape=jax.ShapeDtypeStruct((m, d), BF16),
        grid=(m // tr,),
        in_specs=[pl.BlockSpec((tr, d), lambda i: (i, 0)),
                  pl.BlockSpec((1, d), lambda i: (0, 0))],
        out_specs=pl.BlockSpec((tr, d), lambda i: (i, 0)),
        compiler_params=_params("parallel"),
        name="norm_rows",
    )(x2, g)


def _store_group_norm(acc, gain, width, scale, out_ref):
    for s in range(0, acc.shape[1], width):
        blk = acc[:, s:s + width]
        out_ref[:, s:s + width] = (_rms(blk, gain) * scale).astype(out_ref.dtype)


def _w_tile_specs(d, src_tile):
    hi_per_tile = IN_TILE // GLA_RANK
    return [pl.BlockSpec((IN_TILE, d), lambda *g: (src_tile(*g), 0)),
            pl.BlockSpec((GLA_RANK, d), lambda *g: ((src_tile(*g) + 1) * hi_per_tile, 0))]


def _cast_w_tile(w_lo_ref, w_hi_ref, w_scr, first_step, shifted):
    @pl.when(first_step & jnp.logical_not(shifted))
    def _():
        w_scr[...] = w_lo_ref[...].astype(BF16)

    @pl.when(first_step & shifted)
    def _():
        w_scr[:IN_TILE - GLA_RANK, :] = w_lo_ref[GLA_RANK:, :].astype(BF16)
        w_scr[IN_TILE - GLA_RANK:, :] = w_hi_ref[...].astype(BF16)


def _in_proj_kernel(h_ref, w_lo_ref, w_hi_ref, dq_g_ref, dk_g_ref, mq_g_ref, r0_ref, r1_ref,
                    z_ref, c0_ref, c1_ref, w_scr):
    _cast_riders((r0_ref, r1_ref), (c0_ref, c1_ref))
    j = pl.program_id(0)
    _cast_w_tile(w_lo_ref, w_hi_ref, w_scr, pl.program_id(1) == 0, j >= W_FIRST_SHIFTED_TILE)
    j_dq, j_dk, j_mq = Z_DIFF_Q // IN_TILE, Z_DIFF_K // IN_TILE, Z_MEM_Q // IN_TILE

    def tile(epilogue, row_parts):
        part = h_ref.shape[0] // row_parts
        for r in range(row_parts):
            rows = slice(r * part, (r + 1) * part)
            epilogue(_nt_dot(h_ref[rows, :], w_scr[...]), z_ref.at[rows, :])

    def plain(acc, out_ref):
        out_ref[...] = acc.astype(out_ref.dtype)

    @pl.when((j != j_dq) & (j != j_dk) & (j != j_mq))
    def _():
        tile(plain, 1)

    @pl.when(j == j_dq)
    def _():
        tile(lambda acc, out: _store_group_norm(acc, dq_g_ref[...], DIFF_DH, DIFF_Q_SCALE, out), 2)

    @pl.when(j == j_dk)
    def _():
        tile(lambda acc, out: _store_group_norm(acc, dk_g_ref[...], DIFF_DH, 1.0, out), 2)

    @pl.when(j == j_mq)
    def _():
        tile(lambda acc, out: _store_group_norm(acc, mq_g_ref[...], MEM_DH, MEM_DH ** -0.5, out), 2)


def _in_proj(h, w_in_t, dq_g, dk_g, mq_g, riders, tm=1024):
    m, d = h.shape
    assert w_in_t.shape[0] == W_ROWS
    nj, ni = Z_WIDTH // IN_TILE, m // tm
    assert W_SRC_TILES == tuple(j + (j >= W_DIFF_V_TILE) for j in range(nj))
    r_specs, r_shapes = _rider_specs(riders, nj * ni, lambda j, i: j * ni + i)
    return pl.pallas_call(
        _in_proj_kernel,
        out_shape=(jax.ShapeDtypeStruct((m, Z_WIDTH), BF16), *r_shapes),
        grid=(nj, ni),
        in_specs=[pl.BlockSpec((tm, d), lambda j, i: (i, 0)),
                  *_w_tile_specs(d, lambda j, i: jnp.where(j >= W_DIFF_V_TILE, j + 1, j)),
                  pl.BlockSpec((1, DIFF_DH), lambda j, i: (0, 0)),
                  pl.BlockSpec((1, DIFF_DH), lambda j, i: (0, 0)),
                  pl.BlockSpec((1, MEM_DH), lambda j, i: (0, 0)),
                  *r_specs],
        out_specs=(pl.BlockSpec((tm, IN_TILE), lambda j, i: (i, j)), *r_specs),
        scratch_shapes=[pltpu.VMEM((IN_TILE, d), BF16)],
        compiler_params=_params("arbitrary", "arbitrary"),
        name="in_proj",
    )(h, w_in_t, w_in_t, dq_g, dk_g, mq_g, *riders)


def _v_proj_t_kernel(h_ref, w_lo_ref, w_hi_ref, mq_ref, mk_ref, mv_ref, vt_ref, ym_ref, w_scr, *,
                     sub_rows):
    _cast_w_tile(w_lo_ref, w_hi_ref, w_scr, pl.program_id(0) == 0,
                 W_DIFF_V_TILE >= W_FIRST_SHIFTED_TILE)
    for r in range(0, mq_ref.shape[0], sub_rows):
        rows = slice(r, r + sub_rows)
        vt_ref[:, rows] = _nt_dot(w_scr[...], h_ref[rows, :]).astype(vt_ref.dtype)
        for head in range(MEM_HEADS):
            cols = slice(head * MEM_DH, (head + 1) * MEM_DH)
            s = _nt_dot(mq_ref[rows, cols], mk_ref[:, cols])
            e = jnp.exp(s - jnp.max(s, axis=-1, keepdims=True))
            p = (e / jnp.sum(e, axis=-1, keepdims=True)).astype(BF16)
            ym_ref[rows, cols] = jnp.dot(p, mv_ref[:, cols],
                                         preferred_element_type=F32).astype(ym_ref.dtype)


def _v_proj_t(h, w_in_t, z, kv, seq, n_mem, tm=1024, sub_rows=512):
    m, d = h.shape
    width = MEM_HEADS * MEM_DH
    tiles_per_batch = seq // tm
    return pl.pallas_call(
        functools.partial(_v_proj_t_kernel, sub_rows=sub_rows),
        out_shape=(jax.ShapeDtypeStruct((IN_TILE, m), BF16), jax.ShapeDtypeStruct((m, width), BF16)),
        grid=(m // tm,),
        in_specs=[pl.BlockSpec((tm, d), lambda i: (i, 0)),
                  *_w_tile_specs(d, lambda i: W_DIFF_V_TILE),
                  pl.BlockSpec((tm, width), lambda i: (i, Z_MEM_Q // width)),
                  pl.BlockSpec((n_mem, width), lambda i: (i // tiles_per_batch, 0)),
                  pl.BlockSpec((n_mem, width), lambda i: (i // tiles_per_batch, 1))],
        out_specs=(pl.BlockSpec((IN_TILE, tm), lambda i: (0, i)),
                   pl.BlockSpec((tm, width), lambda i: (i, 0))),
        scratch_shapes=[pltpu.VMEM((IN_TILE, d), BF16)],
        compiler_params=_params("arbitrary"),
        name="v_proj_t",
    )(h, w_in_t, w_in_t, z, kv, kv)


def _chunk_cumsum(x):
    row_in_chunk = lax.broadcasted_iota(jnp.int32, x.shape, 0) % CHUNK
    shift = 1
    while shift < CHUNK:
        x = x + jnp.where(row_in_chunk >= shift, pltpu.roll(x, shift, 0), 0.0)
        shift *= 2
    return x


def _gla_kernel(q_ref, k_ref, v_ref, g_ref, la_ref, ng_ref, o_ref, s_ref, *, ts):
    @pl.when(pl.program_id(2) == 0)
    def _():
        s_ref[...] = jnp.zeros_like(s_ref)

    bcum_all = _chunk_cumsum(la_ref[...])
    row = lax.broadcasted_iota(jnp.int32, (CHUNK, CHUNK), 0)
    col = lax.broadcasted_iota(jnp.int32, (CHUNK, CHUNK), 1)
    causal = row >= col

    for c in range(ts // CHUNK):
        rows = slice(c * CHUNK, (c + 1) * CHUNK)
        bcum = bcum_all[rows]
        b_last = bcum[CHUNK - 1:CHUNK, :]
        q = q_ref[rows, :].astype(F32) * (GLA_DK ** -0.5)
        k = k_ref[rows, :].astype(F32)
        v = v_ref[rows, :]
        q_dec = (q * jnp.exp(bcum)).astype(BF16)
        k_dec = (k * jnp.exp(-bcum)).astype(BF16)
        k_tail = (k * jnp.exp(b_last - bcum)).astype(BF16)
        att = jnp.where(causal, _nt_dot(q_dec, k_dec), 0.0).astype(BF16)
        state = s_ref[...]
        o = jnp.dot(jnp.concatenate([q_dec, att], axis=1),
                    jnp.concatenate([state.astype(BF16), v], axis=0), preferred_element_type=F32)
        inc = _tn_dot(k_tail, v)
        decay_col = jnp.broadcast_to(jnp.exp(b_last), (GLA_DK, GLA_DK)).T
        decay = jnp.concatenate([decay_col] * (GLA_DV // GLA_DK), axis=1)
        s_ref[...] = decay * state + inc
        gate = g_ref[rows, :].astype(F32)
        o_ref[rows, :] = (_rms(o, ng_ref[...]) * (gate * _sigmoid(gate))).astype(o_ref.dtype)


def _gla(z, log_a, ng, batch, seq, ts=2048):
    m = z.shape[0]
    nt = seq // ts
    rows = lambda b, h, t: b * nt + t
    return pl.pallas_call(
        functools.partial(_gla_kernel, ts=ts),
        out_shape=jax.ShapeDtypeStruct((m, GLA_HEADS * GLA_DV), BF16),
        grid=(batch, GLA_HEADS, nt),
        in_specs=[pl.BlockSpec((ts, GLA_DK), lambda b, h, t: (rows(b, h, t), Z_GLA_Q // GLA_DK + h)),
                  pl.BlockSpec((ts, GLA_DK), lambda b, h, t: (rows(b, h, t), Z_GLA_K // GLA_DK + h)),
                  pl.BlockSpec((ts, GLA_DV), lambda b, h, t: (rows(b, h, t), Z_GLA_V // GLA_DV + h)),
                  pl.BlockSpec((ts, GLA_DV), lambda b, h, t: (rows(b, h, t), Z_GLA_G // GLA_DV + h)),
                  pl.BlockSpec((ts, GLA_DK), lambda b, h, t: (rows(b, h, t), h)),
                  pl.BlockSpec((1, GLA_DV), lambda b, h, t: (0, 0))],
        out_specs=pl.BlockSpec((ts, GLA_DV), lambda b, h, t: (rows(b, h, t), h)),
        scratch_shapes=[pltpu.VMEM((GLA_DK, GLA_DV), F32)],
        compiler_params=_params("parallel", "parallel", "arbitrary"),
        name="gla",
    )(z, z, z, z, log_a, ng)


def _diff_kernel(q_ref, k_ref, vt_ref, qg_ref, kg_ref, lq1_ref, lk1_ref, lq2_ref, lk2_ref, sg_ref,
                 o_ref, acc_scr, m_scr, l_scr, *, tq):
    qi = pl.program_id(2)
    m_scr[...] = jnp.full_like(m_scr, NEG_INF)
    l_scr[...] = jnp.zeros_like(l_scr)
    acc_scr[...] = jnp.zeros_like(acc_scr)
    n_streams = q_ref.shape[1] // DIFF_DH
    half = tq // 2

    score_bound = (1.02 * DIFF_DH * DIFF_Q_SCALE) * (jnp.max(jnp.abs(qg_ref[...]))
                                                    * jnp.max(jnp.abs(kg_ref[...])))
    bounded = score_bound <= SCORE_LIMIT

    def update(c, lanes, s, vt, shifted):
        if shifted:
            m_old = m_scr[c, :, lanes]
            m_new = jnp.maximum(m_old, jnp.max(s, axis=0, keepdims=True))
            alpha = jnp.exp2(m_old - m_new)
            p = jnp.exp2(s - m_new)
            l_scr[c, :, lanes] = alpha * l_scr[c, :, lanes] + jnp.sum(p, axis=0, keepdims=True)
            m_scr[c, :, lanes] = m_new
            acc_scr[c, :, lanes] = (alpha * acc_scr[c, :, lanes]
                                    + jnp.dot(vt, p.astype(BF16), preferred_element_type=F32))
        else:
            p = jnp.exp2(s)
            l_scr[c, :, lanes] = l_scr[c, :, lanes] + jnp.sum(p, axis=0, keepdims=True)
            acc_scr[c, :, lanes] = (acc_scr[c, :, lanes]
                                    + jnp.dot(vt, p.astype(BF16), preferred_element_type=F32))

    def values_t(c, start, n):
        head = c // 2
        return vt_ref[head * DIFF_DV:(head + 1) * DIFF_DV, pl.ds(start, n)]

    def full_block(kb, carry, shifted):
        start = pl.multiple_of(kb * tq, tq)
        scores = []
        for c in range(n_streams):
            cols = slice(c * DIFF_DH, (c + 1) * DIFF_DH)
            scores.append(_nt_dot(k_ref[pl.ds(start, tq), cols], q_ref[:, cols]))
        for c in range(n_streams):
            update(c, slice(0, tq), scores[c], values_t(c, start, tq), shifted)
        return carry

    def diag_block(kb, shifted):
        lo = pl.multiple_of(kb * tq, tq)
        hi = pl.multiple_of(kb * tq + half, half)
        key_chunk = lax.broadcasted_iota(jnp.int32, (half, half), 0) // CHUNK
        query_chunk = lax.broadcasted_iota(jnp.int32, (half, half), 1) // CHUNK
        visible = key_chunk <= query_chunk
        scores = []
        for c in range(n_streams):
            cols = slice(c * DIFF_DH, (c + 1) * DIFF_DH)
            k_lo, k_hi = k_ref[pl.ds(lo, half), cols], k_ref[pl.ds(hi, half), cols]
            q_lo, q_hi = q_ref[:half, cols], q_ref[half:, cols]
            s_lo = jnp.where(visible, _nt_dot(k_lo, q_lo), NEG_INF)
            s_hi = jnp.concatenate([_nt_dot(k_lo, q_hi),
                                    jnp.where(visible, _nt_dot(k_hi, q_hi), NEG_INF)], axis=0)
            scores.append((s_lo, s_hi))
        for c in range(n_streams):
            update(c, slice(0, half), scores[c][0], values_t(c, lo, half), shifted)
            update(c, slice(half, tq), scores[c][1], values_t(c, lo, tq), shifted)

    def finish():
        lam = (jnp.exp(jnp.sum(lq1_ref[...] * lk1_ref[...], axis=-1, keepdims=True))
               - jnp.exp(jnp.sum(lq2_ref[...] * lk2_ref[...], axis=-1, keepdims=True)) + LAM_INIT)
        for head in range(n_streams // 2):
            c1, c2 = 2 * head, 2 * head + 1
            o_t = acc_scr[c1] / l_scr[c1] - lam * (acc_scr[c2] / l_scr[c2])
            o_ref[:, head * DIFF_DV:(head + 1) * DIFF_DV] = (
                _rms(o_t.T, sg_ref[...]) * (1.0 - LAM_INIT)).astype(o_ref.dtype)

    def all_blocks(shifted):
        lax.fori_loop(0, qi, functools.partial(full_block, shifted=shifted), 0)
        diag_block(qi, shifted)
        finish()

    @pl.when(bounded)
    def _():
        all_blocks(shifted=False)

    @pl.when(jnp.logical_not(bounded))
    def _():
        all_blocks(shifted=True)


def _diff_attn(z, v_t, qg, kg, lq1, lk1, lq2, lk2, sg, batch, seq, tq=512, heads_per_step=4):
    m = z.shape[0]
    nq = seq // tq
    width = heads_per_step * DIFF_DV
    ns = 2 * heads_per_step
    vec = pl.BlockSpec((1, DIFF_DH), lambda b, g, i: (0, 0))
    return pl.pallas_call(
        functools.partial(_diff_kernel, tq=tq),
        out_shape=jax.ShapeDtypeStruct((m, DIFF_HEADS * DIFF_DV), BF16),
        grid=(batch, DIFF_HEADS // heads_per_step, nq),
        in_specs=[pl.BlockSpec((tq, width), lambda b, g, i: (b * nq + i, Z_DIFF_Q // width + g)),
                  pl.BlockSpec((seq, width), lambda b, g, i: (b, Z_DIFF_K // width + g)),
                  pl.BlockSpec((width, seq), lambda b, g, i: (g, b)),
                  vec, vec, vec, vec, vec, vec,
                  pl.BlockSpec((1, DIFF_DV), lambda b, g, i: (0, 0))],
        out_specs=pl.BlockSpec((tq, width), lambda b, g, i: (b * nq + i, g)),
        scratch_shapes=[pltpu.VMEM((ns, DIFF_DV, tq), F32),
                        pltpu.VMEM((ns, 1, tq), F32),
                        pltpu.VMEM((ns, 1, tq), F32)],
        compiler_params=_params("parallel", "parallel", "arbitrary"),
        name="diff_attn",
    )(z, z, v_t, qg, kg, lq1, lk1, lq2, lk2, sg)


def _mem_kv_kernel(mn_ref, w_ref, kg_ref, kv_ref, *, n_key_tiles):
    j = pl.program_id(0)
    acc = jnp.dot(mn_ref[...], w_ref[...].astype(BF16), preferred_element_type=F32)

    @pl.when(j < n_key_tiles)
    def _():
        _store_group_norm(acc, kg_ref[...], MEM_DH, 1.0, kv_ref)

    @pl.when(j >= n_key_tiles)
    def _():
        kv_ref[...] = acc.astype(kv_ref.dtype)


def _mem_kv(mem_n, w_kv, kg, tn=512):
    m, d = mem_n.shape
    n = w_kv.shape[1]
    return pl.pallas_call(
        functools.partial(_mem_kv_kernel, n_key_tiles=(n // 2) // tn),
        out_shape=jax.ShapeDtypeStruct((m, n), BF16),
        grid=(n // tn,),
        in_specs=[pl.BlockSpec((m, d), lambda j: (0, 0)),
                  pl.BlockSpec((d, tn), lambda j: (0, j)),
                  pl.BlockSpec((1, MEM_DH), lambda j: (0, 0))],
        out_specs=pl.BlockSpec((m, tn), lambda j: (0, j)),
        compiler_params=_params("parallel"),
        name="mem_kv",
    )(mem_n, w_kv, kg)


def _gate_merge_kernel(h_ref, y0_ref, y1_ref, y2_ref, wg0_ref, wg1_ref, wg2_ref,
                       bg0_ref, bg1_ref, bg2_ref, wb0_ref, wb1_ref, wb2_ref, r0_ref, r1_ref,
                       o_ref, c0_ref, c1_ref):
    _cast_riders((r0_ref, r1_ref), (c0_ref, c1_ref))
    h = h_ref[...]
    merged = None
    for y_ref, wg_ref, bg_ref, wb_ref in ((y0_ref, wg0_ref, bg0_ref, wb0_ref),
                                          (y1_ref, wg1_ref, bg1_ref, wb1_ref),
                                          (y2_ref, wg2_ref, bg2_ref, wb2_ref)):
        gate = _sigmoid(jnp.dot(h, wg_ref[...], preferred_element_type=F32) + bg_ref[...])
        term = gate * jnp.dot(y_ref[...], wb_ref[...], preferred_element_type=F32)
        merged = term if merged is None else merged + term
    o_ref[...] = merged.astype(o_ref.dtype)


def _gate_merge(h, ys, w_gate, b_gate, w_branch, riders, tm=1024, tn=256):
    m, d = h.shape
    bw = w_branch.shape[1]
    ni, nj = m // tm, d // tn
    act = lambda width: pl.BlockSpec((tm, width), lambda i, j: (i, 0))
    wg = lambda b: pl.BlockSpec((d, tn), lambda i, j: (0, b * nj + j))
    bg = lambda b: pl.BlockSpec((1, tn), lambda i, j: (0, b * nj + j))
    wb = lambda b: pl.BlockSpec((None, bw, tn), lambda i, j: (b, 0, j))
    r_specs, r_shapes = _rider_specs(riders, ni * nj, lambda i, j: i * nj + j)
    return pl.pallas_call(
        _gate_merge_kernel,
        out_shape=(jax.ShapeDtypeStruct((m, d), BF16), *r_shapes),
        grid=(ni, nj),
        in_specs=[act(d), act(bw), act(bw), act(bw), wg(0), wg(1), wg(2), bg(0), bg(1), bg(2),
                  wb(0), wb(1), wb(2), *r_specs],
        out_specs=(pl.BlockSpec((tm, tn), lambda i, j: (i, j)), *r_specs),
        compiler_params=_params("arbitrary", "arbitrary"),
        name="gate_merge",
    )(h, ys[0], ys[1], ys[2], w_gate, w_gate, w_gate, b_gate, b_gate, b_gate,
      w_branch, w_branch, w_branch, *riders)


def _out_proj_kernel(mg_ref, w_ref, x_ref, g_ref, x1_ref, hf_ref):
    x1 = x_ref[...] + jnp.dot(mg_ref[...], w_ref[...], preferred_element_type=F32)
    x1_ref[...] = x1
    hf_ref[...] = _rms(x1, g_ref[...]).astype(hf_ref.dtype)


def _out_proj(merged, w_out, x2, g, tm=512):
    m, d = x2.shape
    row = lambda i: (i, 0)
    fixed = lambda i: (0, 0)
    return pl.pallas_call(
        _out_proj_kernel,
        out_shape=(jax.ShapeDtypeStruct((m, d), F32), jax.ShapeDtypeStruct((m, d), BF16)),
        grid=(m // tm,),
        in_specs=[pl.BlockSpec((tm, d), row),
                  pl.BlockSpec((d, d), fixed, pipeline_mode=pl.Buffered(1)),
                  pl.BlockSpec((tm, d), row), pl.BlockSpec((1, d), fixed)],
        out_specs=(pl.BlockSpec((tm, d), row), pl.BlockSpec((tm, d), row)),
        compiler_params=_params("parallel"),
        name="out_proj",
    )(merged, w_out, x2, g)


def _ffn_up_kernel(hf_ref, wg_ref, wu_ref, r0_ref, a_ref, c0_ref):
    _cast_riders((r0_ref,), (c0_ref,))
    hf = hf_ref[...]
    gate = jnp.dot(hf, wg_ref[...], preferred_element_type=F32)
    up = jnp.dot(hf, wu_ref[...], preferred_element_type=F32)
    a_ref[...] = (gate * _sigmoid(gate) * up).astype(a_ref.dtype)


def _ffn_up(hf, w_in, riders, tm=1024, tf=512):
    m, d = hf.shape
    d_ff = w_in.shape[1] // 2
    ni, nj = m // tm, d_ff // tf
    r_specs, r_shapes = _rider_specs(riders, ni * nj, lambda i, j: i * nj + j)
    return pl.pallas_call(
        _ffn_up_kernel,
        out_shape=(jax.ShapeDtypeStruct((m, d_ff), BF16), *r_shapes),
        grid=(ni, nj),
        in_specs=[pl.BlockSpec((tm, d), lambda i, j: (i, 0)),
                  pl.BlockSpec((d, tf), lambda i, j: (0, j)),
                  pl.BlockSpec((d, tf), lambda i, j: (0, nj + j)),
                  *r_specs],
        out_specs=(pl.BlockSpec((tm, tf), lambda i, j: (i, j)), *r_specs),
        compiler_params=_params("arbitrary", "arbitrary"),
        name="ffn_up",
    )(hf, w_in, w_in, *riders)


def _ffn_down_kernel(a_ref, w_ref, x1_ref, o_ref):
    o_ref[...] = x1_ref[...] + jnp.dot(a_ref[...], w_ref[...], preferred_element_type=F32)


def _ffn_down(a, w_down, x1, tm=1024, tn=512):
    m, d_ff = a.shape
    d = w_down.shape[1]
    return pl.pallas_call(
        _ffn_down_kernel,
        out_shape=jax.ShapeDtypeStruct((m, d), F32),
        grid=(m // tm, d // tn),
        in_specs=[pl.BlockSpec((tm, d_ff), lambda i, j: (i, 0)),
                  pl.BlockSpec((d_ff, tn), lambda i, j: (0, j)),
                  pl.BlockSpec((tm, tn), lambda i, j: (i, j))],
        out_specs=pl.BlockSpec((tm, tn), lambda i, j: (i, j)),
        compiler_params=_params("parallel", "arbitrary"),
        name="ffn_down",
    )(a, w_down, x1)


def kernel(x, mem, norm_mix_g, norm_mem_g, w_in, gla_w_alpha_up, gla_b_alpha, gla_norm_g,
           diff_q_norm_g, diff_k_norm_g, diff_lambda_q1, diff_lambda_k1, diff_lambda_q2,
           diff_lambda_k2, diff_subln_g, mem_q_norm_g, mem_k_norm_g, w_mem_kv, w_branch,
           w_gate, b_gate, w_out, norm_ffn_g, w_ffn_in, w_ffn_down):
    batch, seq, d = x.shape
    n_mem = mem.shape[1]
    depth = w_in.shape[0]
    assert depth == 1, "LAM_INIT is the layer-0 value"
    x2 = x.reshape(batch * seq, d)
    mem2 = mem.reshape(batch * n_mem, d)
    for l in range(depth):
        w_in_t = w_in[l].T
        w_up = jnp.pad(gla_w_alpha_up[l], ((0, LANES - GLA_RANK), (0, 0)))
        row = lambda v: v.reshape(1, -1)

        h, log_a = _norm_mix(x2, row(norm_mix_g[l]), w_in_t, w_up, row(gla_b_alpha[l]))
        z, w_gate_bf, w_branch_bf = _in_proj(
            h, w_in_t, row(diff_q_norm_g[l]), row(diff_k_norm_g[l]), row(mem_q_norm_g[l]),
            riders=(w_gate[l], w_branch[l].reshape(-1, d)))
        y_gla = _gla(z, log_a, row(gla_norm_g[l]), batch, seq)
        mem_n = _norm_rows(mem2, row(norm_mem_g[l]))
        kv = _mem_kv(mem_n, w_mem_kv[l], row(mem_k_norm_g[l]))
        v_t, y_mem = _v_proj_t(h, w_in_t, z, kv, seq, n_mem)
        y_diff = _diff_attn(z, v_t, row(diff_q_norm_g[l]), row(diff_k_norm_g[l]),
                            row(diff_lambda_q1[l]), row(diff_lambda_k1[l]),
                            row(diff_lambda_q2[l]), row(diff_lambda_k2[l]),
                            row(diff_subln_g[l]), batch, seq)
        merged, w_out_bf, w_ffn_in_bf = _gate_merge(
            h, (y_gla, y_diff, y_mem), w_gate_bf, row(b_gate[l]),
            w_branch_bf.reshape(w_branch[l].shape), riders=(w_out[l], w_ffn_in[l]))
        x1, hf = _out_proj(merged, w_out_bf, x2, row(norm_ffn_g[l]))
        a, w_ffn_down_bf = _ffn_up(hf, w_ffn_in_bf, riders=(w_ffn_down[l],))
        x2 = _ffn_down(a, w_ffn_down_bf, x1)
    return x2.reshape(batch, seq, d)
```

```python
import functools

import jax
import jax.numpy as jnp
from jax import lax
from jax.experimental import pallas as pl
from jax.experimental.pallas import tpu as pltpu

F32 = jnp.float32
BF16 = jnp.bfloat16

CHUNK = 64
GLA_HEADS = 4
GLA_DK = 128
GLA_DV = 256
GLA_RANK = 16
GLA_GATE_NORM = 16.0
DIFF_HEADS = 4
DIFF_DH = 128
DIFF_DV = 256
MEM_HEADS = 4
MEM_DH = 256
N_BRANCH = 3
NORM_EPS = 1e-6
NEG_INF = -1e30
LAM_INIT = 0.8 - 0.6 * 1.0
LOG2_E = 1.4426950408889634
SCORE_LIMIT = 60.0
DIFF_Q_SCALE = DIFF_DH ** -0.5 * LOG2_E

LANES = 128
VMEM_LIMIT = 56 * 1024 * 1024

IN_TILE = 1024
Z_GLA_Q, Z_GLA_K, Z_GLA_V, Z_GLA_G = 0, 512, 1024, 2048
Z_DIFF_Q, Z_DIFF_K, Z_MEM_Q = 3072, 4096, 5120
Z_WIDTH = 6144
W_SRC_TILES = (0, 1, 2, 3, 4, 6)
W_DIFF_V_TILE = 5
W_ROWS = 7 * IN_TILE + GLA_RANK
W_FIRST_SHIFTED_TILE = 3


def _params(*sem):
    return pltpu.CompilerParams(dimension_semantics=sem, vmem_limit_bytes=VMEM_LIMIT)


def _nt_dot(a, b):
    return lax.dot_general(a, b, (((1,), (1,)), ((), ())), preferred_element_type=F32)


def _tn_dot(a, b, precision=None):
    return lax.dot_general(a, b, (((0,), (0,)), ((), ())), preferred_element_type=F32,
                           precision=precision)


def _rms(v, gain):
    ms = jnp.mean(v * v, axis=-1, keepdims=True)
    return v * lax.rsqrt(ms + NORM_EPS) * gain


def _sigmoid(v):
    return 1.0 / (1.0 + jnp.exp(-v))


BF16_SUBLANES = 16


def _rider_specs(weights, n_steps, step_of):
    specs, shapes = [], []
    for w in weights:
        rows, cols = w.shape
        chunk = BF16_SUBLANES
        while rows % chunk or rows // chunk > n_steps:
            chunk += BF16_SUBLANES
        last = rows // chunk - 1
        specs.append(pl.BlockSpec((chunk, cols),
                                  lambda *g, last=last: (jnp.minimum(step_of(*g), last), 0)))
        shapes.append(jax.ShapeDtypeStruct((rows, cols), BF16))
    return specs, shapes


def _cast_riders(in_refs, out_refs):
    for src, dst in zip(in_refs, out_refs):
        dst[...] = src[...].astype(BF16)


def _norm_mix_kernel(x_ref, g_ref, wa_ref, wup_ref, bal_ref, h_ref, la_ref):
    h = _rms(x_ref[...], g_ref[...]).astype(BF16)
    h_ref[...] = h
    a_low = _nt_dot(h, wa_ref[...].astype(BF16))
    w_up = wup_ref[...]
    a_hi, w_hi = a_low.astype(BF16), w_up.astype(BF16)
    a_lo = (a_low - a_hi.astype(F32)).astype(BF16)
    w_lo = (w_up - w_hi.astype(F32)).astype(BF16)
    pre = (jnp.dot(a_hi, w_hi, preferred_element_type=F32)
           + (jnp.dot(a_lo, w_hi, preferred_element_type=F32)
              + jnp.dot(a_hi, w_lo, preferred_element_type=F32))) + bal_ref[...]
    log_sig = jnp.minimum(pre, 0.0) - jnp.log1p(jnp.exp(-jnp.abs(pre)))
    la_ref[...] = log_sig * (1.0 / GLA_GATE_NORM)


def _norm_mix(x2, g, w_in_t, wup, bal, tr=512):
    m, d = x2.shape
    n = wup.shape[1]
    fixed = lambda i: (0, 0)
    decay_block = (Z_GLA_G + GLA_HEADS * GLA_DV) // LANES
    return pl.pallas_call(
        _norm_mix_kernel,
        out_shape=(jax.ShapeDtypeStruct((m, d), BF16), jax.ShapeDtypeStruct((m, n), F32)),
        grid=(m // tr,),
        in_specs=[pl.BlockSpec((tr, d), lambda i: (i, 0)),
                  pl.BlockSpec((1, d), fixed),
                  pl.BlockSpec((LANES, d), lambda i: (decay_block, 0)),
                  pl.BlockSpec((LANES, n), fixed),
                  pl.BlockSpec((1, n), fixed)],
        out_specs=(pl.BlockSpec((tr, d), lambda i: (i, 0)),
                   pl.BlockSpec((tr, n), lambda i: (i, 0))),
        compiler_params=_params("parallel"),
        name="norm_mix",
    )(x2, g, w_in_t, wup, bal)


def _norm_rows_kernel(x_ref, g_ref, h_ref):
    h_ref[...] = _rms(x_ref[...], g_ref[...]).astype(BF16)


def _norm_rows(x2, g, tr=512):
    m, d = x2.shape
    return pl.pallas_call(
        _norm_rows_kernel,
        out_shape=jax.ShapeDtypeStruct((m, d), BF16),
        grid=(m // tr,),
        in_specs=[pl.BlockSpec((tr, d), lambda i: (i, 0)),
                  pl.BlockSpec((1, d), lambda i: (0, 0))],
        out_specs=pl.BlockSpec((tr, d), lambda i: (i, 0)),
        compiler_params=_params("parallel"),
        name="norm_rows",
    )(x2, g)


def _store_group_norm(acc, gain, width, scale, out_ref):
    for s in range(0, acc.shape[1], width):
        blk = acc[:, s:s + width]
        out_ref[:, s:s + width] = (_rms(blk, gain) * scale).astype(out_ref.dtype)


def _w_tile_specs(d, src_tile):
    hi_per_tile = IN_TILE // GLA_RANK
    return [pl.BlockSpec((IN_TILE, d), lambda *g: (src_tile(*g), 0)),
            pl.BlockSpec((GLA_RANK, d), lambda *g: ((src_tile(*g) + 1) * hi_per_tile, 0))]


def _cast_w_tile(w_lo_ref, w_hi_ref, w_scr, first_step, shifted):
    @pl.when(first_step & jnp.logical_not(shifted))
    def _():
        w_scr[...] = w_lo_ref[...].astype(BF16)

    @pl.when(first_step & shifted)
    def _():
        w_scr[:IN_TILE - GLA_RANK, :] = w_lo_ref[GLA_RANK:, :].astype(BF16)
        w_scr[IN_TILE - GLA_RANK:, :] = w_hi_ref[...].astype(BF16)


def _in_proj_kernel(h_ref, w_lo_ref, w_hi_ref, dq_g_ref, dk_g_ref, mq_g_ref, r0_ref, r1_ref,
                    z_ref, c0_ref, c1_ref, w_scr):
    _cast_riders((r0_ref, r1_ref), (c0_ref, c1_ref))
    j = pl.program_id(0)
    _cast_w_tile(w_lo_ref, w_hi_ref, w_scr, pl.program_id(1) == 0, j >= W_FIRST_SHIFTED_TILE)
    j_dq, j_dk, j_mq = Z_DIFF_Q // IN_TILE, Z_DIFF_K // IN_TILE, Z_MEM_Q // IN_TILE

    def tile(epilogue, row_parts):
        part = h_ref.shape[0] // row_parts
        for r in range(row_parts):
            rows = slice(r * part, (r + 1) * part)
            epilogue(_nt_dot(h_ref[rows, :], w_scr[...]), z_ref.at[rows, :])

    def plain(acc, out_ref):
        out_ref[...] = acc.astype(out_ref.dtype)

    @pl.when((j != j_dq) & (j != j_dk) & (j != j_mq))
    def _():
        tile(plain, 1)

    @pl.when(j == j_dq)
    def _():
        tile(lambda acc, out: _store_group_norm(acc, dq_g_ref[...], DIFF_DH, DIFF_Q_SCALE, out), 2)

    @pl.when(j == j_dk)
    def _():
        tile(lambda acc, out: _store_group_norm(acc, dk_g_ref[...], DIFF_DH, 1.0, out), 2)

    @pl.when(j == j_mq)
    def _():
        tile(lambda acc, out: _store_group_norm(acc, mq_g_ref[...], MEM_DH, MEM_DH ** -0.5, out), 2)


def _in_proj(h, w_in_t, dq_g, dk_g, mq_g, riders, tm=1024):
    m, d = h.shape
    assert w_in_t.shape[0] == W_ROWS
    nj, ni = Z_WIDTH // IN_TILE, m // tm
    assert W_SRC_TILES == tuple(j + (j >= W_DIFF_V_TILE) for j in range(nj))
    r_specs, r_shapes = _rider_specs(riders, nj * ni, lambda j, i: j * ni + i)
    return pl.pallas_call(
        _in_proj_kernel,
        out_shape=(jax.ShapeDtypeStruct((m, Z_WIDTH), BF16), *r_shapes),
        grid=(nj, ni),
        in_specs=[pl.BlockSpec((tm, d), lambda j, i: (i, 0)),
                  *_w_tile_specs(d, lambda j, i: jnp.where(j >= W_DIFF_V_TILE, j + 1, j)),
                  pl.BlockSpec((1, DIFF_DH), lambda j, i: (0, 0)),
                  pl.BlockSpec((1, DIFF_DH), lambda j, i: (0, 0)),
                  pl.BlockSpec((1, MEM_DH), lambda j, i: (0, 0)),
                  *r_specs],
        out_specs=(pl.BlockSpec((tm, IN_TILE), lambda j, i: (i, j)), *r_specs),
        scratch_shapes=[pltpu.VMEM((IN_TILE, d), BF16)],
        compiler_params=_params("arbitrary", "arbitrary"),
        name="in_proj",
    )(h, w_in_t, w_in_t, dq_g, dk_g, mq_g, *riders)


def _v_proj_t_kernel(h_ref, w_lo_ref, w_hi_ref, mq_ref, mk_ref, mv_ref, vt_ref, ym_ref, w_scr, *,
                     sub_rows):
    _cast_w_tile(w_lo_ref, w_hi_ref, w_scr, pl.program_id(0) == 0,
                 W_DIFF_V_TILE >= W_FIRST_SHIFTED_TILE)
    parts = [slice(r, r + sub_rows) for r in range(0, mq_ref.shape[0], sub_rows)]
    heads = [slice(hd * MEM_DH, (hd + 1) * MEM_DH) for hd in range(MEM_HEADS)]
    scores = {(r, hd): _nt_dot(mq_ref[rows, cols], mk_ref[:, cols])
              for r, rows in enumerate(parts) for hd, cols in enumerate(heads)}
    for r, rows in enumerate(parts):
        vt_ref[:, rows] = _nt_dot(w_scr[...], h_ref[rows, :]).astype(vt_ref.dtype)
        for hd, cols in enumerate(heads):
            s = scores[r, hd]
            e = jnp.exp(s - jnp.max(s, axis=-1, keepdims=True))
            p = (e / jnp.sum(e, axis=-1, keepdims=True)).astype(BF16)
            ym_ref[rows, cols] = jnp.dot(p, mv_ref[:, cols],
                                         preferred_element_type=F32).astype(ym_ref.dtype)


def _v_proj_t(h, w_in_t, z, kv, seq, n_mem, tm=1024, sub_rows=512):
    m, d = h.shape
    width = MEM_HEADS * MEM_DH
    tiles_per_batch = seq // tm
    return pl.pallas_call(
        functools.partial(_v_proj_t_kernel, sub_rows=sub_rows),
        out_shape=(jax.ShapeDtypeStruct((IN_TILE, m), BF16), jax.ShapeDtypeStruct((m, width), BF16)),
        grid=(m // tm,),
        in_specs=[pl.BlockSpec((tm, d), lambda i: (i, 0)),
                  *_w_tile_specs(d, lambda i: W_DIFF_V_TILE),
                  pl.BlockSpec((tm, width), lambda i: (i, Z_MEM_Q // width)),
                  pl.BlockSpec((n_mem, width), lambda i: (i // tiles_per_batch, 0)),
                  pl.BlockSpec((n_mem, width), lambda i: (i // tiles_per_batch, 1))],
        out_specs=(pl.BlockSpec((IN_TILE, tm), lambda i: (0, i)),
                   pl.BlockSpec((tm, width), lambda i: (i, 0))),
        scratch_shapes=[pltpu.VMEM((IN_TILE, d), BF16)],
        compiler_params=_params("arbitrary"),
        name="v_proj_t",
    )(h, w_in_t, w_in_t, z, kv, kv)


def _chunk_cumsum(x):
    row_in_chunk = lax.broadcasted_iota(jnp.int32, x.shape, 0) % CHUNK
    shift = 1
    while shift < CHUNK:
        x = x + jnp.where(row_in_chunk >= shift, pltpu.roll(x, shift, 0), 0.0)
        shift *= 2
    return x


def _gla_kernel(q_ref, k_ref, v_ref, g_ref, la_ref, ng_ref, o_ref, s_ref, *, ts):
    @pl.when(pl.program_id(2) == 0)
    def _():
        s_ref[...] = jnp.zeros_like(s_ref)

    bcum_all = _chunk_cumsum(la_ref[...])
    row = lax.broadcasted_iota(jnp.int32, (CHUNK, CHUNK), 0)
    col = lax.broadcasted_iota(jnp.int32, (CHUNK, CHUNK), 1)
    causal = row >= col

    n_chunks = ts // CHUNK
    chunk_rows = [slice(c * CHUNK, (c + 1) * CHUNK) for c in range(n_chunks)]

    lhs, incs, decays = [], [], []
    for rows in chunk_rows:
        bcum = bcum_all[rows]
        b_last = bcum[CHUNK - 1:CHUNK, :]
        q = q_ref[rows, :].astype(F32) * (GLA_DK ** -0.5)
        k = k_ref[rows, :].astype(F32)
        q_dec = (q * jnp.exp(bcum)).astype(BF16)
        k_dec = (k * jnp.exp(-bcum)).astype(BF16)
        k_tail = (k * jnp.exp(b_last - bcum)).astype(BF16)
        att = jnp.where(causal, _nt_dot(q_dec, k_dec), 0.0).astype(BF16)
        lhs.append(jnp.concatenate([q_dec, att], axis=1))
        incs.append(_tn_dot(k_tail, v_ref[rows, :]))
        decay_col = jnp.broadcast_to(jnp.exp(b_last), (GLA_DK, GLA_DK)).T
        decays.append(jnp.concatenate([decay_col] * (GLA_DV // GLA_DK), axis=1))

    state = s_ref[...]
    states = []
    for c in range(n_chunks):
        states.append(state.astype(BF16))
        state = decays[c] * state + incs[c]
    s_ref[...] = state

    for c, rows in enumerate(chunk_rows):
        o = jnp.dot(lhs[c], jnp.concatenate([states[c], v_ref[rows, :]], axis=0),
                    preferred_element_type=F32)
        gate = g_ref[rows, :].astype(F32)
        o_ref[rows, :] = (_rms(o, ng_ref[...]) * (gate * _sigmoid(gate))).astype(o_ref.dtype)


def _gla(z, log_a, ng, batch, seq, ts=2048):
    m = z.shape[0]
    nt = seq // ts
    rows = lambda b, h, t: b * nt + t
    return pl.pallas_call(
        functools.partial(_gla_kernel, ts=ts),
        out_shape=jax.ShapeDtypeStruct((m, GLA_HEADS * GLA_DV), BF16),
        grid=(batch, GLA_HEADS, nt),
        in_specs=[pl.BlockSpec((ts, GLA_DK), lambda b, h, t: (rows(b, h, t), Z_GLA_Q // GLA_DK + h)),
                  pl.BlockSpec((ts, GLA_DK), lambda b, h, t: (rows(b, h, t), Z_GLA_K // GLA_DK + h)),
                  pl.BlockSpec((ts, GLA_DV), lambda b, h, t: (rows(b, h, t), Z_GLA_V // GLA_DV + h)),
                  pl.BlockSpec((ts, GLA_DV), lambda b, h, t: (rows(b, h, t), Z_GLA_G // GLA_DV + h)),
                  pl.BlockSpec((ts, GLA_DK), lambda b, h, t: (rows(b, h, t), h)),
                  pl.BlockSpec((1, GLA_DV), lambda b, h, t: (0, 0))],
        out_specs=pl.BlockSpec((ts, GLA_DV), lambda b, h, t: (rows(b, h, t), h)),
        scratch_shapes=[pltpu.VMEM((GLA_DK, GLA_DV), F32)],
        compiler_params=_params("parallel", "parallel", "arbitrary"),
        name="gla",
    )(z, z, z, z, log_a, ng)


def _diff_kernel(q_ref, k_ref, vt_ref, qg_ref, kg_ref, lq1_ref, lk1_ref, lq2_ref, lk2_ref, sg_ref,
                 o_ref, acc_scr, m_scr, l_scr, *, tq):
    qi = pl.program_id(2)
    m_scr[...] = jnp.full_like(m_scr, NEG_INF)
    l_scr[...] = jnp.zeros_like(l_scr)
    acc_scr[...] = jnp.zeros_like(acc_scr)
    n_streams = q_ref.shape[1] // DIFF_DH
    half = tq // 2

    score_bound = (1.02 * DIFF_DH * DIFF_Q_SCALE) * (jnp.max(jnp.abs(qg_ref[...]))
                                                    * jnp.max(jnp.abs(kg_ref[...])))
    bounded = score_bound <= SCORE_LIMIT

    def update(c, lanes, s, vt, shifted):
        if shifted:
            m_old = m_scr[c, :, lanes]
            m_new = jnp.maximum(m_old, jnp.max(s, axis=0, keepdims=True))
            alpha = jnp.exp2(m_old - m_new)
            p = jnp.exp2(s - m_new)
            l_scr[c, :, lanes] = alpha * l_scr[c, :, lanes] + jnp.sum(p, axis=0, keepdims=True)
            m_scr[c, :, lanes] = m_new
            acc_scr[c, :, lanes] = (alpha * acc_scr[c, :, lanes]
                                    + jnp.dot(vt, p.astype(BF16), preferred_element_type=F32))
        else:
            p = jnp.exp2(s)
            l_scr[c, :, lanes] = l_scr[c, :, lanes] + jnp.sum(p, axis=0, keepdims=True)
            acc_scr[c, :, lanes] = (acc_scr[c, :, lanes]
                                    + jnp.dot(vt, p.astype(BF16), preferred_element_type=F32))

    def values_t(c, start, n):
        head = c // 2
        return vt_ref[head * DIFF_DV:(head + 1) * DIFF_DV, pl.ds(start, n)]

    def full_block(kb, carry, shifted):
        start = pl.multiple_of(kb * tq, tq)
        scores = []
        for c in range(n_streams):
            cols = slice(c * DIFF_DH, (c + 1) * DIFF_DH)
            scores.append(_nt_dot(k_ref[pl.ds(start, tq), cols], q_ref[:, cols]))
        for c in range(n_streams):
            update(c, slice(0, tq), scores[c], values_t(c, start, tq), shifted)
        return carry

    def diag_block(kb, shifted):
        lo = pl.multiple_of(kb * tq, tq)
        hi = pl.multiple_of(kb * tq + half, half)
        key_chunk = lax.broadcasted_iota(jnp.int32, (half, half), 0) // CHUNK
        query_chunk = lax.broadcasted_iota(jnp.int32, (half, half), 1) // CHUNK
        visible = key_chunk <= query_chunk
        scores = []
        for c in range(n_streams):
            cols = slice(c * DIFF_DH, (c + 1) * DIFF_DH)
            k_lo, k_hi = k_ref[pl.ds(lo, half), cols], k_ref[pl.ds(hi, half), cols]
            q_lo, q_hi = q_ref[:half, cols], q_ref[half:, cols]
            s_lo = jnp.where(visible, _nt_dot(k_lo, q_lo), NEG_INF)
            s_hi = jnp.concatenate([_nt_dot(k_lo, q_hi),
                                    jnp.where(visible, _nt_dot(k_hi, q_hi), NEG_INF)], axis=0)
            scores.append((s_lo, s_hi))
        for c in range(n_streams):
            update(c, slice(0, half), scores[c][0], values_t(c, lo, half), shifted)
            update(c, slice(half, tq), scores[c][1], values_t(c, lo, tq), shifted)

    def finish():
        lam = (jnp.exp(jnp.sum(lq1_ref[...] * lk1_ref[...], axis=-1, keepdims=True))
               - jnp.exp(jnp.sum(lq2_ref[...] * lk2_ref[...], axis=-1, keepdims=True)) + LAM_INIT)
        for head in range(n_streams // 2):
            c1, c2 = 2 * head, 2 * head + 1
            o_t = acc_scr[c1] / l_scr[c1] - lam * (acc_scr[c2] / l_scr[c2])
            o_ref[:, head * DIFF_DV:(head + 1) * DIFF_DV] = (
                _rms(o_t.T, sg_ref[...]) * (1.0 - LAM_INIT)).astype(o_ref.dtype)

    def all_blocks(shifted):
        lax.fori_loop(0, qi, functools.partial(full_block, shifted=shifted), 0)
        diag_block(qi, shifted)
        finish()

    @pl.when(bounded)
    def _():
        all_blocks(shifted=False)

    @pl.when(jnp.logical_not(bounded))
    def _():
        all_blocks(shifted=True)


def _diff_attn(z, v_t, qg, kg, lq1, lk1, lq2, lk2, sg, batch, seq, tq=512, heads_per_step=4):
    m = z.shape[0]
    nq = seq // tq
    width = heads_per_step * DIFF_DV
    ns = 2 * heads_per_step
    vec = pl.BlockSpec((1, DIFF_DH), lambda b, g, i: (0, 0))
    return pl.pallas_call(
        functools.partial(_diff_kernel, tq=tq),
        out_shape=jax.ShapeDtypeStruct((m, DIFF_HEADS * DIFF_DV), BF16),
        grid=(batch, DIFF_HEADS // heads_per_step, nq),
        in_specs=[pl.BlockSpec((tq, width), lambda b, g, i: (b * nq + i, Z_DIFF_Q // width + g)),
                  pl.BlockSpec((seq, width), lambda b, g, i: (b, Z_DIFF_K // width + g)),
                  pl.BlockSpec((width, seq), lambda b, g, i: (g, b)),
                  vec, vec, vec, vec, vec, vec,
                  pl.BlockSpec((1, DIFF_DV), lambda b, g, i: (0, 0))],
        out_specs=pl.BlockSpec((tq, width), lambda b, g, i: (b * nq + i, g)),
        scratch_shapes=[pltpu.VMEM((ns, DIFF_DV, tq), F32),
                        pltpu.VMEM((ns, 1, tq), F32),
                        pltpu.VMEM((ns, 1, tq), F32)],
        compiler_params=_params("parallel", "parallel", "arbitrary"),
        name="diff_attn",
    )(z, z, v_t, qg, kg, lq1, lk1, lq2, lk2, sg)


def _mem_kv_kernel(mn_ref, w_ref, kg_ref, kv_ref, *, n_key_tiles):
    j = pl.program_id(0)
    acc = jnp.dot(mn_ref[...], w_ref[...].astype(BF16), preferred_element_type=F32)

    @pl.when(j < n_key_tiles)
    def _():
        _store_group_norm(acc, kg_ref[...], MEM_DH, 1.0, kv_ref)

    @pl.when(j >= n_key_tiles)
    def _():
        kv_ref[...] = acc.astype(kv_ref.dtype)


def _mem_kv(mem_n, w_kv, kg, tn=512):
    m, d = mem_n.shape
    n = w_kv.shape[1]
    return pl.pallas_call(
        functools.partial(_mem_kv_kernel, n_key_tiles=(n // 2) // tn),
        out_shape=jax.ShapeDtypeStruct((m, n), BF16),
        grid=(n // tn,),
        in_specs=[pl.BlockSpec((m, d), lambda j: (0, 0)),
                  pl.BlockSpec((d, tn), lambda j: (0, j)),
                  pl.BlockSpec((1, MEM_DH), lambda j: (0, 0))],
        out_specs=pl.BlockSpec((m, tn), lambda j: (0, j)),
        compiler_params=_params("parallel"),
        name="mem_kv",
    )(mem_n, w_kv, kg)


def _gate_merge_kernel(h_ref, y0_ref, y1_ref, y2_ref, wg0_ref, wg1_ref, wg2_ref,
                       bg0_ref, bg1_ref, bg2_ref, wb0_ref, wb1_ref, wb2_ref, r0_ref, r1_ref,
                       o_ref, c0_ref, c1_ref):
    _cast_riders((r0_ref, r1_ref), (c0_ref, c1_ref))
    h = h_ref[...]
    merged = None
    for y_ref, wg_ref, bg_ref, wb_ref in ((y0_ref, wg0_ref, bg0_ref, wb0_ref),
                                          (y1_ref, wg1_ref, bg1_ref, wb1_ref),
                                          (y2_ref, wg2_ref, bg2_ref, wb2_ref)):
        gate = _sigmoid(jnp.dot(h, wg_ref[...], preferred_element_type=F32) + bg_ref[...])
        term = gate * jnp.dot(y_ref[...], wb_ref[...], preferred_element_type=F32)
        merged = term if merged is None else merged + term
    o_ref[...] = merged.astype(o_ref.dtype)


def _gate_merge(h, ys, w_gate, b_gate, w_branch, riders, tm=1024, tn=256):
    m, d = h.shape
    bw = w_branch.shape[1]
    ni, nj = m // tm, d // tn
    act = lambda width: pl.BlockSpec((tm, width), lambda i, j: (i, 0))
    wg = lambda b: pl.BlockSpec((d, tn), lambda i, j: (0, b * nj + j))
    bg = lambda b: pl.BlockSpec((1, tn), lambda i, j: (0, b * nj + j))
    wb = lambda b: pl.BlockSpec((None, bw, tn), lambda i, j: (b, 0, j))
    r_specs, r_shapes = _rider_specs(riders, ni * nj, lambda i, j: i * nj + j)
    return pl.pallas_call(
        _gate_merge_kernel,
        out_shape=(jax.ShapeDtypeStruct((m, d), BF16), *r_shapes),
        grid=(ni, nj),
        in_specs=[act(d), act(bw), act(bw), act(bw), wg(0), wg(1), wg(2), bg(0), bg(1), bg(2),
                  wb(0), wb(1), wb(2), *r_specs],
        out_specs=(pl.BlockSpec((tm, tn), lambda i, j: (i, j)), *r_specs),
        compiler_params=_params("arbitrary", "arbitrary"),
        name="gate_merge",
    )(h, ys[0], ys[1], ys[2], w_gate, w_gate, w_gate, b_gate, b_gate, b_gate,
      w_branch, w_branch, w_branch, *riders)


def _out_proj_kernel(mg_ref, w_ref, x_ref, g_ref, x1_ref, hf_ref):
    x1 = x_ref[...] + jnp.dot(mg_ref[...], w_ref[...], preferred_element_type=F32)
    x1_ref[...] = x1
    hf_ref[...] = _rms(x1, g_ref[...]).astype(hf_ref.dtype)


def _out_proj(merged, w_out, x2, g, tm=512):
    m, d = x2.shape
    row = lambda i: (i, 0)
    fixed = lambda i: (0, 0)
    return pl.pallas_call(
        _out_proj_kernel,
        out_shape=(jax.ShapeDtypeStruct((m, d), F32), jax.ShapeDtypeStruct((m, d), BF16)),
        grid=(m // tm,),
        in_specs=[pl.BlockSpec((tm, d), row),
                  pl.BlockSpec((d, d), fixed, pipeline_mode=pl.Buffered(1)),
                  pl.BlockSpec((tm, d), row), pl.BlockSpec((1, d), fixed)],
        out_specs=(pl.BlockSpec((tm, d), row), pl.BlockSpec((tm, d), row)),
        compiler_params=_params("parallel"),
        name="out_proj",
    )(merged, w_out, x2, g)


def _ffn_up_kernel(hf_ref, wg_ref, wu_ref, r0_ref, a_ref, c0_ref):
    _cast_riders((r0_ref,), (c0_ref,))
    hf = hf_ref[...]
    gate = jnp.dot(hf, wg_ref[...], preferred_element_type=F32)
    up = jnp.dot(hf, wu_ref[...], preferred_element_type=F32)
    a_ref[...] = (gate * _sigmoid(gate) * up).astype(a_ref.dtype)


def _ffn_up(hf, w_in, riders, tm=1024, tf=512):
    m, d = hf.shape
    d_ff = w_in.shape[1] // 2
    ni, nj = m // tm, d_ff // tf
    r_specs, r_shapes = _rider_specs(riders, ni * nj, lambda i, j: i * nj + j)
    return pl.pallas_call(
        _ffn_up_kernel,
        out_shape=(jax.ShapeDtypeStruct((m, d_ff), BF16), *r_shapes),
        grid=(ni, nj),
        in_specs=[pl.BlockSpec((tm, d), lambda i, j: (i, 0)),
                  pl.BlockSpec((d, tf), lambda i, j: (0, j)),
                  pl.BlockSpec((d, tf), lambda i, j: (0, nj + j)),
                  *r_specs],
        out_specs=(pl.BlockSpec((tm, tf), lambda i, j: (i, j)), *r_specs),
        compiler_params=_params("arbitrary", "arbitrary"),
        name="ffn_up",
    )(hf, w_in, w_in, *riders)


def _ffn_down_kernel(a_ref, w_ref, x1_ref, o_ref):
    o_ref[...] = x1_ref[...] + jnp.dot(a_ref[...], w_ref[...], preferred_element_type=F32)


def _ffn_down(a, w_down, x1, tm=1024, tn=512):
    m, d_ff = a.shape
    d = w_down.shape[1]
    return pl.pallas_call(
        _ffn_down_kernel,
        out_shape=jax.ShapeDtypeStruct((m, d), F32),
        grid=(m // tm, d // tn),
        in_specs=[pl.BlockSpec((tm, d_ff), lambda i, j: (i, 0)),
                  pl.BlockSpec((d_ff, tn), lambda i, j: (0, j)),
                  pl.BlockSpec((tm, tn), lambda i, j: (i, j))],
        out_specs=pl.BlockSpec((tm, tn), lambda i, j: (i, j)),
        compiler_params=_params("parallel", "arbitrary"),
        name="ffn_down",
    )(a, w_down, x1)


def kernel(x, mem, norm_mix_g, norm_mem_g, w_in, gla_w_alpha_up, gla_b_alpha, gla_norm_g,
           diff_q_norm_g, diff_k_norm_g, diff_lambda_q1, diff_lambda_k1, diff_lambda_q2,
           diff_lambda_k2, diff_subln_g, mem_q_norm_g, mem_k_norm_g, w_mem_kv, w_branch,
           w_gate, b_gate, w_out, norm_ffn_g, w_ffn_in, w_ffn_down):
    batch, seq, d = x.shape
    n_mem = mem.shape[1]
    depth = w_in.shape[0]
    assert depth == 1, "LAM_INIT is the layer-0 value"
    x2 = x.reshape(batch * seq, d)
    mem2 = mem.reshape(batch * n_mem, d)
    for l in range(depth):
        w_in_t = w_in[l].T
        w_up = jnp.pad(gla_w_alpha_up[l], ((0, LANES - GLA_RANK), (0, 0)))
        row = lambda v: v.reshape(1, -1)

        h, log_a = _norm_mix(x2, row(norm_mix_g[l]), w_in_t, w_up, row(gla_b_alpha[l]))
        z, w_gate_bf, w_branch_bf = _in_proj(
            h, w_in_t, row(diff_q_norm_g[l]), row(diff_k_norm_g[l]), row(mem_q_norm_g[l]),
            riders=(w_gate[l], w_branch[l].reshape(-1, d)))
        y_gla = _gla(z, log_a, row(gla_norm_g[l]), batch, seq)
        mem_n = _norm_rows(mem2, row(norm_mem_g[l]))
        kv = _mem_kv(mem_n, w_mem_kv[l], row(mem_k_norm_g[l]))
        v_t, y_mem = _v_proj_t(h, w_in_t, z, kv, seq, n_mem)
        y_diff = _diff_attn(z, v_t, row(diff_q_norm_g[l]), row(diff_k_norm_g[l]),
                            row(diff_lambda_q1[l]), row(diff_lambda_k1[l]),
                            row(diff_lambda_q2[l]), row(diff_lambda_k2[l]),
                            row(diff_subln_g[l]), batch, seq)
        merged, w_out_bf, w_ffn_in_bf = _gate_merge(
            h, (y_gla, y_diff, y_mem), w_gate_bf, row(b_gate[l]),
            w_branch_bf.reshape(w_branch[l].shape), riders=(w_out[l], w_ffn_in[l]))
        x1, hf = _out_proj(merged, w_out_bf, x2, row(norm_ffn_g[l]))
        a, w_ffn_down_bf = _ffn_up(hf, w_ffn_in_bf, riders=(w_ffn_down[l],))
        x2 = _ffn_down(a, w_ffn_down_bf, x1)
    return x2.reshape(batch, seq, d)
```

```python
import functools

import jax
import jax.numpy as jnp
from jax import lax
from jax.experimental import pallas as pl
from jax.experimental.pallas import tpu as pltpu

F32 = jnp.float32
BF16 = jnp.bfloat16

CHUNK = 64
GLA_HEADS = 4
GLA_DK = 128
GLA_DV = 256
GLA_RANK = 16
GLA_GATE_NORM = 16.0
DIFF_HEADS = 4
DIFF_DH = 128
DIFF_DV = 256
MEM_HEADS = 4
MEM_DH = 256
N_BRANCH = 3
NORM_EPS = 1e-6
NEG_INF = -1e30
LAM_INIT = 0.8 - 0.6 * 1.0
LOG2_E = 1.4426950408889634
SCORE_LIMIT = 40.0
DIFF_Q_SCALE = DIFF_DH ** -0.5 * LOG2_E

LANES = 128
VMEM_LIMIT = 56 * 1024 * 1024

IN_TILE = 1024
Z_GLA_Q, Z_GLA_K, Z_GLA_V, Z_GLA_G = 0, 512, 1024, 2048
Z_DIFF_Q, Z_DIFF_K, Z_MEM_Q = 3072, 4096, 5120
Z_WIDTH = 6144
W_SRC_TILES = (0, 1, 2, 3, 4, 6)
W_DIFF_V_TILE = 5
W_ROWS = 7 * IN_TILE + GLA_RANK
W_FIRST_SHIFTED_TILE = 3


def _params(*sem):
    return pltpu.CompilerParams(dimension_semantics=sem, vmem_limit_bytes=VMEM_LIMIT)


def _nt_dot(a, b):
    return lax.dot_general(a, b, (((1,), (1,)), ((), ())), preferred_element_type=F32)


def _tn_dot(a, b, precision=None):
    return lax.dot_general(a, b, (((0,), (0,)), ((), ())), preferred_element_type=F32,
                           precision=precision)


def _rms(v, gain):
    ms = jnp.mean(v * v, axis=-1, keepdims=True)
    return v * lax.rsqrt(ms + NORM_EPS) * gain


def _sigmoid(v):
    return 1.0 / (1.0 + jnp.exp(-v))


BF16_SUBLANES = 16


def _rider_specs(weights, n_steps, step_of):
    specs, shapes = [], []
    for w in weights:
        rows, cols = w.shape
        chunk = BF16_SUBLANES
        while rows % chunk or rows // chunk > n_steps:
            chunk += BF16_SUBLANES
        last = rows // chunk - 1
        specs.append(pl.BlockSpec((chunk, cols),
                                  lambda *g, last=last: (jnp.minimum(step_of(*g), last), 0)))
        shapes.append(jax.ShapeDtypeStruct((rows, cols), BF16))
    return specs, shapes


def _cast_riders(in_refs, out_refs):
    for src, dst in zip(in_refs, out_refs):
        dst[...] = src[...].astype(BF16)


def _norm_mix_kernel(x_ref, g_ref, wa_ref, wup_ref, bal_ref, h_ref, la_ref):
    h = _rms(x_ref[...], g_ref[...]).astype(BF16)
    h_ref[...] = h
    a_low = _nt_dot(h, wa_ref[...].astype(BF16))
    w_up = wup_ref[...]
    a_hi, w_hi = a_low.astype(BF16), w_up.astype(BF16)
    a_lo = (a_low - a_hi.astype(F32)).astype(BF16)
    w_lo = (w_up - w_hi.astype(F32)).astype(BF16)
    pre = (jnp.dot(a_hi, w_hi, preferred_element_type=F32)
           + (jnp.dot(a_lo, w_hi, preferred_element_type=F32)
              + jnp.dot(a_hi, w_lo, preferred_element_type=F32))) + bal_ref[...]
    log_sig = jnp.minimum(pre, 0.0) - jnp.log1p(jnp.exp(-jnp.abs(pre)))
    la_ref[...] = log_sig * (1.0 / GLA_GATE_NORM)


def _norm_mix(x2, g, w_in_t, wup, bal, tr=1024):
    m, d = x2.shape
    n = wup.shape[1]
    fixed = lambda i: (0, 0)
    decay_block = (Z_GLA_G + GLA_HEADS * GLA_DV) // LANES
    return pl.pallas_call(
        _norm_mix_kernel,
        out_shape=(jax.ShapeDtypeStruct((m, d), BF16), jax.ShapeDtypeStruct((m, n), F32)),
        grid=(m // tr,),
        in_specs=[pl.BlockSpec((tr, d), lambda i: (i, 0)),
                  pl.BlockSpec((1, d), fixed),
                  pl.BlockSpec((LANES, d), lambda i: (decay_block, 0)),
                  pl.BlockSpec((LANES, n), fixed),
                  pl.BlockSpec((1, n), fixed)],
        out_specs=(pl.BlockSpec((tr, d), lambda i: (i, 0)),
                   pl.BlockSpec((tr, n), lambda i: (i, 0))),
        compiler_params=_params("parallel"),
        name="norm_mix",
    )(x2, g, w_in_t, wup, bal)


def _store_group_norm(acc, gain, width, scale, out_ref):
    for s in range(0, acc.shape[1], width):
        blk = acc[:, s:s + width]
        out_ref[:, s:s + width] = (_rms(blk, gain) * scale).astype(out_ref.dtype)


def _w_tile_specs(d, src_tile):
    hi_per_tile = IN_TILE // GLA_RANK
    return [pl.BlockSpec((IN_TILE, d), lambda *g: (src_tile(*g), 0)),
            pl.BlockSpec((GLA_RANK, d), lambda *g: ((src_tile(*g) + 1) * hi_per_tile, 0))]


def _cast_w_tile(w_lo_ref, w_hi_ref, w_scr, first_step, shifted):
    @pl.when(first_step & jnp.logical_not(shifted))
    def _():
        w_scr[...] = w_lo_ref[...].astype(BF16)

    @pl.when(first_step & shifted)
    def _():
        w_scr[:IN_TILE - GLA_RANK, :] = w_lo_ref[GLA_RANK:, :].astype(BF16)
        w_scr[IN_TILE - GLA_RANK:, :] = w_hi_ref[...].astype(BF16)


def _in_proj_kernel(h_ref, w_lo_ref, w_hi_ref, dq_g_ref, dk_g_ref, mq_g_ref, r0_ref, r1_ref,
                    z_ref, c0_ref, c1_ref, w_scr):
    _cast_riders((r0_ref, r1_ref), (c0_ref, c1_ref))
    j = pl.program_id(0)
    _cast_w_tile(w_lo_ref, w_hi_ref, w_scr, pl.program_id(1) == 0, j >= W_FIRST_SHIFTED_TILE)
    j_dq, j_dk, j_mq = Z_DIFF_Q // IN_TILE, Z_DIFF_K // IN_TILE, Z_MEM_Q // IN_TILE

    def tile(epilogue, row_parts):
        part = h_ref.shape[0] // row_parts
        for r in range(row_parts):
            rows = slice(r * part, (r + 1) * part)
            epilogue(_nt_dot(h_ref[rows, :], w_scr[...]), z_ref.at[rows, :])

    def plain(acc, out_ref):
        out_ref[...] = acc.astype(out_ref.dtype)

    @pl.when((j != j_dq) & (j != j_dk) & (j != j_mq))
    def _():
        tile(plain, 1)

    @pl.when((j == j_dq) | (j == j_dk))
    def _():
        gain = jnp.where(j == j_dq, dq_g_ref[...] * DIFF_Q_SCALE, dk_g_ref[...])
        tile(lambda acc, out: _store_group_norm(acc, gain, DIFF_DH, 1.0, out), 2)

    @pl.when(j == j_mq)
    def _():
        tile(lambda acc, out: _store_group_norm(acc, mq_g_ref[...], MEM_DH, MEM_DH ** -0.5, out), 2)


def _in_proj(h, w_in_t, dq_g, dk_g, mq_g, riders, tm=1024):
    m, d = h.shape
    assert w_in_t.shape[0] == W_ROWS
    nj, ni = Z_WIDTH // IN_TILE, m // tm
    assert W_SRC_TILES == tuple(j + (j >= W_DIFF_V_TILE) for j in range(nj))
    r_specs, r_shapes = _rider_specs(riders, nj * ni, lambda j, i: j * ni + i)
    return pl.pallas_call(
        _in_proj_kernel,
        out_shape=(jax.ShapeDtypeStruct((m, Z_WIDTH), BF16), *r_shapes),
        grid=(nj, ni),
        in_specs=[pl.BlockSpec((tm, d), lambda j, i: (i, 0)),
                  *_w_tile_specs(d, lambda j, i: jnp.where(j >= W_DIFF_V_TILE, j + 1, j)),
                  pl.BlockSpec((1, DIFF_DH), lambda j, i: (0, 0)),
                  pl.BlockSpec((1, DIFF_DH), lambda j, i: (0, 0)),
                  pl.BlockSpec((1, MEM_DH), lambda j, i: (0, 0)),
                  *r_specs],
        out_specs=(pl.BlockSpec((tm, IN_TILE), lambda j, i: (i, j)), *r_specs),
        scratch_shapes=[pltpu.VMEM((IN_TILE, d), BF16)],
        compiler_params=_params("arbitrary", "arbitrary"),
        name="in_proj",
    )(h, w_in_t, w_in_t, dq_g, dk_g, mq_g, *riders)


def _v_proj_t_kernel(h_ref, w_lo_ref, w_hi_ref, mq_ref, mk_ref, mv_ref, vt_ref, ym_ref, w_scr, *,
                     sub_rows):
    _cast_w_tile(w_lo_ref, w_hi_ref, w_scr, pl.program_id(0) == 0,
                 W_DIFF_V_TILE >= W_FIRST_SHIFTED_TILE)
    parts = [slice(r, r + sub_rows) for r in range(0, mq_ref.shape[0], sub_rows)]
    heads = [slice(hd * MEM_DH, (hd + 1) * MEM_DH) for hd in range(MEM_HEADS)]
    scores = {(r, hd): _nt_dot(mq_ref[rows, cols], mk_ref[:, cols])
              for r, rows in enumerate(parts) for hd, cols in enumerate(heads)}
    for r, rows in enumerate(parts):
        vt_ref[:, rows] = _nt_dot(w_scr[...], h_ref[rows, :]).astype(vt_ref.dtype)
        for hd, cols in enumerate(heads):
            s = scores[r, hd]
            e = jnp.exp(s - jnp.max(s, axis=-1, keepdims=True))
            p = (e / jnp.sum(e, axis=-1, keepdims=True)).astype(BF16)
            ym_ref[rows, cols] = jnp.dot(p, mv_ref[:, cols],
                                         preferred_element_type=F32).astype(ym_ref.dtype)


def _v_proj_t(h, w_in_t, z, kv, seq, n_mem, tm=1024, sub_rows=512):
    m, d = h.shape
    width = MEM_HEADS * MEM_DH
    tiles_per_batch = seq // tm
    return pl.pallas_call(
        functools.partial(_v_proj_t_kernel, sub_rows=sub_rows),
        out_shape=(jax.ShapeDtypeStruct((IN_TILE, m), BF16), jax.ShapeDtypeStruct((m, width), BF16)),
        grid=(m // tm,),
        in_specs=[pl.BlockSpec((tm, d), lambda i: (i, 0)),
                  *_w_tile_specs(d, lambda i: W_DIFF_V_TILE),
                  pl.BlockSpec((tm, width), lambda i: (i, Z_MEM_Q // width)),
                  pl.BlockSpec((n_mem, width), lambda i: (i // tiles_per_batch, 0)),
                  pl.BlockSpec((n_mem, width), lambda i: (i // tiles_per_batch, 1))],
        out_specs=(pl.BlockSpec((IN_TILE, tm), lambda i: (0, i)),
                   pl.BlockSpec((tm, width), lambda i: (i, 0))),
        scratch_shapes=[pltpu.VMEM((IN_TILE, d), BF16)],
        compiler_params=_params("arbitrary"),
        name="v_proj_t",
    )(h, w_in_t, w_in_t, z, kv, kv)


def _chunk_cumsum(x):
    row_in_chunk = lax.broadcasted_iota(jnp.int32, x.shape, 0) % CHUNK
    shift = 1
    while shift < CHUNK:
        x = x + jnp.where(row_in_chunk >= shift, pltpu.roll(x, shift, 0), 0.0)
        shift *= 2
    return x


def _gla_kernel(q_ref, k_ref, v_ref, g_ref, la_ref, ng_ref, o_ref, s_ref, *, ts):
    @pl.when(pl.program_id(2) == 0)
    def _():
        s_ref[...] = jnp.zeros_like(s_ref)

    bcum_all = _chunk_cumsum(la_ref[...])
    row = lax.broadcasted_iota(jnp.int32, (CHUNK, CHUNK), 0)
    col = lax.broadcasted_iota(jnp.int32, (CHUNK, CHUNK), 1)
    causal = row >= col

    n_chunks = ts // CHUNK
    chunk_rows = [slice(c * CHUNK, (c + 1) * CHUNK) for c in range(n_chunks)]

    lhs, incs, decays = [], [], []
    for rows in chunk_rows:
        bcum = bcum_all[rows]
        b_last = bcum[CHUNK - 1:CHUNK, :]
        q = q_ref[rows, :].astype(F32) * (GLA_DK ** -0.5)
        k = k_ref[rows, :].astype(F32)
        q_dec = (q * jnp.exp(bcum)).astype(BF16)
        k_dec = (k * jnp.exp(-bcum)).astype(BF16)
        k_tail = (k * jnp.exp(b_last - bcum)).astype(BF16)
        att = jnp.where(causal, _nt_dot(q_dec, k_dec), 0.0).astype(BF16)
        lhs.append(jnp.concatenate([q_dec, att], axis=1))
        incs.append(_tn_dot(k_tail, v_ref[rows, :]))
        decay_col = jnp.broadcast_to(jnp.exp(b_last), (GLA_DK, GLA_DK)).T
        decays.append(jnp.concatenate([decay_col] * (GLA_DV // GLA_DK), axis=1))

    state = s_ref[...]
    states = []
    for c in range(n_chunks):
        states.append(state.astype(BF16))
        state = decays[c] * state + incs[c]
    s_ref[...] = state

    for c, rows in enumerate(chunk_rows):
        o = jnp.dot(lhs[c], jnp.concatenate([states[c], v_ref[rows, :]], axis=0),
                    preferred_element_type=F32)
        gate = g_ref[rows, :].astype(F32)
        o_ref[rows, :] = (_rms(o, ng_ref[...]) * (gate * _sigmoid(gate))).astype(o_ref.dtype)


def _gla(z, log_a, ng, batch, seq, ts=2048):
    m = z.shape[0]
    nt = seq // ts
    rows = lambda b, h, t: b * nt + t
    return pl.pallas_call(
        functools.partial(_gla_kernel, ts=ts),
        out_shape=jax.ShapeDtypeStruct((m, GLA_HEADS * GLA_DV), BF16),
        grid=(batch, GLA_HEADS, nt),
        in_specs=[pl.BlockSpec((ts, GLA_DK), lambda b, h, t: (rows(b, h, t), Z_GLA_Q // GLA_DK + h)),
                  pl.BlockSpec((ts, GLA_DK), lambda b, h, t: (rows(b, h, t), Z_GLA_K // GLA_DK + h)),
                  pl.BlockSpec((ts, GLA_DV), lambda b, h, t: (rows(b, h, t), Z_GLA_V // GLA_DV + h)),
                  pl.BlockSpec((ts, GLA_DV), lambda b, h, t: (rows(b, h, t), Z_GLA_G // GLA_DV + h)),
                  pl.BlockSpec((ts, GLA_DK), lambda b, h, t: (rows(b, h, t), h)),
                  pl.BlockSpec((1, GLA_DV), lambda b, h, t: (0, 0))],
        out_specs=pl.BlockSpec((ts, GLA_DV), lambda b, h, t: (rows(b, h, t), h)),
        scratch_shapes=[pltpu.VMEM((GLA_DK, GLA_DV), F32)],
        compiler_params=_params("parallel", "parallel", "arbitrary"),
        name="gla",
    )(z, z, z, z, log_a, ng)


def _diff_kernel(q_ref, k_ref, vt_ref, qg_ref, kg_ref, lq1_ref, lk1_ref, lq2_ref, lk2_ref, sg_ref,
                 o_ref, acc_scr, m_scr, l_scr, *, tq):
    qi = pl.program_id(2)
    m_scr[...] = jnp.full_like(m_scr, NEG_INF)
    l_scr[...] = jnp.zeros_like(l_scr)
    acc_scr[...] = jnp.zeros_like(acc_scr)
    n_streams = q_ref.shape[1] // DIFF_DH
    half = tq // 2

    score_bound = (1.02 * DIFF_DH * DIFF_Q_SCALE) * (jnp.max(jnp.abs(qg_ref[...]))
                                                    * jnp.max(jnp.abs(kg_ref[...])))
    bounded = score_bound <= SCORE_LIMIT

    def update(c, lanes, s, vt, shifted):
        if shifted:
            m_old = m_scr[c, :, lanes]
            m_new = jnp.maximum(m_old, jnp.max(s, axis=0, keepdims=True))
            alpha = jnp.exp2(m_old - m_new)
            p = jnp.exp2(s - m_new)
            l_scr[c, :, lanes] = alpha * l_scr[c, :, lanes] + jnp.sum(p, axis=0, keepdims=True)
            m_scr[c, :, lanes] = m_new
            acc_scr[c, :, lanes] = (alpha * acc_scr[c, :, lanes]
                                    + jnp.dot(vt, p.astype(BF16), preferred_element_type=F32))
        else:
            p = jnp.exp2(s)
            l_scr[c, :, lanes] = l_scr[c, :, lanes] + jnp.sum(p, axis=0, keepdims=True)
            acc_scr[c, :, lanes] = (acc_scr[c, :, lanes]
                                    + jnp.dot(vt, p.astype(BF16), preferred_element_type=F32))

    def values_t(c, start, n):
        head = c // 2
        return vt_ref[head * DIFF_DV:(head + 1) * DIFF_DV, pl.ds(start, n)]

    def full_block(kb, carry, shifted):
        start = pl.multiple_of(kb * tq, tq)
        scores = []
        for c in range(n_streams):
            cols = slice(c * DIFF_DH, (c + 1) * DIFF_DH)
            scores.append(_nt_dot(k_ref[pl.ds(start, tq), cols], q_ref[:, cols]))
        for c in range(n_streams):
            update(c, slice(0, tq), scores[c], values_t(c, start, tq), shifted)
        return carry

    def diag_block(kb, shifted):
        lo = pl.multiple_of(kb * tq, tq)
        hi = pl.multiple_of(kb * tq + half, half)
        key_chunk = lax.broadcasted_iota(jnp.int32, (half, half), 0) // CHUNK
        query_chunk = lax.broadcasted_iota(jnp.int32, (half, half), 1) // CHUNK
        visible = key_chunk <= query_chunk
        scores = []
        for c in range(n_streams):
            cols = slice(c * DIFF_DH, (c + 1) * DIFF_DH)
            k_lo, k_hi = k_ref[pl.ds(lo, half), cols], k_ref[pl.ds(hi, half), cols]
            q_lo, q_hi = q_ref[:half, cols], q_ref[half:, cols]
            s_lo = jnp.where(visible, _nt_dot(k_lo, q_lo), NEG_INF)
            s_hi = jnp.concatenate([_nt_dot(k_lo, q_hi),
                                    jnp.where(visible, _nt_dot(k_hi, q_hi), NEG_INF)], axis=0)
            scores.append((s_lo, s_hi))
        lam = (jnp.exp(jnp.sum(lq1_ref[...] * lk1_ref[...], axis=-1, keepdims=True))
               - jnp.exp(jnp.sum(lq2_ref[...] * lk2_ref[...], axis=-1, keepdims=True)) + LAM_INIT)
        for head in range(n_streams // 2):
            c1, c2 = 2 * head, 2 * head + 1
            for c in (c1, c2):
                update(c, slice(0, half), scores[c][0], values_t(c, lo, half), shifted)
                update(c, slice(half, tq), scores[c][1], values_t(c, lo, tq), shifted)
            o_t = acc_scr[c1] / l_scr[c1] - lam * (acc_scr[c2] / l_scr[c2])
            o_ref[:, head * DIFF_DV:(head + 1) * DIFF_DV] = (
                _rms(o_t.T, sg_ref[...]) * (1.0 - LAM_INIT)).astype(o_ref.dtype)

    def all_blocks(shifted):
        lax.fori_loop(0, qi, functools.partial(full_block, shifted=shifted), 0)
        diag_block(qi, shifted)

    @pl.when(bounded)
    def _():
        all_blocks(shifted=False)

    @pl.when(jnp.logical_not(bounded))
    def _():
        all_blocks(shifted=True)


def _diff_attn(z, v_t, qg, kg, lq1, lk1, lq2, lk2, sg, batch, seq, tq=512, heads_per_step=4):
    m = z.shape[0]
    nq = seq // tq
    width = heads_per_step * DIFF_DV
    ns = 2 * heads_per_step
    vec = pl.BlockSpec((1, DIFF_DH), lambda b, g, i: (0, 0))
    return pl.pallas_call(
        functools.partial(_diff_kernel, tq=tq),
        out_shape=jax.ShapeDtypeStruct((m, DIFF_HEADS * DIFF_DV), BF16),
        grid=(batch, DIFF_HEADS // heads_per_step, nq),
        in_specs=[pl.BlockSpec((tq, width), lambda b, g, i: (b * nq + i, Z_DIFF_Q // width + g)),
                  pl.BlockSpec((seq, width), lambda b, g, i: (b, Z_DIFF_K // width + g)),
                  pl.BlockSpec((width, seq), lambda b, g, i: (g, b)),
                  vec, vec, vec, vec, vec, vec,
                  pl.BlockSpec((1, DIFF_DV), lambda b, g, i: (0, 0))],
        out_specs=pl.BlockSpec((tq, width), lambda b, g, i: (b * nq + i, g)),
        scratch_shapes=[pltpu.VMEM((ns, DIFF_DV, tq), F32),
                        pltpu.VMEM((ns, 1, tq), F32),
                        pltpu.VMEM((ns, 1, tq), F32)],
        compiler_params=_params("parallel", "parallel", "arbitrary"),
        name="diff_attn",
    )(z, z, v_t, qg, kg, lq1, lk1, lq2, lk2, sg)


def _mem_kv_kernel(mem_ref, ng_ref, w_ref, kg_ref, kv_ref, mn_scr, *, n_key_tiles):
    j = pl.program_id(0)

    @pl.when(j == 0)
    def _():
        mn_scr[...] = _rms(mem_ref[...], ng_ref[...]).astype(BF16)

    acc = jnp.dot(mn_scr[...], w_ref[...].astype(BF16), preferred_element_type=F32)

    @pl.when(j < n_key_tiles)
    def _():
        _store_group_norm(acc, kg_ref[...], MEM_DH, 1.0, kv_ref)

    @pl.when(j >= n_key_tiles)
    def _():
        kv_ref[...] = acc.astype(kv_ref.dtype)


def _mem_kv(mem2, ng, w_kv, kg, tn=512):
    m, d = mem2.shape
    n = w_kv.shape[1]
    return pl.pallas_call(
        functools.partial(_mem_kv_kernel, n_key_tiles=(n // 2) // tn),
        out_shape=jax.ShapeDtypeStruct((m, n), BF16),
        grid=(n // tn,),
        in_specs=[pl.BlockSpec((m, d), lambda j: (0, 0)),
                  pl.BlockSpec((1, d), lambda j: (0, 0)),
                  pl.BlockSpec((d, tn), lambda j: (0, j)),
                  pl.BlockSpec((1, MEM_DH), lambda j: (0, 0))],
        out_specs=pl.BlockSpec((m, tn), lambda j: (0, j)),
        scratch_shapes=[pltpu.VMEM((m, d), BF16)],
        compiler_params=_params("arbitrary"),
        name="mem_kv",
    )(mem2, ng, w_kv, kg)


def _gate_merge_kernel(h_ref, y0_ref, y1_ref, y2_ref, wg0_ref, wg1_ref, wg2_ref,
                       bg0_ref, bg1_ref, bg2_ref, wb0_ref, wb1_ref, wb2_ref, r0_ref, r1_ref,
                       o_ref, c0_ref, c1_ref):
    _cast_riders((r0_ref, r1_ref), (c0_ref, c1_ref))
    h = h_ref[...]
    merged = None
    for y_ref, wg_ref, bg_ref, wb_ref in ((y0_ref, wg0_ref, bg0_ref, wb0_ref),
                                          (y1_ref, wg1_ref, bg1_ref, wb1_ref),
                                          (y2_ref, wg2_ref, bg2_ref, wb2_ref)):
        gate = _sigmoid(jnp.dot(h, wg_ref[...], preferred_element_type=F32) + bg_ref[...])
        term = gate * jnp.dot(y_ref[...], wb_ref[...], preferred_element_type=F32)
        merged = term if merged is None else merged + term
    o_ref[...] = merged.astype(o_ref.dtype)


def _gate_merge(h, ys, w_gate, b_gate, w_branch, riders, tm=1024, tn=256):
    m, d = h.shape
    bw = w_branch.shape[1]
    ni, nj = m // tm, d // tn
    act = lambda width: pl.BlockSpec((tm, width), lambda i, j: (i, 0))
    wg = lambda b: pl.BlockSpec((d, tn), lambda i, j: (0, b * nj + j))
    bg = lambda b: pl.BlockSpec((1, tn), lambda i, j: (0, b * nj + j))
    wb = lambda b: pl.BlockSpec((None, bw, tn), lambda i, j: (b, 0, j))
    r_specs, r_shapes = _rider_specs(riders, ni * nj, lambda i, j: i * nj + j)
    return pl.pallas_call(
        _gate_merge_kernel,
        out_shape=(jax.ShapeDtypeStruct((m, d), BF16), *r_shapes),
        grid=(ni, nj),
        in_specs=[act(d), act(bw), act(bw), act(bw), wg(0), wg(1), wg(2), bg(0), bg(1), bg(2),
                  wb(0), wb(1), wb(2), *r_specs],
        out_specs=(pl.BlockSpec((tm, tn), lambda i, j: (i, j)), *r_specs),
        compiler_params=_params("arbitrary", "arbitrary"),
        name="gate_merge",
    )(h, ys[0], ys[1], ys[2], w_gate, w_gate, w_gate, b_gate, b_gate, b_gate,
      w_branch, w_branch, w_branch, *riders)


def _out_proj_kernel(mg_ref, w_ref, x_ref, g_ref, x1_ref, hf_ref):
    x1 = x_ref[...] + jnp.dot(mg_ref[...], w_ref[...], preferred_element_type=F32)
    x1_ref[...] = x1
    hf_ref[...] = _rms(x1, g_ref[...]).astype(hf_ref.dtype)


def _out_proj(merged, w_out, x2, g, tm=512):
    m, d = x2.shape
    row = lambda i: (i, 0)
    fixed = lambda i: (0, 0)
    return pl.pallas_call(
        _out_proj_kernel,
        out_shape=(jax.ShapeDtypeStruct((m, d), F32), jax.ShapeDtypeStruct((m, d), BF16)),
        grid=(m // tm,),
        in_specs=[pl.BlockSpec((tm, d), row),
                  pl.BlockSpec((d, d), fixed, pipeline_mode=pl.Buffered(1)),
                  pl.BlockSpec((tm, d), row), pl.BlockSpec((1, d), fixed)],
        out_specs=(pl.BlockSpec((tm, d), row), pl.BlockSpec((tm, d), row)),
        compiler_params=_params("parallel"),
        name="out_proj",
    )(merged, w_out, x2, g)


def _ffn_up_kernel(hf_ref, wg_ref, wu_ref, r0_ref, a_ref, c0_ref):
    _cast_riders((r0_ref,), (c0_ref,))
    hf = hf_ref[...]
    gate = jnp.dot(hf, wg_ref[...], preferred_element_type=F32)
    up = jnp.dot(hf, wu_ref[...], preferred_element_type=F32)
    a_ref[...] = (gate * _sigmoid(gate) * up).astype(a_ref.dtype)


def _ffn_up(hf, w_in, riders, tm=1024, tf=512):
    m, d = hf.shape
    d_ff = w_in.shape[1] // 2
    ni, nj = m // tm, d_ff // tf
    r_specs, r_shapes = _rider_specs(riders, ni * nj, lambda i, j: i * nj + j)
    return pl.pallas_call(
        _ffn_up_kernel,
        out_shape=(jax.ShapeDtypeStruct((m, d_ff), BF16), *r_shapes),
        grid=(ni, nj),
        in_specs=[pl.BlockSpec((tm, d), lambda i, j: (i, 0)),
                  pl.BlockSpec((d, tf), lambda i, j: (0, j)),
                  pl.BlockSpec((d, tf), lambda i, j: (0, nj + j)),
                  *r_specs],
        out_specs=(pl.BlockSpec((tm, tf), lambda i, j: (i, j)), *r_specs),
        compiler_params=_params("arbitrary", "arbitrary"),
        name="ffn_up",
    )(hf, w_in, w_in, *riders)


def _ffn_down_kernel(a_ref, w_ref, x1_ref, o_ref):
    o_ref[...] = x1_ref[...] + jnp.dot(a_ref[...], w_ref[...], preferred_element_type=F32)


def _ffn_down(a, w_down, x1, tm=1024, tn=512):
    m, d_ff = a.shape
    d = w_down.shape[1]
    return pl.pallas_call(
        _ffn_down_kernel,
        out_shape=jax.ShapeDtypeStruct((m, d), F32),
        grid=(m // tm, d // tn),
        in_specs=[pl.BlockSpec((tm, d_ff), lambda i, j: (i, 0)),
                  pl.BlockSpec((d_ff, tn), lambda i, j: (0, j)),
                  pl.BlockSpec((tm, tn), lambda i, j: (i, j))],
        out_specs=pl.BlockSpec((tm, tn), lambda i, j: (i, j)),
        compiler_params=_params("parallel", "arbitrary"),
        name="ffn_down",
    )(a, w_down, x1)


def kernel(x, mem, norm_mix_g, norm_mem_g, w_in, gla_w_alpha_up, gla_b_alpha, gla_norm_g,
           diff_q_norm_g, diff_k_norm_g, diff_lambda_q1, diff_lambda_k1, diff_lambda_q2,
           diff_lambda_k2, diff_subln_g, mem_q_norm_g, mem_k_norm_g, w_mem_kv, w_branch,
           w_gate, b_gate, w_out, norm_ffn_g, w_ffn_in, w_ffn_down):
    batch, seq, d = x.shape
    n_mem = mem.shape[1]
    depth = w_in.shape[0]
    assert depth == 1, "LAM_INIT is the layer-0 value"
    x2 = x.reshape(batch * seq, d)
    mem2 = mem.reshape(batch * n_mem, d)
    for l in range(depth):
        w_in_t = w_in[l].T
        w_up = jnp.pad(gla_w_alpha_up[l], ((0, LANES - GLA_RANK), (0, 0)))
        row = lambda v: v.reshape(1, -1)

        h, log_a = _norm_mix(x2, row(norm_mix_g[l]), w_in_t, w_up, row(gla_b_alpha[l]))
        z, w_gate_bf, w_branch_bf = _in_proj(
            h, w_in_t, row(diff_q_norm_g[l]), row(diff_k_norm_g[l]), row(mem_q_norm_g[l]),
            riders=(w_gate[l], w_branch[l].reshape(-1, d)))
        y_gla = _gla(z, log_a, row(gla_norm_g[l]), batch, seq)
        kv = _mem_kv(mem2, row(norm_mem_g[l]), w_mem_kv[l], row(mem_k_norm_g[l]))
        v_t, y_mem = _v_proj_t(h, w_in_t, z, kv, seq, n_mem)
        y_diff = _diff_attn(z, v_t, row(diff_q_norm_g[l]), row(diff_k_norm_g[l]),
                            row(diff_lambda_q1[l]), row(diff_lambda_k1[l]),
                            row(diff_lambda_q2[l]), row(diff_lambda_k2[l]),
                            row(diff_subln_g[l]), batch, seq)
        merged, w_out_bf, w_ffn_in_bf = _gate_merge(
            h, (y_gla, y_diff, y_mem), w_gate_bf, row(b_gate[l]),
            w_branch_bf.reshape(w_branch[l].shape), riders=(w_out[l], w_ffn_in[l]))
        x1, hf = _out_proj(merged, w_out_bf, x2, row(norm_ffn_g[l]))
        a, w_ffn_down_bf = _ffn_up(hf, w_ffn_in_bf, riders=(w_ffn_down[l],))
        x2 = _ffn_down(a, w_ffn_down_bf, x1)
    return x2.reshape(batch, seq, d)
```

```python
import functools

import jax
import jax.numpy as jnp
from jax import lax
from jax.experimental import pallas as pl
from jax.experimental.pallas import tpu as pltpu

F32 = jnp.float32
BF16 = jnp.bfloat16

CHUNK = 64
GLA_HEADS = 4
GLA_DK = 128
GLA_DV = 256
GLA_RANK = 16
GLA_GATE_NORM = 16.0
DIFF_HEADS = 4
DIFF_DH = 128
DIFF_DV = 256
MEM_HEADS = 4
MEM_DH = 256
N_BRANCH = 3
NORM_EPS = 1e-6
NEG_INF = -1e30
LAM_INIT = 0.8 - 0.6 * 1.0
LOG2_E = 1.4426950408889634
SCORE_LIMIT = 40.0
SCORES_AHEAD = 2
DIFF_Q_SCALE = DIFF_DH ** -0.5 * LOG2_E

LANES = 128
VMEM_LIMIT = 56 * 1024 * 1024

IN_TILE = 1024
Z_GLA_Q, Z_GLA_K, Z_GLA_V, Z_GLA_G = 0, 512, 1024, 2048
Z_DIFF_Q, Z_DIFF_K, Z_MEM_Q = 3072, 4096, 5120
Z_WIDTH = 6144
W_SRC_TILES = (0, 1, 2, 3, 4, 6)
W_DIFF_V_TILE = 5
W_ROWS = 7 * IN_TILE + GLA_RANK
W_FIRST_SHIFTED_TILE = 3


def _params(*sem):
    return pltpu.CompilerParams(dimension_semantics=sem, vmem_limit_bytes=VMEM_LIMIT)


def _nt_dot(a, b):
    return lax.dot_general(a, b, (((1,), (1,)), ((), ())), preferred_element_type=F32)


def _tn_dot(a, b, precision=None):
    return lax.dot_general(a, b, (((0,), (0,)), ((), ())), preferred_element_type=F32,
                           precision=precision)


def _rms(v, gain):
    ms = jnp.mean(v * v, axis=-1, keepdims=True)
    return v * lax.rsqrt(ms + NORM_EPS) * gain


def _sigmoid(v):
    return 1.0 / (1.0 + jnp.exp(-v))


BF16_SUBLANES = 16


def _rider_specs(weights, n_steps, step_of):
    specs, shapes = [], []
    for w in weights:
        rows, cols = w.shape
        chunk = BF16_SUBLANES
        while rows % chunk or rows // chunk > n_steps:
            chunk += BF16_SUBLANES
        last = rows // chunk - 1
        specs.append(pl.BlockSpec((chunk, cols),
                                  lambda *g, last=last: (jnp.minimum(step_of(*g), last), 0)))
        shapes.append(jax.ShapeDtypeStruct((rows, cols), BF16))
    return specs, shapes


def _cast_riders(in_refs, out_refs):
    for src, dst in zip(in_refs, out_refs):
        dst[...] = src[...].astype(BF16)


GATE_TILE = 256


def _tile_major_gate_cols(n_cols):
    per_branch = n_cols // N_BRANCH
    return [((j * N_BRANCH + b) * GATE_TILE, b * per_branch + j * GATE_TILE)
            for j in range(per_branch // GATE_TILE) for b in range(N_BRANCH)]


def _cast_gate_rider(src, dst):
    for dst_col, src_col in _tile_major_gate_cols(src.shape[1]):
        dst[:, dst_col:dst_col + GATE_TILE] = src[:, src_col:src_col + GATE_TILE].astype(BF16)


def _norm_mix_kernel(x_ref, g_ref, wa_ref, wup_ref, bal_ref, h_ref, la_ref):
    h = _rms(x_ref[...], g_ref[...]).astype(BF16)
    h_ref[...] = h
    a_low = _nt_dot(h, wa_ref[...].astype(BF16))
    w_up = wup_ref[...]
    a_hi, w_hi = a_low.astype(BF16), w_up.astype(BF16)
    a_lo = (a_low - a_hi.astype(F32)).astype(BF16)
    w_lo = (w_up - w_hi.astype(F32)).astype(BF16)
    pre = (jnp.dot(a_hi, w_hi, preferred_element_type=F32)
           + (jnp.dot(a_lo, w_hi, preferred_element_type=F32)
              + jnp.dot(a_hi, w_lo, preferred_element_type=F32))) + bal_ref[...]
    log_sig = jnp.minimum(pre, 0.0) - jnp.log1p(jnp.exp(-jnp.abs(pre)))
    la_ref[...] = log_sig * (1.0 / GLA_GATE_NORM)


def _norm_mix(x2, g, w_in_t, wup, bal, tr=1024):
    m, d = x2.shape
    n = wup.shape[1]
    fixed = lambda i: (0, 0)
    decay_block = (Z_GLA_G + GLA_HEADS * GLA_DV) // LANES
    return pl.pallas_call(
        _norm_mix_kernel,
        out_shape=(jax.ShapeDtypeStruct((m, d), BF16), jax.ShapeDtypeStruct((m, n), F32)),
        grid=(m // tr,),
        in_specs=[pl.BlockSpec((tr, d), lambda i: (i, 0)),
                  pl.BlockSpec((1, d), fixed),
                  pl.BlockSpec((LANES, d), lambda i: (decay_block, 0)),
                  pl.BlockSpec((LANES, n), fixed),
                  pl.BlockSpec((1, n), fixed)],
        out_specs=(pl.BlockSpec((tr, d), lambda i: (i, 0)),
                   pl.BlockSpec((tr, n), lambda i: (i, 0))),
        compiler_params=_params("parallel"),
        name="norm_mix",
    )(x2, g, w_in_t, wup, bal)


def _store_group_norm(acc, gain, width, scale, out_ref):
    for s in range(0, acc.shape[1], width):
        blk = acc[:, s:s + width]
        out_ref[:, s:s + width] = (_rms(blk, gain) * scale).astype(out_ref.dtype)


def _w_tile_specs(d, src_tile):
    hi_per_tile = IN_TILE // GLA_RANK
    return [pl.BlockSpec((IN_TILE, d), lambda *g: (src_tile(*g), 0)),
            pl.BlockSpec((GLA_RANK, d), lambda *g: ((src_tile(*g) + 1) * hi_per_tile, 0))]


def _cast_w_tile(w_lo_ref, w_hi_ref, w_scr, first_step, shifted):
    @pl.when(first_step & jnp.logical_not(shifted))
    def _():
        w_scr[...] = w_lo_ref[...].astype(BF16)

    @pl.when(first_step & shifted)
    def _():
        w_scr[:IN_TILE - GLA_RANK, :] = w_lo_ref[GLA_RANK:, :].astype(BF16)
        w_scr[IN_TILE - GLA_RANK:, :] = w_hi_ref[...].astype(BF16)


def _in_proj_kernel(h_ref, w_lo_ref, w_hi_ref, dq_g_ref, dk_g_ref, mq_g_ref, r0_ref, r1_ref,
                    z_ref, c0_ref, c1_ref, w_scr):
    _cast_gate_rider(r0_ref, c0_ref)
    _cast_riders((r1_ref,), (c1_ref,))
    j = pl.program_id(0)
    _cast_w_tile(w_lo_ref, w_hi_ref, w_scr, pl.program_id(1) == 0, j >= W_FIRST_SHIFTED_TILE)
    j_dq, j_dk, j_mq = Z_DIFF_Q // IN_TILE, Z_DIFF_K // IN_TILE, Z_MEM_Q // IN_TILE

    def tile(epilogue, row_parts):
        part = h_ref.shape[0] // row_parts
        for r in range(row_parts):
            rows = slice(r * part, (r + 1) * part)
            epilogue(_nt_dot(h_ref[rows, :], w_scr[...]), z_ref.at[rows, :])

    def plain(acc, out_ref):
        out_ref[...] = acc.astype(out_ref.dtype)

    @pl.when((j != j_dq) & (j != j_dk) & (j != j_mq))
    def _():
        tile(plain, 1)

    @pl.when((j == j_dq) | (j == j_dk))
    def _():
        gain = jnp.where(j == j_dq, dq_g_ref[...] * DIFF_Q_SCALE, dk_g_ref[...])
        tile(lambda acc, out: _store_group_norm(acc, gain, DIFF_DH, 1.0, out), 2)

    @pl.when(j == j_mq)
    def _():
        tile(lambda acc, out: _store_group_norm(acc, mq_g_ref[...], MEM_DH, MEM_DH ** -0.5, out), 2)


def _in_proj(h, w_in_t, dq_g, dk_g, mq_g, riders, tm=1024):
    m, d = h.shape
    assert w_in_t.shape[0] == W_ROWS
    nj, ni = Z_WIDTH // IN_TILE, m // tm
    assert W_SRC_TILES == tuple(j + (j >= W_DIFF_V_TILE) for j in range(nj))
    r_specs, r_shapes = _rider_specs(riders, nj * ni, lambda j, i: j * ni + i)
    return pl.pallas_call(
        _in_proj_kernel,
        out_shape=(jax.ShapeDtypeStruct((m, Z_WIDTH), BF16), *r_shapes),
        grid=(nj, ni),
        in_specs=[pl.BlockSpec((tm, d), lambda j, i: (i, 0)),
                  *_w_tile_specs(d, lambda j, i: jnp.where(j >= W_DIFF_V_TILE, j + 1, j)),
                  pl.BlockSpec((1, DIFF_DH), lambda j, i: (0, 0)),
                  pl.BlockSpec((1, DIFF_DH), lambda j, i: (0, 0)),
                  pl.BlockSpec((1, MEM_DH), lambda j, i: (0, 0)),
                  *r_specs],
        out_specs=(pl.BlockSpec((tm, IN_TILE), lambda j, i: (i, j)), *r_specs),
        scratch_shapes=[pltpu.VMEM((IN_TILE, d), BF16)],
        compiler_params=_params("arbitrary", "arbitrary"),
        name="in_proj",
    )(h, w_in_t, w_in_t, dq_g, dk_g, mq_g, *riders)


def _v_proj_t_kernel(h_ref, w_lo_ref, w_hi_ref, mq_ref, mk_ref, mv_ref, vt_ref, ym_ref, w_scr, *,
                     sub_rows):
    _cast_w_tile(w_lo_ref, w_hi_ref, w_scr, pl.program_id(0) == 0,
                 W_DIFF_V_TILE >= W_FIRST_SHIFTED_TILE)
    parts = [slice(r, r + sub_rows) for r in range(0, mq_ref.shape[0], sub_rows)]
    heads = [slice(hd * MEM_DH, (hd + 1) * MEM_DH) for hd in range(MEM_HEADS)]
    scores = {(r, hd): _nt_dot(mq_ref[rows, cols], mk_ref[:, cols])
              for r, rows in enumerate(parts) for hd, cols in enumerate(heads)}
    for r, rows in enumerate(parts):
        vt_ref[:, rows] = _nt_dot(w_scr[...], h_ref[rows, :]).astype(vt_ref.dtype)
        for hd, cols in enumerate(heads):
            s = scores[r, hd]
            e = jnp.exp(s - jnp.max(s, axis=-1, keepdims=True))
            p = (e / jnp.sum(e, axis=-1, keepdims=True)).astype(BF16)
            ym_ref[rows, cols] = jnp.dot(p, mv_ref[:, cols],
                                         preferred_element_type=F32).astype(ym_ref.dtype)


def _v_proj_t(h, w_in_t, z, kv, seq, n_mem, tm=1024, sub_rows=512):
    m, d = h.shape
    width = MEM_HEADS * MEM_DH
    tiles_per_batch = seq // tm
    return pl.pallas_call(
        functools.partial(_v_proj_t_kernel, sub_rows=sub_rows),
        out_shape=(jax.ShapeDtypeStruct((IN_TILE, m), BF16), jax.ShapeDtypeStruct((m, width), BF16)),
        grid=(m // tm,),
        in_specs=[pl.BlockSpec((tm, d), lambda i: (i, 0)),
                  *_w_tile_specs(d, lambda i: W_DIFF_V_TILE),
                  pl.BlockSpec((tm, width), lambda i: (i, Z_MEM_Q // width)),
                  pl.BlockSpec((n_mem, width), lambda i: (i // tiles_per_batch, 0)),
                  pl.BlockSpec((n_mem, width), lambda i: (i // tiles_per_batch, 1))],
        out_specs=(pl.BlockSpec((IN_TILE, tm), lambda i: (0, i)),
                   pl.BlockSpec((tm, width), lambda i: (i, 0))),
        scratch_shapes=[pltpu.VMEM((IN_TILE, d), BF16)],
        compiler_params=_params("arbitrary"),
        name="v_proj_t",
    )(h, w_in_t, w_in_t, z, kv, kv)


def _chunk_cumsum(x):
    row_in_chunk = lax.broadcasted_iota(jnp.int32, x.shape, 0) % CHUNK
    shift = 1
    while shift < CHUNK:
        x = x + jnp.where(row_in_chunk >= shift, pltpu.roll(x, shift, 0), 0.0)
        shift *= 2
    return x


def _gla_kernel(q_ref, k_ref, v_ref, g_ref, la_ref, ng_ref, o_ref, s_ref, *, ts):
    @pl.when(pl.program_id(2) == 0)
    def _():
        s_ref[...] = jnp.zeros_like(s_ref)

    bcum_all = _chunk_cumsum(la_ref[...])
    row = lax.broadcasted_iota(jnp.int32, (CHUNK, CHUNK), 0)
    col = lax.broadcasted_iota(jnp.int32, (CHUNK, CHUNK), 1)
    causal = row >= col

    n_chunks = ts // CHUNK
    chunk_rows = [slice(c * CHUNK, (c + 1) * CHUNK) for c in range(n_chunks)]

    lhs, incs, decays = [], [], []
    for rows in chunk_rows:
        bcum = bcum_all[rows]
        b_last = bcum[CHUNK - 1:CHUNK, :]
        q = q_ref[rows, :].astype(F32) * (GLA_DK ** -0.5)
        k = k_ref[rows, :].astype(F32)
        q_dec = (q * jnp.exp(bcum)).astype(BF16)
        k_dec = (k * jnp.exp(-bcum)).astype(BF16)
        k_tail = (k * jnp.exp(b_last - bcum)).astype(BF16)
        att = jnp.where(causal, _nt_dot(q_dec, k_dec), 0.0).astype(BF16)
        lhs.append(jnp.concatenate([q_dec, att], axis=1))
        incs.append(_tn_dot(k_tail, v_ref[rows, :]))
        decay_col = jnp.broadcast_to(jnp.exp(b_last), (GLA_DK, GLA_DK)).T
        decays.append(jnp.concatenate([decay_col] * (GLA_DV // GLA_DK), axis=1))

    state = s_ref[...]
    states = []
    for c in range(n_chunks):
        states.append(state.astype(BF16))
        state = decays[c] * state + incs[c]
    s_ref[...] = state

    for c, rows in enumerate(chunk_rows):
        o = jnp.dot(lhs[c], jnp.concatenate([states[c], v_ref[rows, :]], axis=0),
                    preferred_element_type=F32)
        gate = g_ref[rows, :].astype(F32)
        o_ref[rows, :] = (_rms(o, ng_ref[...]) * (gate * _sigmoid(gate))).astype(o_ref.dtype)


def _gla(z, log_a, ng, batch, seq, ts=2048):
    m = z.shape[0]
    nt = seq // ts
    rows = lambda b, h, t: b * nt + t
    return pl.pallas_call(
        functools.partial(_gla_kernel, ts=ts),
        out_shape=jax.ShapeDtypeStruct((m, GLA_HEADS * GLA_DV), BF16),
        grid=(batch, GLA_HEADS, nt),
        in_specs=[pl.BlockSpec((ts, GLA_DK), lambda b, h, t: (rows(b, h, t), Z_GLA_Q // GLA_DK + h)),
                  pl.BlockSpec((ts, GLA_DK), lambda b, h, t: (rows(b, h, t), Z_GLA_K // GLA_DK + h)),
                  pl.BlockSpec((ts, GLA_DV), lambda b, h, t: (rows(b, h, t), Z_GLA_V // GLA_DV + h)),
                  pl.BlockSpec((ts, GLA_DV), lambda b, h, t: (rows(b, h, t), Z_GLA_G // GLA_DV + h)),
                  pl.BlockSpec((ts, GLA_DK), lambda b, h, t: (rows(b, h, t), h)),
                  pl.BlockSpec((1, GLA_DV), lambda b, h, t: (0, 0))],
        out_specs=pl.BlockSpec((ts, GLA_DV), lambda b, h, t: (rows(b, h, t), h)),
        scratch_shapes=[pltpu.VMEM((GLA_DK, GLA_DV), F32)],
        compiler_params=_params("parallel", "parallel", "arbitrary"),
        name="gla",
    )(z, z, z, z, log_a, ng)


def _diff_kernel(q_ref, k_ref, vt_ref, qg_ref, kg_ref, lq1_ref, lk1_ref, lq2_ref, lk2_ref, sg_ref,
                 o_ref, acc_scr, m_scr, l_scr, *, tq):
    qi = pl.program_id(2)
    m_scr[...] = jnp.full_like(m_scr, NEG_INF)
    l_scr[...] = jnp.zeros_like(l_scr)
    acc_scr[...] = jnp.zeros_like(acc_scr)
    n_streams = q_ref.shape[1] // DIFF_DH
    half = tq // 2

    score_bound = (1.02 * DIFF_DH * DIFF_Q_SCALE) * (jnp.max(jnp.abs(qg_ref[...]))
                                                    * jnp.max(jnp.abs(kg_ref[...])))
    bounded = score_bound <= SCORE_LIMIT

    def update(c, lanes, s, vt, shifted):
        if shifted:
            m_old = m_scr[c, :, lanes]
            m_new = jnp.maximum(m_old, jnp.max(s, axis=0, keepdims=True))
            alpha = jnp.exp2(m_old - m_new)
            p = jnp.exp2(s - m_new)
            l_scr[c, :, lanes] = alpha * l_scr[c, :, lanes] + jnp.sum(p, axis=0, keepdims=True)
            m_scr[c, :, lanes] = m_new
            acc_scr[c, :, lanes] = (alpha * acc_scr[c, :, lanes]
                                    + jnp.dot(vt, p.astype(BF16), preferred_element_type=F32))
        else:
            p = jnp.exp2(s)
            l_scr[c, :, lanes] = l_scr[c, :, lanes] + jnp.sum(p, axis=0, keepdims=True)
            acc_scr[c, :, lanes] = (acc_scr[c, :, lanes]
                                    + jnp.dot(vt, p.astype(BF16), preferred_element_type=F32))

    def values_t(c, start, n):
        head = c // 2
        return vt_ref[head * DIFF_DV:(head + 1) * DIFF_DV, pl.ds(start, n)]

    def full_block(kb, carry, shifted):
        start = pl.multiple_of(kb * tq, tq)
        scores = []
        for c in range(n_streams):
            cols = slice(c * DIFF_DH, (c + 1) * DIFF_DH)
            scores.append(_nt_dot(k_ref[pl.ds(start, tq), cols], q_ref[:, cols]))
        for c in range(n_streams):
            update(c, slice(0, tq), scores[c], values_t(c, start, tq), shifted)
        return carry

    def diag_block(kb, shifted):
        lo = pl.multiple_of(kb * tq, tq)
        hi = pl.multiple_of(kb * tq + half, half)
        key_chunk = lax.broadcasted_iota(jnp.int32, (half, half), 0) // CHUNK
        query_chunk = lax.broadcasted_iota(jnp.int32, (half, half), 1) // CHUNK
        visible = key_chunk <= query_chunk
        def stream_scores(c):
            cols = slice(c * DIFF_DH, (c + 1) * DIFF_DH)
            k_lo, k_hi = k_ref[pl.ds(lo, half), cols], k_ref[pl.ds(hi, half), cols]
            q_lo, q_hi = q_ref[:half, cols], q_ref[half:, cols]
            s_lo = jnp.where(visible, _nt_dot(k_lo, q_lo), NEG_INF)
            s_hi = jnp.concatenate([_nt_dot(k_lo, q_hi),
                                    jnp.where(visible, _nt_dot(k_hi, q_hi), NEG_INF)], axis=0)
            return s_lo, s_hi

        scores = [stream_scores(c) for c in range(SCORES_AHEAD)]
        lam = (jnp.exp(jnp.sum(lq1_ref[...] * lk1_ref[...], axis=-1, keepdims=True))
               - jnp.exp(jnp.sum(lq2_ref[...] * lk2_ref[...], axis=-1, keepdims=True)) + LAM_INIT)
        for head in range(n_streams // 2):
            c1, c2 = 2 * head, 2 * head + 1
            for c in (c1, c2):
                if c + SCORES_AHEAD < n_streams:
                    scores.append(stream_scores(c + SCORES_AHEAD))
                update(c, slice(0, half), scores[c][0], values_t(c, lo, half), shifted)
                update(c, slice(half, tq), scores[c][1], values_t(c, lo, tq), shifted)
            o_t = (acc_scr[c1] * (1.0 / l_scr[c1])
                   - acc_scr[c2] * (lam / l_scr[c2]))
            o_ref[:, head * DIFF_DV:(head + 1) * DIFF_DV] = (
                _rms(o_t.T, sg_ref[...]) * (1.0 - LAM_INIT)).astype(o_ref.dtype)

    def all_blocks(shifted):
        lax.fori_loop(0, qi, functools.partial(full_block, shifted=shifted), 0)
        diag_block(qi, shifted)

    @pl.when(bounded)
    def _():
        all_blocks(shifted=False)

    @pl.when(jnp.logical_not(bounded))
    def _():
        all_blocks(shifted=True)


def _diff_attn(z, v_t, qg, kg, lq1, lk1, lq2, lk2, sg, batch, seq, tq=512, heads_per_step=4):
    m = z.shape[0]
    nq = seq // tq
    width = heads_per_step * DIFF_DV
    ns = 2 * heads_per_step
    vec = pl.BlockSpec((1, DIFF_DH), lambda b, g, i: (0, 0))
    return pl.pallas_call(
        functools.partial(_diff_kernel, tq=tq),
        out_shape=jax.ShapeDtypeStruct((m, DIFF_HEADS * DIFF_DV), BF16),
        grid=(batch, DIFF_HEADS // heads_per_step, nq),
        in_specs=[pl.BlockSpec((tq, width), lambda b, g, i: (b * nq + i, Z_DIFF_Q // width + g)),
                  pl.BlockSpec((seq, width), lambda b, g, i: (b, Z_DIFF_K // width + g)),
                  pl.BlockSpec((width, seq), lambda b, g, i: (g, b)),
                  vec, vec, vec, vec, vec, vec,
                  pl.BlockSpec((1, DIFF_DV), lambda b, g, i: (0, 0))],
        out_specs=pl.BlockSpec((tq, width), lambda b, g, i: (b * nq + i, g)),
        scratch_shapes=[pltpu.VMEM((ns, DIFF_DV, tq), F32),
                        pltpu.VMEM((ns, 1, tq), F32),
                        pltpu.VMEM((ns, 1, tq), F32)],
        compiler_params=_params("parallel", "parallel", "arbitrary"),
        name="diff_attn",
    )(z, z, v_t, qg, kg, lq1, lk1, lq2, lk2, sg)


def _mem_kv_kernel(mem_ref, ng_ref, w_ref, kg_ref, kv_ref, mn_scr, *, n_key_tiles):
    j = pl.program_id(0)

    @pl.when(j == 0)
    def _():
        mn_scr[...] = _rms(mem_ref[...], ng_ref[...]).astype(BF16)

    acc = jnp.dot(mn_scr[...], w_ref[...].astype(BF16), preferred_element_type=F32)

    @pl.when(j < n_key_tiles)
    def _():
        _store_group_norm(acc, kg_ref[...], MEM_DH, 1.0, kv_ref)

    @pl.when(j >= n_key_tiles)
    def _():
        kv_ref[...] = acc.astype(kv_ref.dtype)


def _mem_kv(mem2, ng, w_kv, kg, tn=512):
    m, d = mem2.shape
    n = w_kv.shape[1]
    return pl.pallas_call(
        functools.partial(_mem_kv_kernel, n_key_tiles=(n // 2) // tn),
        out_shape=jax.ShapeDtypeStruct((m, n), BF16),
        grid=(n // tn,),
        in_specs=[pl.BlockSpec((m, d), lambda j: (0, 0)),
                  pl.BlockSpec((1, d), lambda j: (0, 0)),
                  pl.BlockSpec((d, tn), lambda j: (0, j)),
                  pl.BlockSpec((1, MEM_DH), lambda j: (0, 0))],
        out_specs=pl.BlockSpec((m, tn), lambda j: (0, j)),
        scratch_shapes=[pltpu.VMEM((m, d), BF16)],
        compiler_params=_params("arbitrary"),
        name="mem_kv",
    )(mem2, ng, w_kv, kg)


def _gate_merge_kernel(h_ref, y0_ref, y1_ref, y2_ref, wg_ref, bg_ref, wb0_ref, wb1_ref, wb2_ref,
                       r0_ref, r1_ref, o_ref, c0_ref, c1_ref):
    _cast_riders((r0_ref, r1_ref), (c0_ref, c1_ref))
    gates = _sigmoid(jnp.dot(h_ref[...], wg_ref[...], preferred_element_type=F32) + bg_ref[...])
    merged = None
    for b, (y_ref, wb_ref) in enumerate(((y0_ref, wb0_ref), (y1_ref, wb1_ref), (y2_ref, wb2_ref))):
        term = (gates[:, b * GATE_TILE:(b + 1) * GATE_TILE]
                * jnp.dot(y_ref[...], wb_ref[...], preferred_element_type=F32))
        merged = term if merged is None else merged + term
    o_ref[...] = merged.astype(o_ref.dtype)


def _gate_merge(h, ys, w_gate_tiled, b_gate_tiled, w_branch, riders, tm=1024):
    m, d = h.shape
    bw = w_branch.shape[1]
    tn = GATE_TILE
    ni, nj = m // tm, d // tn
    act = lambda width: pl.BlockSpec((tm, width), lambda i, j: (i, 0))
    wb = lambda b: pl.BlockSpec((None, bw, tn), lambda i, j: (b, 0, j))
    r_specs, r_shapes = _rider_specs(riders, ni * nj, lambda i, j: i * nj + j)
    return pl.pallas_call(
        _gate_merge_kernel,
        out_shape=(jax.ShapeDtypeStruct((m, d), BF16), *r_shapes),
        grid=(ni, nj),
        in_specs=[act(d), act(bw), act(bw), act(bw),
                  pl.BlockSpec((d, N_BRANCH * tn), lambda i, j: (0, j)),
                  pl.BlockSpec((1, N_BRANCH * tn), lambda i, j: (0, j)),
                  wb(0), wb(1), wb(2), *r_specs],
        out_specs=(pl.BlockSpec((tm, tn), lambda i, j: (i, j)), *r_specs),
        compiler_params=_params("arbitrary", "arbitrary"),
        name="gate_merge",
    )(h, ys[0], ys[1], ys[2], w_gate_tiled, b_gate_tiled, w_branch, w_branch, w_branch, *riders)


def _out_proj_kernel(mg_ref, w_ref, x_ref, g_ref, x1_ref, hf_ref):
    x1 = x_ref[...] + jnp.dot(mg_ref[...], w_ref[...], preferred_element_type=F32)
    x1_ref[...] = x1
    hf_ref[...] = _rms(x1, g_ref[...]).astype(hf_ref.dtype)


def _out_proj(merged, w_out, x2, g, tm=512):
    m, d = x2.shape
    row = lambda i: (i, 0)
    fixed = lambda i: (0, 0)
    return pl.pallas_call(
        _out_proj_kernel,
        out_shape=(jax.ShapeDtypeStruct((m, d), F32), jax.ShapeDtypeStruct((m, d), BF16)),
        grid=(m // tm,),
        in_specs=[pl.BlockSpec((tm, d), row),
                  pl.BlockSpec((d, d), fixed, pipeline_mode=pl.Buffered(1)),
                  pl.BlockSpec((tm, d), row), pl.BlockSpec((1, d), fixed)],
        out_specs=(pl.BlockSpec((tm, d), row), pl.BlockSpec((tm, d), row)),
        compiler_params=_params("parallel"),
        name="out_proj",
    )(merged, w_out, x2, g)


def _ffn_up_kernel(hf_ref, wg_ref, wu_ref, r0_ref, a_ref, c0_ref):
    _cast_riders((r0_ref,), (c0_ref,))
    hf = hf_ref[...]
    gate = jnp.dot(hf, wg_ref[...], preferred_element_type=F32)
    up = jnp.dot(hf, wu_ref[...], preferred_element_type=F32)
    a_ref[...] = (gate * _sigmoid(gate) * up).astype(a_ref.dtype)


def _ffn_up(hf, w_in, riders, tm=1024, tf=512):
    m, d = hf.shape
    d_ff = w_in.shape[1] // 2
    ni, nj = m // tm, d_ff // tf
    r_specs, r_shapes = _rider_specs(riders, ni * nj, lambda i, j: i * nj + j)
    return pl.pallas_call(
        _ffn_up_kernel,
        out_shape=(jax.ShapeDtypeStruct((m, d_ff), BF16), *r_shapes),
        grid=(ni, nj),
        in_specs=[pl.BlockSpec((tm, d), lambda i, j: (i, 0)),
                  pl.BlockSpec((d, tf), lambda i, j: (0, j)),
                  pl.BlockSpec((d, tf), lambda i, j: (0, nj + j)),
                  *r_specs],
        out_specs=(pl.BlockSpec((tm, tf), lambda i, j: (i, j)), *r_specs),
        compiler_params=_params("arbitrary", "arbitrary"),
        name="ffn_up",
    )(hf, w_in, w_in, *riders)


def _ffn_down_kernel(a_ref, w_ref, x1_ref, o_ref):
    o_ref[...] = x1_ref[...] + jnp.dot(a_ref[...], w_ref[...], preferred_element_type=F32)


def _ffn_down(a, w_down, x1, tm=1024, tn=512):
    m, d_ff = a.shape
    d = w_down.shape[1]
    return pl.pallas_call(
        _ffn_down_kernel,
        out_shape=jax.ShapeDtypeStruct((m, d), F32),
        grid=(m // tm, d // tn),
        in_specs=[pl.BlockSpec((tm, d_ff), lambda i, j: (i, 0)),
                  pl.BlockSpec((d_ff, tn), lambda i, j: (0, j)),
                  pl.BlockSpec((tm, tn), lambda i, j: (i, j))],
        out_specs=pl.BlockSpec((tm, tn), lambda i, j: (i, j)),
        compiler_params=_params("parallel", "arbitrary"),
        name="ffn_down",
    )(a, w_down, x1)


def kernel(x, mem, norm_mix_g, norm_mem_g, w_in, gla_w_alpha_up, gla_b_alpha, gla_norm_g,
           diff_q_norm_g, diff_k_norm_g, diff_lambda_q1, diff_lambda_k1, diff_lambda_q2,
           diff_lambda_k2, diff_subln_g, mem_q_norm_g, mem_k_norm_g, w_mem_kv, w_branch,
           w_gate, b_gate, w_out, norm_ffn_g, w_ffn_in, w_ffn_down):
    batch, seq, d = x.shape
    n_mem = mem.shape[1]
    depth = w_in.shape[0]
    assert depth == 1, "LAM_INIT is the layer-0 value"
    x2 = x.reshape(batch * seq, d)
    mem2 = mem.reshape(batch * n_mem, d)
    for l in range(depth):
        w_in_t = w_in[l].T
        w_up = jnp.pad(gla_w_alpha_up[l], ((0, LANES - GLA_RANK), (0, 0)))
        row = lambda v: v.reshape(1, -1)

        h, log_a = _norm_mix(x2, row(norm_mix_g[l]), w_in_t, w_up, row(gla_b_alpha[l]))
        z, w_gate_bf, w_branch_bf = _in_proj(
            h, w_in_t, row(diff_q_norm_g[l]), row(diff_k_norm_g[l]), row(mem_q_norm_g[l]),
            riders=(w_gate[l], w_branch[l].reshape(-1, d)))
        y_gla = _gla(z, log_a, row(gla_norm_g[l]), batch, seq)
        kv = _mem_kv(mem2, row(norm_mem_g[l]), w_mem_kv[l], row(mem_k_norm_g[l]))
        v_t, y_mem = _v_proj_t(h, w_in_t, z, kv, seq, n_mem)
        y_diff = _diff_attn(z, v_t, row(diff_q_norm_g[l]), row(diff_k_norm_g[l]),
                            row(diff_lambda_q1[l]), row(diff_lambda_k1[l]),
                            row(diff_lambda_q2[l]), row(diff_lambda_k2[l]),
                            row(diff_subln_g[l]), batch, seq)
        b_gate_tiled = jnp.concatenate(
            [b_gate[l][src:src + GATE_TILE] for _, src in _tile_major_gate_cols(b_gate.shape[1])])
        merged, w_out_bf, w_ffn_in_bf = _gate_merge(
            h, (y_gla, y_diff, y_mem), w_gate_bf, row(b_gate_tiled),
            w_branch_bf.reshape(w_branch[l].shape), riders=(w_out[l], w_ffn_in[l]))
        x1, hf = _out_proj(merged, w_out_bf, x2, row(norm_ffn_g[l]))
        a, w_ffn_down_bf = _ffn_up(hf, w_ffn_in_bf, riders=(w_ffn_down[l],))
        x2 = _ffn_down(a, w_ffn_down_bf, x1)
    return x2.reshape(batch, seq, d)
```

```python
import functools

import jax
import jax.numpy as jnp
from jax import lax
from jax.experimental import pallas as pl
from jax.experimental.pallas import tpu as pltpu

F32 = jnp.float32
BF16 = jnp.bfloat16

CHUNK = 64
GLA_HEADS = 4
GLA_DK = 128
GLA_DV = 256
GLA_RANK = 16
GLA_GATE_NORM = 16.0
DIFF_HEADS = 4
DIFF_DH = 128
DIFF_DV = 256
MEM_HEADS = 4
MEM_DH = 256
N_BRANCH = 3
NORM_EPS = 1e-6
NEG_INF = -1e30
LAM_INIT = 0.8 - 0.6 * 1.0
LOG2_E = 1.4426950408889634
SCORE_LIMIT = 40.0
SCORES_AHEAD = 2
DIFF_Q_SCALE = DIFF_DH ** -0.5 * LOG2_E

LANES = 128
VMEM_LIMIT = 56 * 1024 * 1024

IN_TILE = 1024
Z_GLA_Q, Z_GLA_K, Z_GLA_V, Z_GLA_G = 0, 512, 1024, 2048
Z_DIFF_Q, Z_DIFF_K, Z_MEM_Q = 3072, 4096, 5120
Z_WIDTH = 6144
W_SRC_TILES = (0, 1, 2, 3, 4, 6)
W_DIFF_V_TILE = 5
W_ROWS = 7 * IN_TILE + GLA_RANK
W_FIRST_SHIFTED_TILE = 3


def _params(*sem):
    return pltpu.CompilerParams(dimension_semantics=sem, vmem_limit_bytes=VMEM_LIMIT)


def _nt_dot(a, b):
    return lax.dot_general(a, b, (((1,), (1,)), ((), ())), preferred_element_type=F32)


def _tn_dot(a, b, precision=None):
    return lax.dot_general(a, b, (((0,), (0,)), ((), ())), preferred_element_type=F32,
                           precision=precision)


def _rms(v, gain):
    ms = jnp.mean(v * v, axis=-1, keepdims=True)
    return v * lax.rsqrt(ms + NORM_EPS) * gain


def _sigmoid(v):
    return 1.0 / (1.0 + jnp.exp(-v))


BF16_SUBLANES = 16


def _rider_specs(weights, n_steps, step_of):
    specs, shapes = [], []
    for w in weights:
        rows, cols = w.shape
        chunk = BF16_SUBLANES
        while rows % chunk or rows // chunk > n_steps:
            chunk += BF16_SUBLANES
        last = rows // chunk - 1
        specs.append(pl.BlockSpec((chunk, cols),
                                  lambda *g, last=last: (jnp.minimum(step_of(*g), last), 0)))
        shapes.append(jax.ShapeDtypeStruct((rows, cols), BF16))
    return specs, shapes


def _cast_riders(in_refs, out_refs):
    for src, dst in zip(in_refs, out_refs):
        dst[...] = src[...].astype(BF16)


GATE_TILE = 256


def _tile_major_gate_cols(n_cols):
    per_branch = n_cols // N_BRANCH
    return [((j * N_BRANCH + b) * GATE_TILE, b * per_branch + j * GATE_TILE)
            for j in range(per_branch // GATE_TILE) for b in range(N_BRANCH)]


def _cast_gate_rider(src, dst):
    for dst_col, src_col in _tile_major_gate_cols(src.shape[1]):
        dst[:, dst_col:dst_col + GATE_TILE] = src[:, src_col:src_col + GATE_TILE].astype(BF16)


def _norm_mix_kernel(x_ref, g_ref, wa_ref, wup_ref, bal_ref, h_ref, la_ref):
    h = _rms(x_ref[...], g_ref[...]).astype(BF16)
    h_ref[...] = h
    a_low = _nt_dot(h, wa_ref[...].astype(BF16))
    w_up = wup_ref[...]
    a_hi, w_hi = a_low.astype(BF16), w_up.astype(BF16)
    a_lo = (a_low - a_hi.astype(F32)).astype(BF16)
    w_lo = (w_up - w_hi.astype(F32)).astype(BF16)
    pre = (jnp.dot(a_hi, w_hi, preferred_element_type=F32)
           + (jnp.dot(a_lo, w_hi, preferred_element_type=F32)
              + jnp.dot(a_hi, w_lo, preferred_element_type=F32))) + bal_ref[...]
    log_sig = jnp.minimum(pre, 0.0) - jnp.log1p(jnp.exp(-jnp.abs(pre)))
    la_ref[...] = log_sig * (1.0 / GLA_GATE_NORM)


def _norm_mix(x2, g, w_in_t, wup, bal, tr=1024):
    m, d = x2.shape
    n = wup.shape[1]
    fixed = lambda i: (0, 0)
    decay_block = (Z_GLA_G + GLA_HEADS * GLA_DV) // LANES
    return pl.pallas_call(
        _norm_mix_kernel,
        out_shape=(jax.ShapeDtypeStruct((m, d), BF16), jax.ShapeDtypeStruct((m, n), F32)),
        grid=(m // tr,),
        in_specs=[pl.BlockSpec((tr, d), lambda i: (i, 0)),
                  pl.BlockSpec((1, d), fixed),
                  pl.BlockSpec((LANES, d), lambda i: (decay_block, 0)),
                  pl.BlockSpec((LANES, n), fixed),
                  pl.BlockSpec((1, n), fixed)],
        out_specs=(pl.BlockSpec((tr, d), lambda i: (i, 0)),
                   pl.BlockSpec((tr, n), lambda i: (i, 0))),
        compiler_params=_params("parallel"),
        name="norm_mix",
    )(x2, g, w_in_t, wup, bal)


def _store_group_norm(acc, gain, width, scale, out_ref):
    for s in range(0, acc.shape[1], width):
        blk = acc[:, s:s + width]
        out_ref[:, s:s + width] = (_rms(blk, gain) * scale).astype(out_ref.dtype)


def _w_tile_specs(d, src_tile):
    hi_per_tile = IN_TILE // GLA_RANK
    return [pl.BlockSpec((IN_TILE, d), lambda *g: (src_tile(*g), 0)),
            pl.BlockSpec((GLA_RANK, d), lambda *g: ((src_tile(*g) + 1) * hi_per_tile, 0))]


def _cast_w_tile(w_lo_ref, w_hi_ref, w_scr, first_step, shifted):
    @pl.when(first_step & jnp.logical_not(shifted))
    def _():
        w_scr[...] = w_lo_ref[...].astype(BF16)

    @pl.when(first_step & shifted)
    def _():
        w_scr[:IN_TILE - GLA_RANK, :] = w_lo_ref[GLA_RANK:, :].astype(BF16)
        w_scr[IN_TILE - GLA_RANK:, :] = w_hi_ref[...].astype(BF16)


def _in_proj_kernel(h_ref, w_lo_ref, w_hi_ref, dq_g_ref, dk_g_ref, mq_g_ref, r0_ref, r1_ref,
                    z_ref, c0_ref, c1_ref, w_scr):
    _cast_gate_rider(r0_ref, c0_ref)
    _cast_riders((r1_ref,), (c1_ref,))
    j = pl.program_id(0)
    _cast_w_tile(w_lo_ref, w_hi_ref, w_scr, pl.program_id(1) == 0, j >= W_FIRST_SHIFTED_TILE)
    j_dq, j_dk, j_mq = Z_DIFF_Q // IN_TILE, Z_DIFF_K // IN_TILE, Z_MEM_Q // IN_TILE

    def tile(epilogue, row_parts):
        part = h_ref.shape[0] // row_parts
        for r in range(row_parts):
            rows = slice(r * part, (r + 1) * part)
            epilogue(_nt_dot(h_ref[rows, :], w_scr[...]), z_ref.at[rows, :])

    def plain(acc, out_ref):
        out_ref[...] = acc.astype(out_ref.dtype)

    @pl.when((j != j_dq) & (j != j_dk) & (j != j_mq))
    def _():
        tile(plain, 1)

    @pl.when((j == j_dq) | (j == j_dk))
    def _():
        gain = jnp.where(j == j_dq, dq_g_ref[...] * DIFF_Q_SCALE, dk_g_ref[...])
        tile(lambda acc, out: _store_group_norm(acc, gain, DIFF_DH, 1.0, out), 2)

    @pl.when(j == j_mq)
    def _():
        tile(lambda acc, out: _store_group_norm(acc, mq_g_ref[...], MEM_DH, MEM_DH ** -0.5, out), 2)


def _in_proj(h, w_in_t, dq_g, dk_g, mq_g, riders, tm=1024):
    m, d = h.shape
    assert w_in_t.shape[0] == W_ROWS
    nj, ni = Z_WIDTH // IN_TILE, m // tm
    assert W_SRC_TILES == tuple(j + (j >= W_DIFF_V_TILE) for j in range(nj))
    r_specs, r_shapes = _rider_specs(riders, nj * ni, lambda j, i: j * ni + i)
    return pl.pallas_call(
        _in_proj_kernel,
        out_shape=(jax.ShapeDtypeStruct((m, Z_WIDTH), BF16), *r_shapes),
        grid=(nj, ni),
        in_specs=[pl.BlockSpec((tm, d), lambda j, i: (i, 0)),
                  *_w_tile_specs(d, lambda j, i: jnp.where(j >= W_DIFF_V_TILE, j + 1, j)),
                  pl.BlockSpec((1, DIFF_DH), lambda j, i: (0, 0)),
                  pl.BlockSpec((1, DIFF_DH), lambda j, i: (0, 0)),
                  pl.BlockSpec((1, MEM_DH), lambda j, i: (0, 0)),
                  *r_specs],
        out_specs=(pl.BlockSpec((tm, IN_TILE), lambda j, i: (i, j)), *r_specs),
        scratch_shapes=[pltpu.VMEM((IN_TILE, d), BF16)],
        compiler_params=_params("arbitrary", "arbitrary"),
        name="in_proj",
    )(h, w_in_t, w_in_t, dq_g, dk_g, mq_g, *riders)


def _v_proj_t_kernel(h_ref, w_lo_ref, w_hi_ref, mq_ref, mk_ref, mv_ref, vt_ref, ym_ref, w_scr, *,
                     sub_rows):
    _cast_w_tile(w_lo_ref, w_hi_ref, w_scr, pl.program_id(0) == 0,
                 W_DIFF_V_TILE >= W_FIRST_SHIFTED_TILE)
    parts = [slice(r, r + sub_rows) for r in range(0, mq_ref.shape[0], sub_rows)]
    heads = [slice(hd * MEM_DH, (hd + 1) * MEM_DH) for hd in range(MEM_HEADS)]
    scores = {(r, hd): _nt_dot(mq_ref[rows, cols], mk_ref[:, cols])
              for r, rows in enumerate(parts) for hd, cols in enumerate(heads)}
    for r, rows in enumerate(parts):
        vt_ref[:, rows] = _nt_dot(w_scr[...], h_ref[rows, :]).astype(vt_ref.dtype)
        for hd, cols in enumerate(heads):
            s = scores[r, hd]
            e = jnp.exp(s - jnp.max(s, axis=-1, keepdims=True))
            p = (e / jnp.sum(e, axis=-1, keepdims=True)).astype(BF16)
            ym_ref[rows, cols] = jnp.dot(p, mv_ref[:, cols],
                                         preferred_element_type=F32).astype(ym_ref.dtype)


def _v_proj_t(h, w_in_t, z, kv, seq, n_mem, tm=1024, sub_rows=512):
    m, d = h.shape
    width = MEM_HEADS * MEM_DH
    tiles_per_batch = seq // tm
    return pl.pallas_call(
        functools.partial(_v_proj_t_kernel, sub_rows=sub_rows),
        out_shape=(jax.ShapeDtypeStruct((IN_TILE, m), BF16), jax.ShapeDtypeStruct((m, width), BF16)),
        grid=(m // tm,),
        in_specs=[pl.BlockSpec((tm, d), lambda i: (i, 0)),
                  *_w_tile_specs(d, lambda i: W_DIFF_V_TILE),
                  pl.BlockSpec((tm, width), lambda i: (i, Z_MEM_Q // width)),
                  pl.BlockSpec((n_mem, width), lambda i: (i // tiles_per_batch, 0)),
                  pl.BlockSpec((n_mem, width), lambda i: (i // tiles_per_batch, 1))],
        out_specs=(pl.BlockSpec((IN_TILE, tm), lambda i: (0, i)),
                   pl.BlockSpec((tm, width), lambda i: (i, 0))),
        scratch_shapes=[pltpu.VMEM((IN_TILE, d), BF16)],
        compiler_params=_params("arbitrary"),
        name="v_proj_t",
    )(h, w_in_t, w_in_t, z, kv, kv)


def _chunk_cumsum(x):
    row_in_chunk = lax.broadcasted_iota(jnp.int32, x.shape, 0) % CHUNK
    shift = 1
    while shift < CHUNK:
        x = x + jnp.where(row_in_chunk >= shift, pltpu.roll(x, shift, 0), 0.0)
        shift *= 2
    return x


def _gla_kernel(q_ref, k_ref, v_ref, g_ref, la_ref, ng_ref, o_ref, s_ref, *, ts):
    @pl.when(pl.program_id(2) == 0)
    def _():
        s_ref[...] = jnp.zeros_like(s_ref)

    bcum_all = _chunk_cumsum(la_ref[...])
    row = lax.broadcasted_iota(jnp.int32, (CHUNK, CHUNK), 0)
    col = lax.broadcasted_iota(jnp.int32, (CHUNK, CHUNK), 1)
    causal = row >= col

    n_chunks = ts // CHUNK
    chunk_rows = [slice(c * CHUNK, (c + 1) * CHUNK) for c in range(n_chunks)]

    lhs, incs, decays = [], [], []
    for rows in chunk_rows:
        bcum = bcum_all[rows]
        b_last = bcum[CHUNK - 1:CHUNK, :]
        q = q_ref[rows, :].astype(F32) * (GLA_DK ** -0.5)
        k = k_ref[rows, :].astype(F32)
        q_dec = (q * jnp.exp(bcum)).astype(BF16)
        k_dec = (k * jnp.exp(-bcum)).astype(BF16)
        k_tail = (k * jnp.exp(b_last - bcum)).astype(BF16)
        att = jnp.where(causal, _nt_dot(q_dec, k_dec), 0.0).astype(BF16)
        lhs.append(jnp.concatenate([q_dec, att], axis=1))
        incs.append(_tn_dot(k_tail, v_ref[rows, :]))
        decay_col = jnp.broadcast_to(jnp.exp(b_last), (GLA_DK, GLA_DK)).T
        decays.append(jnp.concatenate([decay_col] * (GLA_DV // GLA_DK), axis=1))

    state = s_ref[...]
    states = []
    for c in range(n_chunks):
        states.append(state.astype(BF16))
        state = decays[c] * state + incs[c]
    s_ref[...] = state

    for c, rows in enumerate(chunk_rows):
        o = jnp.dot(lhs[c], jnp.concatenate([states[c], v_ref[rows, :]], axis=0),
                    preferred_element_type=F32)
        gate = g_ref[rows, :].astype(F32)
        o_ref[rows, :] = (_rms(o, ng_ref[...]) * (gate * _sigmoid(gate))).astype(o_ref.dtype)


def _gla(z, log_a, ng, batch, seq, ts=2048):
    m = z.shape[0]
    nt = seq // ts
    rows = lambda b, h, t: b * nt + t
    return pl.pallas_call(
        functools.partial(_gla_kernel, ts=ts),
        out_shape=jax.ShapeDtypeStruct((m, GLA_HEADS * GLA_DV), BF16),
        grid=(batch, GLA_HEADS, nt),
        in_specs=[pl.BlockSpec((ts, GLA_DK), lambda b, h, t: (rows(b, h, t), Z_GLA_Q // GLA_DK + h)),
                  pl.BlockSpec((ts, GLA_DK), lambda b, h, t: (rows(b, h, t), Z_GLA_K // GLA_DK + h)),
                  pl.BlockSpec((ts, GLA_DV), lambda b, h, t: (rows(b, h, t), Z_GLA_V // GLA_DV + h)),
                  pl.BlockSpec((ts, GLA_DV), lambda b, h, t: (rows(b, h, t), Z_GLA_G // GLA_DV + h)),
                  pl.BlockSpec((ts, GLA_DK), lambda b, h, t: (rows(b, h, t), h)),
                  pl.BlockSpec((1, GLA_DV), lambda b, h, t: (0, 0))],
        out_specs=pl.BlockSpec((ts, GLA_DV), lambda b, h, t: (rows(b, h, t), h)),
        scratch_shapes=[pltpu.VMEM((GLA_DK, GLA_DV), F32)],
        compiler_params=_params("parallel", "parallel", "arbitrary"),
        name="gla",
    )(z, z, z, z, log_a, ng)


def _diff_kernel(q_ref, k_ref, vt_ref, qg_ref, kg_ref, lq1_ref, lk1_ref, lq2_ref, lk2_ref, sg_ref,
                 o_ref, acc_scr, m_scr, l_scr, *, tq):
    qi = pl.program_id(2)
    m_scr[...] = jnp.full_like(m_scr, NEG_INF)
    l_scr[...] = jnp.zeros_like(l_scr)
    acc_scr[...] = jnp.zeros_like(acc_scr)
    n_streams = q_ref.shape[1] // DIFF_DH
    half = tq // 2

    score_bound = (1.02 * DIFF_DH * DIFF_Q_SCALE) * (jnp.max(jnp.abs(qg_ref[...]))
                                                    * jnp.max(jnp.abs(kg_ref[...])))
    bounded = score_bound <= SCORE_LIMIT

    def update(c, lanes, s, vt, shifted):
        if shifted:
            m_old = m_scr[c, :, lanes]
            m_new = jnp.maximum(m_old, jnp.max(s, axis=0, keepdims=True))
            alpha = jnp.exp2(m_old - m_new)
            p = jnp.exp2(s - m_new)
            l_scr[c, :, lanes] = alpha * l_scr[c, :, lanes] + jnp.sum(p, axis=0, keepdims=True)
            m_scr[c, :, lanes] = m_new
            acc_scr[c, :, lanes] = (alpha * acc_scr[c, :, lanes]
                                    + jnp.dot(vt, p.astype(BF16), preferred_element_type=F32))
        else:
            p = jnp.exp2(s)
            l_scr[c, :, lanes] = l_scr[c, :, lanes] + jnp.sum(p, axis=0, keepdims=True)
            acc_scr[c, :, lanes] = (acc_scr[c, :, lanes]
                                    + jnp.dot(vt, p.astype(BF16), preferred_element_type=F32))

    def values_t(c, start, n):
        head = c // 2
        return vt_ref[head * DIFF_DV:(head + 1) * DIFF_DV, pl.ds(start, n)]

    def full_block(kb, carry, shifted):
        start = pl.multiple_of(kb * tq, tq)
        scores = []
        for c in range(n_streams):
            cols = slice(c * DIFF_DH, (c + 1) * DIFF_DH)
            scores.append(_nt_dot(k_ref[pl.ds(start, tq), cols], q_ref[:, cols]))
        for c in range(n_streams):
            update(c, slice(0, tq), scores[c], values_t(c, start, tq), shifted)
        return carry

    def diag_block(kb, shifted):
        lo = pl.multiple_of(kb * tq, tq)
        hi = pl.multiple_of(kb * tq + half, half)
        key_chunk = lax.broadcasted_iota(jnp.int32, (half, half), 0) // CHUNK
        query_chunk = lax.broadcasted_iota(jnp.int32, (half, half), 1) // CHUNK
        visible = key_chunk <= query_chunk
        def stream_scores(c):
            cols = slice(c * DIFF_DH, (c + 1) * DIFF_DH)
            k_lo, k_hi = k_ref[pl.ds(lo, half), cols], k_ref[pl.ds(hi, half), cols]
            q_lo, q_hi = q_ref[:half, cols], q_ref[half:, cols]
            s_lo = jnp.where(visible, _nt_dot(k_lo, q_lo), NEG_INF)
            s_hi = jnp.concatenate([_nt_dot(k_lo, q_hi),
                                    jnp.where(visible, _nt_dot(k_hi, q_hi), NEG_INF)], axis=0)
            return s_lo, s_hi

        scores = [stream_scores(c) for c in range(SCORES_AHEAD)]
        lam = (jnp.exp(jnp.sum(lq1_ref[...] * lk1_ref[...], axis=-1, keepdims=True))
               - jnp.exp(jnp.sum(lq2_ref[...] * lk2_ref[...], axis=-1, keepdims=True)) + LAM_INIT)
        for head in range(n_streams // 2):
            c1, c2 = 2 * head, 2 * head + 1
            for c in (c1, c2):
                if c + SCORES_AHEAD < n_streams:
                    scores.append(stream_scores(c + SCORES_AHEAD))
                update(c, slice(0, half), scores[c][0], values_t(c, lo, half), shifted)
                update(c, slice(half, tq), scores[c][1], values_t(c, lo, tq), shifted)
            o_t = (acc_scr[c1] * (1.0 / l_scr[c1])
                   - acc_scr[c2] * (lam / l_scr[c2]))
            o_ref[:, head * DIFF_DV:(head + 1) * DIFF_DV] = (
                _rms(o_t.T, sg_ref[...]) * (1.0 - LAM_INIT)).astype(o_ref.dtype)

    def all_blocks(shifted):
        lax.fori_loop(0, qi, functools.partial(full_block, shifted=shifted), 0)
        diag_block(qi, shifted)

    @pl.when(bounded)
    def _():
        all_blocks(shifted=False)

    @pl.when(jnp.logical_not(bounded))
    def _():
        all_blocks(shifted=True)


def _diff_attn(z, v_t, qg, kg, lq1, lk1, lq2, lk2, sg, batch, seq, tq=512, heads_per_step=4):
    m = z.shape[0]
    nq = seq // tq
    width = heads_per_step * DIFF_DV
    ns = 2 * heads_per_step
    vec = pl.BlockSpec((1, DIFF_DH), lambda b, g, i: (0, 0))
    return pl.pallas_call(
        functools.partial(_diff_kernel, tq=tq),
        out_shape=jax.ShapeDtypeStruct((m, DIFF_HEADS * DIFF_DV), BF16),
        grid=(batch, DIFF_HEADS // heads_per_step, nq),
        in_specs=[pl.BlockSpec((tq, width), lambda b, g, i: (b * nq + i, Z_DIFF_Q // width + g)),
                  pl.BlockSpec((seq, width), lambda b, g, i: (b, Z_DIFF_K // width + g)),
                  pl.BlockSpec((width, seq), lambda b, g, i: (g, b)),
                  vec, vec, vec, vec, vec, vec,
                  pl.BlockSpec((1, DIFF_DV), lambda b, g, i: (0, 0))],
        out_specs=pl.BlockSpec((tq, width), lambda b, g, i: (b * nq + i, g)),
        scratch_shapes=[pltpu.VMEM((ns, DIFF_DV, tq), F32),
                        pltpu.VMEM((ns, 1, tq), F32),
                        pltpu.VMEM((ns, 1, tq), F32)],
        compiler_params=_params("parallel", "parallel", "arbitrary"),
        name="diff_attn",
    )(z, z, v_t, qg, kg, lq1, lk1, lq2, lk2, sg)


def _mem_kv_kernel(mem_ref, ng_ref, w_ref, kg_ref, kv_ref, mn_scr, *, n_key_tiles):
    j = pl.program_id(0)

    @pl.when(j == 0)
    def _():
        mn_scr[...] = _rms(mem_ref[...], ng_ref[...]).astype(BF16)

    acc = jnp.dot(mn_scr[...], w_ref[...].astype(BF16), preferred_element_type=F32)

    @pl.when(j < n_key_tiles)
    def _():
        _store_group_norm(acc, kg_ref[...], MEM_DH, 1.0, kv_ref)

    @pl.when(j >= n_key_tiles)
    def _():
        kv_ref[...] = acc.astype(kv_ref.dtype)


def _mem_kv(mem2, ng, w_kv, kg, tn=512):
    m, d = mem2.shape
    n = w_kv.shape[1]
    return pl.pallas_call(
        functools.partial(_mem_kv_kernel, n_key_tiles=(n // 2) // tn),
        out_shape=jax.ShapeDtypeStruct((m, n), BF16),
        grid=(n // tn,),
        in_specs=[pl.BlockSpec((m, d), lambda j: (0, 0)),
                  pl.BlockSpec((1, d), lambda j: (0, 0)),
                  pl.BlockSpec((d, tn), lambda j: (0, j)),
                  pl.BlockSpec((1, MEM_DH), lambda j: (0, 0))],
        out_specs=pl.BlockSpec((m, tn), lambda j: (0, j)),
        scratch_shapes=[pltpu.VMEM((m, d), BF16)],
        compiler_params=_params("arbitrary"),
        name="mem_kv",
    )(mem2, ng, w_kv, kg)


def _gate_merge_kernel(h_ref, y0_ref, y1_ref, y2_ref, wg_ref, bg0_ref, bg1_ref, bg2_ref,
                       wb0_ref, wb1_ref, wb2_ref, r0_ref, r1_ref, o_ref, c0_ref, c1_ref):
    _cast_riders((r0_ref, r1_ref), (c0_ref, c1_ref))
    bias = jnp.concatenate([bg0_ref[...], bg1_ref[...], bg2_ref[...]], axis=1)
    gates = _sigmoid(jnp.dot(h_ref[...], wg_ref[...], preferred_element_type=F32) + bias)
    merged = None
    for b, (y_ref, wb_ref) in enumerate(((y0_ref, wb0_ref), (y1_ref, wb1_ref), (y2_ref, wb2_ref))):
        term = (gates[:, b * GATE_TILE:(b + 1) * GATE_TILE]
                * jnp.dot(y_ref[...], wb_ref[...], preferred_element_type=F32))
        merged = term if merged is None else merged + term
    o_ref[...] = merged.astype(o_ref.dtype)


def _gate_merge(h, ys, w_gate_tiled, b_gate, w_branch, riders, tm=1024):
    m, d = h.shape
    bw = w_branch.shape[1]
    tn = GATE_TILE
    ni, nj = m // tm, d // tn
    act = lambda width: pl.BlockSpec((tm, width), lambda i, j: (i, 0))
    bg = lambda b: pl.BlockSpec((1, tn), lambda i, j: (0, b * nj + j))
    wb = lambda b: pl.BlockSpec((None, bw, tn), lambda i, j: (b, 0, j))
    r_specs, r_shapes = _rider_specs(riders, ni * nj, lambda i, j: i * nj + j)
    return pl.pallas_call(
        _gate_merge_kernel,
        out_shape=(jax.ShapeDtypeStruct((m, d), BF16), *r_shapes),
        grid=(ni, nj),
        in_specs=[act(d), act(bw), act(bw), act(bw),
                  pl.BlockSpec((d, N_BRANCH * tn), lambda i, j: (0, j)),
                  bg(0), bg(1), bg(2), wb(0), wb(1), wb(2), *r_specs],
        out_specs=(pl.BlockSpec((tm, tn), lambda i, j: (i, j)), *r_specs),
        compiler_params=_params("arbitrary", "arbitrary"),
        name="gate_merge",
    )(h, ys[0], ys[1], ys[2], w_gate_tiled, b_gate, b_gate, b_gate,
      w_branch, w_branch, w_branch, *riders)


def _out_proj_kernel(mg_ref, w_ref, x_ref, g_ref, x1_ref, hf_ref):
    x1 = x_ref[...] + jnp.dot(mg_ref[...], w_ref[...], preferred_element_type=F32)
    x1_ref[...] = x1
    hf_ref[...] = _rms(x1, g_ref[...]).astype(hf_ref.dtype)


def _out_proj(merged, w_out, x2, g, tm=512):
    m, d = x2.shape
    row = lambda i: (i, 0)
    fixed = lambda i: (0, 0)
    return pl.pallas_call(
        _out_proj_kernel,
        out_shape=(jax.ShapeDtypeStruct((m, d), F32), jax.ShapeDtypeStruct((m, d), BF16)),
        grid=(m // tm,),
        in_specs=[pl.BlockSpec((tm, d), row),
                  pl.BlockSpec((d, d), fixed, pipeline_mode=pl.Buffered(1)),
                  pl.BlockSpec((tm, d), row), pl.BlockSpec((1, d), fixed)],
        out_specs=(pl.BlockSpec((tm, d), row), pl.BlockSpec((tm, d), row)),
        compiler_params=_params("parallel"),
        name="out_proj",
    )(merged, w_out, x2, g)


def _ffn_up_kernel(hf_ref, wg_ref, wu_ref, r0_ref, a_ref, c0_ref):
    _cast_riders((r0_ref,), (c0_ref,))
    hf = hf_ref[...]
    gate = jnp.dot(hf, wg_ref[...], preferred_element_type=F32)
    up = jnp.dot(hf, wu_ref[...], preferred_element_type=F32)
    a_ref[...] = (gate * _sigmoid(gate) * up).astype(a_ref.dtype)


def _ffn_up(hf, w_in, riders, tm=1024, tf=512):
    m, d = hf.shape
    d_ff = w_in.shape[1] // 2
    ni, nj = m // tm, d_ff // tf
    r_specs, r_shapes = _rider_specs(riders, ni * nj, lambda i, j: i * nj + j)
    return pl.pallas_call(
        _ffn_up_kernel,
        out_shape=(jax.ShapeDtypeStruct((m, d_ff), BF16), *r_shapes),
        grid=(ni, nj),
        in_specs=[pl.BlockSpec((tm, d), lambda i, j: (i, 0)),
                  pl.BlockSpec((d, tf), lambda i, j: (0, j)),
                  pl.BlockSpec((d, tf), lambda i, j: (0, nj + j)),
                  *r_specs],
        out_specs=(pl.BlockSpec((tm, tf), lambda i, j: (i, j)), *r_specs),
        compiler_params=_params("arbitrary", "arbitrary"),
        name="ffn_up",
    )(hf, w_in, w_in, *riders)


def _ffn_down_kernel(a_ref, w_ref, x1_ref, o_ref):
    o_ref[...] = x1_ref[...] + jnp.dot(a_ref[...], w_ref[...], preferred_element_type=F32)


def _ffn_down(a, w_down, x1, tm=1024, tn=512):
    m, d_ff = a.shape
    d = w_down.shape[1]
    return pl.pallas_call(
        _ffn_down_kernel,
        out_shape=jax.ShapeDtypeStruct((m, d), F32),
        grid=(m // tm, d // tn),
        in_specs=[pl.BlockSpec((tm, d_ff), lambda i, j: (i, 0)),
                  pl.BlockSpec((d_ff, tn), lambda i, j: (0, j)),
                  pl.BlockSpec((tm, tn), lambda i, j: (i, j))],
        out_specs=pl.BlockSpec((tm, tn), lambda i, j: (i, j)),
        compiler_params=_params("parallel", "arbitrary"),
        name="ffn_down",
    )(a, w_down, x1)


def kernel(x, mem, norm_mix_g, norm_mem_g, w_in, gla_w_alpha_up, gla_b_alpha, gla_norm_g,
           diff_q_norm_g, diff_k_norm_g, diff_lambda_q1, diff_lambda_k1, diff_lambda_q2,
           diff_lambda_k2, diff_subln_g, mem_q_norm_g, mem_k_norm_g, w_mem_kv, w_branch,
           w_gate, b_gate, w_out, norm_ffn_g, w_ffn_in, w_ffn_down):
    batch, seq, d = x.shape
    n_mem = mem.shape[1]
    depth = w_in.shape[0]
    assert depth == 1, "LAM_INIT is the layer-0 value"
    x2 = x.reshape(batch * seq, d)
    mem2 = mem.reshape(batch * n_mem, d)
    for l in range(depth):
        w_in_t = w_in[l].T
        w_up = jnp.pad(gla_w_alpha_up[l], ((0, LANES - GLA_RANK), (0, 0)))
        row = lambda v: v.reshape(1, -1)

        h, log_a = _norm_mix(x2, row(norm_mix_g[l]), w_in_t, w_up, row(gla_b_alpha[l]))
        z, w_gate_bf, w_branch_bf = _in_proj(
            h, w_in_t, row(diff_q_norm_g[l]), row(diff_k_norm_g[l]), row(mem_q_norm_g[l]),
            riders=(w_gate[l], w_branch[l].reshape(-1, d)))
        y_gla = _gla(z, log_a, row(gla_norm_g[l]), batch, seq)
        kv = _mem_kv(mem2, row(norm_mem_g[l]), w_mem_kv[l], row(mem_k_norm_g[l]))
        v_t, y_mem = _v_proj_t(h, w_in_t, z, kv, seq, n_mem)
        y_diff = _diff_attn(z, v_t, row(diff_q_norm_g[l]), row(diff_k_norm_g[l]),
                            row(diff_lambda_q1[l]), row(diff_lambda_k1[l]),
                            row(diff_lambda_q2[l]), row(diff_lambda_k2[l]),
                            row(diff_subln_g[l]), batch, seq)
        merged, w_out_bf, w_ffn_in_bf = _gate_merge(
            h, (y_gla, y_diff, y_mem), w_gate_bf, row(b_gate[l]),
            w_branch_bf.reshape(w_branch[l].shape), riders=(w_out[l], w_ffn_in[l]))
        x1, hf = _out_proj(merged, w_out_bf, x2, row(norm_ffn_g[l]))
        a, w_ffn_down_bf = _ffn_up(hf, w_ffn_in_bf, riders=(w_ffn_down[l],))
        x2 = _ffn_down(a, w_ffn_down_bf, x1)
    return x2.reshape(batch, seq, d)
```

```python
import functools

import jax
import jax.numpy as jnp
from jax import lax
from jax.experimental import pallas as pl
from jax.experimental.pallas import tpu as pltpu

F32 = jnp.float32
BF16 = jnp.bfloat16

CHUNK = 64
GLA_HEADS = 4
GLA_DK = 128
GLA_DV = 256
GLA_RANK = 16
GLA_GATE_NORM = 16.0
DIFF_HEADS = 4
DIFF_DH = 128
DIFF_DV = 256
MEM_HEADS = 4
MEM_DH = 256
N_BRANCH = 3
NORM_EPS = 1e-6
NEG_INF = -1e30
LAM_INIT = 0.8 - 0.6 * 1.0
LOG2_E = 1.4426950408889634
SCORE_LIMIT = 40.0
SCORES_AHEAD = 2
DIFF_Q_SCALE = DIFF_DH ** -0.5 * LOG2_E

LANES = 128
VMEM_LIMIT = 56 * 1024 * 1024

IN_TILE = 1024
Z_GLA_Q, Z_GLA_K, Z_GLA_V, Z_GLA_G = 0, 512, 1024, 2048
Z_DIFF_Q, Z_DIFF_K, Z_MEM_Q = 3072, 4096, 5120
Z_WIDTH = 6144
W_SRC_TILES = (0, 1, 2, 3, 4, 6)
W_DIFF_V_TILE = 5
W_ROWS = 7 * IN_TILE + GLA_RANK
W_FIRST_SHIFTED_TILE = 3


def _params(*sem):
    return pltpu.CompilerParams(dimension_semantics=sem, vmem_limit_bytes=VMEM_LIMIT)


def _nt_dot(a, b):
    return lax.dot_general(a, b, (((1,), (1,)), ((), ())), preferred_element_type=F32)


def _tn_dot(a, b, precision=None):
    return lax.dot_general(a, b, (((0,), (0,)), ((), ())), preferred_element_type=F32,
                           precision=precision)


def _rms(v, gain):
    ms = jnp.mean(v * v, axis=-1, keepdims=True)
    return v * lax.rsqrt(ms + NORM_EPS) * gain


def _sigmoid(v):
    return 1.0 / (1.0 + jnp.exp(-v))


BF16_SUBLANES = 16


def _rider_specs(weights, n_steps, step_of):
    specs, shapes = [], []
    for w in weights:
        rows, cols = w.shape
        chunk = BF16_SUBLANES
        while rows % chunk or rows // chunk > n_steps:
            chunk += BF16_SUBLANES
        last = rows // chunk - 1
        specs.append(pl.BlockSpec((chunk, cols),
                                  lambda *g, last=last: (jnp.minimum(step_of(*g), last), 0)))
        shapes.append(jax.ShapeDtypeStruct((rows, cols), BF16))
    return specs, shapes


def _cast_riders(in_refs, out_refs):
    for src, dst in zip(in_refs, out_refs):
        dst[...] = src[...].astype(BF16)


GATE_TILE = 256


def _tile_major_gate_cols(n_cols):
    per_branch = n_cols // N_BRANCH
    return [((j * N_BRANCH + b) * GATE_TILE, b * per_branch + j * GATE_TILE)
            for j in range(per_branch // GATE_TILE) for b in range(N_BRANCH)]


def _cast_gate_rider(src, dst):
    for dst_col, src_col in _tile_major_gate_cols(src.shape[1]):
        dst[:, dst_col:dst_col + GATE_TILE] = src[:, src_col:src_col + GATE_TILE].astype(BF16)


def _norm_mix_kernel(x_ref, g_ref, wa_ref, wup_ref, bal_ref, h_ref, la_ref):
    h = _rms(x_ref[...], g_ref[...]).astype(BF16)
    h_ref[...] = h
    a_low = _nt_dot(h, wa_ref[...].astype(BF16))
    w_up = wup_ref[...]
    a_hi, w_hi = a_low.astype(BF16), w_up.astype(BF16)
    a_lo = (a_low - a_hi.astype(F32)).astype(BF16)
    w_lo = (w_up - w_hi.astype(F32)).astype(BF16)
    pre = (jnp.dot(a_hi, w_hi, preferred_element_type=F32)
           + (jnp.dot(a_lo, w_hi, preferred_element_type=F32)
              + jnp.dot(a_hi, w_lo, preferred_element_type=F32))) + bal_ref[...]
    log_sig = jnp.minimum(pre, 0.0) - jnp.log1p(jnp.exp(-jnp.abs(pre)))
    la_ref[...] = log_sig * (1.0 / GLA_GATE_NORM)


def _norm_mix(x2, g, w_in_t, wup, bal, tr=1024):
    m, d = x2.shape
    n = wup.shape[1]
    fixed = lambda i: (0, 0)
    decay_block = (Z_GLA_G + GLA_HEADS * GLA_DV) // LANES
    return pl.pallas_call(
        _norm_mix_kernel,
        out_shape=(jax.ShapeDtypeStruct((m, d), BF16), jax.ShapeDtypeStruct((m, n), F32)),
        grid=(m // tr,),
        in_specs=[pl.BlockSpec((tr, d), lambda i: (i, 0)),
                  pl.BlockSpec((1, d), fixed),
                  pl.BlockSpec((LANES, d), lambda i: (decay_block, 0)),
                  pl.BlockSpec((LANES, n), fixed),
                  pl.BlockSpec((1, n), fixed)],
        out_specs=(pl.BlockSpec((tr, d), lambda i: (i, 0)),
                   pl.BlockSpec((tr, n), lambda i: (i, 0))),
        compiler_params=_params("parallel"),
        name="norm_mix",
    )(x2, g, w_in_t, wup, bal)


def _store_group_norm(acc, gain, width, scale, out_ref):
    for s in range(0, acc.shape[1], width):
        blk = acc[:, s:s + width]
        out_ref[:, s:s + width] = (_rms(blk, gain) * scale).astype(out_ref.dtype)


def _w_tile_specs(d, src_tile):
    hi_per_tile = IN_TILE // GLA_RANK
    return [pl.BlockSpec((IN_TILE, d), lambda *g: (src_tile(*g), 0)),
            pl.BlockSpec((GLA_RANK, d), lambda *g: ((src_tile(*g) + 1) * hi_per_tile, 0))]


def _cast_w_tile(w_lo_ref, w_hi_ref, w_scr, first_step, shifted):
    @pl.when(first_step & jnp.logical_not(shifted))
    def _():
        w_scr[...] = w_lo_ref[...].astype(BF16)

    @pl.when(first_step & shifted)
    def _():
        w_scr[:IN_TILE - GLA_RANK, :] = w_lo_ref[GLA_RANK:, :].astype(BF16)
        w_scr[IN_TILE - GLA_RANK:, :] = w_hi_ref[...].astype(BF16)


def _in_proj_kernel(h_ref, w_lo_ref, w_hi_ref, dq_g_ref, dk_g_ref, mq_g_ref, r0_ref, r1_ref,
                    z_ref, c0_ref, c1_ref, w_scr):
    _cast_gate_rider(r0_ref, c0_ref)
    _cast_riders((r1_ref,), (c1_ref,))
    j = pl.program_id(0)
    _cast_w_tile(w_lo_ref, w_hi_ref, w_scr, pl.program_id(1) == 0, j >= W_FIRST_SHIFTED_TILE)
    j_dq, j_dk, j_mq = Z_DIFF_Q // IN_TILE, Z_DIFF_K // IN_TILE, Z_MEM_Q // IN_TILE

    def tile(epilogue, row_parts):
        part = h_ref.shape[0] // row_parts
        for r in range(row_parts):
            rows = slice(r * part, (r + 1) * part)
            epilogue(_nt_dot(h_ref[rows, :], w_scr[...]), z_ref.at[rows, :])

    def plain(acc, out_ref):
        out_ref[...] = acc.astype(out_ref.dtype)

    @pl.when((j != j_dq) & (j != j_dk) & (j != j_mq))
    def _():
        tile(plain, 1)

    @pl.when((j == j_dq) | (j == j_dk))
    def _():
        gain = jnp.where(j == j_dq, dq_g_ref[...] * DIFF_Q_SCALE, dk_g_ref[...])
        tile(lambda acc, out: _store_group_norm(acc, gain, DIFF_DH, 1.0, out), 2)

    @pl.when(j == j_mq)
    def _():
        tile(lambda acc, out: _store_group_norm(acc, mq_g_ref[...], MEM_DH, MEM_DH ** -0.5, out), 2)


def _in_proj(h, w_in_t, dq_g, dk_g, mq_g, riders, tm=1024):
    m, d = h.shape
    assert w_in_t.shape[0] == W_ROWS
    nj, ni = Z_WIDTH // IN_TILE, m // tm
    assert W_SRC_TILES == tuple(j + (j >= W_DIFF_V_TILE) for j in range(nj))
    r_specs, r_shapes = _rider_specs(riders, nj * ni, lambda j, i: j * ni + i)
    return pl.pallas_call(
        _in_proj_kernel,
        out_shape=(jax.ShapeDtypeStruct((m, Z_WIDTH), BF16), *r_shapes),
        grid=(nj, ni),
        in_specs=[pl.BlockSpec((tm, d), lambda j, i: (i, 0)),
                  *_w_tile_specs(d, lambda j, i: jnp.where(j >= W_DIFF_V_TILE, j + 1, j)),
                  pl.BlockSpec((1, DIFF_DH), lambda j, i: (0, 0)),
                  pl.BlockSpec((1, DIFF_DH), lambda j, i: (0, 0)),
                  pl.BlockSpec((1, MEM_DH), lambda j, i: (0, 0)),
                  *r_specs],
        out_specs=(pl.BlockSpec((tm, IN_TILE), lambda j, i: (i, j)), *r_specs),
        scratch_shapes=[pltpu.VMEM((IN_TILE, d), BF16)],
        compiler_params=_params("arbitrary", "arbitrary"),
        name="in_proj",
    )(h, w_in_t, w_in_t, dq_g, dk_g, mq_g, *riders)


def _v_proj_t_kernel(h_ref, w_lo_ref, w_hi_ref, mq_ref, mk_ref, mv_ref, vt_ref, ym_ref, w_scr, *,
                     sub_rows):
    _cast_w_tile(w_lo_ref, w_hi_ref, w_scr, pl.program_id(0) == 0,
                 W_DIFF_V_TILE >= W_FIRST_SHIFTED_TILE)
    parts = [slice(r, r + sub_rows) for r in range(0, mq_ref.shape[0], sub_rows)]
    heads = [slice(hd * MEM_DH, (hd + 1) * MEM_DH) for hd in range(MEM_HEADS)]
    scores = {(r, hd): _nt_dot(mq_ref[rows, cols], mk_ref[:, cols])
              for r, rows in enumerate(parts) for hd, cols in enumerate(heads)}
    for r, rows in enumerate(parts):
        vt_ref[:, rows] = _nt_dot(w_scr[...], h_ref[rows, :]).astype(vt_ref.dtype)
        for hd, cols in enumerate(heads):
            s = scores[r, hd]
            e = jnp.exp(s - jnp.max(s, axis=-1, keepdims=True))
            p = (e / jnp.sum(e, axis=-1, keepdims=True)).astype(BF16)
            ym_ref[rows, cols] = jnp.dot(p, mv_ref[:, cols],
                                         preferred_element_type=F32).astype(ym_ref.dtype)


def _v_proj_t(h, w_in_t, z, kv, seq, n_mem, tm=1024, sub_rows=512):
    m, d = h.shape
    width = MEM_HEADS * MEM_DH
    tiles_per_batch = seq // tm
    return pl.pallas_call(
        functools.partial(_v_proj_t_kernel, sub_rows=sub_rows),
        out_shape=(jax.ShapeDtypeStruct((IN_TILE, m), BF16), jax.ShapeDtypeStruct((m, width), BF16)),
        grid=(m // tm,),
        in_specs=[pl.BlockSpec((tm, d), lambda i: (i, 0)),
                  *_w_tile_specs(d, lambda i: W_DIFF_V_TILE),
                  pl.BlockSpec((tm, width), lambda i: (i, Z_MEM_Q // width)),
                  pl.BlockSpec((n_mem, width), lambda i: (i // tiles_per_batch, 0)),
                  pl.BlockSpec((n_mem, width), lambda i: (i // tiles_per_batch, 1))],
        out_specs=(pl.BlockSpec((IN_TILE, tm), lambda i: (0, i)),
                   pl.BlockSpec((tm, width), lambda i: (i, 0))),
        scratch_shapes=[pltpu.VMEM((IN_TILE, d), BF16)],
        compiler_params=_params("arbitrary"),
        name="v_proj_t",
    )(h, w_in_t, w_in_t, z, kv, kv)


def _chunk_cumsum(x):
    row_in_chunk = lax.broadcasted_iota(jnp.int32, x.shape, 0) % CHUNK
    shift = 1
    while shift < CHUNK:
        x = x + jnp.where(row_in_chunk >= shift, pltpu.roll(x, shift, 0), 0.0)
        shift *= 2
    return x


def _gla_kernel(q_ref, k_ref, v_ref, g_ref, la_ref, ng_ref, o_ref, s_ref, *, ts):
    @pl.when(pl.program_id(2) == 0)
    def _():
        s_ref[...] = jnp.zeros_like(s_ref)

    bcum_all = _chunk_cumsum(la_ref[...])
    row = lax.broadcasted_iota(jnp.int32, (CHUNK, CHUNK), 0)
    col = lax.broadcasted_iota(jnp.int32, (CHUNK, CHUNK), 1)
    causal = row >= col

    n_chunks = ts // CHUNK
    chunk_rows = [slice(c * CHUNK, (c + 1) * CHUNK) for c in range(n_chunks)]

    lhs, incs, decays = [], [], []
    for rows in chunk_rows:
        bcum = bcum_all[rows]
        b_last = bcum[CHUNK - 1:CHUNK, :]
        q = q_ref[rows, :].astype(F32) * (GLA_DK ** -0.5)
        k = k_ref[rows, :].astype(F32)
        q_dec = (q * jnp.exp(bcum)).astype(BF16)
        k_dec = (k * jnp.exp(-bcum)).astype(BF16)
        k_tail = (k * jnp.exp(b_last - bcum)).astype(BF16)
        att = jnp.where(causal, _nt_dot(q_dec, k_dec), 0.0).astype(BF16)
        lhs.append(jnp.concatenate([q_dec, att], axis=1))
        incs.append(_tn_dot(k_tail, v_ref[rows, :]))
        decay_col = jnp.broadcast_to(jnp.exp(b_last), (GLA_DK, GLA_DK)).T
        decays.append(jnp.concatenate([decay_col] * (GLA_DV // GLA_DK), axis=1))

    state = s_ref[...]
    states = []
    for c in range(n_chunks):
        states.append(state.astype(BF16))
        state = decays[c] * state + incs[c]
    s_ref[...] = state

    for c, rows in enumerate(chunk_rows):
        o = jnp.dot(lhs[c], jnp.concatenate([states[c], v_ref[rows, :]], axis=0),
                    preferred_element_type=F32)
        gate = g_ref[rows, :].astype(F32)
        o_ref[rows, :] = (_rms(o, ng_ref[...]) * (gate * _sigmoid(gate))).astype(o_ref.dtype)


def _gla(z, log_a, ng, batch, seq, ts=2048):
    m = z.shape[0]
    nt = seq // ts
    rows = lambda b, h, t: b * nt + t
    return pl.pallas_call(
        functools.partial(_gla_kernel, ts=ts),
        out_shape=jax.ShapeDtypeStruct((m, GLA_HEADS * GLA_DV), BF16),
        grid=(batch, GLA_HEADS, nt),
        in_specs=[pl.BlockSpec((ts, GLA_DK), lambda b, h, t: (rows(b, h, t), Z_GLA_Q // GLA_DK + h)),
                  pl.BlockSpec((ts, GLA_DK), lambda b, h, t: (rows(b, h, t), Z_GLA_K // GLA_DK + h)),
                  pl.BlockSpec((ts, GLA_DV), lambda b, h, t: (rows(b, h, t), Z_GLA_V // GLA_DV + h)),
                  pl.BlockSpec((ts, GLA_DV), lambda b, h, t: (rows(b, h, t), Z_GLA_G // GLA_DV + h)),
                  pl.BlockSpec((ts, GLA_DK), lambda b, h, t: (rows(b, h, t), h)),
                  pl.BlockSpec((1, GLA_DV), lambda b, h, t: (0, 0))],
        out_specs=pl.BlockSpec((ts, GLA_DV), lambda b, h, t: (rows(b, h, t), h)),
        scratch_shapes=[pltpu.VMEM((GLA_DK, GLA_DV), F32)],
        compiler_params=_params("parallel", "parallel", "arbitrary"),
        name="gla",
    )(z, z, z, z, log_a, ng)


def _diff_kernel(q_ref, k_ref, vt_ref, qg_ref, kg_ref, lq1_ref, lk1_ref, lq2_ref, lk2_ref, sg_ref,
                 o_ref, acc_scr, m_scr, l_scr, *, tq):
    qi = pl.program_id(2)
    m_scr[...] = jnp.full_like(m_scr, NEG_INF)
    l_scr[...] = jnp.zeros_like(l_scr)
    acc_scr[...] = jnp.zeros_like(acc_scr)
    n_streams = q_ref.shape[1] // DIFF_DH
    half = tq // 2

    score_bound = (1.02 * DIFF_DH * DIFF_Q_SCALE) * (jnp.max(jnp.abs(qg_ref[...]))
                                                    * jnp.max(jnp.abs(kg_ref[...])))
    bounded = score_bound <= SCORE_LIMIT

    def update(c, lanes, s, vt, shifted):
        if shifted:
            m_old = m_scr[c, :, lanes]
            m_new = jnp.maximum(m_old, jnp.max(s, axis=0, keepdims=True))
            alpha = jnp.exp2(m_old - m_new)
            p = jnp.exp2(s - m_new)
            l_scr[c, :, lanes] = alpha * l_scr[c, :, lanes] + jnp.sum(p, axis=0, keepdims=True)
            m_scr[c, :, lanes] = m_new
            acc_scr[c, :, lanes] = (alpha * acc_scr[c, :, lanes]
                                    + jnp.dot(vt, p.astype(BF16), preferred_element_type=F32))
        else:
            p = jnp.exp2(s)
            l_scr[c, :, lanes] = l_scr[c, :, lanes] + jnp.sum(p, axis=0, keepdims=True)
            acc_scr[c, :, lanes] = (acc_scr[c, :, lanes]
                                    + jnp.dot(vt, p.astype(BF16), preferred_element_type=F32))

    def values_t(c, start, n):
        head = c // 2
        return vt_ref[head * DIFF_DV:(head + 1) * DIFF_DV, pl.ds(start, n)]

    def full_block(kb, carry, shifted):
        start = pl.multiple_of(kb * tq, tq)
        scores = []
        for c in range(n_streams):
            cols = slice(c * DIFF_DH, (c + 1) * DIFF_DH)
            scores.append(_nt_dot(k_ref[pl.ds(start, tq), cols], q_ref[:, cols]))
        for c in range(n_streams):
            update(c, slice(0, tq), scores[c], values_t(c, start, tq), shifted)
        return carry

    def diag_block(kb, shifted):
        lo = pl.multiple_of(kb * tq, tq)
        hi = pl.multiple_of(kb * tq + half, half)
        key_chunk = lax.broadcasted_iota(jnp.int32, (half, half), 0) // CHUNK
        query_chunk = lax.broadcasted_iota(jnp.int32, (half, half), 1) // CHUNK
        visible = key_chunk <= query_chunk
        def stream_scores(c):
            cols = slice(c * DIFF_DH, (c + 1) * DIFF_DH)
            k_lo, k_hi = k_ref[pl.ds(lo, half), cols], k_ref[pl.ds(hi, half), cols]
            q_lo, q_hi = q_ref[:half, cols], q_ref[half:, cols]
            s_lo = jnp.where(visible, _nt_dot(k_lo, q_lo), NEG_INF)
            s_hi = jnp.concatenate([_nt_dot(k_lo, q_hi),
                                    jnp.where(visible, _nt_dot(k_hi, q_hi), NEG_INF)], axis=0)
            return s_lo, s_hi

        scores = [stream_scores(c) for c in range(SCORES_AHEAD)]
        lam = (jnp.exp(jnp.sum(lq1_ref[...] * lk1_ref[...], axis=-1, keepdims=True))
               - jnp.exp(jnp.sum(lq2_ref[...] * lk2_ref[...], axis=-1, keepdims=True)) + LAM_INIT)
        for head in range(n_streams // 2):
            c1, c2 = 2 * head, 2 * head + 1
            for c in (c1, c2):
                if c + SCORES_AHEAD < n_streams:
                    scores.append(stream_scores(c + SCORES_AHEAD))
                update(c, slice(0, half), scores[c][0], values_t(c, lo, half), shifted)
                update(c, slice(half, tq), scores[c][1], values_t(c, lo, tq), shifted)
            o_t = (acc_scr[c1] * (1.0 / l_scr[c1])
                   - acc_scr[c2] * (lam / l_scr[c2]))
            o_ref[:, head * DIFF_DV:(head + 1) * DIFF_DV] = (
                _rms(o_t.T, sg_ref[...]) * (1.0 - LAM_INIT)).astype(o_ref.dtype)

    def all_blocks(shifted):
        lax.fori_loop(0, qi, functools.partial(full_block, shifted=shifted), 0)
        diag_block(qi, shifted)

    @pl.when(bounded)
    def _():
        all_blocks(shifted=False)

    @pl.when(jnp.logical_not(bounded))
    def _():
        all_blocks(shifted=True)


def _diff_attn(z, v_t, qg, kg, lq1, lk1, lq2, lk2, sg, batch, seq, tq=512, heads_per_step=4):
    m = z.shape[0]
    nq = seq // tq
    width = heads_per_step * DIFF_DV
    ns = 2 * heads_per_step
    vec = pl.BlockSpec((1, DIFF_DH), lambda b, g, i: (0, 0))
    return pl.pallas_call(
        functools.partial(_diff_kernel, tq=tq),
        out_shape=jax.ShapeDtypeStruct((m, DIFF_HEADS * DIFF_DV), BF16),
        grid=(batch, DIFF_HEADS // heads_per_step, nq),
        in_specs=[pl.BlockSpec((tq, width), lambda b, g, i: (b * nq + i, Z_DIFF_Q // width + g)),
                  pl.BlockSpec((seq, width), lambda b, g, i: (b, Z_DIFF_K // width + g)),
                  pl.BlockSpec((width, seq), lambda b, g, i: (g, b)),
                  vec, vec, vec, vec, vec, vec,
                  pl.BlockSpec((1, DIFF_DV), lambda b, g, i: (0, 0))],
        out_specs=pl.BlockSpec((tq, width), lambda b, g, i: (b * nq + i, g)),
        scratch_shapes=[pltpu.VMEM((ns, DIFF_DV, tq), F32),
                        pltpu.VMEM((ns, 1, tq), F32),
                        pltpu.VMEM((ns, 1, tq), F32)],
        compiler_params=_params("parallel", "parallel", "arbitrary"),
        name="diff_attn",
    )(z, z, v_t, qg, kg, lq1, lk1, lq2, lk2, sg)


def _mem_kv_kernel(mem_ref, ng_ref, w_ref, kg_ref, kv_ref, mn_scr, *, n_key_tiles):
    j = pl.program_id(0)

    @pl.when(j == 0)
    def _():
        mn_scr[...] = _rms(mem_ref[...], ng_ref[...]).astype(BF16)

    acc = jnp.dot(mn_scr[...], w_ref[...].astype(BF16), preferred_element_type=F32)

    @pl.when(j < n_key_tiles)
    def _():
        _store_group_norm(acc, kg_ref[...], MEM_DH, 1.0, kv_ref)

    @pl.when(j >= n_key_tiles)
    def _():
        kv_ref[...] = acc.astype(kv_ref.dtype)


def _mem_kv(mem2, ng, w_kv, kg, tn=512):
    m, d = mem2.shape
    n = w_kv.shape[1]
    return pl.pallas_call(
        functools.partial(_mem_kv_kernel, n_key_tiles=(n // 2) // tn),
        out_shape=jax.ShapeDtypeStruct((m, n), BF16),
        grid=(n // tn,),
        in_specs=[pl.BlockSpec((m, d), lambda j: (0, 0)),
                  pl.BlockSpec((1, d), lambda j: (0, 0)),
                  pl.BlockSpec((d, tn), lambda j: (0, j)),
                  pl.BlockSpec((1, MEM_DH), lambda j: (0, 0))],
        out_specs=pl.BlockSpec((m, tn), lambda j: (0, j)),
        scratch_shapes=[pltpu.VMEM((m, d), BF16)],
        compiler_params=_params("arbitrary"),
        name="mem_kv",
    )(mem2, ng, w_kv, kg)


def _gate_merge_kernel(h_ref, y0_ref, y1_ref, y2_ref, wg_ref, bg0_ref, bg1_ref, bg2_ref,
                       wb0_ref, wb1_ref, wb2_ref, r0_ref, r1_ref, o_ref, c0_ref, c1_ref):
    _cast_riders((r0_ref,), (c0_ref,))
    _cast_ffn_in_rider(r1_ref, c1_ref)
    bias = jnp.concatenate([bg0_ref[...], bg1_ref[...], bg2_ref[...]], axis=1)
    gates = _sigmoid(jnp.dot(h_ref[...], wg_ref[...], preferred_element_type=F32) + bias)
    merged = None
    for b, (y_ref, wb_ref) in enumerate(((y0_ref, wb0_ref), (y1_ref, wb1_ref), (y2_ref, wb2_ref))):
        term = (gates[:, b * GATE_TILE:(b + 1) * GATE_TILE]
                * jnp.dot(y_ref[...], wb_ref[...], preferred_element_type=F32))
        merged = term if merged is None else merged + term
    o_ref[...] = merged.astype(o_ref.dtype)


def _gate_merge(h, ys, w_gate_tiled, b_gate, w_branch, riders, tm=1024):
    m, d = h.shape
    bw = w_branch.shape[1]
    tn = GATE_TILE
    ni, nj = m // tm, d // tn
    act = lambda width: pl.BlockSpec((tm, width), lambda i, j: (i, 0))
    bg = lambda b: pl.BlockSpec((1, tn), lambda i, j: (0, b * nj + j))
    wb = lambda b: pl.BlockSpec((None, bw, tn), lambda i, j: (b, 0, j))
    r_specs, r_shapes = _rider_specs(riders, ni * nj, lambda i, j: i * nj + j)
    return pl.pallas_call(
        _gate_merge_kernel,
        out_shape=(jax.ShapeDtypeStruct((m, d), BF16), *r_shapes),
        grid=(ni, nj),
        in_specs=[act(d), act(bw), act(bw), act(bw),
                  pl.BlockSpec((d, N_BRANCH * tn), lambda i, j: (0, j)),
                  bg(0), bg(1), bg(2), wb(0), wb(1), wb(2), *r_specs],
        out_specs=(pl.BlockSpec((tm, tn), lambda i, j: (i, j)), *r_specs),
        compiler_params=_params("arbitrary", "arbitrary"),
        name="gate_merge",
    )(h, ys[0], ys[1], ys[2], w_gate_tiled, b_gate, b_gate, b_gate,
      w_branch, w_branch, w_branch, *riders)


def _out_proj_kernel(mg_ref, w_ref, x_ref, g_ref, x1_ref, hf_ref):
    x1 = x_ref[...] + jnp.dot(mg_ref[...], w_ref[...], preferred_element_type=F32)
    x1_ref[...] = x1
    hf_ref[...] = _rms(x1, g_ref[...]).astype(hf_ref.dtype)


def _out_proj(merged, w_out, x2, g, tm=512):
    m, d = x2.shape
    row = lambda i: (i, 0)
    fixed = lambda i: (0, 0)
    return pl.pallas_call(
        _out_proj_kernel,
        out_shape=(jax.ShapeDtypeStruct((m, d), F32), jax.ShapeDtypeStruct((m, d), BF16)),
        grid=(m // tm,),
        in_specs=[pl.BlockSpec((tm, d), row),
                  pl.BlockSpec((d, d), fixed, pipeline_mode=pl.Buffered(1)),
                  pl.BlockSpec((tm, d), row), pl.BlockSpec((1, d), fixed)],
        out_specs=(pl.BlockSpec((tm, d), row), pl.BlockSpec((tm, d), row)),
        compiler_params=_params("parallel"),
        name="out_proj",
    )(merged, w_out, x2, g)


FFN_TILE = 512


def _cast_ffn_in_rider(src, dst):
    d_ff = src.shape[1] // 2
    for j in range(d_ff // FFN_TILE):
        for part in range(2):
            dst_col = (2 * j + part) * FFN_TILE
            src_col = part * d_ff + j * FFN_TILE
            dst[:, dst_col:dst_col + FFN_TILE] = src[:, src_col:src_col + FFN_TILE].astype(BF16)


def _ffn_up_kernel(hf_ref, w_ref, r0_ref, a_ref, c0_ref):
    _cast_riders((r0_ref,), (c0_ref,))
    both = jnp.dot(hf_ref[...], w_ref[...], preferred_element_type=F32)
    gate, up = both[:, :FFN_TILE], both[:, FFN_TILE:]
    a_ref[...] = (gate * _sigmoid(gate) * up).astype(a_ref.dtype)


def _ffn_up(hf, w_in_tiled, riders, tm=1024):
    m, d = hf.shape
    d_ff = w_in_tiled.shape[1] // 2
    tf = FFN_TILE
    ni, nj = m // tm, d_ff // tf
    r_specs, r_shapes = _rider_specs(riders, ni * nj, lambda i, j: i * nj + j)
    return pl.pallas_call(
        _ffn_up_kernel,
        out_shape=(jax.ShapeDtypeStruct((m, d_ff), BF16), *r_shapes),
        grid=(ni, nj),
        in_specs=[pl.BlockSpec((tm, d), lambda i, j: (i, 0)),
                  pl.BlockSpec((d, 2 * tf), lambda i, j: (0, j)),
                  *r_specs],
        out_specs=(pl.BlockSpec((tm, tf), lambda i, j: (i, j)), *r_specs),
        compiler_params=_params("arbitrary", "arbitrary"),
        name="ffn_up",
    )(hf, w_in_tiled, *riders)


def _ffn_down_kernel(a_ref, w_ref, x1_ref, o_ref):
    o_ref[...] = x1_ref[...] + jnp.dot(a_ref[...], w_ref[...], preferred_element_type=F32)


def _ffn_down(a, w_down, x1, tm=1024, tn=512):
    m, d_ff = a.shape
    d = w_down.shape[1]
    return pl.pallas_call(
        _ffn_down_kernel,
        out_shape=jax.ShapeDtypeStruct((m, d), F32),
        grid=(m // tm, d // tn),
        in_specs=[pl.BlockSpec((tm, d_ff), lambda i, j: (i, 0)),
                  pl.BlockSpec((d_ff, tn), lambda i, j: (0, j)),
                  pl.BlockSpec((tm, tn), lambda i, j: (i, j))],
        out_specs=pl.BlockSpec((tm, tn), lambda i, j: (i, j)),
        compiler_params=_params("parallel", "arbitrary"),
        name="ffn_down",
    )(a, w_down, x1)


def kernel(x, mem, norm_mix_g, norm_mem_g, w_in, gla_w_alpha_up, gla_b_alpha, gla_norm_g,
           diff_q_norm_g, diff_k_norm_g, diff_lambda_q1, diff_lambda_k1, diff_lambda_q2,
           diff_lambda_k2, diff_subln_g, mem_q_norm_g, mem_k_norm_g, w_mem_kv, w_branch,
           w_gate, b_gate, w_out, norm_ffn_g, w_ffn_in, w_ffn_down):
    batch, seq, d = x.shape
    n_mem = mem.shape[1]
    depth = w_in.shape[0]
    assert depth == 1, "LAM_INIT is the layer-0 value"
    x2 = x.reshape(batch * seq, d)
    mem2 = mem.reshape(batch * n_mem, d)
    for l in range(depth):
        w_in_t = w_in[l].T
        w_up = jnp.pad(gla_w_alpha_up[l], ((0, LANES - GLA_RANK), (0, 0)))
        row = lambda v: v.reshape(1, -1)

        h, log_a = _norm_mix(x2, row(norm_mix_g[l]), w_in_t, w_up, row(gla_b_alpha[l]))
        z, w_gate_bf, w_branch_bf = _in_proj(
            h, w_in_t, row(diff_q_norm_g[l]), row(diff_k_norm_g[l]), row(mem_q_norm_g[l]),
            riders=(w_gate[l], w_branch[l].reshape(-1, d)))
        y_gla = _gla(z, log_a, row(gla_norm_g[l]), batch, seq)
        kv = _mem_kv(mem2, row(norm_mem_g[l]), w_mem_kv[l], row(mem_k_norm_g[l]))
        v_t, y_mem = _v_proj_t(h, w_in_t, z, kv, seq, n_mem)
        y_diff = _diff_attn(z, v_t, row(diff_q_norm_g[l]), row(diff_k_norm_g[l]),
                            row(diff_lambda_q1[l]), row(diff_lambda_k1[l]),
                            row(diff_lambda_q2[l]), row(diff_lambda_k2[l]),
                            row(diff_subln_g[l]), batch, seq)
        merged, w_out_bf, w_ffn_in_bf = _gate_merge(
            h, (y_gla, y_diff, y_mem), w_gate_bf, row(b_gate[l]),
            w_branch_bf.reshape(w_branch[l].shape), riders=(w_out[l], w_ffn_in[l]))
        x1, hf = _out_proj(merged, w_out_bf, x2, row(norm_ffn_g[l]))
        a, w_ffn_down_bf = _ffn_up(hf, w_ffn_in_bf, riders=(w_ffn_down[l],))
        x2 = _ffn_down(a, w_ffn_down_bf, x1)
    return x2.reshape(batch, seq, d)
```

```python
import functools

import jax
import jax.numpy as jnp
from jax import lax
from jax.experimental import pallas as pl
from jax.experimental.pallas import tpu as pltpu

F32 = jnp.float32
BF16 = jnp.bfloat16

CHUNK = 64
GLA_HEADS = 4
GLA_DK = 128
GLA_DV = 256
GLA_RANK = 16
GLA_GATE_NORM = 16.0
DIFF_HEADS = 4
DIFF_DH = 128
DIFF_DV = 256
MEM_HEADS = 4
MEM_DH = 256
N_BRANCH = 3
NORM_EPS = 1e-6
NEG_INF = -1e30
LAM_INIT = 0.8 - 0.6 * 1.0
LOG2_E = 1.4426950408889634
SCORE_LIMIT = 40.0
SCORES_AHEAD = 2
DIFF_Q_SCALE = DIFF_DH ** -0.5 * LOG2_E

LANES = 128
VMEM_LIMIT = 56 * 1024 * 1024

IN_TILE = 1024
Z_GLA_Q, Z_GLA_K, Z_GLA_V, Z_GLA_G = 0, 512, 1024, 2048
Z_DIFF_Q, Z_DIFF_K, Z_MEM_Q = 3072, 4096, 5120
Z_WIDTH = 6144
W_SRC_TILES = (0, 1, 2, 3, 4, 6)
W_DIFF_V_TILE = 5
W_ROWS = 7 * IN_TILE + GLA_RANK
W_FIRST_SHIFTED_TILE = 3


def _params(*sem):
    return pltpu.CompilerParams(dimension_semantics=sem, vmem_limit_bytes=VMEM_LIMIT)


def _nt_dot(a, b):
    return lax.dot_general(a, b, (((1,), (1,)), ((), ())), preferred_element_type=F32)


def _tn_dot(a, b, precision=None):
    return lax.dot_general(a, b, (((0,), (0,)), ((), ())), preferred_element_type=F32,
                           precision=precision)


def _rms(v, gain):
    ms = jnp.mean(v * v, axis=-1, keepdims=True)
    return v * lax.rsqrt(ms + NORM_EPS) * gain


def _sigmoid(v):
    return 1.0 / (1.0 + jnp.exp(-v))


BF16_SUBLANES = 16


def _rider_specs(weights, n_steps, step_of):
    specs, shapes = [], []
    for w in weights:
        rows, cols = w.shape
        chunk = BF16_SUBLANES
        while rows % chunk or rows // chunk > n_steps:
            chunk += BF16_SUBLANES
        last = rows // chunk - 1
        specs.append(pl.BlockSpec((chunk, cols),
                                  lambda *g, last=last: (jnp.minimum(step_of(*g), last), 0)))
        shapes.append(jax.ShapeDtypeStruct((rows, cols), BF16))
    return specs, shapes


def _cast_riders(in_refs, out_refs):
    for src, dst in zip(in_refs, out_refs):
        dst[...] = src[...].astype(BF16)


GATE_TILE = 256


def _tile_major_gate_cols(n_cols):
    per_branch = n_cols // N_BRANCH
    return [((j * N_BRANCH + b) * GATE_TILE, b * per_branch + j * GATE_TILE)
            for j in range(per_branch // GATE_TILE) for b in range(N_BRANCH)]


def _cast_gate_rider(src, dst):
    for dst_col, src_col in _tile_major_gate_cols(src.shape[1]):
        dst[:, dst_col:dst_col + GATE_TILE] = src[:, src_col:src_col + GATE_TILE].astype(BF16)


def _norm_mix_kernel(x_ref, g_ref, wa_ref, wup_ref, bal_ref, h_ref, la_ref):
    h = _rms(x_ref[...], g_ref[...]).astype(BF16)
    h_ref[...] = h
    a_low = _nt_dot(h, wa_ref[...].astype(BF16))
    w_up = wup_ref[...]
    a_hi, w_hi = a_low.astype(BF16), w_up.astype(BF16)
    a_lo = (a_low - a_hi.astype(F32)).astype(BF16)
    w_lo = (w_up - w_hi.astype(F32)).astype(BF16)
    pre = (jnp.dot(a_hi, w_hi, preferred_element_type=F32)
           + (jnp.dot(a_lo, w_hi, preferred_element_type=F32)
              + jnp.dot(a_hi, w_lo, preferred_element_type=F32))) + bal_ref[...]
    log_sig = jnp.minimum(pre, 0.0) - jnp.log1p(jnp.exp(-jnp.abs(pre)))
    la_ref[...] = log_sig * (1.0 / GLA_GATE_NORM)


def _norm_mix(x2, g, w_in_t, wup, bal, tr=1024):
    m, d = x2.shape
    n = wup.shape[1]
    fixed = lambda i: (0, 0)
    decay_block = (Z_GLA_G + GLA_HEADS * GLA_DV) // LANES
    return pl.pallas_call(
        _norm_mix_kernel,
        out_shape=(jax.ShapeDtypeStruct((m, d), BF16), jax.ShapeDtypeStruct((m, n), F32)),
        grid=(m // tr,),
        in_specs=[pl.BlockSpec((tr, d), lambda i: (i, 0)),
                  pl.BlockSpec((1, d), fixed),
                  pl.BlockSpec((LANES, d), lambda i: (decay_block, 0)),
                  pl.BlockSpec((LANES, n), fixed),
                  pl.BlockSpec((1, n), fixed)],
        out_specs=(pl.BlockSpec((tr, d), lambda i: (i, 0)),
                   pl.BlockSpec((tr, n), lambda i: (i, 0))),
        compiler_params=_params("parallel"),
        name="norm_mix",
    )(x2, g, w_in_t, wup, bal)


def _store_group_norm(acc, gain, width, scale, out_ref):
    for s in range(0, acc.shape[1], width):
        blk = acc[:, s:s + width]
        out_ref[:, s:s + width] = (_rms(blk, gain) * scale).astype(out_ref.dtype)


def _w_tile_specs(d, src_tile):
    hi_per_tile = IN_TILE // GLA_RANK
    return [pl.BlockSpec((IN_TILE, d), lambda *g: (src_tile(*g), 0)),
            pl.BlockSpec((GLA_RANK, d), lambda *g: ((src_tile(*g) + 1) * hi_per_tile, 0))]


def _cast_w_tile(w_lo_ref, w_hi_ref, w_scr, first_step, shifted):
    @pl.when(first_step & jnp.logical_not(shifted))
    def _():
        w_scr[...] = w_lo_ref[...].astype(BF16)

    @pl.when(first_step & shifted)
    def _():
        w_scr[:IN_TILE - GLA_RANK, :] = w_lo_ref[GLA_RANK:, :].astype(BF16)
        w_scr[IN_TILE - GLA_RANK:, :] = w_hi_ref[...].astype(BF16)


def _in_proj_kernel(h_ref, w_lo_ref, w_hi_ref, dq_g_ref, dk_g_ref, mq_g_ref, r0_ref, r1_ref,
                    z_ref, c0_ref, c1_ref, w_scr):
    _cast_gate_rider(r0_ref, c0_ref)
    _cast_riders((r1_ref,), (c1_ref,))
    j = pl.program_id(0)
    _cast_w_tile(w_lo_ref, w_hi_ref, w_scr, pl.program_id(1) == 0, j >= W_FIRST_SHIFTED_TILE)
    j_dq, j_dk, j_mq = Z_DIFF_Q // IN_TILE, Z_DIFF_K // IN_TILE, Z_MEM_Q // IN_TILE

    def tile(epilogue, row_parts):
        part = h_ref.shape[0] // row_parts
        for r in range(row_parts):
            rows = slice(r * part, (r + 1) * part)
            epilogue(_nt_dot(h_ref[rows, :], w_scr[...]), z_ref.at[rows, :])

    def plain(acc, out_ref):
        out_ref[...] = acc.astype(out_ref.dtype)

    @pl.when((j != j_dq) & (j != j_dk) & (j != j_mq))
    def _():
        tile(plain, 1)

    @pl.when((j == j_dq) | (j == j_dk))
    def _():
        gain = jnp.where(j == j_dq, dq_g_ref[...] * DIFF_Q_SCALE, dk_g_ref[...])
        tile(lambda acc, out: _store_group_norm(acc, gain, DIFF_DH, 1.0, out), 4)

    @pl.when(j == j_mq)
    def _():
        tile(lambda acc, out: _store_group_norm(acc, mq_g_ref[...], MEM_DH, MEM_DH ** -0.5, out), 4)


def _in_proj(h, w_in_t, dq_g, dk_g, mq_g, riders, tm=1024):
    m, d = h.shape
    assert w_in_t.shape[0] == W_ROWS
    nj, ni = Z_WIDTH // IN_TILE, m // tm
    assert W_SRC_TILES == tuple(j + (j >= W_DIFF_V_TILE) for j in range(nj))
    r_specs, r_shapes = _rider_specs(riders, nj * ni, lambda j, i: j * ni + i)
    return pl.pallas_call(
        _in_proj_kernel,
        out_shape=(jax.ShapeDtypeStruct((m, Z_WIDTH), BF16), *r_shapes),
        grid=(nj, ni),
        in_specs=[pl.BlockSpec((tm, d), lambda j, i: (i, 0)),
                  *_w_tile_specs(d, lambda j, i: jnp.where(j >= W_DIFF_V_TILE, j + 1, j)),
                  pl.BlockSpec((1, DIFF_DH), lambda j, i: (0, 0)),
                  pl.BlockSpec((1, DIFF_DH), lambda j, i: (0, 0)),
                  pl.BlockSpec((1, MEM_DH), lambda j, i: (0, 0)),
                  *r_specs],
        out_specs=(pl.BlockSpec((tm, IN_TILE), lambda j, i: (i, j)), *r_specs),
        scratch_shapes=[pltpu.VMEM((IN_TILE, d), BF16)],
        compiler_params=_params("arbitrary", "arbitrary"),
        name="in_proj",
    )(h, w_in_t, w_in_t, dq_g, dk_g, mq_g, *riders)


def _v_proj_t_kernel(h_ref, w_lo_ref, w_hi_ref, mq_ref, mk_ref, mv_ref, vt_ref, ym_ref, w_scr, *,
                     sub_rows):
    _cast_w_tile(w_lo_ref, w_hi_ref, w_scr, pl.program_id(0) == 0,
                 W_DIFF_V_TILE >= W_FIRST_SHIFTED_TILE)
    parts = [slice(r, r + sub_rows) for r in range(0, mq_ref.shape[0], sub_rows)]
    heads = [slice(hd * MEM_DH, (hd + 1) * MEM_DH) for hd in range(MEM_HEADS)]
    scores = {(r, hd): _nt_dot(mq_ref[rows, cols], mk_ref[:, cols])
              for r, rows in enumerate(parts) for hd, cols in enumerate(heads)}
    for r, rows in enumerate(parts):
        vt_ref[:, rows] = _nt_dot(w_scr[...], h_ref[rows, :]).astype(vt_ref.dtype)
        for hd, cols in enumerate(heads):
            s = scores[r, hd]
            e = jnp.exp(s - jnp.max(s, axis=-1, keepdims=True))
            p = (e / jnp.sum(e, axis=-1, keepdims=True)).astype(BF16)
            ym_ref[rows, cols] = jnp.dot(p, mv_ref[:, cols],
                                         preferred_element_type=F32).astype(ym_ref.dtype)


def _v_proj_t(h, w_in_t, z, kv, seq, n_mem, tm=1024, sub_rows=512):
    m, d = h.shape
    width = MEM_HEADS * MEM_DH
    tiles_per_batch = seq // tm
    return pl.pallas_call(
        functools.partial(_v_proj_t_kernel, sub_rows=sub_rows),
        out_shape=(jax.ShapeDtypeStruct((IN_TILE, m), BF16), jax.ShapeDtypeStruct((m, width), BF16)),
        grid=(m // tm,),
        in_specs=[pl.BlockSpec((tm, d), lambda i: (i, 0)),
                  *_w_tile_specs(d, lambda i: W_DIFF_V_TILE),
                  pl.BlockSpec((tm, width), lambda i: (i, Z_MEM_Q // width)),
                  pl.BlockSpec((n_mem, width), lambda i: (i // tiles_per_batch, 0)),
                  pl.BlockSpec((n_mem, width), lambda i: (i // tiles_per_batch, 1))],
        out_specs=(pl.BlockSpec((IN_TILE, tm), lambda i: (0, i)),
                   pl.BlockSpec((tm, width), lambda i: (i, 0))),
        scratch_shapes=[pltpu.VMEM((IN_TILE, d), BF16)],
        compiler_params=_params("arbitrary"),
        name="v_proj_t",
    )(h, w_in_t, w_in_t, z, kv, kv)


def _chunk_cumsum(x):
    row_in_chunk = lax.broadcasted_iota(jnp.int32, x.shape, 0) % CHUNK
    shift = 1
    while shift < CHUNK:
        x = x + jnp.where(row_in_chunk >= shift, pltpu.roll(x, shift, 0), 0.0)
        shift *= 2
    return x


def _gla_kernel(q_ref, k_ref, v_ref, g_ref, la_ref, ng_ref, o_ref, s_ref, *, ts):
    @pl.when(pl.program_id(2) == 0)
    def _():
        s_ref[...] = jnp.zeros_like(s_ref)

    bcum_all = _chunk_cumsum(la_ref[...])
    row = lax.broadcasted_iota(jnp.int32, (CHUNK, CHUNK), 0)
    col = lax.broadcasted_iota(jnp.int32, (CHUNK, CHUNK), 1)
    causal = row >= col

    n_chunks = ts // CHUNK
    chunk_rows = [slice(c * CHUNK, (c + 1) * CHUNK) for c in range(n_chunks)]

    lhs, incs, decays = [], [], []
    for rows in chunk_rows:
        bcum = bcum_all[rows]
        b_last = bcum[CHUNK - 1:CHUNK, :]
        q = q_ref[rows, :].astype(F32)
        k = k_ref[rows, :].astype(F32)
        grow = jnp.exp(bcum)
        decay_row = jnp.exp(b_last)
        k_shrunk = k / grow
        q_dec = (q * grow).astype(BF16)
        k_dec = k_shrunk.astype(BF16)
        k_tail = (k_shrunk * decay_row).astype(BF16)
        att = jnp.where(causal, _nt_dot(q_dec, k_dec), 0.0).astype(BF16)
        lhs.append(jnp.concatenate([q_dec, att], axis=1))
        incs.append(_tn_dot(k_tail, v_ref[rows, :]))
        decay_col = jnp.broadcast_to(decay_row, (GLA_DK, GLA_DK)).T
        decays.append(jnp.concatenate([decay_col] * (GLA_DV // GLA_DK), axis=1))

    state = s_ref[...]
    states = []
    for c in range(n_chunks):
        states.append(state.astype(BF16))
        state = decays[c] * state + incs[c]
    s_ref[...] = state

    for c, rows in enumerate(chunk_rows):
        o = jnp.dot(lhs[c], jnp.concatenate([states[c], v_ref[rows, :]], axis=0),
                    preferred_element_type=F32)
        gate = g_ref[rows, :].astype(F32)
        ms = jnp.mean(o * o, axis=-1, keepdims=True)
        normed = o * lax.rsqrt(ms + NORM_EPS * GLA_DK) * ng_ref[...]
        o_ref[rows, :] = (normed * (gate * _sigmoid(gate))).astype(o_ref.dtype)


def _gla(z, log_a, ng, batch, seq, ts=2048):
    m = z.shape[0]
    nt = seq // ts
    rows = lambda b, h, t: b * nt + t
    return pl.pallas_call(
        functools.partial(_gla_kernel, ts=ts),
        out_shape=jax.ShapeDtypeStruct((m, GLA_HEADS * GLA_DV), BF16),
        grid=(batch, GLA_HEADS, nt),
        in_specs=[pl.BlockSpec((ts, GLA_DK), lambda b, h, t: (rows(b, h, t), Z_GLA_Q // GLA_DK + h)),
                  pl.BlockSpec((ts, GLA_DK), lambda b, h, t: (rows(b, h, t), Z_GLA_K // GLA_DK + h)),
                  pl.BlockSpec((ts, GLA_DV), lambda b, h, t: (rows(b, h, t), Z_GLA_V // GLA_DV + h)),
                  pl.BlockSpec((ts, GLA_DV), lambda b, h, t: (rows(b, h, t), Z_GLA_G // GLA_DV + h)),
                  pl.BlockSpec((ts, GLA_DK), lambda b, h, t: (rows(b, h, t), h)),
                  pl.BlockSpec((1, GLA_DV), lambda b, h, t: (0, 0))],
        out_specs=pl.BlockSpec((ts, GLA_DV), lambda b, h, t: (rows(b, h, t), h)),
        scratch_shapes=[pltpu.VMEM((GLA_DK, GLA_DV), F32)],
        compiler_params=_params("parallel", "parallel", "arbitrary"),
        name="gla",
    )(z, z, z, z, log_a, ng)


def _diff_kernel(q_ref, k_ref, vt_ref, qg_ref, kg_ref, lq1_ref, lk1_ref, lq2_ref, lk2_ref, sg_ref,
                 o_ref, acc_scr, m_scr, l_scr, *, tq):
    qi = pl.program_id(2)
    m_scr[...] = jnp.full_like(m_scr, NEG_INF)
    l_scr[...] = jnp.zeros_like(l_scr)
    acc_scr[...] = jnp.zeros_like(acc_scr)
    n_streams = q_ref.shape[1] // DIFF_DH
    half = tq // 2

    score_bound = (1.02 * DIFF_DH * DIFF_Q_SCALE) * (jnp.max(jnp.abs(qg_ref[...]))
                                                    * jnp.max(jnp.abs(kg_ref[...])))
    bounded = score_bound <= SCORE_LIMIT

    def update(c, lanes, s, vt, shifted):
        if shifted:
            m_old = m_scr[c, :, lanes]
            m_new = jnp.maximum(m_old, jnp.max(s, axis=0, keepdims=True))
            alpha = jnp.exp2(m_old - m_new)
            p = jnp.exp2(s - m_new)
            l_scr[c, :, lanes] = alpha * l_scr[c, :, lanes] + jnp.sum(p, axis=0, keepdims=True)
            m_scr[c, :, lanes] = m_new
            acc_scr[c, :, lanes] = (alpha * acc_scr[c, :, lanes]
                                    + jnp.dot(vt, p.astype(BF16), preferred_element_type=F32))
        else:
            p = jnp.exp2(s)
            l_scr[c, :, lanes] = l_scr[c, :, lanes] + jnp.sum(p, axis=0, keepdims=True)
            acc_scr[c, :, lanes] = (acc_scr[c, :, lanes]
                                    + jnp.dot(vt, p.astype(BF16), preferred_element_type=F32))

    def values_t(c, start, n):
        head = c // 2
        return vt_ref[head * DIFF_DV:(head + 1) * DIFF_DV, pl.ds(start, n)]

    def full_block(kb, carry, shifted):
        start = pl.multiple_of(kb * tq, tq)
        scores = []
        for c in range(n_streams):
            cols = slice(c * DIFF_DH, (c + 1) * DIFF_DH)
            scores.append(_nt_dot(k_ref[pl.ds(start, tq), cols], q_ref[:, cols]))
        for c in range(n_streams):
            update(c, slice(0, tq), scores[c], values_t(c, start, tq), shifted)
        return carry

    def diag_block(kb, shifted):
        lo = pl.multiple_of(kb * tq, tq)
        hi = pl.multiple_of(kb * tq + half, half)
        key_chunk = lax.broadcasted_iota(jnp.int32, (half, half), 0) // CHUNK
        query_chunk = lax.broadcasted_iota(jnp.int32, (half, half), 1) // CHUNK
        visible = key_chunk <= query_chunk
        def stream_scores(c):
            cols = slice(c * DIFF_DH, (c + 1) * DIFF_DH)
            k_lo, k_hi = k_ref[pl.ds(lo, half), cols], k_ref[pl.ds(hi, half), cols]
            q_lo, q_hi = q_ref[:half, cols], q_ref[half:, cols]
            s_lo = jnp.where(visible, _nt_dot(k_lo, q_lo), NEG_INF)
            s_hi = jnp.concatenate([_nt_dot(k_lo, q_hi),
                                    jnp.where(visible, _nt_dot(k_hi, q_hi), NEG_INF)], axis=0)
            return s_lo, s_hi

        scores = [stream_scores(c) for c in range(SCORES_AHEAD)]
        lam = (jnp.exp(jnp.sum(lq1_ref[...] * lk1_ref[...], axis=-1, keepdims=True))
               - jnp.exp(jnp.sum(lq2_ref[...] * lk2_ref[...], axis=-1, keepdims=True)) + LAM_INIT)
        for head in range(n_streams // 2):
            c1, c2 = 2 * head, 2 * head + 1
            for c in (c1, c2):
                if c + SCORES_AHEAD < n_streams:
                    scores.append(stream_scores(c + SCORES_AHEAD))
                update(c, slice(0, half), scores[c][0], values_t(c, lo, half), shifted)
                update(c, slice(half, tq), scores[c][1], values_t(c, lo, tq), shifted)
            o_t = (acc_scr[c1] * (1.0 / l_scr[c1])
                   - acc_scr[c2] * (lam / l_scr[c2]))
            o_ref[:, head * DIFF_DV:(head + 1) * DIFF_DV] = (
                _rms(o_t.T, sg_ref[...]) * (1.0 - LAM_INIT)).astype(o_ref.dtype)

    def all_blocks(shifted):
        lax.fori_loop(0, qi, functools.partial(full_block, shifted=shifted), 0)
        diag_block(qi, shifted)

    @pl.when(bounded)
    def _():
        all_blocks(shifted=False)

    @pl.when(jnp.logical_not(bounded))
    def _():
        all_blocks(shifted=True)


def _diff_attn(z, v_t, qg, kg, lq1, lk1, lq2, lk2, sg, batch, seq, tq=512, heads_per_step=4):
    m = z.shape[0]
    nq = seq // tq
    width = heads_per_step * DIFF_DV
    ns = 2 * heads_per_step
    vec = pl.BlockSpec((1, DIFF_DH), lambda b, g, i: (0, 0))
    return pl.pallas_call(
        functools.partial(_diff_kernel, tq=tq),
        out_shape=jax.ShapeDtypeStruct((m, DIFF_HEADS * DIFF_DV), BF16),
        grid=(batch, DIFF_HEADS // heads_per_step, nq),
        in_specs=[pl.BlockSpec((tq, width), lambda b, g, i: (b * nq + i, Z_DIFF_Q // width + g)),
                  pl.BlockSpec((seq, width), lambda b, g, i: (b, Z_DIFF_K // width + g)),
                  pl.BlockSpec((width, seq), lambda b, g, i: (g, b)),
                  vec, vec, vec, vec, vec, vec,
                  pl.BlockSpec((1, DIFF_DV), lambda b, g, i: (0, 0))],
        out_specs=pl.BlockSpec((tq, width), lambda b, g, i: (b * nq + i, g)),
        scratch_shapes=[pltpu.VMEM((ns, DIFF_DV, tq), F32),
                        pltpu.VMEM((ns, 1, tq), F32),
                        pltpu.VMEM((ns, 1, tq), F32)],
        compiler_params=_params("parallel", "parallel", "arbitrary"),
        name="diff_attn",
    )(z, z, v_t, qg, kg, lq1, lk1, lq2, lk2, sg)


def _mem_kv_kernel(mem_ref, ng_ref, w_ref, kg_ref, kv_ref, mn_scr, *, n_key_tiles):
    j = pl.program_id(0)

    @pl.when(j == 0)
    def _():
        mn_scr[...] = _rms(mem_ref[...], ng_ref[...]).astype(BF16)

    acc = jnp.dot(mn_scr[...], w_ref[...].astype(BF16), preferred_element_type=F32)

    @pl.when(j < n_key_tiles)
    def _():
        _store_group_norm(acc, kg_ref[...], MEM_DH, 1.0, kv_ref)

    @pl.when(j >= n_key_tiles)
    def _():
        kv_ref[...] = acc.astype(kv_ref.dtype)


def _mem_kv(mem2, ng, w_kv, kg, tn=512):
    m, d = mem2.shape
    n = w_kv.shape[1]
    return pl.pallas_call(
        functools.partial(_mem_kv_kernel, n_key_tiles=(n // 2) // tn),
        out_shape=jax.ShapeDtypeStruct((m, n), BF16),
        grid=(n // tn,),
        in_specs=[pl.BlockSpec((m, d), lambda j: (0, 0)),
                  pl.BlockSpec((1, d), lambda j: (0, 0)),
                  pl.BlockSpec((d, tn), lambda j: (0, j)),
                  pl.BlockSpec((1, MEM_DH), lambda j: (0, 0))],
        out_specs=pl.BlockSpec((m, tn), lambda j: (0, j)),
        scratch_shapes=[pltpu.VMEM((m, d), BF16)],
        compiler_params=_params("arbitrary"),
        name="mem_kv",
    )(mem2, ng, w_kv, kg)


def _gate_merge_kernel(h_ref, y0_ref, y1_ref, y2_ref, wg_ref, bg0_ref, bg1_ref, bg2_ref,
                       wb0_ref, wb1_ref, wb2_ref, r0_ref, r1_ref, o_ref, c0_ref, c1_ref):
    _cast_riders((r0_ref,), (c0_ref,))
    _cast_ffn_in_rider(r1_ref, c1_ref)
    bias = jnp.concatenate([bg0_ref[...], bg1_ref[...], bg2_ref[...]], axis=1)
    gates = _sigmoid(jnp.dot(h_ref[...], wg_ref[...], preferred_element_type=F32) + bias)
    merged = None
    for b, (y_ref, wb_ref) in enumerate(((y0_ref, wb0_ref), (y1_ref, wb1_ref), (y2_ref, wb2_ref))):
        term = (gates[:, b * GATE_TILE:(b + 1) * GATE_TILE]
                * jnp.dot(y_ref[...], wb_ref[...], preferred_element_type=F32))
        merged = term if merged is None else merged + term
    o_ref[...] = merged.astype(o_ref.dtype)


def _gate_merge(h, ys, w_gate_tiled, b_gate, w_branch, riders, tm=1024):
    m, d = h.shape
    bw = w_branch.shape[1]
    tn = GATE_TILE
    ni, nj = m // tm, d // tn
    act = lambda width: pl.BlockSpec((tm, width), lambda i, j: (i, 0))
    bg = lambda b: pl.BlockSpec((1, tn), lambda i, j: (0, b * nj + j))
    wb = lambda b: pl.BlockSpec((None, bw, tn), lambda i, j: (b, 0, j))
    r_specs, r_shapes = _rider_specs(riders, ni * nj, lambda i, j: i * nj + j)
    return pl.pallas_call(
        _gate_merge_kernel,
        out_shape=(jax.ShapeDtypeStruct((m, d), BF16), *r_shapes),
        grid=(ni, nj),
        in_specs=[act(d), act(bw), act(bw), act(bw),
                  pl.BlockSpec((d, N_BRANCH * tn), lambda i, j: (0, j)),
                  bg(0), bg(1), bg(2), wb(0), wb(1), wb(2), *r_specs],
        out_specs=(pl.BlockSpec((tm, tn), lambda i, j: (i, j)), *r_specs),
        compiler_params=_params("arbitrary", "arbitrary"),
        name="gate_merge",
    )(h, ys[0], ys[1], ys[2], w_gate_tiled, b_gate, b_gate, b_gate,
      w_branch, w_branch, w_branch, *riders)


def _out_proj_kernel(mg_ref, w_ref, x_ref, g_ref, x1_ref, hf_ref):
    x1 = x_ref[...] + jnp.dot(mg_ref[...], w_ref[...], preferred_element_type=F32)
    x1_ref[...] = x1
    hf_ref[...] = _rms(x1, g_ref[...]).astype(hf_ref.dtype)


def _out_proj(merged, w_out, x2, g, tm=512):
    m, d = x2.shape
    row = lambda i: (i, 0)
    fixed = lambda i: (0, 0)
    return pl.pallas_call(
        _out_proj_kernel,
        out_shape=(jax.ShapeDtypeStruct((m, d), F32), jax.ShapeDtypeStruct((m, d), BF16)),
        grid=(m // tm,),
        in_specs=[pl.BlockSpec((tm, d), row),
                  pl.BlockSpec((d, d), fixed, pipeline_mode=pl.Buffered(1)),
                  pl.BlockSpec((tm, d), row), pl.BlockSpec((1, d), fixed)],
        out_specs=(pl.BlockSpec((tm, d), row), pl.BlockSpec((tm, d), row)),
        compiler_params=_params("parallel"),
        name="out_proj",
    )(merged, w_out, x2, g)


FFN_TILE = 512


def _cast_ffn_in_rider(src, dst):
    d_ff = src.shape[1] // 2
    for j in range(d_ff // FFN_TILE):
        for part in range(2):
            dst_col = (2 * j + part) * FFN_TILE
            src_col = part * d_ff + j * FFN_TILE
            dst[:, dst_col:dst_col + FFN_TILE] = src[:, src_col:src_col + FFN_TILE].astype(BF16)


def _ffn_up_kernel(hf_ref, w_ref, r0_ref, a_ref, c0_ref):
    _cast_riders((r0_ref,), (c0_ref,))
    both = jnp.dot(hf_ref[...], w_ref[...], preferred_element_type=F32)
    gate, up = both[:, :FFN_TILE], both[:, FFN_TILE:]
    a_ref[...] = (gate * _sigmoid(gate) * up).astype(a_ref.dtype)


def _ffn_up(hf, w_in_tiled, riders, tm=1024):
    m, d = hf.shape
    d_ff = w_in_tiled.shape[1] // 2
    tf = FFN_TILE
    ni, nj = m // tm, d_ff // tf
    r_specs, r_shapes = _rider_specs(riders, ni * nj, lambda i, j: i * nj + j)
    return pl.pallas_call(
        _ffn_up_kernel,
        out_shape=(jax.ShapeDtypeStruct((m, d_ff), BF16), *r_shapes),
        grid=(ni, nj),
        in_specs=[pl.BlockSpec((tm, d), lambda i, j: (i, 0)),
                  pl.BlockSpec((d, 2 * tf), lambda i, j: (0, j)),
                  *r_specs],
        out_specs=(pl.BlockSpec((tm, tf), lambda i, j: (i, j)), *r_specs),
        compiler_params=_params("arbitrary", "arbitrary"),
        name="ffn_up",
    )(hf, w_in_tiled, *riders)


def _ffn_down_kernel(a_ref, w_ref, x1_ref, o_ref):
    o_ref[...] = x1_ref[...] + jnp.dot(a_ref[...], w_ref[...], preferred_element_type=F32)


def _ffn_down(a, w_down, x1, tm=1024, tn=512):
    m, d_ff = a.shape
    d = w_down.shape[1]
    return pl.pallas_call(
        _ffn_down_kernel,
        out_shape=jax.ShapeDtypeStruct((m, d), F32),
        grid=(m // tm, d // tn),
        in_specs=[pl.BlockSpec((tm, d_ff), lambda i, j: (i, 0)),
                  pl.BlockSpec((d_ff, tn), lambda i, j: (0, j)),
                  pl.BlockSpec((tm, tn), lambda i, j: (i, j))],
        out_specs=pl.BlockSpec((tm, tn), lambda i, j: (i, j)),
        compiler_params=_params("parallel", "arbitrary"),
        name="ffn_down",
    )(a, w_down, x1)


def kernel(x, mem, norm_mix_g, norm_mem_g, w_in, gla_w_alpha_up, gla_b_alpha, gla_norm_g,
           diff_q_norm_g, diff_k_norm_g, diff_lambda_q1, diff_lambda_k1, diff_lambda_q2,
           diff_lambda_k2, diff_subln_g, mem_q_norm_g, mem_k_norm_g, w_mem_kv, w_branch,
           w_gate, b_gate, w_out, norm_ffn_g, w_ffn_in, w_ffn_down):
    batch, seq, d = x.shape
    n_mem = mem.shape[1]
    depth = w_in.shape[0]
    assert depth == 1, "LAM_INIT is the layer-0 value"
    x2 = x.reshape(batch * seq, d)
    mem2 = mem.reshape(batch * n_mem, d)
    for l in range(depth):
        w_in_t = w_in[l].T
        w_up = jnp.pad(gla_w_alpha_up[l], ((0, LANES - GLA_RANK), (0, 0)))
        row = lambda v: v.reshape(1, -1)

        h, log_a = _norm_mix(x2, row(norm_mix_g[l]), w_in_t, w_up, row(gla_b_alpha[l]))
        z, w_gate_bf, w_branch_bf = _in_proj(
            h, w_in_t, row(diff_q_norm_g[l]), row(diff_k_norm_g[l]), row(mem_q_norm_g[l]),
            riders=(w_gate[l], w_branch[l].reshape(-1, d)))
        y_gla = _gla(z, log_a, row(gla_norm_g[l]), batch, seq)
        kv = _mem_kv(mem2, row(norm_mem_g[l]), w_mem_kv[l], row(mem_k_norm_g[l]))
        v_t, y_mem = _v_proj_t(h, w_in_t, z, kv, seq, n_mem)
        y_diff = _diff_attn(z, v_t, row(diff_q_norm_g[l]), row(diff_k_norm_g[l]),
                            row(diff_lambda_q1[l]), row(diff_lambda_k1[l]),
                            row(diff_lambda_q2[l]), row(diff_lambda_k2[l]),
                            row(diff_subln_g[l]), batch, seq)
        merged, w_out_bf, w_ffn_in_bf = _gate_merge(
            h, (y_gla, y_diff, y_mem), w_gate_bf, row(b_gate[l]),
            w_branch_bf.reshape(w_branch[l].shape), riders=(w_out[l], w_ffn_in[l]))
        x1, hf = _out_proj(merged, w_out_bf, x2, row(norm_ffn_g[l]))
        a, w_ffn_down_bf = _ffn_up(hf, w_ffn_in_bf, riders=(w_ffn_down[l],))
        x2 = _ffn_down(a, w_ffn_down_bf, x1)
    return x2.reshape(batch, seq, d)
```

```python
import functools

import jax
import jax.numpy as jnp
from jax import lax
from jax.experimental import pallas as pl
from jax.experimental.pallas import tpu as pltpu

F32 = jnp.float32
BF16 = jnp.bfloat16

CHUNK = 64
GLA_HEADS = 4
GLA_DK = 128
GLA_DV = 256
GLA_RANK = 16
GLA_GATE_NORM = 16.0
DIFF_HEADS = 4
DIFF_DH = 128
DIFF_DV = 256
MEM_HEADS = 4
MEM_DH = 256
N_BRANCH = 3
NORM_EPS = 1e-6
NEG_INF = -1e30
LAM_INIT = 0.8 - 0.6 * 1.0
LOG2_E = 1.4426950408889634
SCORE_LIMIT = 40.0
SCORES_AHEAD = 2
DIFF_Q_SCALE = DIFF_DH ** -0.5 * LOG2_E

LANES = 128
VMEM_LIMIT = 56 * 1024 * 1024

IN_TILE = 1024
Z_GLA_Q, Z_GLA_K, Z_GLA_V, Z_GLA_G = 0, 512, 1024, 2048
Z_DIFF_Q, Z_DIFF_K, Z_MEM_Q = 3072, 4096, 5120
Z_WIDTH = 6144
W_SRC_TILES = (0, 1, 2, 3, 4, 6)
W_DIFF_V_TILE = 5
W_ROWS = 7 * IN_TILE + GLA_RANK
W_FIRST_SHIFTED_TILE = 3


def _params(*sem):
    return pltpu.CompilerParams(dimension_semantics=sem, vmem_limit_bytes=VMEM_LIMIT)


def _nt_dot(a, b):
    return lax.dot_general(a, b, (((1,), (1,)), ((), ())), preferred_element_type=F32)


def _tn_dot(a, b):
    return lax.dot_general(a, b, (((0,), (0,)), ((), ())), preferred_element_type=F32)


def _rms(v, gain):
    ms = jnp.mean(v * v, axis=-1, keepdims=True)
    return v * lax.rsqrt(ms + NORM_EPS) * gain


def _sigmoid(v):
    return 1.0 / (1.0 + jnp.exp(-v))


BF16_SUBLANES = 16


def _rider_specs(weights, n_steps, step_of):
    specs, shapes = [], []
    for w in weights:
        rows, cols = w.shape
        chunk = BF16_SUBLANES
        while rows % chunk or rows // chunk > n_steps:
            chunk += BF16_SUBLANES
        last = rows // chunk - 1
        specs.append(pl.BlockSpec((chunk, cols),
                                  lambda *g, last=last: (jnp.minimum(step_of(*g), last), 0)))
        shapes.append(jax.ShapeDtypeStruct((rows, cols), BF16))
    return specs, shapes


def _cast_riders(in_refs, out_refs):
    for src, dst in zip(in_refs, out_refs):
        dst[...] = src[...].astype(BF16)


GATE_TILE = 256


def _tile_major_gate_cols(n_cols):
    per_branch = n_cols // N_BRANCH
    return [((j * N_BRANCH + b) * GATE_TILE, b * per_branch + j * GATE_TILE)
            for j in range(per_branch // GATE_TILE) for b in range(N_BRANCH)]


def _cast_gate_rider(src, dst):
    for dst_col, src_col in _tile_major_gate_cols(src.shape[1]):
        dst[:, dst_col:dst_col + GATE_TILE] = src[:, src_col:src_col + GATE_TILE].astype(BF16)


def _norm_mix_kernel(x_ref, g_ref, wa_ref, wup_ref, bal_ref, h_ref, la_ref):
    h = _rms(x_ref[...], g_ref[...]).astype(BF16)
    h_ref[...] = h
    a_low = _nt_dot(h, wa_ref[...].astype(BF16))
    w_up = wup_ref[...]
    a_hi, w_hi = a_low.astype(BF16), w_up.astype(BF16)
    a_lo = (a_low - a_hi.astype(F32)).astype(BF16)
    w_lo = (w_up - w_hi.astype(F32)).astype(BF16)
    pre = (jnp.dot(a_hi, w_hi, preferred_element_type=F32)
           + (jnp.dot(a_lo, w_hi, preferred_element_type=F32)
              + jnp.dot(a_hi, w_lo, preferred_element_type=F32))) + bal_ref[...]
    log_sig = jnp.minimum(pre, 0.0) - jnp.log1p(jnp.exp(-jnp.abs(pre)))
    la_ref[...] = log_sig * (1.0 / GLA_GATE_NORM)


def _norm_mix(x2, g, w_in_t, wup, bal, tr=1024):
    m, d = x2.shape
    n = wup.shape[1]
    fixed = lambda i: (0, 0)
    decay_block = (Z_GLA_G + GLA_HEADS * GLA_DV) // LANES
    return pl.pallas_call(
        _norm_mix_kernel,
        out_shape=(jax.ShapeDtypeStruct((m, d), BF16), jax.ShapeDtypeStruct((m, n), F32)),
        grid=(m // tr,),
        in_specs=[pl.BlockSpec((tr, d), lambda i: (i, 0)),
                  pl.BlockSpec((1, d), fixed),
                  pl.BlockSpec((LANES, d), lambda i: (decay_block, 0)),
                  pl.BlockSpec((LANES, n), fixed),
                  pl.BlockSpec((1, n), fixed)],
        out_specs=(pl.BlockSpec((tr, d), lambda i: (i, 0)),
                   pl.BlockSpec((tr, n), lambda i: (i, 0))),
        compiler_params=_params("parallel"),
        name="norm_mix",
    )(x2, g, w_in_t, wup, bal)


def _store_group_norm(acc, gain, width, scale, out_ref):
    for s in range(0, acc.shape[1], width):
        blk = acc[:, s:s + width]
        out_ref[:, s:s + width] = (_rms(blk, gain) * scale).astype(out_ref.dtype)


def _w_tile_specs(d, src_tile):
    hi_per_tile = IN_TILE // GLA_RANK
    return [pl.BlockSpec((IN_TILE, d), lambda *g: (src_tile(*g), 0)),
            pl.BlockSpec((GLA_RANK, d), lambda *g: ((src_tile(*g) + 1) * hi_per_tile, 0))]


def _cast_w_tile(w_lo_ref, w_hi_ref, w_scr, first_step, shifted):
    @pl.when(first_step & jnp.logical_not(shifted))
    def _():
        w_scr[...] = w_lo_ref[...].astype(BF16)

    @pl.when(first_step & shifted)
    def _():
        w_scr[:IN_TILE - GLA_RANK, :] = w_lo_ref[GLA_RANK:, :].astype(BF16)
        w_scr[IN_TILE - GLA_RANK:, :] = w_hi_ref[...].astype(BF16)


def _in_proj_kernel(h_ref, w_lo_ref, w_hi_ref, dq_g_ref, dk_g_ref, mq_g_ref, r0_ref, r1_ref,
                    z_ref, c0_ref, c1_ref, w_scr):
    _cast_gate_rider(r0_ref, c0_ref)
    _cast_riders((r1_ref,), (c1_ref,))
    j = pl.program_id(0)
    _cast_w_tile(w_lo_ref, w_hi_ref, w_scr, pl.program_id(1) == 0, j >= W_FIRST_SHIFTED_TILE)
    j_dq, j_dk, j_mq = Z_DIFF_Q // IN_TILE, Z_DIFF_K // IN_TILE, Z_MEM_Q // IN_TILE

    def tile(epilogue, row_parts):
        part = h_ref.shape[0] // row_parts
        for r in range(row_parts):
            rows = slice(r * part, (r + 1) * part)
            epilogue(_nt_dot(h_ref[rows, :], w_scr[...]), z_ref.at[rows, :])

    def plain(acc, out_ref):
        out_ref[...] = acc.astype(out_ref.dtype)

    @pl.when((j != j_dq) & (j != j_dk) & (j != j_mq))
    def _():
        tile(plain, 1)

    @pl.when((j == j_dq) | (j == j_dk))
    def _():
        gain = jnp.where(j == j_dq, dq_g_ref[...] * DIFF_Q_SCALE, dk_g_ref[...])
        tile(lambda acc, out: _store_group_norm(acc, gain, DIFF_DH, 1.0, out), 4)

    @pl.when(j == j_mq)
    def _():
        tile(lambda acc, out: _store_group_norm(acc, mq_g_ref[...], MEM_DH, MEM_DH ** -0.5, out), 4)


def _in_proj(h, w_in_t, dq_g, dk_g, mq_g, riders, tm=1024):
    m, d = h.shape
    assert w_in_t.shape[0] == W_ROWS
    nj, ni = Z_WIDTH // IN_TILE, m // tm
    assert W_SRC_TILES == tuple(j + (j >= W_DIFF_V_TILE) for j in range(nj))
    r_specs, r_shapes = _rider_specs(riders, nj * ni, lambda j, i: j * ni + i)
    return pl.pallas_call(
        _in_proj_kernel,
        out_shape=(jax.ShapeDtypeStruct((m, Z_WIDTH), BF16), *r_shapes),
        grid=(nj, ni),
        in_specs=[pl.BlockSpec((tm, d), lambda j, i: (i, 0)),
                  *_w_tile_specs(d, lambda j, i: jnp.where(j >= W_DIFF_V_TILE, j + 1, j)),
                  pl.BlockSpec((1, DIFF_DH), lambda j, i: (0, 0)),
                  pl.BlockSpec((1, DIFF_DH), lambda j, i: (0, 0)),
                  pl.BlockSpec((1, MEM_DH), lambda j, i: (0, 0)),
                  *r_specs],
        out_specs=(pl.BlockSpec((tm, IN_TILE), lambda j, i: (i, j)), *r_specs),
        scratch_shapes=[pltpu.VMEM((IN_TILE, d), BF16)],
        compiler_params=_params("arbitrary", "arbitrary"),
        name="in_proj",
    )(h, w_in_t, w_in_t, dq_g, dk_g, mq_g, *riders)


def _v_proj_t_kernel(h_ref, w_lo_ref, w_hi_ref, mq_ref, mk_ref, mv_ref, vt_ref, ym_ref, w_scr, *,
                     sub_rows):
    _cast_w_tile(w_lo_ref, w_hi_ref, w_scr, pl.program_id(0) == 0,
                 W_DIFF_V_TILE >= W_FIRST_SHIFTED_TILE)
    parts = [slice(r, r + sub_rows) for r in range(0, mq_ref.shape[0], sub_rows)]
    heads = [slice(hd * MEM_DH, (hd + 1) * MEM_DH) for hd in range(MEM_HEADS)]
    scores = {(r, hd): _nt_dot(mq_ref[rows, cols], mk_ref[:, cols])
              for r, rows in enumerate(parts) for hd, cols in enumerate(heads)}
    for r, rows in enumerate(parts):
        vt_ref[:, rows] = _nt_dot(w_scr[...], h_ref[rows, :]).astype(vt_ref.dtype)
        for hd, cols in enumerate(heads):
            s = scores[r, hd]
            e = jnp.exp(s - jnp.max(s, axis=-1, keepdims=True))
            p = (e / jnp.sum(e, axis=-1, keepdims=True)).astype(BF16)
            ym_ref[rows, cols] = jnp.dot(p, mv_ref[:, cols],
                                         preferred_element_type=F32).astype(ym_ref.dtype)


def _v_proj_t(h, w_in_t, z, kv, seq, n_mem, tm=1024, sub_rows=512):
    m, d = h.shape
    width = MEM_HEADS * MEM_DH
    tiles_per_batch = seq // tm
    return pl.pallas_call(
        functools.partial(_v_proj_t_kernel, sub_rows=sub_rows),
        out_shape=(jax.ShapeDtypeStruct((IN_TILE, m), BF16), jax.ShapeDtypeStruct((m, width), BF16)),
        grid=(m // tm,),
        in_specs=[pl.BlockSpec((tm, d), lambda i: (i, 0)),
                  *_w_tile_specs(d, lambda i: W_DIFF_V_TILE),
                  pl.BlockSpec((tm, width), lambda i: (i, Z_MEM_Q // width)),
                  pl.BlockSpec((n_mem, width), lambda i: (i // tiles_per_batch, 0)),
                  pl.BlockSpec((n_mem, width), lambda i: (i // tiles_per_batch, 1))],
        out_specs=(pl.BlockSpec((IN_TILE, tm), lambda i: (0, i)),
                   pl.BlockSpec((tm, width), lambda i: (i, 0))),
        scratch_shapes=[pltpu.VMEM((IN_TILE, d), BF16)],
        compiler_params=_params("arbitrary"),
        name="v_proj_t",
    )(h, w_in_t, w_in_t, z, kv, kv)


def _chunk_cumsum(x):
    row_in_chunk = lax.broadcasted_iota(jnp.int32, x.shape, 0) % CHUNK
    shift = 1
    while shift < CHUNK:
        x = x + jnp.where(row_in_chunk >= shift, pltpu.roll(x, shift, 0), 0.0)
        shift *= 2
    return x


def _gla_kernel(q_ref, k_ref, v_ref, g_ref, la_ref, ng_ref, o_ref, s_ref, *, ts):
    @pl.when(pl.program_id(2) == 0)
    def _():
        s_ref[...] = jnp.zeros_like(s_ref)

    bcum_all = _chunk_cumsum(la_ref[...])
    row = lax.broadcasted_iota(jnp.int32, (CHUNK, CHUNK), 0)
    col = lax.broadcasted_iota(jnp.int32, (CHUNK, CHUNK), 1)
    causal = row >= col

    n_chunks = ts // CHUNK
    chunk_rows = [slice(c * CHUNK, (c + 1) * CHUNK) for c in range(n_chunks)]

    lhs, incs, decays = [], [], []
    for rows in chunk_rows:
        bcum = bcum_all[rows]
        b_last = bcum[CHUNK - 1:CHUNK, :]
        q = q_ref[rows, :].astype(F32)
        k = k_ref[rows, :].astype(F32)
        grow = jnp.exp(bcum)
        decay_row = jnp.exp(b_last)
        k_shrunk = k / grow
        q_dec = (q * grow).astype(BF16)
        k_dec = k_shrunk.astype(BF16)
        k_tail = (k_shrunk * decay_row).astype(BF16)
        att = jnp.where(causal, _nt_dot(q_dec, k_dec), 0.0).astype(BF16)
        lhs.append(jnp.concatenate([q_dec, att], axis=1))
        incs.append(_tn_dot(k_tail, v_ref[rows, :]))
        decay_col = jnp.broadcast_to(decay_row, (GLA_DK, GLA_DK)).T
        decays.append(jnp.concatenate([decay_col] * (GLA_DV // GLA_DK), axis=1))

    state = s_ref[...]
    states = []
    for c in range(n_chunks):
        states.append(state.astype(BF16))
        state = decays[c] * state + incs[c]
    s_ref[...] = state

    for c, rows in enumerate(chunk_rows):
        o = jnp.dot(lhs[c], jnp.concatenate([states[c], v_ref[rows, :]], axis=0),
                    preferred_element_type=F32)
        gate = g_ref[rows, :].astype(F32)
        ms = jnp.mean(o * o, axis=-1, keepdims=True)
        normed = o * lax.rsqrt(ms + NORM_EPS * GLA_DK) * ng_ref[...]
        o_ref[rows, :] = (normed * (gate * _sigmoid(gate))).astype(o_ref.dtype)


def _gla(z, log_a, ng, batch, seq, ts=2048):
    m = z.shape[0]
    nt = seq // ts
    rows = lambda b, h, t: b * nt + t
    return pl.pallas_call(
        functools.partial(_gla_kernel, ts=ts),
        out_shape=jax.ShapeDtypeStruct((m, GLA_HEADS * GLA_DV), BF16),
        grid=(batch, GLA_HEADS, nt),
        in_specs=[pl.BlockSpec((ts, GLA_DK), lambda b, h, t: (rows(b, h, t), Z_GLA_Q // GLA_DK + h)),
                  pl.BlockSpec((ts, GLA_DK), lambda b, h, t: (rows(b, h, t), Z_GLA_K // GLA_DK + h)),
                  pl.BlockSpec((ts, GLA_DV), lambda b, h, t: (rows(b, h, t), Z_GLA_V // GLA_DV + h)),
                  pl.BlockSpec((ts, GLA_DV), lambda b, h, t: (rows(b, h, t), Z_GLA_G // GLA_DV + h)),
                  pl.BlockSpec((ts, GLA_DK), lambda b, h, t: (rows(b, h, t), h)),
                  pl.BlockSpec((1, GLA_DV), lambda b, h, t: (0, 0))],
        out_specs=pl.BlockSpec((ts, GLA_DV), lambda b, h, t: (rows(b, h, t), h)),
        scratch_shapes=[pltpu.VMEM((GLA_DK, GLA_DV), F32)],
        compiler_params=_params("parallel", "parallel", "arbitrary"),
        name="gla",
    )(z, z, z, z, log_a, ng)


def _diff_kernel(q_ref, k_ref, vt_ref, qg_ref, kg_ref, lq1_ref, lk1_ref, lq2_ref, lk2_ref, sg_ref,
                 o_ref, acc_scr, m_scr, l_scr, *, tq):
    qi = pl.program_id(2)
    m_scr[...] = jnp.full_like(m_scr, NEG_INF)
    l_scr[...] = jnp.zeros_like(l_scr)
    acc_scr[...] = jnp.zeros_like(acc_scr)
    n_streams = q_ref.shape[1] // DIFF_DH
    half = tq // 2

    score_bound = (1.02 * DIFF_DH * DIFF_Q_SCALE) * (jnp.max(jnp.abs(qg_ref[...]))
                                                    * jnp.max(jnp.abs(kg_ref[...])))
    bounded = score_bound <= SCORE_LIMIT

    def update(c, lanes, s, vt, shifted):
        if shifted:
            m_old = m_scr[c, :, lanes]
            m_new = jnp.maximum(m_old, jnp.max(s, axis=0, keepdims=True))
            alpha = jnp.exp2(m_old - m_new)
            p = jnp.exp2(s - m_new)
            l_scr[c, :, lanes] = alpha * l_scr[c, :, lanes] + jnp.sum(p, axis=0, keepdims=True)
            m_scr[c, :, lanes] = m_new
            acc_scr[c, :, lanes] = (alpha * acc_scr[c, :, lanes]
                                    + jnp.dot(vt, p.astype(BF16), preferred_element_type=F32))
        else:
            p = jnp.exp2(s)
            l_scr[c, :, lanes] = l_scr[c, :, lanes] + jnp.sum(p, axis=0, keepdims=True)
            acc_scr[c, :, lanes] = (acc_scr[c, :, lanes]
                                    + jnp.dot(vt, p.astype(BF16), preferred_element_type=F32))

    def values_t(c, start, n):
        head = c // 2
        return vt_ref[head * DIFF_DV:(head + 1) * DIFF_DV, pl.ds(start, n)]

    def full_block(kb, carry, shifted):
        start = pl.multiple_of(kb * tq, tq)
        scores = []
        for c in range(n_streams):
            cols = slice(c * DIFF_DH, (c + 1) * DIFF_DH)
            scores.append(_nt_dot(k_ref[pl.ds(start, tq), cols], q_ref[:, cols]))
        for c in range(n_streams):
            update(c, slice(0, tq), scores[c], values_t(c, start, tq), shifted)
        return carry

    def diag_block(kb, shifted):
        lo = pl.multiple_of(kb * tq, tq)
        hi = pl.multiple_of(kb * tq + half, half)
        key_chunk = lax.broadcasted_iota(jnp.int32, (half, half), 0) // CHUNK
        query_chunk = lax.broadcasted_iota(jnp.int32, (half, half), 1) // CHUNK
        visible = key_chunk <= query_chunk

        def stream_scores(c):
            cols = slice(c * DIFF_DH, (c + 1) * DIFF_DH)
            k_lo, k_hi = k_ref[pl.ds(lo, half), cols], k_ref[pl.ds(hi, half), cols]
            q_lo, q_hi = q_ref[:half, cols], q_ref[half:, cols]
            s_lo = jnp.where(visible, _nt_dot(k_lo, q_lo), NEG_INF)
            s_hi = jnp.concatenate([_nt_dot(k_lo, q_hi),
                                    jnp.where(visible, _nt_dot(k_hi, q_hi), NEG_INF)], axis=0)
            return s_lo, s_hi

        scores = [stream_scores(c) for c in range(SCORES_AHEAD)]
        lam = (jnp.exp(jnp.sum(lq1_ref[...] * lk1_ref[...], axis=-1, keepdims=True))
               - jnp.exp(jnp.sum(lq2_ref[...] * lk2_ref[...], axis=-1, keepdims=True)) + LAM_INIT)
        for head in range(n_streams // 2):
            c1, c2 = 2 * head, 2 * head + 1
            for c in (c1, c2):
                if c + SCORES_AHEAD < n_streams:
                    scores.append(stream_scores(c + SCORES_AHEAD))
                update(c, slice(0, half), scores[c][0], values_t(c, lo, half), shifted)
                update(c, slice(half, tq), scores[c][1], values_t(c, lo, tq), shifted)
            o_t = (acc_scr[c1] * (1.0 / l_scr[c1])
                   - acc_scr[c2] * (lam / l_scr[c2]))
            o_ref[:, head * DIFF_DV:(head + 1) * DIFF_DV] = (
                _rms(o_t.T, sg_ref[...] * (1.0 - LAM_INIT))).astype(o_ref.dtype)

    def all_blocks(shifted):
        lax.fori_loop(0, qi, functools.partial(full_block, shifted=shifted), 0)
        diag_block(qi, shifted)

    @pl.when(bounded)
    def _():
        all_blocks(shifted=False)

    @pl.when(jnp.logical_not(bounded))
    def _():
        all_blocks(shifted=True)


def _diff_attn(z, v_t, qg, kg, lq1, lk1, lq2, lk2, sg, batch, seq, tq=512, heads_per_step=4):
    m = z.shape[0]
    nq = seq // tq
    width = heads_per_step * DIFF_DV
    ns = 2 * heads_per_step
    vec = pl.BlockSpec((1, DIFF_DH), lambda b, g, i: (0, 0))
    return pl.pallas_call(
        functools.partial(_diff_kernel, tq=tq),
        out_shape=jax.ShapeDtypeStruct((m, DIFF_HEADS * DIFF_DV), BF16),
        grid=(batch, DIFF_HEADS // heads_per_step, nq),
        in_specs=[pl.BlockSpec((tq, width), lambda b, g, i: (b * nq + i, Z_DIFF_Q // width + g)),
                  pl.BlockSpec((seq, width), lambda b, g, i: (b, Z_DIFF_K // width + g)),
                  pl.BlockSpec((width, seq), lambda b, g, i: (g, b)),
                  vec, vec, vec, vec, vec, vec,
                  pl.BlockSpec((1, DIFF_DV), lambda b, g, i: (0, 0))],
        out_specs=pl.BlockSpec((tq, width), lambda b, g, i: (b * nq + i, g)),
        scratch_shapes=[pltpu.VMEM((ns, DIFF_DV, tq), F32),
                        pltpu.VMEM((ns, 1, tq), F32),
                        pltpu.VMEM((ns, 1, tq), F32)],
        compiler_params=_params("parallel", "parallel", "arbitrary"),
        name="diff_attn",
    )(z, z, v_t, qg, kg, lq1, lk1, lq2, lk2, sg)


def _mem_kv_kernel(mem_ref, ng_ref, w_ref, kg_ref, kv_ref, mn_scr, *, n_key_tiles):
    j = pl.program_id(0)

    @pl.when(j == 0)
    def _():
        mn_scr[...] = _rms(mem_ref[...], ng_ref[...]).astype(BF16)

    acc = jnp.dot(mn_scr[...], w_ref[...].astype(BF16), preferred_element_type=F32)

    @pl.when(j < n_key_tiles)
    def _():
        _store_group_norm(acc, kg_ref[...], MEM_DH, 1.0, kv_ref)

    @pl.when(j >= n_key_tiles)
    def _():
        kv_ref[...] = acc.astype(kv_ref.dtype)


def _mem_kv(mem2, ng, w_kv, kg, tn=512):
    m, d = mem2.shape
    n = w_kv.shape[1]
    return pl.pallas_call(
        functools.partial(_mem_kv_kernel, n_key_tiles=(n // 2) // tn),
        out_shape=jax.ShapeDtypeStruct((m, n), BF16),
        grid=(n // tn,),
        in_specs=[pl.BlockSpec((m, d), lambda j: (0, 0)),
                  pl.BlockSpec((1, d), lambda j: (0, 0)),
                  pl.BlockSpec((d, tn), lambda j: (0, j)),
                  pl.BlockSpec((1, MEM_DH), lambda j: (0, 0))],
        out_specs=pl.BlockSpec((m, tn), lambda j: (0, j)),
        scratch_shapes=[pltpu.VMEM((m, d), BF16)],
        compiler_params=_params("arbitrary"),
        name="mem_kv",
    )(mem2, ng, w_kv, kg)


def _gate_merge_kernel(h_ref, y0_ref, y1_ref, y2_ref, wg_ref, bg0_ref, bg1_ref, bg2_ref,
                       wb0_ref, wb1_ref, wb2_ref, r0_ref, r1_ref, o_ref, c0_ref, c1_ref):
    _cast_riders((r0_ref,), (c0_ref,))
    _cast_ffn_in_rider(r1_ref, c1_ref)
    bias = jnp.concatenate([bg0_ref[...], bg1_ref[...], bg2_ref[...]], axis=1)
    gates = _sigmoid(jnp.dot(h_ref[...], wg_ref[...], preferred_element_type=F32) + bias)
    merged = None
    for b, (y_ref, wb_ref) in enumerate(((y0_ref, wb0_ref), (y1_ref, wb1_ref), (y2_ref, wb2_ref))):
        term = (gates[:, b * GATE_TILE:(b + 1) * GATE_TILE]
                * jnp.dot(y_ref[...], wb_ref[...], preferred_element_type=F32))
        merged = term if merged is None else merged + term
    o_ref[...] = merged.astype(o_ref.dtype)


def _gate_merge(h, ys, w_gate_tiled, b_gate, w_branch, riders, tm=1024):
    m, d = h.shape
    bw = w_branch.shape[1]
    tn = GATE_TILE
    ni, nj = m // tm, d // tn
    act = lambda width: pl.BlockSpec((tm, width), lambda i, j: (i, 0))
    bg = lambda b: pl.BlockSpec((1, tn), lambda i, j: (0, b * nj + j))
    wb = lambda b: pl.BlockSpec((None, bw, tn), lambda i, j: (b, 0, j))
    r_specs, r_shapes = _rider_specs(riders, ni * nj, lambda i, j: i * nj + j)
    return pl.pallas_call(
        _gate_merge_kernel,
        out_shape=(jax.ShapeDtypeStruct((m, d), BF16), *r_shapes),
        grid=(ni, nj),
        in_specs=[act(d), act(bw), act(bw), act(bw),
                  pl.BlockSpec((d, N_BRANCH * tn), lambda i, j: (0, j)),
                  bg(0), bg(1), bg(2), wb(0), wb(1), wb(2), *r_specs],
        out_specs=(pl.BlockSpec((tm, tn), lambda i, j: (i, j)), *r_specs),
        compiler_params=_params("arbitrary", "arbitrary"),
        name="gate_merge",
    )(h, ys[0], ys[1], ys[2], w_gate_tiled, b_gate, b_gate, b_gate,
      w_branch, w_branch, w_branch, *riders)


def _out_proj_kernel(mg_ref, w_ref, x_ref, g_ref, x1_ref, hf_ref):
    x1 = x_ref[...] + jnp.dot(mg_ref[...], w_ref[...], preferred_element_type=F32)
    x1_ref[...] = x1
    hf_ref[...] = _rms(x1, g_ref[...]).astype(hf_ref.dtype)


def _out_proj(merged, w_out, x2, g, tm=512):
    m, d = x2.shape
    row = lambda i: (i, 0)
    fixed = lambda i: (0, 0)
    return pl.pallas_call(
        _out_proj_kernel,
        out_shape=(jax.ShapeDtypeStruct((m, d), F32), jax.ShapeDtypeStruct((m, d), BF16)),
        grid=(m // tm,),
        in_specs=[pl.BlockSpec((tm, d), row),
                  pl.BlockSpec((d, d), fixed, pipeline_mode=pl.Buffered(1)),
                  pl.BlockSpec((tm, d), row), pl.BlockSpec((1, d), fixed)],
        out_specs=(pl.BlockSpec((tm, d), row), pl.BlockSpec((tm, d), row)),
        compiler_params=_params("parallel"),
        name="out_proj",
    )(merged, w_out, x2, g)


FFN_TILE = 512


def _cast_ffn_in_rider(src, dst):
    d_ff = src.shape[1] // 2
    for j in range(d_ff // FFN_TILE):
        for part in range(2):
            dst_col = (2 * j + part) * FFN_TILE
            src_col = part * d_ff + j * FFN_TILE
            dst[:, dst_col:dst_col + FFN_TILE] = src[:, src_col:src_col + FFN_TILE].astype(BF16)


def _ffn_up_kernel(hf_ref, w_ref, r0_ref, a_ref, c0_ref):
    _cast_riders((r0_ref,), (c0_ref,))
    both = jnp.dot(hf_ref[...], w_ref[...], preferred_element_type=F32)
    gate, up = both[:, :FFN_TILE], both[:, FFN_TILE:]
    a_ref[...] = (gate * _sigmoid(gate) * up).astype(a_ref.dtype)


def _ffn_up(hf, w_in_tiled, riders, tm=1024):
    m, d = hf.shape
    d_ff = w_in_tiled.shape[1] // 2
    tf = FFN_TILE
    ni, nj = m // tm, d_ff // tf
    r_specs, r_shapes = _rider_specs(riders, ni * nj, lambda i, j: i * nj + j)
    return pl.pallas_call(
        _ffn_up_kernel,
        out_shape=(jax.ShapeDtypeStruct((m, d_ff), BF16), *r_shapes),
        grid=(ni, nj),
        in_specs=[pl.BlockSpec((tm, d), lambda i, j: (i, 0)),
                  pl.BlockSpec((d, 2 * tf), lambda i, j: (0, j)),
                  *r_specs],
        out_specs=(pl.BlockSpec((tm, tf), lambda i, j: (i, j)), *r_specs),
        compiler_params=_params("arbitrary", "arbitrary"),
        name="ffn_up",
    )(hf, w_in_tiled, *riders)


def _ffn_down_kernel(a_ref, w_ref, x1_ref, o_ref):
    o_ref[...] = x1_ref[...] + jnp.dot(a_ref[...], w_ref[...], preferred_element_type=F32)


def _ffn_down(a, w_down, x1, tm=1024, tn=512):
    m, d_ff = a.shape
    d = w_down.shape[1]
    return pl.pallas_call(
        _ffn_down_kernel,
        out_shape=jax.ShapeDtypeStruct((m, d), F32),
        grid=(m // tm, d // tn),
        in_specs=[pl.BlockSpec((tm, d_ff), lambda i, j: (i, 0)),
                  pl.BlockSpec((d_ff, tn), lambda i, j: (0, j)),
                  pl.BlockSpec((tm, tn), lambda i, j: (i, j))],
        out_specs=pl.BlockSpec((tm, tn), lambda i, j: (i, j)),
        compiler_params=_params("parallel", "arbitrary"),
        name="ffn_down",
    )(a, w_down, x1)


def kernel(x, mem, norm_mix_g, norm_mem_g, w_in, gla_w_alpha_up, gla_b_alpha, gla_norm_g,
           diff_q_norm_g, diff_k_norm_g, diff_lambda_q1, diff_lambda_k1, diff_lambda_q2,
           diff_lambda_k2, diff_subln_g, mem_q_norm_g, mem_k_norm_g, w_mem_kv, w_branch,
           w_gate, b_gate, w_out, norm_ffn_g, w_ffn_in, w_ffn_down):
    batch, seq, d = x.shape
    n_mem = mem.shape[1]
    depth = w_in.shape[0]
    assert depth == 1, "LAM_INIT is the layer-0 value"
    x2 = x.reshape(batch * seq, d)
    mem2 = mem.reshape(batch * n_mem, d)
    for l in range(depth):
        w_in_t = w_in[l].T
        w_up = jnp.pad(gla_w_alpha_up[l], ((0, LANES - GLA_RANK), (0, 0)))
        row = lambda v: v.reshape(1, -1)

        h, log_a = _norm_mix(x2, row(norm_mix_g[l]), w_in_t, w_up, row(gla_b_alpha[l]))
        z, w_gate_bf, w_branch_bf = _in_proj(
            h, w_in_t, row(diff_q_norm_g[l]), row(diff_k_norm_g[l]), row(mem_q_norm_g[l]),
            riders=(w_gate[l], w_branch[l].reshape(-1, d)))
        y_gla = _gla(z, log_a, row(gla_norm_g[l]), batch, seq)
        kv = _mem_kv(mem2, row(norm_mem_g[l]), w_mem_kv[l], row(mem_k_norm_g[l]))
        v_t, y_mem = _v_proj_t(h, w_in_t, z, kv, seq, n_mem)
        y_diff = _diff_attn(z, v_t, row(diff_q_norm_g[l]), row(diff_k_norm_g[l]),
                            row(diff_lambda_q1[l]), row(diff_lambda_k1[l]),
                            row(diff_lambda_q2[l]), row(diff_lambda_k2[l]),
                            row(diff_subln_g[l]), batch, seq)
        merged, w_out_bf, w_ffn_in_bf = _gate_merge(
            h, (y_gla, y_diff, y_mem), w_gate_bf, row(b_gate[l]),
            w_branch_bf.reshape(w_branch[l].shape), riders=(w_out[l], w_ffn_in[l]))
        x1, hf = _out_proj(merged, w_out_bf, x2, row(norm_ffn_g[l]))
        a, w_ffn_down_bf = _ffn_up(hf, w_ffn_in_bf, riders=(w_ffn_down[l],))
        x2 = _ffn_down(a, w_ffn_down_bf, x1)
    return x2.reshape(batch, seq, d)
```

```python
import functools

import jax
import jax.numpy as jnp
from jax import lax
from jax.experimental import pallas as pl
from jax.experimental.pallas import tpu as pltpu

F32 = jnp.float32
BF16 = jnp.bfloat16

CHUNK = 64
GLA_HEADS = 4
GLA_DK = 128
GLA_DV = 256
GLA_RANK = 16
GLA_GATE_NORM = 16.0
DIFF_HEADS = 4
DIFF_DH = 128
DIFF_DV = 256
MEM_HEADS = 4
MEM_DH = 256
N_BRANCH = 3
NORM_EPS = 1e-6
NEG_INF = -1e30
LAM_INIT = 0.8 - 0.6 * 1.0
LOG2_E = 1.4426950408889634
SCORE_LIMIT = 40.0
SCORES_AHEAD = 2
DIFF_Q_SCALE = DIFF_DH ** -0.5 * LOG2_E

LANES = 128
VMEM_LIMIT = 56 * 1024 * 1024

IN_TILE = 1024
Z_GLA_Q, Z_GLA_K, Z_GLA_V, Z_GLA_G = 0, 512, 1024, 2048
Z_DIFF_Q, Z_DIFF_K, Z_MEM_Q = 3072, 4096, 5120
Z_WIDTH = 6144
W_SRC_TILES = (0, 1, 2, 3, 4, 6)
W_DIFF_V_TILE = 5
W_ROWS = 7 * IN_TILE + GLA_RANK
W_FIRST_SHIFTED_TILE = 3


def _params(*sem):
    return pltpu.CompilerParams(dimension_semantics=sem, vmem_limit_bytes=VMEM_LIMIT)


def _nt_dot(a, b):
    return lax.dot_general(a, b, (((1,), (1,)), ((), ())), preferred_element_type=F32)


def _tn_dot(a, b, precision=None):
    return lax.dot_general(a, b, (((0,), (0,)), ((), ())), preferred_element_type=F32,
                           precision=precision)


def _rms(v, gain):
    ms = jnp.mean(v * v, axis=-1, keepdims=True)
    return v * lax.rsqrt(ms + NORM_EPS) * gain


def _sigmoid(v):
    return 1.0 / (1.0 + jnp.exp(-v))


BF16_SUBLANES = 16


def _rider_specs(weights, n_steps, step_of):
    specs, shapes = [], []
    for w in weights:
        rows, cols = w.shape
        chunk = BF16_SUBLANES
        while rows % chunk or rows // chunk > n_steps:
            chunk += BF16_SUBLANES
        last = rows // chunk - 1
        specs.append(pl.BlockSpec((chunk, cols),
                                  lambda *g, last=last: (jnp.minimum(step_of(*g), last), 0)))
        shapes.append(jax.ShapeDtypeStruct((rows, cols), BF16))
    return specs, shapes


def _cast_riders(in_refs, out_refs):
    for src, dst in zip(in_refs, out_refs):
        dst[...] = src[...].astype(BF16)


GATE_TILE = 256


def _tile_major_gate_cols(n_cols):
    per_branch = n_cols // N_BRANCH
    return [((j * N_BRANCH + b) * GATE_TILE, b * per_branch + j * GATE_TILE)
            for j in range(per_branch // GATE_TILE) for b in range(N_BRANCH)]


def _cast_gate_rider(src, dst):
    for dst_col, src_col in _tile_major_gate_cols(src.shape[1]):
        dst[:, dst_col:dst_col + GATE_TILE] = src[:, src_col:src_col + GATE_TILE].astype(BF16)


def _norm_mix_kernel(x_ref, g_ref, wa_ref, wup_ref, bal_ref, h_ref, la_ref):
    h = _rms(x_ref[...], g_ref[...]).astype(BF16)
    h_ref[...] = h
    a_low = _nt_dot(h, wa_ref[...].astype(BF16))
    w_up = wup_ref[...]
    a_hi, w_hi = a_low.astype(BF16), w_up.astype(BF16)
    a_lo = (a_low - a_hi.astype(F32)).astype(BF16)
    w_lo = (w_up - w_hi.astype(F32)).astype(BF16)
    pre = (jnp.dot(a_hi, w_hi, preferred_element_type=F32)
           + (jnp.dot(a_lo, w_hi, preferred_element_type=F32)
              + jnp.dot(a_hi, w_lo, preferred_element_type=F32))) + bal_ref[...]
    log_sig = jnp.minimum(pre, 0.0) - jnp.log1p(jnp.exp(-jnp.abs(pre)))
    la_ref[...] = log_sig * (1.0 / GLA_GATE_NORM)


def _norm_mix(x2, g, w_in_t, wup, bal, tr=1024):
    m, d = x2.shape
    n = wup.shape[1]
    fixed = lambda i: (0, 0)
    decay_block = (Z_GLA_G + GLA_HEADS * GLA_DV) // LANES
    return pl.pallas_call(
        _norm_mix_kernel,
        out_shape=(jax.ShapeDtypeStruct((m, d), BF16), jax.ShapeDtypeStruct((m, n), F32)),
        grid=(m // tr,),
        in_specs=[pl.BlockSpec((tr, d), lambda i: (i, 0)),
                  pl.BlockSpec((1, d), fixed),
                  pl.BlockSpec((LANES, d), lambda i: (decay_block, 0)),
                  pl.BlockSpec((LANES, n), fixed),
                  pl.BlockSpec((1, n), fixed)],
        out_specs=(pl.BlockSpec((tr, d), lambda i: (i, 0)),
                   pl.BlockSpec((tr, n), lambda i: (i, 0))),
        compiler_params=_params("parallel"),
        name="norm_mix",
    )(x2, g, w_in_t, wup, bal)


def _store_group_norm(acc, gain, width, scale, out_ref):
    for s in range(0, acc.shape[1], width):
        blk = acc[:, s:s + width]
        out_ref[:, s:s + width] = (_rms(blk, gain) * scale).astype(out_ref.dtype)


def _w_tile_specs(d, src_tile):
    hi_per_tile = IN_TILE // GLA_RANK
    return [pl.BlockSpec((IN_TILE, d), lambda *g: (src_tile(*g), 0)),
            pl.BlockSpec((GLA_RANK, d), lambda *g: ((src_tile(*g) + 1) * hi_per_tile, 0))]


def _cast_w_tile(w_lo_ref, w_hi_ref, w_scr, first_step, shifted):
    @pl.when(first_step & jnp.logical_not(shifted))
    def _():
        w_scr[...] = w_lo_ref[...].astype(BF16)

    @pl.when(first_step & shifted)
    def _():
        w_scr[:IN_TILE - GLA_RANK, :] = w_lo_ref[GLA_RANK:, :].astype(BF16)
        w_scr[IN_TILE - GLA_RANK:, :] = w_hi_ref[...].astype(BF16)


def _in_proj_kernel(h_ref, w_lo_ref, w_hi_ref, dq_g_ref, dk_g_ref, mq_g_ref, r0_ref, r1_ref,
                    z_ref, c0_ref, c1_ref, w_scr):
    _cast_gate_rider(r0_ref, c0_ref)
    _cast_riders((r1_ref,), (c1_ref,))
    j = pl.program_id(0)
    _cast_w_tile(w_lo_ref, w_hi_ref, w_scr, pl.program_id(1) == 0, j >= W_FIRST_SHIFTED_TILE)
    j_dq, j_dk, j_mq = Z_DIFF_Q // IN_TILE, Z_DIFF_K // IN_TILE, Z_MEM_Q // IN_TILE

    def tile(epilogue, row_parts):
        part = h_ref.shape[0] // row_parts
        for r in range(row_parts):
            rows = slice(r * part, (r + 1) * part)
            epilogue(_nt_dot(h_ref[rows, :], w_scr[...]), z_ref.at[rows, :])

    def plain(acc, out_ref):
        out_ref[...] = acc.astype(out_ref.dtype)

    @pl.when((j != j_dq) & (j != j_dk) & (j != j_mq))
    def _():
        tile(plain, 1)

    @pl.when((j == j_dq) | (j == j_dk))
    def _():
        gain = jnp.where(j == j_dq, dq_g_ref[...] * DIFF_Q_SCALE, dk_g_ref[...])
        tile(lambda acc, out: _store_group_norm(acc, gain, DIFF_DH, 1.0, out), 4)

    @pl.when(j == j_mq)
    def _():
        tile(lambda acc, out: _store_group_norm(acc, mq_g_ref[...], MEM_DH, MEM_DH ** -0.5, out), 4)


def _in_proj(h, w_in_t, dq_g, dk_g, mq_g, riders, tm=1024):
    m, d = h.shape
    assert w_in_t.shape[0] == W_ROWS
    nj, ni = Z_WIDTH // IN_TILE, m // tm
    assert W_SRC_TILES == tuple(j + (j >= W_DIFF_V_TILE) for j in range(nj))
    r_specs, r_shapes = _rider_specs(riders, nj * ni, lambda j, i: j * ni + i)
    return pl.pallas_call(
        _in_proj_kernel,
        out_shape=(jax.ShapeDtypeStruct((m, Z_WIDTH), BF16), *r_shapes),
        grid=(nj, ni),
        in_specs=[pl.BlockSpec((tm, d), lambda j, i: (i, 0)),
                  *_w_tile_specs(d, lambda j, i: jnp.where(j >= W_DIFF_V_TILE, j + 1, j)),
                  pl.BlockSpec((1, DIFF_DH), lambda j, i: (0, 0)),
                  pl.BlockSpec((1, DIFF_DH), lambda j, i: (0, 0)),
                  pl.BlockSpec((1, MEM_DH), lambda j, i: (0, 0)),
                  *r_specs],
        out_specs=(pl.BlockSpec((tm, IN_TILE), lambda j, i: (i, j)), *r_specs),
        scratch_shapes=[pltpu.VMEM((IN_TILE, d), BF16)],
        compiler_params=_params("arbitrary", "arbitrary"),
        name="in_proj",
    )(h, w_in_t, w_in_t, dq_g, dk_g, mq_g, *riders)


def _v_proj_t_kernel(h_ref, w_lo_ref, w_hi_ref, mq_ref, mk_ref, mv_ref, kg_ref, vt_ref, ym_ref,
                     w_scr, *, sub_rows):
    _cast_w_tile(w_lo_ref, w_hi_ref, w_scr, pl.program_id(0) == 0,
                 W_DIFF_V_TILE >= W_FIRST_SHIFTED_TILE)
    parts = [slice(r, r + sub_rows) for r in range(0, mq_ref.shape[0], sub_rows)]
    heads = [slice(hd * MEM_DH, (hd + 1) * MEM_DH) for hd in range(MEM_HEADS)]
    keys = [_rms(mk_ref[:, cols].astype(F32), kg_ref[...]).astype(BF16) for cols in heads]
    scores = {(r, hd): _nt_dot(mq_ref[rows, cols], keys[hd])
              for r, rows in enumerate(parts) for hd, cols in enumerate(heads)}
    for r, rows in enumerate(parts):
        vt_ref[:, rows] = _nt_dot(w_scr[...], h_ref[rows, :]).astype(vt_ref.dtype)
        for hd, cols in enumerate(heads):
            s = scores[r, hd]
            e = jnp.exp(s - jnp.max(s, axis=-1, keepdims=True))
            p = (e / jnp.sum(e, axis=-1, keepdims=True)).astype(BF16)
            ym_ref[rows, cols] = jnp.dot(p, mv_ref[:, cols],
                                         preferred_element_type=F32).astype(ym_ref.dtype)


def _v_proj_t(h, w_in_t, z, kv, kg, seq, n_mem, tm=1024, sub_rows=512):
    m, d = h.shape
    width = MEM_HEADS * MEM_DH
    tiles_per_batch = seq // tm
    return pl.pallas_call(
        functools.partial(_v_proj_t_kernel, sub_rows=sub_rows),
        out_shape=(jax.ShapeDtypeStruct((IN_TILE, m), BF16), jax.ShapeDtypeStruct((m, width), BF16)),
        grid=(m // tm,),
        in_specs=[pl.BlockSpec((tm, d), lambda i: (i, 0)),
                  *_w_tile_specs(d, lambda i: W_DIFF_V_TILE),
                  pl.BlockSpec((tm, width), lambda i: (i, Z_MEM_Q // width)),
                  pl.BlockSpec((n_mem, width), lambda i: (i // tiles_per_batch, 0)),
                  pl.BlockSpec((n_mem, width), lambda i: (i // tiles_per_batch, 1)),
                  pl.BlockSpec((1, MEM_DH), lambda i: (0, 0))],
        out_specs=(pl.BlockSpec((IN_TILE, tm), lambda i: (0, i)),
                   pl.BlockSpec((tm, width), lambda i: (i, 0))),
        scratch_shapes=[pltpu.VMEM((IN_TILE, d), BF16)],
        compiler_params=_params("arbitrary"),
        name="v_proj_t",
    )(h, w_in_t, w_in_t, z, kv, kv, kg)


def _chunk_cumsum(x):
    row_in_chunk = lax.broadcasted_iota(jnp.int32, x.shape, 0) % CHUNK
    shift = 1
    while shift < CHUNK:
        x = x + jnp.where(row_in_chunk >= shift, pltpu.roll(x, shift, 0), 0.0)
        shift *= 2
    return x


def _gla_kernel(q_ref, k_ref, v_ref, g_ref, la_ref, ng_ref, mem_ref, mg_ref, wkv_ref,
                o_ref, kv_ref, s_ref, mn_scr, *, ts):
    @pl.when((pl.program_id(0) == 0) & (pl.program_id(1) == 0) & (pl.program_id(2) == 0))
    def _():
        mn_scr[...] = _rms(mem_ref[...], mg_ref[...]).astype(BF16)

    @pl.when(pl.program_id(2) == 0)
    def _():
        s_ref[...] = jnp.zeros_like(s_ref)

    kv_ref[...] = jnp.dot(mn_scr[...], wkv_ref[...].astype(BF16),
                          preferred_element_type=F32).astype(kv_ref.dtype)

    bcum_all = _chunk_cumsum(la_ref[...])
    row = lax.broadcasted_iota(jnp.int32, (CHUNK, CHUNK), 0)
    col = lax.broadcasted_iota(jnp.int32, (CHUNK, CHUNK), 1)
    causal = row >= col

    n_chunks = ts // CHUNK
    chunk_rows = [slice(c * CHUNK, (c + 1) * CHUNK) for c in range(n_chunks)]

    lhs, incs, decays = [], [], []
    for rows in chunk_rows:
        bcum = bcum_all[rows]
        b_last = bcum[CHUNK - 1:CHUNK, :]
        q = q_ref[rows, :].astype(F32)
        k = k_ref[rows, :].astype(F32)
        grow = jnp.exp(bcum)
        decay_row = jnp.exp(b_last)
        k_shrunk = k / grow
        q_dec = (q * grow).astype(BF16)
        k_dec = k_shrunk.astype(BF16)
        k_tail = (k_shrunk * decay_row).astype(BF16)
        att = jnp.where(causal, _nt_dot(q_dec, k_dec), 0.0).astype(BF16)
        lhs.append(jnp.concatenate([q_dec, att], axis=1))
        incs.append(_tn_dot(k_tail, v_ref[rows, :]))
        decay_col = jnp.broadcast_to(decay_row, (GLA_DK, GLA_DK)).T
        decays.append(jnp.concatenate([decay_col] * (GLA_DV // GLA_DK), axis=1))

    state = s_ref[...]
    states = []
    for c in range(n_chunks):
        states.append(state.astype(BF16))
        state = decays[c] * state + incs[c]
    s_ref[...] = state

    for c, rows in enumerate(chunk_rows):
        o = jnp.dot(lhs[c], jnp.concatenate([states[c], v_ref[rows, :]], axis=0),
                    preferred_element_type=F32)
        gate = g_ref[rows, :].astype(F32)
        ms = jnp.mean(o * o, axis=-1, keepdims=True)
        normed = o * lax.rsqrt(ms + NORM_EPS * GLA_DK) * ng_ref[...]
        o_ref[rows, :] = (normed * (gate * _sigmoid(gate))).astype(o_ref.dtype)


def _gla(z, log_a, ng, mem2, mem_g, w_kv, batch, seq, ts=2048):
    m = z.shape[0]
    nt = seq // ts
    rows = lambda b, h, t: b * nt + t
    n_steps = batch * GLA_HEADS * nt
    mm, d = mem2.shape
    kv_slice = w_kv.shape[1] // n_steps
    assert kv_slice * n_steps == w_kv.shape[1] and kv_slice % LANES == 0
    step = lambda b, h, t: (b * GLA_HEADS + h) * nt + t
    fixed = lambda b, h, t: (0, 0)
    return pl.pallas_call(
        functools.partial(_gla_kernel, ts=ts),
        out_shape=(jax.ShapeDtypeStruct((m, GLA_HEADS * GLA_DV), BF16),
                   jax.ShapeDtypeStruct((mm, w_kv.shape[1]), BF16)),
        grid=(batch, GLA_HEADS, nt),
        in_specs=[pl.BlockSpec((ts, GLA_DK), lambda b, h, t: (rows(b, h, t), Z_GLA_Q // GLA_DK + h)),
                  pl.BlockSpec((ts, GLA_DK), lambda b, h, t: (rows(b, h, t), Z_GLA_K // GLA_DK + h)),
                  pl.BlockSpec((ts, GLA_DV), lambda b, h, t: (rows(b, h, t), Z_GLA_V // GLA_DV + h)),
                  pl.BlockSpec((ts, GLA_DV), lambda b, h, t: (rows(b, h, t), Z_GLA_G // GLA_DV + h)),
                  pl.BlockSpec((ts, GLA_DK), lambda b, h, t: (rows(b, h, t), h)),
                  pl.BlockSpec((1, GLA_DV), fixed),
                  pl.BlockSpec((mm, d), fixed),
                  pl.BlockSpec((1, d), fixed),
                  pl.BlockSpec((d, kv_slice), lambda b, h, t: (0, step(b, h, t)))],
        out_specs=(pl.BlockSpec((ts, GLA_DV), lambda b, h, t: (rows(b, h, t), h)),
                   pl.BlockSpec((mm, kv_slice), lambda b, h, t: (0, step(b, h, t)))),
        scratch_shapes=[pltpu.VMEM((GLA_DK, GLA_DV), F32), pltpu.VMEM((mm, d), BF16)],
        compiler_params=_params("arbitrary", "arbitrary", "arbitrary"),
        name="gla",
    )(z, z, z, z, log_a, ng, mem2, mem_g, w_kv)


def _diff_kernel(q_ref, k_ref, vt_ref, qg_ref, kg_ref, lq1_ref, lk1_ref, lq2_ref, lk2_ref, sg_ref,
                 o_ref, acc_scr, m_scr, l_scr, *, tq):
    qi = pl.program_id(2)
    m_scr[...] = jnp.full_like(m_scr, NEG_INF)
    l_scr[...] = jnp.zeros_like(l_scr)
    acc_scr[...] = jnp.zeros_like(acc_scr)
    n_streams = q_ref.shape[1] // DIFF_DH
    half = tq // 2

    score_bound = (1.02 * DIFF_DH * DIFF_Q_SCALE) * (jnp.max(jnp.abs(qg_ref[...]))
                                                    * jnp.max(jnp.abs(kg_ref[...])))
    bounded = score_bound <= SCORE_LIMIT

    def update(c, lanes, s, vt, shifted):
        if shifted:
            m_old = m_scr[c, :, lanes]
            m_new = jnp.maximum(m_old, jnp.max(s, axis=0, keepdims=True))
            alpha = jnp.exp2(m_old - m_new)
            p = jnp.exp2(s - m_new)
            l_scr[c, :, lanes] = alpha * l_scr[c, :, lanes] + jnp.sum(p, axis=0, keepdims=True)
            m_scr[c, :, lanes] = m_new
            acc_scr[c, :, lanes] = (alpha * acc_scr[c, :, lanes]
                                    + jnp.dot(vt, p.astype(BF16), preferred_element_type=F32))
        else:
            p = jnp.exp2(s)
            l_scr[c, :, lanes] = l_scr[c, :, lanes] + jnp.sum(p, axis=0, keepdims=True)
            acc_scr[c, :, lanes] = (acc_scr[c, :, lanes]
                                    + jnp.dot(vt, p.astype(BF16), preferred_element_type=F32))

    def values_t(c, start, n):
        head = c // 2
        return vt_ref[head * DIFF_DV:(head + 1) * DIFF_DV, pl.ds(start, n)]

    def full_block(kb, carry, shifted):
        start = pl.multiple_of(kb * tq, tq)
        scores = []
        for c in range(n_streams):
            cols = slice(c * DIFF_DH, (c + 1) * DIFF_DH)
            scores.append(_nt_dot(k_ref[pl.ds(start, tq), cols], q_ref[:, cols]))
        for c in range(n_streams):
            update(c, slice(0, tq), scores[c], values_t(c, start, tq), shifted)
        return carry

    def diag_block(kb, shifted):
        lo = pl.multiple_of(kb * tq, tq)
        hi = pl.multiple_of(kb * tq + half, half)
        key_chunk = lax.broadcasted_iota(jnp.int32, (half, half), 0) // CHUNK
        query_chunk = lax.broadcasted_iota(jnp.int32, (half, half), 1) // CHUNK
        visible = key_chunk <= query_chunk
        def stream_scores(c):
            cols = slice(c * DIFF_DH, (c + 1) * DIFF_DH)
            k_lo, k_hi = k_ref[pl.ds(lo, half), cols], k_ref[pl.ds(hi, half), cols]
            q_lo, q_hi = q_ref[:half, cols], q_ref[half:, cols]
            s_lo = jnp.where(visible, _nt_dot(k_lo, q_lo), NEG_INF)
            s_hi = jnp.concatenate([_nt_dot(k_lo, q_hi),
                                    jnp.where(visible, _nt_dot(k_hi, q_hi), NEG_INF)], axis=0)
            return s_lo, s_hi

        scores = [stream_scores(c) for c in range(SCORES_AHEAD)]
        lam = (jnp.exp(jnp.sum(lq1_ref[...] * lk1_ref[...], axis=-1, keepdims=True))
               - jnp.exp(jnp.sum(lq2_ref[...] * lk2_ref[...], axis=-1, keepdims=True)) + LAM_INIT)
        for head in range(n_streams // 2):
            c1, c2 = 2 * head, 2 * head + 1
            for c in (c1, c2):
                if c + SCORES_AHEAD < n_streams:
                    scores.append(stream_scores(c + SCORES_AHEAD))
                update(c, slice(0, half), scores[c][0], values_t(c, lo, half), shifted)
                update(c, slice(half, tq), scores[c][1], values_t(c, lo, tq), shifted)
            o_t = (acc_scr[c1] * (1.0 / l_scr[c1])
                   - acc_scr[c2] * (lam / l_scr[c2]))
            o_ref[:, head * DIFF_DV:(head + 1) * DIFF_DV] = (
                _rms(o_t.T, sg_ref[...]) * (1.0 - LAM_INIT)).astype(o_ref.dtype)

    def all_blocks(shifted):
        lax.fori_loop(0, qi, functools.partial(full_block, shifted=shifted), 0)
        diag_block(qi, shifted)

    @pl.when(bounded)
    def _():
        all_blocks(shifted=False)

    @pl.when(jnp.logical_not(bounded))
    def _():
        all_blocks(shifted=True)


def _diff_attn(z, v_t, qg, kg, lq1, lk1, lq2, lk2, sg, batch, seq, tq=512, heads_per_step=4):
    m = z.shape[0]
    nq = seq // tq
    width = heads_per_step * DIFF_DV
    ns = 2 * heads_per_step
    vec = pl.BlockSpec((1, DIFF_DH), lambda b, g, i: (0, 0))
    return pl.pallas_call(
        functools.partial(_diff_kernel, tq=tq),
        out_shape=jax.ShapeDtypeStruct((m, DIFF_HEADS * DIFF_DV), BF16),
        grid=(batch, DIFF_HEADS // heads_per_step, nq),
        in_specs=[pl.BlockSpec((tq, width), lambda b, g, i: (b * nq + i, Z_DIFF_Q // width + g)),
                  pl.BlockSpec((seq, width), lambda b, g, i: (b, Z_DIFF_K // width + g)),
                  pl.BlockSpec((width, seq), lambda b, g, i: (g, b)),
                  vec, vec, vec, vec, vec, vec,
                  pl.BlockSpec((1, DIFF_DV), lambda b, g, i: (0, 0))],
        out_specs=pl.BlockSpec((tq, width), lambda b, g, i: (b * nq + i, g)),
        scratch_shapes=[pltpu.VMEM((ns, DIFF_DV, tq), F32),
                        pltpu.VMEM((ns, 1, tq), F32),
                        pltpu.VMEM((ns, 1, tq), F32)],
        compiler_params=_params("parallel", "parallel", "arbitrary"),
        name="diff_attn",
    )(z, z, v_t, qg, kg, lq1, lk1, lq2, lk2, sg)


def _gate_merge_kernel(h_ref, y0_ref, y1_ref, y2_ref, wg_ref, bg0_ref, bg1_ref, bg2_ref,
                       wb0_ref, wb1_ref, wb2_ref, r0_ref, r1_ref, o_ref, c0_ref, c1_ref):
    _cast_riders((r0_ref,), (c0_ref,))
    _cast_ffn_in_rider(r1_ref, c1_ref)
    bias = jnp.concatenate([bg0_ref[...], bg1_ref[...], bg2_ref[...]], axis=1)
    gates = _sigmoid(jnp.dot(h_ref[...], wg_ref[...], preferred_element_type=F32) + bias)
    merged = None
    for b, (y_ref, wb_ref) in enumerate(((y0_ref, wb0_ref), (y1_ref, wb1_ref), (y2_ref, wb2_ref))):
        term = (gates[:, b * GATE_TILE:(b + 1) * GATE_TILE]
                * jnp.dot(y_ref[...], wb_ref[...], preferred_element_type=F32))
        merged = term if merged is None else merged + term
    o_ref[...] = merged.astype(o_ref.dtype)


def _gate_merge(h, ys, w_gate_tiled, b_gate, w_branch, riders, tm=1024):
    m, d = h.shape
    bw = w_branch.shape[1]
    tn = GATE_TILE
    ni, nj = m // tm, d // tn
    act = lambda width: pl.BlockSpec((tm, width), lambda i, j: (i, 0))
    bg = lambda b: pl.BlockSpec((1, tn), lambda i, j: (0, b * nj + j))
    wb = lambda b: pl.BlockSpec((None, bw, tn), lambda i, j: (b, 0, j))
    r_specs, r_shapes = _rider_specs(riders, ni * nj, lambda i, j: i * nj + j)
    return pl.pallas_call(
        _gate_merge_kernel,
        out_shape=(jax.ShapeDtypeStruct((m, d), BF16), *r_shapes),
        grid=(ni, nj),
        in_specs=[act(d), act(bw), act(bw), act(bw),
                  pl.BlockSpec((d, N_BRANCH * tn), lambda i, j: (0, j)),
                  bg(0), bg(1), bg(2), wb(0), wb(1), wb(2), *r_specs],
        out_specs=(pl.BlockSpec((tm, tn), lambda i, j: (i, j)), *r_specs),
        compiler_params=_params("arbitrary", "arbitrary"),
        name="gate_merge",
    )(h, ys[0], ys[1], ys[2], w_gate_tiled, b_gate, b_gate, b_gate,
      w_branch, w_branch, w_branch, *riders)


def _out_proj_kernel(mg_ref, w_ref, x_ref, g_ref, x1_ref, hf_ref):
    x1 = x_ref[...] + jnp.dot(mg_ref[...], w_ref[...], preferred_element_type=F32)
    x1_ref[...] = x1
    hf_ref[...] = _rms(x1, g_ref[...]).astype(hf_ref.dtype)


def _out_proj(merged, w_out, x2, g, tm=512):
    m, d = x2.shape
    row = lambda i: (i, 0)
    fixed = lambda i: (0, 0)
    return pl.pallas_call(
        _out_proj_kernel,
        out_shape=(jax.ShapeDtypeStruct((m, d), F32), jax.ShapeDtypeStruct((m, d), BF16)),
        grid=(m // tm,),
        in_specs=[pl.BlockSpec((tm, d), row),
                  pl.BlockSpec((d, d), fixed, pipeline_mode=pl.Buffered(1)),
                  pl.BlockSpec((tm, d), row), pl.BlockSpec((1, d), fixed)],
        out_specs=(pl.BlockSpec((tm, d), row), pl.BlockSpec((tm, d), row)),
        compiler_params=_params("parallel"),
        name="out_proj",
    )(merged, w_out, x2, g)


FFN_TILE = 512


def _cast_ffn_in_rider(src, dst):
    d_ff = src.shape[1] // 2
    for j in range(d_ff // FFN_TILE):
        for part in range(2):
            dst_col = (2 * j + part) * FFN_TILE
            src_col = part * d_ff + j * FFN_TILE
            dst[:, dst_col:dst_col + FFN_TILE] = src[:, src_col:src_col + FFN_TILE].astype(BF16)


def _ffn_up_kernel(hf_ref, w_ref, r0_ref, a_ref, c0_ref):
    _cast_riders((r0_ref,), (c0_ref,))
    both = jnp.dot(hf_ref[...], w_ref[...], preferred_element_type=F32)
    gate, up = both[:, :FFN_TILE], both[:, FFN_TILE:]
    a_ref[...] = (gate * _sigmoid(gate) * up).astype(a_ref.dtype)


def _ffn_up(hf, w_in_tiled, riders, tm=1024):
    m, d = hf.shape
    d_ff = w_in_tiled.shape[1] // 2
    tf = FFN_TILE
    ni, nj = m // tm, d_ff // tf
    r_specs, r_shapes = _rider_specs(riders, ni * nj, lambda i, j: i * nj + j)
    return pl.pallas_call(
        _ffn_up_kernel,
        out_shape=(jax.ShapeDtypeStruct((m, d_ff), BF16), *r_shapes),
        grid=(ni, nj),
        in_specs=[pl.BlockSpec((tm, d), lambda i, j: (i, 0)),
                  pl.BlockSpec((d, 2 * tf), lambda i, j: (0, j)),
                  *r_specs],
        out_specs=(pl.BlockSpec((tm, tf), lambda i, j: (i, j)), *r_specs),
        compiler_params=_params("arbitrary", "arbitrary"),
        name="ffn_up",
    )(hf, w_in_tiled, *riders)


def _ffn_down_kernel(a_ref, w_ref, x1_ref, o_ref):
    o_ref[...] = x1_ref[...] + jnp.dot(a_ref[...], w_ref[...], preferred_element_type=F32)


def _ffn_down(a, w_down, x1, tm=1024, tn=512):
    m, d_ff = a.shape
    d = w_down.shape[1]
    return pl.pallas_call(
        _ffn_down_kernel,
        out_shape=jax.ShapeDtypeStruct((m, d), F32),
        grid=(m // tm, d // tn),
        in_specs=[pl.BlockSpec((tm, d_ff), lambda i, j: (i, 0)),
                  pl.BlockSpec((d_ff, tn), lambda i, j: (0, j)),
                  pl.BlockSpec((tm, tn), lambda i, j: (i, j))],
        out_specs=pl.BlockSpec((tm, tn), lambda i, j: (i, j)),
        compiler_params=_params("parallel", "arbitrary"),
        name="ffn_down",
    )(a, w_down, x1)


def kernel(x, mem, norm_mix_g, norm_mem_g, w_in, gla_w_alpha_up, gla_b_alpha, gla_norm_g,
           diff_q_norm_g, diff_k_norm_g, diff_lambda_q1, diff_lambda_k1, diff_lambda_q2,
           diff_lambda_k2, diff_subln_g, mem_q_norm_g, mem_k_norm_g, w_mem_kv, w_branch,
           w_gate, b_gate, w_out, norm_ffn_g, w_ffn_in, w_ffn_down):
    batch, seq, d = x.shape
    n_mem = mem.shape[1]
    depth = w_in.shape[0]
    assert depth == 1, "LAM_INIT is the layer-0 value"
    x2 = x.reshape(batch * seq, d)
    mem2 = mem.reshape(batch * n_mem, d)
    for l in range(depth):
        w_in_t = w_in[l].T
        w_up = jnp.pad(gla_w_alpha_up[l], ((0, LANES - GLA_RANK), (0, 0)))
        row = lambda v: v.reshape(1, -1)

        h, log_a = _norm_mix(x2, row(norm_mix_g[l]), w_in_t, w_up, row(gla_b_alpha[l]))
        z, w_gate_bf, w_branch_bf = _in_proj(
            h, w_in_t, row(diff_q_norm_g[l]), row(diff_k_norm_g[l]), row(mem_q_norm_g[l]),
            riders=(w_gate[l], w_branch[l].reshape(-1, d)))
        y_gla, kv = _gla(z, log_a, row(gla_norm_g[l]), mem2, row(norm_mem_g[l]), w_mem_kv[l],
                         batch, seq)
        v_t, y_mem = _v_proj_t(h, w_in_t, z, kv, row(mem_k_norm_g[l]), seq, n_mem)
        y_diff = _diff_attn(z, v_t, row(diff_q_norm_g[l]), row(diff_k_norm_g[l]),
                            row(diff_lambda_q1[l]), row(diff_lambda_k1[l]),
                            row(diff_lambda_q2[l]), row(diff_lambda_k2[l]),
                            row(diff_subln_g[l]), batch, seq)
        merged, w_out_bf, w_ffn_in_bf = _gate_merge(
            h, (y_gla, y_diff, y_mem), w_gate_bf, row(b_gate[l]),
            w_branch_bf.reshape(w_branch[l].shape), riders=(w_out[l], w_ffn_in[l]))
        x1, hf = _out_proj(merged, w_out_bf, x2, row(norm_ffn_g[l]))
        a, w_ffn_down_bf = _ffn_up(hf, w_ffn_in_bf, riders=(w_ffn_down[l],))
        x2 = _ffn_down(a, w_ffn_down_bf, x1)
    return x2.reshape(batch, seq, d)
```

```python
import functools

import jax
import jax.numpy as jnp
from jax import lax
from jax.experimental import pallas as pl
from jax.experimental.pallas import tpu as pltpu

F32 = jnp.float32
BF16 = jnp.bfloat16

CHUNK = 64
GLA_HEADS = 4
GLA_DK = 128
GLA_DV = 256
GLA_RANK = 16
GLA_GATE_NORM = 16.0
DIFF_HEADS = 4
DIFF_DH = 128
DIFF_DV = 256
MEM_HEADS = 4
MEM_DH = 256
N_BRANCH = 3
NORM_EPS = 1e-6
NEG_INF = -1e30
LAM_INIT = 0.8 - 0.6 * 1.0
LOG2_E = 1.4426950408889634
SCORE_LIMIT = 40.0
SCORES_AHEAD = 2
DIFF_Q_SCALE = DIFF_DH ** -0.5 * LOG2_E

LANES = 128
VMEM_LIMIT = 56 * 1024 * 1024

IN_TILE = 1024
Z_GLA_Q, Z_GLA_K, Z_GLA_V, Z_GLA_G = 0, 512, 1024, 2048
Z_DIFF_Q, Z_DIFF_K, Z_MEM_Q = 3072, 4096, 5120
Z_WIDTH = 6144
W_SRC_TILES = (0, 1, 2, 3, 4, 6)
W_DIFF_V_TILE = 5
W_ROWS = 7 * IN_TILE + GLA_RANK
W_FIRST_SHIFTED_TILE = 3


def _params(*sem):
    return pltpu.CompilerParams(dimension_semantics=sem, vmem_limit_bytes=VMEM_LIMIT)


def _nt_dot(a, b):
    return lax.dot_general(a, b, (((1,), (1,)), ((), ())), preferred_element_type=F32)


def _tn_dot(a, b):
    return lax.dot_general(a, b, (((0,), (0,)), ((), ())), preferred_element_type=F32)


def _rms(v, gain):
    ms = jnp.mean(v * v, axis=-1, keepdims=True)
    return v * lax.rsqrt(ms + NORM_EPS) * gain


def _sigmoid(v):
    return 1.0 / (1.0 + jnp.exp(-v))


BF16_SUBLANES = 16


def _rider_specs(weights, n_steps, step_of):
    specs, shapes = [], []
    for w in weights:
        rows, cols = w.shape
        chunk = BF16_SUBLANES
        while rows % chunk or rows // chunk > n_steps:
            chunk += BF16_SUBLANES
        last = rows // chunk - 1
        specs.append(pl.BlockSpec((chunk, cols),
                                  lambda *g, last=last: (jnp.minimum(step_of(*g), last), 0)))
        shapes.append(jax.ShapeDtypeStruct((rows, cols), BF16))
    return specs, shapes


def _cast_riders(in_refs, out_refs):
    for src, dst in zip(in_refs, out_refs):
        dst[...] = src[...].astype(BF16)


GATE_TILE = 256


def _tile_major_gate_cols(n_cols):
    per_branch = n_cols // N_BRANCH
    return [((j * N_BRANCH + b) * GATE_TILE, b * per_branch + j * GATE_TILE)
            for j in range(per_branch // GATE_TILE) for b in range(N_BRANCH)]


def _cast_gate_rider(src, dst):
    for dst_col, src_col in _tile_major_gate_cols(src.shape[1]):
        dst[:, dst_col:dst_col + GATE_TILE] = src[:, src_col:src_col + GATE_TILE].astype(BF16)


def _norm_mix_kernel(x_ref, g_ref, wa_ref, wup_ref, bal_ref, h_ref, la_ref):
    h = _rms(x_ref[...], g_ref[...]).astype(BF16)
    h_ref[...] = h
    a_low = _nt_dot(h, wa_ref[...].astype(BF16))
    w_up = wup_ref[...]
    a_hi, w_hi = a_low.astype(BF16), w_up.astype(BF16)
    a_lo = (a_low - a_hi.astype(F32)).astype(BF16)
    w_lo = (w_up - w_hi.astype(F32)).astype(BF16)
    pre = (jnp.dot(a_hi, w_hi, preferred_element_type=F32)
           + (jnp.dot(a_lo, w_hi, preferred_element_type=F32)
              + jnp.dot(a_hi, w_lo, preferred_element_type=F32))) + bal_ref[...]
    log_sig = jnp.minimum(pre, 0.0) - jnp.log1p(jnp.exp(-jnp.abs(pre)))
    la_ref[...] = log_sig * (1.0 / GLA_GATE_NORM)


def _norm_mix(x2, g, w_in_t, wup, bal, tr=1024):
    m, d = x2.shape
    n = wup.shape[1]
    fixed = lambda i: (0, 0)
    decay_block = (Z_GLA_G + GLA_HEADS * GLA_DV) // LANES
    return pl.pallas_call(
        _norm_mix_kernel,
        out_shape=(jax.ShapeDtypeStruct((m, d), BF16), jax.ShapeDtypeStruct((m, n), F32)),
        grid=(m // tr,),
        in_specs=[pl.BlockSpec((tr, d), lambda i: (i, 0)),
                  pl.BlockSpec((1, d), fixed),
                  pl.BlockSpec((LANES, d), lambda i: (decay_block, 0)),
                  pl.BlockSpec((LANES, n), fixed),
                  pl.BlockSpec((1, n), fixed)],
        out_specs=(pl.BlockSpec((tr, d), lambda i: (i, 0)),
                   pl.BlockSpec((tr, n), lambda i: (i, 0))),
        compiler_params=_params("parallel"),
        name="norm_mix",
    )(x2, g, w_in_t, wup, bal)


def _store_group_norm(acc, gain, width, scale, out_ref):
    for s in range(0, acc.shape[1], width):
        blk = acc[:, s:s + width]
        out_ref[:, s:s + width] = (_rms(blk, gain) * scale).astype(out_ref.dtype)


def _w_tile_specs(d, src_tile):
    hi_per_tile = IN_TILE // GLA_RANK
    return [pl.BlockSpec((IN_TILE, d), lambda *g: (src_tile(*g), 0)),
            pl.BlockSpec((GLA_RANK, d), lambda *g: ((src_tile(*g) + 1) * hi_per_tile, 0))]


def _cast_w_tile(w_lo_ref, w_hi_ref, w_scr, first_step, shifted):
    @pl.when(first_step & jnp.logical_not(shifted))
    def _():
        w_scr[...] = w_lo_ref[...].astype(BF16)

    @pl.when(first_step & shifted)
    def _():
        w_scr[:IN_TILE - GLA_RANK, :] = w_lo_ref[GLA_RANK:, :].astype(BF16)
        w_scr[IN_TILE - GLA_RANK:, :] = w_hi_ref[...].astype(BF16)


def _in_proj_kernel(h_ref, w_lo_ref, w_hi_ref, dq_g_ref, dk_g_ref, mq_g_ref, r0_ref, r1_ref,
                    z_ref, c0_ref, c1_ref, w_scr):
    _cast_gate_rider(r0_ref, c0_ref)
    _cast_riders((r1_ref,), (c1_ref,))
    j = pl.program_id(0)
    _cast_w_tile(w_lo_ref, w_hi_ref, w_scr, pl.program_id(1) == 0, j >= W_FIRST_SHIFTED_TILE)
    j_dq, j_dk, j_mq = Z_DIFF_Q // IN_TILE, Z_DIFF_K // IN_TILE, Z_MEM_Q // IN_TILE

    def tile(epilogue, row_parts):
        part = h_ref.shape[0] // row_parts
        for r in range(row_parts):
            rows = slice(r * part, (r + 1) * part)
            epilogue(_nt_dot(h_ref[rows, :], w_scr[...]), z_ref.at[rows, :])

    def plain(acc, out_ref):
        out_ref[...] = acc.astype(out_ref.dtype)

    @pl.when((j != j_dq) & (j != j_dk) & (j != j_mq))
    def _():
        tile(plain, 1)

    @pl.when((j == j_dq) | (j == j_dk))
    def _():
        gain = jnp.where(j == j_dq, dq_g_ref[...] * DIFF_Q_SCALE, dk_g_ref[...])
        tile(lambda acc, out: _store_group_norm(acc, gain, DIFF_DH, 1.0, out), 4)

    @pl.when(j == j_mq)
    def _():
        tile(lambda acc, out: _store_group_norm(acc, mq_g_ref[...], MEM_DH, MEM_DH ** -0.5, out), 4)


def _in_proj(h, w_in_t, dq_g, dk_g, mq_g, riders, tm=1024):
    m, d = h.shape
    assert w_in_t.shape[0] == W_ROWS
    nj, ni = Z_WIDTH // IN_TILE, m // tm
    assert W_SRC_TILES == tuple(j + (j >= W_DIFF_V_TILE) for j in range(nj))
    r_specs, r_shapes = _rider_specs(riders, nj * ni, lambda j, i: j * ni + i)
    return pl.pallas_call(
        _in_proj_kernel,
        out_shape=(jax.ShapeDtypeStruct((m, Z_WIDTH), BF16), *r_shapes),
        grid=(nj, ni),
        in_specs=[pl.BlockSpec((tm, d), lambda j, i: (i, 0)),
                  *_w_tile_specs(d, lambda j, i: jnp.where(j >= W_DIFF_V_TILE, j + 1, j)),
                  pl.BlockSpec((1, DIFF_DH), lambda j, i: (0, 0)),
                  pl.BlockSpec((1, DIFF_DH), lambda j, i: (0, 0)),
                  pl.BlockSpec((1, MEM_DH), lambda j, i: (0, 0)),
                  *r_specs],
        out_specs=(pl.BlockSpec((tm, IN_TILE), lambda j, i: (i, j)), *r_specs),
        scratch_shapes=[pltpu.VMEM((IN_TILE, d), BF16)],
        compiler_params=_params("arbitrary", "arbitrary"),
        name="in_proj",
    )(h, w_in_t, w_in_t, dq_g, dk_g, mq_g, *riders)


def _v_proj_t_kernel(h_ref, w_lo_ref, w_hi_ref, mq_ref, mk_ref, mv_ref, kg_ref, vt_ref, ym_ref,
                     w_scr, *, sub_rows):
    _cast_w_tile(w_lo_ref, w_hi_ref, w_scr, pl.program_id(0) == 0,
                 W_DIFF_V_TILE >= W_FIRST_SHIFTED_TILE)
    parts = [slice(r, r + sub_rows) for r in range(0, mq_ref.shape[0], sub_rows)]
    heads = [slice(hd * MEM_DH, (hd + 1) * MEM_DH) for hd in range(MEM_HEADS)]
    keys = [_rms(mk_ref[:, cols].astype(F32), kg_ref[...]).astype(BF16) for cols in heads]
    scores = {(r, hd): _nt_dot(mq_ref[rows, cols], keys[hd])
              for r, rows in enumerate(parts) for hd, cols in enumerate(heads)}
    for r, rows in enumerate(parts):
        vt_ref[:, rows] = _nt_dot(w_scr[...], h_ref[rows, :]).astype(vt_ref.dtype)
        for hd, cols in enumerate(heads):
            s = scores[r, hd]
            e = jnp.exp(s - jnp.max(s, axis=-1, keepdims=True))
            p = (e / jnp.sum(e, axis=-1, keepdims=True)).astype(BF16)
            ym_ref[rows, cols] = jnp.dot(p, mv_ref[:, cols],
                                         preferred_element_type=F32).astype(ym_ref.dtype)


def _v_proj_t(h, w_in_t, z, kv, kg, seq, n_mem, tm=1024, sub_rows=512):
    m, d = h.shape
    width = MEM_HEADS * MEM_DH
    tiles_per_batch = seq // tm
    return pl.pallas_call(
        functools.partial(_v_proj_t_kernel, sub_rows=sub_rows),
        out_shape=(jax.ShapeDtypeStruct((IN_TILE, m), BF16), jax.ShapeDtypeStruct((m, width), BF16)),
        grid=(m // tm,),
        in_specs=[pl.BlockSpec((tm, d), lambda i: (i, 0)),
                  *_w_tile_specs(d, lambda i: W_DIFF_V_TILE),
                  pl.BlockSpec((tm, width), lambda i: (i, Z_MEM_Q // width)),
                  pl.BlockSpec((n_mem, width), lambda i: (i // tiles_per_batch, 0)),
                  pl.BlockSpec((n_mem, width), lambda i: (i // tiles_per_batch, 1)),
                  pl.BlockSpec((1, MEM_DH), lambda i: (0, 0))],
        out_specs=(pl.BlockSpec((IN_TILE, tm), lambda i: (0, i)),
                   pl.BlockSpec((tm, width), lambda i: (i, 0))),
        scratch_shapes=[pltpu.VMEM((IN_TILE, d), BF16)],
        compiler_params=_params("arbitrary"),
        name="v_proj_t",
    )(h, w_in_t, w_in_t, z, kv, kv, kg)


def _chunk_cumsum(x):
    row_in_chunk = lax.broadcasted_iota(jnp.int32, x.shape, 0) % CHUNK
    shift = 1
    while shift < CHUNK:
        x = x + jnp.where(row_in_chunk >= shift, pltpu.roll(x, shift, 0), 0.0)
        shift *= 2
    return x


def _gla_kernel(q_ref, k_ref, v_ref, g_ref, la_ref, ng_ref, mem_ref, mg_ref, wkv_ref,
                o_ref, kv_ref, s_ref, mn_scr, *, ts):
    @pl.when((pl.program_id(0) == 0) & (pl.program_id(1) == 0) & (pl.program_id(2) == 0))
    def _():
        mn_scr[...] = _rms(mem_ref[...], mg_ref[...]).astype(BF16)

    @pl.when(pl.program_id(2) == 0)
    def _():
        s_ref[...] = jnp.zeros_like(s_ref)

    kv_ref[...] = jnp.dot(mn_scr[...], wkv_ref[...].astype(BF16),
                          preferred_element_type=F32).astype(kv_ref.dtype)

    bcum_all = _chunk_cumsum(la_ref[...])
    row = lax.broadcasted_iota(jnp.int32, (CHUNK, CHUNK), 0)
    col = lax.broadcasted_iota(jnp.int32, (CHUNK, CHUNK), 1)
    causal = row >= col

    n_chunks = ts // CHUNK
    chunk_rows = [slice(c * CHUNK, (c + 1) * CHUNK) for c in range(n_chunks)]

    lhs, incs, decays = [], [], []
    for rows in chunk_rows:
        bcum = bcum_all[rows]
        b_last = bcum[CHUNK - 1:CHUNK, :]
        q = q_ref[rows, :].astype(F32)
        k = k_ref[rows, :].astype(F32)
        grow = jnp.exp(bcum)
        decay_row = jnp.exp(b_last)
        k_shrunk = k / grow
        q_dec = (q * grow).astype(BF16)
        k_dec = k_shrunk.astype(BF16)
        k_tail = (k_shrunk * decay_row).astype(BF16)
        att = jnp.where(causal, _nt_dot(q_dec, k_dec), 0.0).astype(BF16)
        lhs.append(jnp.concatenate([q_dec, att], axis=1))
        incs.append(_tn_dot(k_tail, v_ref[rows, :]))
        decay_col = jnp.broadcast_to(decay_row, (GLA_DK, GLA_DK)).T
        decays.append(jnp.concatenate([decay_col] * (GLA_DV // GLA_DK), axis=1))

    state = s_ref[...]
    states = []
    for c in range(n_chunks):
        states.append(state.astype(BF16))
        state = decays[c] * state + incs[c]
    s_ref[...] = state

    for c, rows in enumerate(chunk_rows):
        o = jnp.dot(lhs[c], jnp.concatenate([states[c], v_ref[rows, :]], axis=0),
                    preferred_element_type=F32)
        gate = g_ref[rows, :].astype(F32)
        ms = jnp.mean(o * o, axis=-1, keepdims=True)
        normed = o * lax.rsqrt(ms + NORM_EPS * GLA_DK) * ng_ref[...]
        o_ref[rows, :] = (normed * (gate * _sigmoid(gate))).astype(o_ref.dtype)


def _gla(z, log_a, ng, mem2, mem_g, w_kv, batch, seq, ts=2048):
    m = z.shape[0]
    nt = seq // ts
    rows = lambda b, h, t: b * nt + t
    n_steps = batch * GLA_HEADS * nt
    mm, d = mem2.shape
    kv_slice = w_kv.shape[1] // n_steps
    assert kv_slice * n_steps == w_kv.shape[1] and kv_slice % LANES == 0
    step = lambda b, h, t: (b * GLA_HEADS + h) * nt + t
    fixed = lambda b, h, t: (0, 0)
    return pl.pallas_call(
        functools.partial(_gla_kernel, ts=ts),
        out_shape=(jax.ShapeDtypeStruct((m, GLA_HEADS * GLA_DV), BF16),
                   jax.ShapeDtypeStruct((mm, w_kv.shape[1]), BF16)),
        grid=(batch, GLA_HEADS, nt),
        in_specs=[pl.BlockSpec((ts, GLA_DK), lambda b, h, t: (rows(b, h, t), Z_GLA_Q // GLA_DK + h)),
                  pl.BlockSpec((ts, GLA_DK), lambda b, h, t: (rows(b, h, t), Z_GLA_K // GLA_DK + h)),
                  pl.BlockSpec((ts, GLA_DV), lambda b, h, t: (rows(b, h, t), Z_GLA_V // GLA_DV + h)),
                  pl.BlockSpec((ts, GLA_DV), lambda b, h, t: (rows(b, h, t), Z_GLA_G // GLA_DV + h)),
                  pl.BlockSpec((ts, GLA_DK), lambda b, h, t: (rows(b, h, t), h)),
                  pl.BlockSpec((1, GLA_DV), fixed),
                  pl.BlockSpec((mm, d), fixed),
                  pl.BlockSpec((1, d), fixed),
                  pl.BlockSpec((d, kv_slice), lambda b, h, t: (0, step(b, h, t)))],
        out_specs=(pl.BlockSpec((ts, GLA_DV), lambda b, h, t: (rows(b, h, t), h)),
                   pl.BlockSpec((mm, kv_slice), lambda b, h, t: (0, step(b, h, t)))),
        scratch_shapes=[pltpu.VMEM((GLA_DK, GLA_DV), F32), pltpu.VMEM((mm, d), BF16)],
        compiler_params=_params("arbitrary", "arbitrary", "arbitrary"),
        name="gla",
    )(z, z, z, z, log_a, ng, mem2, mem_g, w_kv)


def _diff_kernel(q_ref, k_ref, vt_ref, qg_ref, kg_ref, lq1_ref, lk1_ref, lq2_ref, lk2_ref, sg_ref,
                 o_ref, acc_scr, m_scr, l_scr, *, tq):
    qi = pl.program_id(2)
    m_scr[...] = jnp.full_like(m_scr, NEG_INF)
    l_scr[...] = jnp.zeros_like(l_scr)
    acc_scr[...] = jnp.zeros_like(acc_scr)
    n_streams = q_ref.shape[1] // DIFF_DH
    half = tq // 2

    score_bound = (1.02 * DIFF_DH * DIFF_Q_SCALE) * (jnp.max(jnp.abs(qg_ref[...]))
                                                    * jnp.max(jnp.abs(kg_ref[...])))
    bounded = score_bound <= SCORE_LIMIT

    def update(c, lanes, s, vt, shifted):
        if shifted:
            m_old = m_scr[c, :, lanes]
            m_new = jnp.maximum(m_old, jnp.max(s, axis=0, keepdims=True))
            alpha = jnp.exp2(m_old - m_new)
            p = jnp.exp2(s - m_new)
            l_scr[c, :, lanes] = alpha * l_scr[c, :, lanes] + jnp.sum(p, axis=0, keepdims=True)
            m_scr[c, :, lanes] = m_new
            acc_scr[c, :, lanes] = (alpha * acc_scr[c, :, lanes]
                                    + jnp.dot(vt, p.astype(BF16), preferred_element_type=F32))
        else:
            p = jnp.exp2(s)
            l_scr[c, :, lanes] = l_scr[c, :, lanes] + jnp.sum(p, axis=0, keepdims=True)
            acc_scr[c, :, lanes] = (acc_scr[c, :, lanes]
                                    + jnp.dot(vt, p.astype(BF16), preferred_element_type=F32))

    def values_t(c, start, n):
        head = c // 2
        return vt_ref[head * DIFF_DV:(head + 1) * DIFF_DV, pl.ds(start, n)]

    def full_block(kb, carry, shifted):
        start = pl.multiple_of(kb * tq, tq)
        scores = []
        for c in range(n_streams):
            cols = slice(c * DIFF_DH, (c + 1) * DIFF_DH)
            scores.append(_nt_dot(k_ref[pl.ds(start, tq), cols], q_ref[:, cols]))
        for c in range(n_streams):
            update(c, slice(0, tq), scores[c], values_t(c, start, tq), shifted)
        return carry

    def diag_block(kb, shifted):
        lo = pl.multiple_of(kb * tq, tq)
        hi = pl.multiple_of(kb * tq + half, half)
        key_chunk = lax.broadcasted_iota(jnp.int32, (half, half), 0) // CHUNK
        query_chunk = lax.broadcasted_iota(jnp.int32, (half, half), 1) // CHUNK
        visible = key_chunk <= query_chunk

        def stream_scores(c):
            cols = slice(c * DIFF_DH, (c + 1) * DIFF_DH)
            k_lo, k_hi = k_ref[pl.ds(lo, half), cols], k_ref[pl.ds(hi, half), cols]
            q_lo, q_hi = q_ref[:half, cols], q_ref[half:, cols]
            s_lo = jnp.where(visible, _nt_dot(k_lo, q_lo), NEG_INF)
            s_hi = jnp.concatenate([_nt_dot(k_lo, q_hi),
                                    jnp.where(visible, _nt_dot(k_hi, q_hi), NEG_INF)], axis=0)
            return s_lo, s_hi

        scores = [stream_scores(c) for c in range(SCORES_AHEAD)]
        lam = (jnp.exp(jnp.sum(lq1_ref[...] * lk1_ref[...], axis=-1, keepdims=True))
               - jnp.exp(jnp.sum(lq2_ref[...] * lk2_ref[...], axis=-1, keepdims=True)) + LAM_INIT)
        for head in range(n_streams // 2):
            c1, c2 = 2 * head, 2 * head + 1
            for c in (c1, c2):
                if c + SCORES_AHEAD < n_streams:
                    scores.append(stream_scores(c + SCORES_AHEAD))
                update(c, slice(0, half), scores[c][0], values_t(c, lo, half), shifted)
                update(c, slice(half, tq), scores[c][1], values_t(c, lo, tq), shifted)
            o_t = (acc_scr[c1] * (1.0 / l_scr[c1])
                   - acc_scr[c2] * (lam / l_scr[c2]))
            o_ref[:, head * DIFF_DV:(head + 1) * DIFF_DV] = (
                _rms(o_t.T, sg_ref[...] * (1.0 - LAM_INIT))).astype(o_ref.dtype)

    def all_blocks(shifted):
        lax.fori_loop(0, qi, functools.partial(full_block, shifted=shifted), 0)
        diag_block(qi, shifted)

    @pl.when(bounded)
    def _():
        all_blocks(shifted=False)

    @pl.when(jnp.logical_not(bounded))
    def _():
        all_blocks(shifted=True)


def _diff_attn(z, v_t, qg, kg, lq1, lk1, lq2, lk2, sg, batch, seq, tq=512, heads_per_step=4):
    m = z.shape[0]
    nq = seq // tq
    width = heads_per_step * DIFF_DV
    ns = 2 * heads_per_step
    vec = pl.BlockSpec((1, DIFF_DH), lambda b, g, i: (0, 0))
    return pl.pallas_call(
        functools.partial(_diff_kernel, tq=tq),
        out_shape=jax.ShapeDtypeStruct((m, DIFF_HEADS * DIFF_DV), BF16),
        grid=(batch, DIFF_HEADS // heads_per_step, nq),
        in_specs=[pl.BlockSpec((tq, width), lambda b, g, i: (b * nq + i, Z_DIFF_Q // width + g)),
                  pl.BlockSpec((seq, width), lambda b, g, i: (b, Z_DIFF_K // width + g)),
                  pl.BlockSpec((width, seq), lambda b, g, i: (g, b)),
                  vec, vec, vec, vec, vec, vec,
                  pl.BlockSpec((1, DIFF_DV), lambda b, g, i: (0, 0))],
        out_specs=pl.BlockSpec((tq, width), lambda b, g, i: (b * nq + i, g)),
        scratch_shapes=[pltpu.VMEM((ns, DIFF_DV, tq), F32),
                        pltpu.VMEM((ns, 1, tq), F32),
                        pltpu.VMEM((ns, 1, tq), F32)],
        compiler_params=_params("parallel", "parallel", "arbitrary"),
        name="diff_attn",
    )(z, z, v_t, qg, kg, lq1, lk1, lq2, lk2, sg)


def _gate_merge_kernel(h_ref, y0_ref, y1_ref, y2_ref, wg_ref, bg0_ref, bg1_ref, bg2_ref,
                       wb0_ref, wb1_ref, wb2_ref, r0_ref, r1_ref, o_ref, c0_ref, c1_ref):
    _cast_riders((r0_ref,), (c0_ref,))
    _cast_ffn_in_rider(r1_ref, c1_ref)
    bias = jnp.concatenate([bg0_ref[...], bg1_ref[...], bg2_ref[...]], axis=1)
    gates = _sigmoid(jnp.dot(h_ref[...], wg_ref[...], preferred_element_type=F32) + bias)
    merged = None
    for b, (y_ref, wb_ref) in enumerate(((y0_ref, wb0_ref), (y1_ref, wb1_ref), (y2_ref, wb2_ref))):
        term = (gates[:, b * GATE_TILE:(b + 1) * GATE_TILE]
                * jnp.dot(y_ref[...], wb_ref[...], preferred_element_type=F32))
        merged = term if merged is None else merged + term
    o_ref[...] = merged.astype(o_ref.dtype)


def _gate_merge(h, ys, w_gate_tiled, b_gate, w_branch, riders, tm=1024):
    m, d = h.shape
    bw = w_branch.shape[1]
    tn = GATE_TILE
    ni, nj = m // tm, d // tn
    act = lambda width: pl.BlockSpec((tm, width), lambda i, j: (i, 0))
    bg = lambda b: pl.BlockSpec((1, tn), lambda i, j: (0, b * nj + j))
    wb = lambda b: pl.BlockSpec((None, bw, tn), lambda i, j: (b, 0, j))
    r_specs, r_shapes = _rider_specs(riders, ni * nj, lambda i, j: i * nj + j)
    return pl.pallas_call(
        _gate_merge_kernel,
        out_shape=(jax.ShapeDtypeStruct((m, d), BF16), *r_shapes),
        grid=(ni, nj),
        in_specs=[act(d), act(bw), act(bw), act(bw),
                  pl.BlockSpec((d, N_BRANCH * tn), lambda i, j: (0, j)),
                  bg(0), bg(1), bg(2), wb(0), wb(1), wb(2), *r_specs],
        out_specs=(pl.BlockSpec((tm, tn), lambda i, j: (i, j)), *r_specs),
        compiler_params=_params("arbitrary", "arbitrary"),
        name="gate_merge",
    )(h, ys[0], ys[1], ys[2], w_gate_tiled, b_gate, b_gate, b_gate,
      w_branch, w_branch, w_branch, *riders)


def _out_proj_kernel(mg_ref, w_ref, x_ref, g_ref, x1_ref, hf_ref):
    x1 = x_ref[...] + jnp.dot(mg_ref[...], w_ref[...], preferred_element_type=F32)
    x1_ref[...] = x1
    hf_ref[...] = _rms(x1, g_ref[...]).astype(hf_ref.dtype)


def _out_proj(merged, w_out, x2, g, tm=512):
    m, d = x2.shape
    row = lambda i: (i, 0)
    fixed = lambda i: (0, 0)
    return pl.pallas_call(
        _out_proj_kernel,
        out_shape=(jax.ShapeDtypeStruct((m, d), F32), jax.ShapeDtypeStruct((m, d), BF16)),
        grid=(m // tm,),
        in_specs=[pl.BlockSpec((tm, d), row),
                  pl.BlockSpec((d, d), fixed, pipeline_mode=pl.Buffered(1)),
                  pl.BlockSpec((tm, d), row), pl.BlockSpec((1, d), fixed)],
        out_specs=(pl.BlockSpec((tm, d), row), pl.BlockSpec((tm, d), row)),
        compiler_params=_params("parallel"),
        name="out_proj",
    )(merged, w_out, x2, g)


FFN_TILE = 512


def _cast_ffn_in_rider(src, dst):
    d_ff = src.shape[1] // 2
    for j in range(d_ff // FFN_TILE):
        for part in range(2):
            dst_col = (2 * j + part) * FFN_TILE
            src_col = part * d_ff + j * FFN_TILE
            dst[:, dst_col:dst_col + FFN_TILE] = src[:, src_col:src_col + FFN_TILE].astype(BF16)


def _ffn_up_kernel(hf_ref, w_ref, r0_ref, a_ref, c0_ref):
    _cast_riders((r0_ref,), (c0_ref,))
    both = jnp.dot(hf_ref[...], w_ref[...], preferred_element_type=F32)
    gate, up = both[:, :FFN_TILE], both[:, FFN_TILE:]
    a_ref[...] = (gate * _sigmoid(gate) * up).astype(a_ref.dtype)


def _ffn_up(hf, w_in_tiled, riders, tm=1024):
    m, d = hf.shape
    d_ff = w_in_tiled.shape[1] // 2
    tf = FFN_TILE
    ni, nj = m // tm, d_ff // tf
    r_specs, r_shapes = _rider_specs(riders, ni * nj, lambda i, j: i * nj + j)
    return pl.pallas_call(
        _ffn_up_kernel,
        out_shape=(jax.ShapeDtypeStruct((m, d_ff), BF16), *r_shapes),
        grid=(ni, nj),
        in_specs=[pl.BlockSpec((tm, d), lambda i, j: (i, 0)),
                  pl.BlockSpec((d, 2 * tf), lambda i, j: (0, j)),
                  *r_specs],
        out_specs=(pl.BlockSpec((tm, tf), lambda i, j: (i, j)), *r_specs),
        compiler_params=_params("arbitrary", "arbitrary"),
        name="ffn_up",
    )(hf, w_in_tiled, *riders)


def _ffn_down_kernel(a_ref, w_ref, x1_ref, o_ref):
    o_ref[...] = x1_ref[...] + jnp.dot(a_ref[...], w_ref[...], preferred_element_type=F32)


def _ffn_down(a, w_down, x1, tm=1024, tn=512):
    m, d_ff = a.shape
    d = w_down.shape[1]
    return pl.pallas_call(
        _ffn_down_kernel,
        out_shape=jax.ShapeDtypeStruct((m, d), F32),
        grid=(m // tm, d // tn),
        in_specs=[pl.BlockSpec((tm, d_ff), lambda i, j: (i, 0)),
                  pl.BlockSpec((d_ff, tn), lambda i, j: (0, j)),
                  pl.BlockSpec((tm, tn), lambda i, j: (i, j))],
        out_specs=pl.BlockSpec((tm, tn), lambda i, j: (i, j)),
        compiler_params=_params("parallel", "arbitrary"),
        name="ffn_down",
    )(a, w_down, x1)


def kernel(x, mem, norm_mix_g, norm_mem_g, w_in, gla_w_alpha_up, gla_b_alpha, gla_norm_g,
           diff_q_norm_g, diff_k_norm_g, diff_lambda_q1, diff_lambda_k1, diff_lambda_q2,
           diff_lambda_k2, diff_subln_g, mem_q_norm_g, mem_k_norm_g, w_mem_kv, w_branch,
           w_gate, b_gate, w_out, norm_ffn_g, w_ffn_in, w_ffn_down):
    batch, seq, d = x.shape
    n_mem = mem.shape[1]
    depth = w_in.shape[0]
    assert depth == 1, "LAM_INIT is the layer-0 value"
    x2 = x.reshape(batch * seq, d)
    mem2 = mem.reshape(batch * n_mem, d)
    for l in range(depth):
        w_in_t = w_in[l].T
        w_up = jnp.pad(gla_w_alpha_up[l], ((0, LANES - GLA_RANK), (0, 0)))
        row = lambda v: v.reshape(1, -1)

        h, log_a = _norm_mix(x2, row(norm_mix_g[l]), w_in_t, w_up, row(gla_b_alpha[l]))
        z, w_gate_bf, w_branch_bf = _in_proj(
            h, w_in_t, row(diff_q_norm_g[l]), row(diff_k_norm_g[l]), row(mem_q_norm_g[l]),
            riders=(w_gate[l], w_branch[l].reshape(-1, d)))
        y_gla, kv = _gla(z, log_a, row(gla_norm_g[l]), mem2, row(norm_mem_g[l]), w_mem_kv[l],
                         batch, seq)
        v_t, y_mem = _v_proj_t(h, w_in_t, z, kv, row(mem_k_norm_g[l]), seq, n_mem)
        y_diff = _diff_attn(z, v_t, row(diff_q_norm_g[l]), row(diff_k_norm_g[l]),
                            row(diff_lambda_q1[l]), row(diff_lambda_k1[l]),
                            row(diff_lambda_q2[l]), row(diff_lambda_k2[l]),
                            row(diff_subln_g[l]), batch, seq)
        merged, w_out_bf, w_ffn_in_bf = _gate_merge(
            h, (y_gla, y_diff, y_mem), w_gate_bf, row(b_gate[l]),
            w_branch_bf.reshape(w_branch[l].shape), riders=(w_out[l], w_ffn_in[l]))
        x1, hf = _out_proj(merged, w_out_bf, x2, row(norm_ffn_g[l]))
        a, w_ffn_down_bf = _ffn_up(hf, w_ffn_in_bf, riders=(w_ffn_down[l],))
        x2 = _ffn_down(a, w_ffn_down_bf, x1)
    return x2.reshape(batch, seq, d)
```

```python
import functools

import jax
import jax.numpy as jnp
from jax import lax
from jax.experimental import pallas as pl
from jax.experimental.pallas import tpu as pltpu

F32 = jnp.float32
BF16 = jnp.bfloat16

CHUNK = 64
GLA_HEADS = 4
GLA_DK = 128
GLA_DV = 256
GLA_RANK = 16
GLA_GATE_NORM = 16.0
DIFF_HEADS = 4
DIFF_DH = 128
DIFF_DV = 256
MEM_HEADS = 4
MEM_DH = 256
N_BRANCH = 3
NORM_EPS = 1e-6
NEG_INF = -1e30
LAM_INIT = 0.8 - 0.6 * 1.0
LOG2_E = 1.4426950408889634
SCORE_LIMIT = 40.0
SCORES_AHEAD = 2
DIFF_Q_SCALE = DIFF_DH ** -0.5 * LOG2_E

LANES = 128
VMEM_LIMIT = 56 * 1024 * 1024

IN_TILE = 1024
Z_GLA_Q, Z_GLA_K, Z_GLA_V, Z_GLA_G = 0, 512, 1024, 2048
Z_DIFF_Q, Z_DIFF_K, Z_MEM_Q = 3072, 4096, 5120
Z_WIDTH = 6144
W_SRC_TILES = (0, 1, 2, 3, 4, 6)
W_DIFF_V_TILE = 5
W_ROWS = 7 * IN_TILE + GLA_RANK
W_FIRST_SHIFTED_TILE = 3


def _params(*sem):
    return pltpu.CompilerParams(dimension_semantics=sem, vmem_limit_bytes=VMEM_LIMIT)


def _nt_dot(a, b):
    return lax.dot_general(a, b, (((1,), (1,)), ((), ())), preferred_element_type=F32)


def _tn_dot(a, b):
    return lax.dot_general(a, b, (((0,), (0,)), ((), ())), preferred_element_type=F32)


def _rms(v, gain):
    ms = jnp.mean(v * v, axis=-1, keepdims=True)
    return v * lax.rsqrt(ms + NORM_EPS) * gain


def _sigmoid(v):
    return 1.0 / (1.0 + jnp.exp(-v))


BF16_SUBLANES = 16


def _rider_specs(weights, n_steps, step_of):
    specs, shapes = [], []
    for w in weights:
        rows, cols = w.shape
        chunk = BF16_SUBLANES
        while rows % chunk or rows // chunk > n_steps:
            chunk += BF16_SUBLANES
        last = rows // chunk - 1
        specs.append(pl.BlockSpec((chunk, cols),
                                  lambda *g, last=last: (jnp.minimum(step_of(*g), last), 0)))
        shapes.append(jax.ShapeDtypeStruct((rows, cols), BF16))
    return specs, shapes


def _cast_riders(in_refs, out_refs):
    for src, dst in zip(in_refs, out_refs):
        dst[...] = src[...].astype(BF16)


GATE_TILE = 256


def _tile_major_gate_cols(n_cols):
    per_branch = n_cols // N_BRANCH
    return [((j * N_BRANCH + b) * GATE_TILE, b * per_branch + j * GATE_TILE)
            for j in range(per_branch // GATE_TILE) for b in range(N_BRANCH)]


def _cast_gate_rider(src, dst):
    for dst_col, src_col in _tile_major_gate_cols(src.shape[1]):
        dst[:, dst_col:dst_col + GATE_TILE] = src[:, src_col:src_col + GATE_TILE].astype(BF16)


def _norm_mix_kernel(x_ref, g_ref, wa_ref, wup_ref, bal_ref, h_ref, la_ref):
    h = _rms(x_ref[...], g_ref[...]).astype(BF16)
    h_ref[...] = h
    a_low = _nt_dot(h, wa_ref[...].astype(BF16))
    w_up = wup_ref[...]
    a_hi, w_hi = a_low.astype(BF16), w_up.astype(BF16)
    a_lo = (a_low - a_hi.astype(F32)).astype(BF16)
    w_lo = (w_up - w_hi.astype(F32)).astype(BF16)
    pre = (jnp.dot(a_hi, w_hi, preferred_element_type=F32)
           + (jnp.dot(a_lo, w_hi, preferred_element_type=F32)
              + jnp.dot(a_hi, w_lo, preferred_element_type=F32))) + bal_ref[...]
    log_sig = jnp.minimum(pre, 0.0) - jnp.log1p(jnp.exp(-jnp.abs(pre)))
    la_ref[...] = log_sig * (1.0 / GLA_GATE_NORM)


def _norm_mix(x2, g, w_in_t, wup, bal, tr=1024):
    m, d = x2.shape
    n = wup.shape[1]
    fixed = lambda i: (0, 0)
    decay_block = (Z_GLA_G + GLA_HEADS * GLA_DV) // LANES
    return pl.pallas_call(
        _norm_mix_kernel,
        out_shape=(jax.ShapeDtypeStruct((m, d), BF16), jax.ShapeDtypeStruct((m, n), F32)),
        grid=(m // tr,),
        in_specs=[pl.BlockSpec((tr, d), lambda i: (i, 0)),
                  pl.BlockSpec((1, d), fixed),
                  pl.BlockSpec((LANES, d), lambda i: (decay_block, 0)),
                  pl.BlockSpec((LANES, n), fixed),
                  pl.BlockSpec((1, n), fixed)],
        out_specs=(pl.BlockSpec((tr, d), lambda i: (i, 0)),
                   pl.BlockSpec((tr, n), lambda i: (i, 0))),
        compiler_params=_params("parallel"),
        name="norm_mix",
    )(x2, g, w_in_t, wup, bal)


def _store_group_norm(acc, gain, width, scale, out_ref):
    for s in range(0, acc.shape[1], width):
        blk = acc[:, s:s + width]
        out_ref[:, s:s + width] = (_rms(blk, gain) * scale).astype(out_ref.dtype)


def _w_tile_specs(d, src_tile):
    hi_per_tile = IN_TILE // GLA_RANK
    return [pl.BlockSpec((IN_TILE, d), lambda *g: (src_tile(*g), 0)),
            pl.BlockSpec((GLA_RANK, d), lambda *g: ((src_tile(*g) + 1) * hi_per_tile, 0))]


def _cast_w_tile(w_lo_ref, w_hi_ref, w_scr, first_step, shifted):
    @pl.when(first_step & jnp.logical_not(shifted))
    def _():
        w_scr[...] = w_lo_ref[...].astype(BF16)

    @pl.when(first_step & shifted)
    def _():
        w_scr[:IN_TILE - GLA_RANK, :] = w_lo_ref[GLA_RANK:, :].astype(BF16)
        w_scr[IN_TILE - GLA_RANK:, :] = w_hi_ref[...].astype(BF16)


def _in_proj_kernel(h_ref, w_lo_ref, w_hi_ref, dq_g_ref, dk_g_ref, mq_g_ref, r0_ref, r1_ref,
                    z_ref, c0_ref, c1_ref, w_scr):
    _cast_gate_rider(r0_ref, c0_ref)
    _cast_riders((r1_ref,), (c1_ref,))
    j = pl.program_id(0)
    _cast_w_tile(w_lo_ref, w_hi_ref, w_scr, pl.program_id(1) == 0, j >= W_FIRST_SHIFTED_TILE)
    j_dq, j_dk, j_mq = Z_DIFF_Q // IN_TILE, Z_DIFF_K // IN_TILE, Z_MEM_Q // IN_TILE

    def tile(epilogue, row_parts):
        part = h_ref.shape[0] // row_parts
        for r in range(row_parts):
            rows = slice(r * part, (r + 1) * part)
            epilogue(_nt_dot(h_ref[rows, :], w_scr[...]), z_ref.at[rows, :])

    def plain(acc, out_ref):
        out_ref[...] = acc.astype(out_ref.dtype)

    @pl.when((j != j_dq) & (j != j_dk) & (j != j_mq))
    def _():
        tile(plain, 1)

    @pl.when((j == j_dq) | (j == j_dk))
    def _():
        gain = jnp.where(j == j_dq, dq_g_ref[...] * DIFF_Q_SCALE, dk_g_ref[...])
        tile(lambda acc, out: _store_group_norm(acc, gain, DIFF_DH, 1.0, out), 4)

    @pl.when(j == j_mq)
    def _():
        tile(lambda acc, out: _store_group_norm(acc, mq_g_ref[...], MEM_DH, MEM_DH ** -0.5, out), 4)


def _in_proj(h, w_in_t, dq_g, dk_g, mq_g, riders, tm=1024):
    m, d = h.shape
    assert w_in_t.shape[0] == W_ROWS
    nj, ni = Z_WIDTH // IN_TILE, m // tm
    assert W_SRC_TILES == tuple(j + (j >= W_DIFF_V_TILE) for j in range(nj))
    r_specs, r_shapes = _rider_specs(riders, nj * ni, lambda j, i: j * ni + i)
    return pl.pallas_call(
        _in_proj_kernel,
        out_shape=(jax.ShapeDtypeStruct((m, Z_WIDTH), BF16), *r_shapes),
        grid=(nj, ni),
        in_specs=[pl.BlockSpec((tm, d), lambda j, i: (i, 0)),
                  *_w_tile_specs(d, lambda j, i: jnp.where(j >= W_DIFF_V_TILE, j + 1, j)),
                  pl.BlockSpec((1, DIFF_DH), lambda j, i: (0, 0)),
                  pl.BlockSpec((1, DIFF_DH), lambda j, i: (0, 0)),
                  pl.BlockSpec((1, MEM_DH), lambda j, i: (0, 0)),
                  *r_specs],
        out_specs=(pl.BlockSpec((tm, IN_TILE), lambda j, i: (i, j)), *r_specs),
        scratch_shapes=[pltpu.VMEM((IN_TILE, d), BF16)],
        compiler_params=_params("arbitrary", "arbitrary"),
        name="in_proj",
    )(h, w_in_t, w_in_t, dq_g, dk_g, mq_g, *riders)


def _v_proj_t_kernel(h_ref, w_lo_ref, w_hi_ref, mq_ref, mk_ref, mv_ref, kg_ref, vt_ref, ym_ref,
                     w_scr, *, sub_rows):
    _cast_w_tile(w_lo_ref, w_hi_ref, w_scr, pl.program_id(0) == 0,
                 W_DIFF_V_TILE >= W_FIRST_SHIFTED_TILE)
    parts = [slice(r, r + sub_rows) for r in range(0, mq_ref.shape[0], sub_rows)]
    heads = [slice(hd * MEM_DH, (hd + 1) * MEM_DH) for hd in range(MEM_HEADS)]
    keys = [_rms(mk_ref[:, cols].astype(F32), kg_ref[...]).astype(BF16) for cols in heads]
    scores = {(r, hd): _nt_dot(mq_ref[rows, cols], keys[hd])
              for r, rows in enumerate(parts) for hd, cols in enumerate(heads)}
    for r, rows in enumerate(parts):
        vt_ref[:, rows] = _nt_dot(w_scr[...], h_ref[rows, :]).astype(vt_ref.dtype)
        for hd, cols in enumerate(heads):
            s = scores[r, hd]
            e = jnp.exp(s - jnp.max(s, axis=-1, keepdims=True))
            p = (e / jnp.sum(e, axis=-1, keepdims=True)).astype(BF16)
            ym_ref[rows, cols] = jnp.dot(p, mv_ref[:, cols],
                                         preferred_element_type=F32).astype(ym_ref.dtype)


def _v_proj_t(h, w_in_t, z, kv, kg, seq, n_mem, tm=1024, sub_rows=512):
    m, d = h.shape
    width = MEM_HEADS * MEM_DH
    tiles_per_batch = seq // tm
    return pl.pallas_call(
        functools.partial(_v_proj_t_kernel, sub_rows=sub_rows),
        out_shape=(jax.ShapeDtypeStruct((IN_TILE, m), BF16), jax.ShapeDtypeStruct((m, width), BF16)),
        grid=(m // tm,),
        in_specs=[pl.BlockSpec((tm, d), lambda i: (i, 0)),
                  *_w_tile_specs(d, lambda i: W_DIFF_V_TILE),
                  pl.BlockSpec((tm, width), lambda i: (i, Z_MEM_Q // width)),
                  pl.BlockSpec((n_mem, width), lambda i: (i // tiles_per_batch, 0)),
                  pl.BlockSpec((n_mem, width), lambda i: (i // tiles_per_batch, 1)),
                  pl.BlockSpec((1, MEM_DH), lambda i: (0, 0))],
        out_specs=(pl.BlockSpec((IN_TILE, tm), lambda i: (0, i)),
                   pl.BlockSpec((tm, width), lambda i: (i, 0))),
        scratch_shapes=[pltpu.VMEM((IN_TILE, d), BF16)],
        compiler_params=_params("arbitrary"),
        name="v_proj_t",
    )(h, w_in_t, w_in_t, z, kv, kv, kg)


def _chunk_cumsum(x):
    row_in_chunk = lax.broadcasted_iota(jnp.int32, x.shape, 0) % CHUNK
    shift = 1
    while shift < CHUNK:
        x = x + jnp.where(row_in_chunk >= shift, pltpu.roll(x, shift, 0), 0.0)
        shift *= 2
    return x


def _gla_kernel(q_ref, k_ref, v_ref, g_ref, la_ref, ng_ref, mem_ref, mg_ref, wkv_ref,
                o_ref, kv_ref, s_ref, mn_scr, *, ts):
    @pl.when((pl.program_id(0) == 0) & (pl.program_id(1) == 0) & (pl.program_id(2) == 0))
    def _():
        mn_scr[...] = _rms(mem_ref[...], mg_ref[...]).astype(BF16)

    @pl.when(pl.program_id(2) == 0)
    def _():
        s_ref[...] = jnp.zeros_like(s_ref)

    bcum_all = _chunk_cumsum(la_ref[...])
    row = lax.broadcasted_iota(jnp.int32, (CHUNK, CHUNK), 0)
    col = lax.broadcasted_iota(jnp.int32, (CHUNK, CHUNK), 1)
    causal = row >= col

    n_chunks = ts // CHUNK
    chunk_rows = [slice(c * CHUNK, (c + 1) * CHUNK) for c in range(n_chunks)]

    lhs, incs, decays = [], [], []
    for rows in chunk_rows:
        bcum = bcum_all[rows]
        b_last = bcum[CHUNK - 1:CHUNK, :]
        q = q_ref[rows, :].astype(F32)
        k = k_ref[rows, :].astype(F32)
        grow = jnp.exp(bcum)
        decay_row = jnp.exp(b_last)
        k_shrunk = k / grow
        q_dec = (q * grow).astype(BF16)
        k_dec = k_shrunk.astype(BF16)
        k_tail = (k_shrunk * decay_row).astype(BF16)
        att = jnp.where(causal, _nt_dot(q_dec, k_dec), 0.0).astype(BF16)
        lhs.append(jnp.concatenate([q_dec, att], axis=1))
        incs.append(_tn_dot(k_tail, v_ref[rows, :]))
        decay_col = jnp.broadcast_to(decay_row, (GLA_DK, GLA_DK)).T
        decays.append(jnp.concatenate([decay_col] * (GLA_DV // GLA_DK), axis=1))

    state = s_ref[...]
    states = []
    for c in range(n_chunks):
        states.append(state.astype(BF16))
        state = decays[c] * state + incs[c]
    s_ref[...] = state

    kv_ref[...] = jnp.dot(mn_scr[...], wkv_ref[...].astype(BF16),
                          preferred_element_type=F32).astype(kv_ref.dtype)

    for c, rows in enumerate(chunk_rows):
        o = jnp.dot(lhs[c], jnp.concatenate([states[c], v_ref[rows, :]], axis=0),
                    preferred_element_type=F32)
        gate = g_ref[rows, :].astype(F32)
        ms = jnp.mean(o * o, axis=-1, keepdims=True)
        normed = o * lax.rsqrt(ms + NORM_EPS * GLA_DK) * ng_ref[...]
        o_ref[rows, :] = (normed * (gate * _sigmoid(gate))).astype(o_ref.dtype)


def _gla(z, log_a, ng, mem2, mem_g, w_kv, batch, seq, ts=2048):
    m = z.shape[0]
    nt = seq // ts
    rows = lambda b, h, t: b * nt + t
    n_steps = batch * GLA_HEADS * nt
    mm, d = mem2.shape
    kv_slice = w_kv.shape[1] // n_steps
    assert kv_slice * n_steps == w_kv.shape[1] and kv_slice % LANES == 0
    step = lambda b, h, t: (b * GLA_HEADS + h) * nt + t
    fixed = lambda b, h, t: (0, 0)
    return pl.pallas_call(
        functools.partial(_gla_kernel, ts=ts),
        out_shape=(jax.ShapeDtypeStruct((m, GLA_HEADS * GLA_DV), BF16),
                   jax.ShapeDtypeStruct((mm, w_kv.shape[1]), BF16)),
        grid=(batch, GLA_HEADS, nt),
        in_specs=[pl.BlockSpec((ts, GLA_DK), lambda b, h, t: (rows(b, h, t), Z_GLA_Q // GLA_DK + h)),
                  pl.BlockSpec((ts, GLA_DK), lambda b, h, t: (rows(b, h, t), Z_GLA_K // GLA_DK + h)),
                  pl.BlockSpec((ts, GLA_DV), lambda b, h, t: (rows(b, h, t), Z_GLA_V // GLA_DV + h)),
                  pl.BlockSpec((ts, GLA_DV), lambda b, h, t: (rows(b, h, t), Z_GLA_G // GLA_DV + h)),
                  pl.BlockSpec((ts, GLA_DK), lambda b, h, t: (rows(b, h, t), h)),
                  pl.BlockSpec((1, GLA_DV), fixed),
                  pl.BlockSpec((mm, d), fixed),
                  pl.BlockSpec((1, d), fixed),
                  pl.BlockSpec((d, kv_slice), lambda b, h, t: (0, step(b, h, t)))],
        out_specs=(pl.BlockSpec((ts, GLA_DV), lambda b, h, t: (rows(b, h, t), h)),
                   pl.BlockSpec((mm, kv_slice), lambda b, h, t: (0, step(b, h, t)))),
        scratch_shapes=[pltpu.VMEM((GLA_DK, GLA_DV), F32), pltpu.VMEM((mm, d), BF16)],
        compiler_params=_params("arbitrary", "arbitrary", "arbitrary"),
        name="gla",
    )(z, z, z, z, log_a, ng, mem2, mem_g, w_kv)


def _diff_kernel(q_ref, k_ref, vt_ref, qg_ref, kg_ref, lq1_ref, lk1_ref, lq2_ref, lk2_ref, sg_ref,
                 o_ref, acc_scr, m_scr, l_scr, *, tq):
    qi = pl.program_id(2)
    m_scr[...] = jnp.full_like(m_scr, NEG_INF)
    l_scr[...] = jnp.zeros_like(l_scr)
    acc_scr[...] = jnp.zeros_like(acc_scr)
    n_streams = q_ref.shape[1] // DIFF_DH
    half = tq // 2

    score_bound = (1.02 * DIFF_DH * DIFF_Q_SCALE) * (jnp.max(jnp.abs(qg_ref[...]))
                                                    * jnp.max(jnp.abs(kg_ref[...])))
    bounded = score_bound <= SCORE_LIMIT

    def update(c, lanes, s, vt, shifted):
        if shifted:
            m_old = m_scr[c, :, lanes]
            m_new = jnp.maximum(m_old, jnp.max(s, axis=0, keepdims=True))
            alpha = jnp.exp2(m_old - m_new)
            p = jnp.exp2(s - m_new)
            l_scr[c, :, lanes] = alpha * l_scr[c, :, lanes] + jnp.sum(p, axis=0, keepdims=True)
            m_scr[c, :, lanes] = m_new
            acc_scr[c, :, lanes] = (alpha * acc_scr[c, :, lanes]
                                    + jnp.dot(vt, p.astype(BF16), preferred_element_type=F32))
        else:
            p = jnp.exp2(s)
            l_scr[c, :, lanes] = l_scr[c, :, lanes] + jnp.sum(p, axis=0, keepdims=True)
            acc_scr[c, :, lanes] = (acc_scr[c, :, lanes]
                                    + jnp.dot(vt, p.astype(BF16), preferred_element_type=F32))

    def values_t(c, start, n):
        head = c // 2
        return vt_ref[head * DIFF_DV:(head + 1) * DIFF_DV, pl.ds(start, n)]

    def full_block(kb, carry, shifted):
        start = pl.multiple_of(kb * tq, tq)
        scores = []
        for c in range(n_streams):
            cols = slice(c * DIFF_DH, (c + 1) * DIFF_DH)
            scores.append(_nt_dot(k_ref[pl.ds(start, tq), cols], q_ref[:, cols]))
        for c in range(n_streams):
            update(c, slice(0, tq), scores[c], values_t(c, start, tq), shifted)
        return carry

    def diag_block(kb, shifted):
        lo = pl.multiple_of(kb * tq, tq)
        hi = pl.multiple_of(kb * tq + half, half)
        key_chunk = lax.broadcasted_iota(jnp.int32, (half, half), 0) // CHUNK
        query_chunk = lax.broadcasted_iota(jnp.int32, (half, half), 1) // CHUNK
        visible = key_chunk <= query_chunk

        def stream_scores(c):
            cols = slice(c * DIFF_DH, (c + 1) * DIFF_DH)
            k_lo, k_hi = k_ref[pl.ds(lo, half), cols], k_ref[pl.ds(hi, half), cols]
            q_lo, q_hi = q_ref[:half, cols], q_ref[half:, cols]
            s_lo = jnp.where(visible, _nt_dot(k_lo, q_lo), NEG_INF)
            s_hi = jnp.concatenate([_nt_dot(k_lo, q_hi),
                                    jnp.where(visible, _nt_dot(k_hi, q_hi), NEG_INF)], axis=0)
            return s_lo, s_hi

        scores = [stream_scores(c) for c in range(SCORES_AHEAD)]
        lam = (jnp.exp(jnp.sum(lq1_ref[...] * lk1_ref[...], axis=-1, keepdims=True))
               - jnp.exp(jnp.sum(lq2_ref[...] * lk2_ref[...], axis=-1, keepdims=True)) + LAM_INIT)
        for head in range(n_streams // 2):
            c1, c2 = 2 * head, 2 * head + 1
            for c in (c1, c2):
                if c + SCORES_AHEAD < n_streams:
                    scores.append(stream_scores(c + SCORES_AHEAD))
                update(c, slice(0, half), scores[c][0], values_t(c, lo, half), shifted)
                update(c, slice(half, tq), scores[c][1], values_t(c, lo, tq), shifted)
            o_t = (acc_scr[c1] * (1.0 / l_scr[c1])
                   - acc_scr[c2] * (lam / l_scr[c2]))
            o_ref[:, head * DIFF_DV:(head + 1) * DIFF_DV] = (
                _rms(o_t.T, sg_ref[...] * (1.0 - LAM_INIT))).astype(o_ref.dtype)

    def all_blocks(shifted):
        lax.fori_loop(0, qi, functools.partial(full_block, shifted=shifted), 0)
        diag_block(qi, shifted)

    @pl.when(bounded)
    def _():
        all_blocks(shifted=False)

    @pl.when(jnp.logical_not(bounded))
    def _():
        all_blocks(shifted=True)


def _diff_attn(z, v_t, qg, kg, lq1, lk1, lq2, lk2, sg, batch, seq, tq=512, heads_per_step=4):
    m = z.shape[0]
    nq = seq // tq
    width = heads_per_step * DIFF_DV
    ns = 2 * heads_per_step
    vec = pl.BlockSpec((1, DIFF_DH), lambda b, g, i: (0, 0))
    return pl.pallas_call(
        functools.partial(_diff_kernel, tq=tq),
        out_shape=jax.ShapeDtypeStruct((m, DIFF_HEADS * DIFF_DV), BF16),
        grid=(batch, DIFF_HEADS // heads_per_step, nq),
        in_specs=[pl.BlockSpec((tq, width), lambda b, g, i: (b * nq + i, Z_DIFF_Q // width + g)),
                  pl.BlockSpec((seq, width), lambda b, g, i: (b, Z_DIFF_K // width + g)),
                  pl.BlockSpec((width, seq), lambda b, g, i: (g, b)),
                  vec, vec, vec, vec, vec, vec,
                  pl.BlockSpec((1, DIFF_DV), lambda b, g, i: (0, 0))],
        out_specs=pl.BlockSpec((tq, width), lambda b, g, i: (b * nq + i, g)),
        scratch_shapes=[pltpu.VMEM((ns, DIFF_DV, tq), F32),
                        pltpu.VMEM((ns, 1, tq), F32),
                        pltpu.VMEM((ns, 1, tq), F32)],
        compiler_params=_params("parallel", "parallel", "arbitrary"),
        name="diff_attn",
    )(z, z, v_t, qg, kg, lq1, lk1, lq2, lk2, sg)


def _gate_merge_kernel(h_ref, y0_ref, y1_ref, y2_ref, wg_ref, bg0_ref, bg1_ref, bg2_ref,
                       wb0_ref, wb1_ref, wb2_ref, r0_ref, r1_ref, o_ref, c0_ref, c1_ref):
    _cast_riders((r0_ref,), (c0_ref,))
    _cast_ffn_in_rider(r1_ref, c1_ref)
    bias = jnp.concatenate([bg0_ref[...], bg1_ref[...], bg2_ref[...]], axis=1)
    gates = _sigmoid(jnp.dot(h_ref[...], wg_ref[...], preferred_element_type=F32) + bias)
    merged = None
    for b, (y_ref, wb_ref) in enumerate(((y0_ref, wb0_ref), (y1_ref, wb1_ref), (y2_ref, wb2_ref))):
        term = (gates[:, b * GATE_TILE:(b + 1) * GATE_TILE]
                * jnp.dot(y_ref[...], wb_ref[...], preferred_element_type=F32))
        merged = term if merged is None else merged + term
    o_ref[...] = merged.astype(o_ref.dtype)


def _gate_merge(h, ys, w_gate_tiled, b_gate, w_branch, riders, tm=1024):
    m, d = h.shape
    bw = w_branch.shape[1]
    tn = GATE_TILE
    ni, nj = m // tm, d // tn
    act = lambda width: pl.BlockSpec((tm, width), lambda i, j: (i, 0))
    bg = lambda b: pl.BlockSpec((1, tn), lambda i, j: (0, b * nj + j))
    wb = lambda b: pl.BlockSpec((None, bw, tn), lambda i, j: (b, 0, j))
    r_specs, r_shapes = _rider_specs(riders, ni * nj, lambda i, j: i * nj + j)
    return pl.pallas_call(
        _gate_merge_kernel,
        out_shape=(jax.ShapeDtypeStruct((m, d), BF16), *r_shapes),
        grid=(ni, nj),
        in_specs=[act(d), act(bw), act(bw), act(bw),
                  pl.BlockSpec((d, N_BRANCH * tn), lambda i, j: (0, j)),
                  bg(0), bg(1), bg(2), wb(0), wb(1), wb(2), *r_specs],
        out_specs=(pl.BlockSpec((tm, tn), lambda i, j: (i, j)), *r_specs),
        compiler_params=_params("arbitrary", "arbitrary"),
        name="gate_merge",
    )(h, ys[0], ys[1], ys[2], w_gate_tiled, b_gate, b_gate, b_gate,
      w_branch, w_branch, w_branch, *riders)


def _out_proj_kernel(mg_ref, w_ref, x_ref, g_ref, x1_ref, hf_ref):
    x1 = x_ref[...] + jnp.dot(mg_ref[...], w_ref[...], preferred_element_type=F32)
    x1_ref[...] = x1
    hf_ref[...] = _rms(x1, g_ref[...]).astype(hf_ref.dtype)


def _out_proj(merged, w_out, x2, g, tm=512):
    m, d = x2.shape
    row = lambda i: (i, 0)
    fixed = lambda i: (0, 0)
    return pl.pallas_call(
        _out_proj_kernel,
        out_shape=(jax.ShapeDtypeStruct((m, d), F32), jax.ShapeDtypeStruct((m, d), BF16)),
        grid=(m // tm,),
        in_specs=[pl.BlockSpec((tm, d), row),
                  pl.BlockSpec((d, d), fixed, pipeline_mode=pl.Buffered(1)),
                  pl.BlockSpec((tm, d), row), pl.BlockSpec((1, d), fixed)],
        out_specs=(pl.BlockSpec((tm, d), row), pl.BlockSpec((tm, d), row)),
        compiler_params=_params("parallel"),
        name="out_proj",
    )(merged, w_out, x2, g)


FFN_TILE = 512


def _cast_ffn_in_rider(src, dst):
    d_ff = src.shape[1] // 2
    for j in range(d_ff // FFN_TILE):
        for part in range(2):
            dst_col = (2 * j + part) * FFN_TILE
            src_col = part * d_ff + j * FFN_TILE
            dst[:, dst_col:dst_col + FFN_TILE] = src[:, src_col:src_col + FFN_TILE].astype(BF16)


def _ffn_up_kernel(hf_ref, w_ref, r0_ref, a_ref, c0_ref):
    _cast_riders((r0_ref,), (c0_ref,))
    both = jnp.dot(hf_ref[...], w_ref[...], preferred_element_type=F32)
    gate, up = both[:, :FFN_TILE], both[:, FFN_TILE:]
    a_ref[...] = (gate * _sigmoid(gate) * up).astype(a_ref.dtype)


def _ffn_up(hf, w_in_tiled, riders, tm=1024):
    m, d = hf.shape
    d_ff = w_in_tiled.shape[1] // 2
    tf = FFN_TILE
    ni, nj = m // tm, d_ff // tf
    r_specs, r_shapes = _rider_specs(riders, ni * nj, lambda i, j: i * nj + j)
    return pl.pallas_call(
        _ffn_up_kernel,
        out_shape=(jax.ShapeDtypeStruct((m, d_ff), BF16), *r_shapes),
        grid=(ni, nj),
        in_specs=[pl.BlockSpec((tm, d), lambda i, j: (i, 0)),
                  pl.BlockSpec((d, 2 * tf), lambda i, j: (0, j)),
                  *r_specs],
        out_specs=(pl.BlockSpec((tm, tf), lambda i, j: (i, j)), *r_specs),
        compiler_params=_params("arbitrary", "arbitrary"),
        name="ffn_up",
    )(hf, w_in_tiled, *riders)


def _ffn_down_kernel(a_ref, w_ref, x1_ref, o_ref):
    o_ref[...] = x1_ref[...] + jnp.dot(a_ref[...], w_ref[...], preferred_element_type=F32)


def _ffn_down(a, w_down, x1, tm=1024, tn=512):
    m, d_ff = a.shape
    d = w_down.shape[1]
    return pl.pallas_call(
        _ffn_down_kernel,
        out_shape=jax.ShapeDtypeStruct((m, d), F32),
        grid=(m // tm, d // tn),
        in_specs=[pl.BlockSpec((tm, d_ff), lambda i, j: (i, 0)),
                  pl.BlockSpec((d_ff, tn), lambda i, j: (0, j)),
                  pl.BlockSpec((tm, tn), lambda i, j: (i, j))],
        out_specs=pl.BlockSpec((tm, tn), lambda i, j: (i, j)),
        compiler_params=_params("parallel", "arbitrary"),
        name="ffn_down",
    )(a, w_down, x1)


def kernel(x, mem, norm_mix_g, norm_mem_g, w_in, gla_w_alpha_up, gla_b_alpha, gla_norm_g,
           diff_q_norm_g, diff_k_norm_g, diff_lambda_q1, diff_lambda_k1, diff_lambda_q2,
           diff_lambda_k2, diff_subln_g, mem_q_norm_g, mem_k_norm_g, w_mem_kv, w_branch,
           w_gate, b_gate, w_out, norm_ffn_g, w_ffn_in, w_ffn_down):
    batch, seq, d = x.shape
    n_mem = mem.shape[1]
    depth = w_in.shape[0]
    assert depth == 1, "LAM_INIT is the layer-0 value"
    x2 = x.reshape(batch * seq, d)
    mem2 = mem.reshape(batch * n_mem, d)
    for l in range(depth):
        w_in_t = w_in[l].T
        w_up = jnp.pad(gla_w_alpha_up[l], ((0, LANES - GLA_RANK), (0, 0)))
        row = lambda v: v.reshape(1, -1)

        h, log_a = _norm_mix(x2, row(norm_mix_g[l]), w_in_t, w_up, row(gla_b_alpha[l]))
        z, w_gate_bf, w_branch_bf = _in_proj(
            h, w_in_t, row(diff_q_norm_g[l]), row(diff_k_norm_g[l]), row(mem_q_norm_g[l]),
            riders=(w_gate[l], w_branch[l].reshape(-1, d)))
        y_gla, kv = _gla(z, log_a, row(gla_norm_g[l]), mem2, row(norm_mem_g[l]), w_mem_kv[l],
                         batch, seq)
        v_t, y_mem = _v_proj_t(h, w_in_t, z, kv, row(mem_k_norm_g[l]), seq, n_mem)
        y_diff = _diff_attn(z, v_t, row(diff_q_norm_g[l]), row(diff_k_norm_g[l]),
                            row(diff_lambda_q1[l]), row(diff_lambda_k1[l]),
                            row(diff_lambda_q2[l]), row(diff_lambda_k2[l]),
                            row(diff_subln_g[l]), batch, seq)
        merged, w_out_bf, w_ffn_in_bf = _gate_merge(
            h, (y_gla, y_diff, y_mem), w_gate_bf, row(b_gate[l]),
            w_branch_bf.reshape(w_branch[l].shape), riders=(w_out[l], w_ffn_in[l]))
        x1, hf = _out_proj(merged, w_out_bf, x2, row(norm_ffn_g[l]))
        a, w_ffn_down_bf = _ffn_up(hf, w_ffn_in_bf, riders=(w_ffn_down[l],))
        x2 = _ffn_down(a, w_ffn_down_bf, x1)
    return x2.reshape(batch, seq, d)
```
